```python
import math
import jax, jax.numpy as jnp
from jax import lax
import numpy as np

D_MODEL = 1024
BATCH = 8
SEQ = 2048
DEPTH = 1
DEC_BATCH = 128
DEC_SEQ = 1
PAST_LEN = 16384
PAGE_SIZE = 128

RET_HEADS = 4
RET_DK = 128
RET_DV = 128
RET_WIDTH = RET_HEADS * RET_DV
SSD_HEADS = 8
SSD_HEAD_DIM = 64
SSD_WIDTH = SSD_HEADS * SSD_HEAD_DIM
SSD_GROUPS = 2
SSD_STATE = 128
CONV_WIDTH = 4
CONV_CH = SSD_WIDTH + 2 * SSD_GROUPS * SSD_STATE
MIX_WIDTH = RET_WIDTH + SSD_WIDTH
CHUNK = 128
MEM_TOKENS = 256
X_HEADS = 4
X_HEAD_DIM = D_MODEL // X_HEADS
D_FF = 2816
ROPE_BASE = 10000.0
EPS = 1e-6
Q_END = RET_HEADS * RET_DK
K_END = Q_END + RET_HEADS * RET_DK
V_END = K_END + RET_WIDTH
G_END = V_END + RET_WIDTH
Z_END = G_END + SSD_WIDTH
XBC_END = Z_END + CONV_CH
IN_PROJ_WIDTH = XBC_END + SSD_HEADS

kernel_name = 'hymba_retnet_ssd_macaron_memxattn_step'


def rmsnorm(x, gain):
    xf = x.astype(jnp.float32)
    y = xf * lax.rsqrt(jnp.mean(xf * xf, axis=-1, keepdims=True) + EPS)
    return (y * gain.astype(jnp.float32)).astype(x.dtype)


def swiglu(x, w1, w3, w2):
    return (jax.nn.silu(x @ w1) * (x @ w3)) @ w2


def rotary(x, pos):
    half = x.shape[-1] // 2
    inv_freq = ROPE_BASE ** (-jnp.arange(half, dtype=jnp.float32) / half)
    ang = pos.astype(jnp.float32)[:, None] * inv_freq[None, :]
    cos = jnp.cos(ang)[None, :, None, :]
    sin = jnp.sin(ang)[None, :, None, :]
    xf = x.astype(jnp.float32)
    x1, x2 = xf[..., :half], xf[..., half:]
    return jnp.concatenate([x1 * cos - x2 * sin, x1 * sin + x2 * cos], axis=-1)


def chunk_len(L):
    return CHUNK if L % CHUNK == 0 else L


def retention_chunkwise(q, k, v, state):
    Bsz, L, H, dk = q.shape
    dv = v.shape[-1]
    C = chunk_len(L)
    n = L // C
    log_g = jnp.log1p(-jnp.exp2(-5.0 - jnp.arange(H, dtype=jnp.float32)))
    idx = jnp.arange(C, dtype=jnp.float32)
    diff = idx[:, None] - idx[None, :]
    causal = diff >= 0
    dmat = jnp.where(causal[None], jnp.exp(log_g[:, None, None] * jnp.where(causal, diff, 0.0)[None]), 0.0)
    q_dec = jnp.exp(log_g[None, :] * (idx[:, None] + 1.0))
    k_dec = jnp.exp(log_g[None, :] * (C - 1.0 - idx[:, None]))
    chunk_dec = jnp.exp(log_g * C)
    qc = q.reshape(Bsz, n, C, H, dk)
    kc = k.reshape(Bsz, n, C, H, dk)
    vc = v.reshape(Bsz, n, C, H, dv)
    inner = jnp.einsum('bclhd,bcshd->bchls', qc, kc) * dmat[None, None]
    y_in = jnp.einsum('bchls,bcshe->bclhe', inner, vc)
    kv = jnp.einsum('bcshd,sh,bcshe->bchde', kc, k_dec, vc)

    def step(S, kv_c):
        return chunk_dec[None, :, None, None] * S + kv_c, S

    S_final, S_prev = lax.scan(step, state, jnp.moveaxis(kv, 1, 0))
    S_prev = jnp.moveaxis(S_prev, 0, 1)
    y_x = jnp.einsum('bclhd,bchde->bclhe', qc, S_prev) * q_dec[None, None, :, :, None]
    return (y_in + y_x).reshape(Bsz, L, H, dv), S_final


def ssd_chunked(x, dt, A, Bm, Cm, state):
    Bsz, L, H, P = x.shape
    G, N = Bm.shape[2], Bm.shape[3]
    C = chunk_len(L)
    n = L // C
    rep = H // G
    Bh = jnp.repeat(Bm, rep, axis=2).reshape(Bsz, n, C, H, N)
    Ch = jnp.repeat(Cm, rep, axis=2).reshape(Bsz, n, C, H, N)
    a_cum = jnp.cumsum((dt * A[None, None, :]).reshape(Bsz, n, C, H), axis=2)
    xdt = (x * dt[..., None]).reshape(Bsz, n, C, H, P)
    idx = jnp.arange(C)
    causal = (idx[:, None] >= idx[None, :])[None, None, :, :, None]
    seg = a_cum[:, :, :, None, :] - a_cum[:, :, None, :, :]
    lmat = jnp.where(causal, jnp.exp(jnp.where(causal, seg, 0.0)), 0.0)
    scores = jnp.einsum('bclhn,bcshn->bclsh', Ch, Bh) * lmat
    y_diag = jnp.einsum('bclsh,bcshp->bclhp', scores, xdt)
    decay_s = jnp.exp(a_cum[:, :, -1:, :] - a_cum)
    states = jnp.einsum('bclhn,bclh,bclhp->bchpn', Bh, decay_s, xdt)
    chunk_dec = jnp.exp(a_cum[:, :, -1, :])

    def step(S, inp):
        st, dec = inp
        return dec[:, :, None, None] * S + st, S

    S_final, S_prev = lax.scan(step, state, (jnp.moveaxis(states, 1, 0), jnp.moveaxis(chunk_dec, 1, 0)))
    S_prev = jnp.moveaxis(S_prev, 0, 1)
    y_off = jnp.einsum('bclhn,bchpn,bclh->bclhp', Ch, S_prev, jnp.exp(a_cum))
    return (y_diag + y_off).reshape(Bsz, L, H, P), S_final


def causal_conv(xbc, conv_state, w, b):
    L = xbc.shape[1]
    full = jnp.concatenate([conv_state.astype(xbc.dtype), xbc], axis=1)
    out = b + sum(full[:, i:i + L, :] * w[i] for i in range(CONV_WIDTH))
    return out, full[:, L:, :]


def decoder_layer(x, pos, mem_k, mem_v, ret_state, ssm_state, conv_state, p):
    f32 = jnp.float32
    Bsz, L, _ = x.shape
    h = x + (0.5 * swiglu(rmsnorm(x, p['ffn1_norm']), p['ffn1_w1'], p['ffn1_w3'], p['ffn1_w2'])).astype(x.dtype)
    u = rmsnorm(h, p['mix_norm'])
    proj = u @ p['w_in']
    q, k, v, g, z, xbc, dt_raw = jnp.split(proj, [Q_END, K_END, V_END, G_END, Z_END, XBC_END], axis=-1)
    qr = rotary(q.reshape(Bsz, L, RET_HEADS, RET_DK), pos)
    kr = rotary(k.reshape(Bsz, L, RET_HEADS, RET_DK), pos) * (RET_DK ** -0.5)
    vr = v.reshape(Bsz, L, RET_HEADS, RET_DV).astype(f32)
    r, ret_new = retention_chunkwise(qr, kr, vr, ret_state.astype(f32))
    mu = jnp.mean(r, axis=-1, keepdims=True)
    var = jnp.mean(jnp.square(r - mu), axis=-1, keepdims=True)
    r = ((r - mu) * lax.rsqrt(var + EPS)).reshape(Bsz, L, RET_WIDTH) * p['ret_gn_gain'].astype(f32)
    ret_out = (jax.nn.silu(g.astype(f32)) * r).astype(x.dtype)
    xbc_c, conv_new = causal_conv(xbc, conv_state, p['conv_w'], p['conv_b'])
    xbc_c = jax.nn.silu(xbc_c.astype(f32))
    xs = xbc_c[..., :SSD_WIDTH].reshape(Bsz, L, SSD_HEADS, SSD_HEAD_DIM)
    Bm = xbc_c[..., SSD_WIDTH:SSD_WIDTH + SSD_GROUPS * SSD_STATE].reshape(Bsz, L, SSD_GROUPS, SSD_STATE)
    Cm = xbc_c[..., SSD_WIDTH + SSD_GROUPS * SSD_STATE:].reshape(Bsz, L, SSD_GROUPS, SSD_STATE)
    dt = jax.nn.softplus(dt_raw.astype(f32) + p['dt_bias'].astype(f32))
    A = -jnp.exp(p['A_log'].astype(f32))
    ys, ssm_new = ssd_chunked(xs, dt, A, Bm, Cm, ssm_state.astype(f32))
    ys = ys + p['D_skip'].astype(f32)[:, None] * xs
    ys = ys.reshape(Bsz, L, SSD_WIDTH) * jax.nn.silu(z.astype(f32))
    ssd_out = rmsnorm(ys, p['ssd_norm']).astype(x.dtype)
    h = h + (jnp.concatenate([ret_out, ssd_out], axis=-1) @ p['w_out']).astype(x.dtype)
    c = rmsnorm(h, p['x_norm'])
    qx = (c @ p['w_xq']).reshape(Bsz, L, X_HEADS, X_HEAD_DIM)
    s = jnp.einsum('blhd,bmhd->bhlm', qx, mem_k.astype(x.dtype)).astype(f32) * (X_HEAD_DIM ** -0.5)
    att = jax.nn.softmax(s, axis=-1).astype(x.dtype)
    o = jnp.einsum('bhlm,bmhd->blhd', att, mem_v.astype(x.dtype)).reshape(Bsz, L, X_HEADS * X_HEAD_DIM)
    h = h + (o @ p['w_xo']).astype(x.dtype)
    h = h + (0.5 * swiglu(rmsnorm(h, p['ffn2_norm']), p['ffn2_w1'], p['ffn2_w3'], p['ffn2_w2'])).astype(x.dtype)
    return (h, ret_new.astype(ret_state.dtype), ssm_new.astype(ssm_state.dtype), conv_new.astype(conv_state.dtype))


def setup_inputs(seed: int = 0) -> dict:
    key = jax.random.key(seed)
    ks = iter(jax.random.split(key, 40))

    def nrm(shape, scale):
        return jax.random.normal(next(ks), shape, jnp.float32) * scale

    def gain(shape):
        return 1.0 + nrm(shape, 0.02)

    x_prompt = nrm((BATCH, SEQ, D_MODEL), 1.0)
    x_sample = nrm((DEC_BATCH, DEC_SEQ, D_MODEL), 1.0)
    mem_prompt = nrm((BATCH, MEM_TOKENS, D_MODEL), 1.0)
    state_ret = nrm((DEPTH, DEC_BATCH, RET_HEADS, RET_DK, RET_DV), 0.1)
    state_ssm = nrm((DEPTH, DEC_BATCH, SSD_HEADS, SSD_HEAD_DIM, SSD_STATE), 0.1)
    state_conv = nrm((DEPTH, DEC_BATCH, CONV_WIDTH - 1, CONV_CH), 1.0)
    cache_mem_k = nrm((DEPTH, DEC_BATCH, MEM_TOKENS, X_HEADS, X_HEAD_DIM), 1.0)
    cache_mem_v = nrm((DEPTH, DEC_BATCH, MEM_TOKENS, X_HEADS, X_HEAD_DIM), 1.0)
    ffn1_norm = gain((DEPTH, D_MODEL))
    ffn1_w1 = nrm((DEPTH, D_MODEL, D_FF), D_MODEL ** -0.5)
    ffn1_w3 = nrm((DEPTH, D_MODEL, D_FF), D_MODEL ** -0.5)
    ffn1_w2 = nrm((DEPTH, D_FF, D_MODEL), D_FF ** -0.5)
    mix_norm = gain((DEPTH, D_MODEL))
    w_in = nrm((DEPTH, D_MODEL, IN_PROJ_WIDTH), D_MODEL ** -0.5)
    ret_gn_gain = gain((DEPTH, RET_WIDTH))
    conv_w = nrm((DEPTH, CONV_WIDTH, CONV_CH), CONV_WIDTH ** -0.5)
    conv_b = nrm((DEPTH, CONV_CH), 0.02)
    dt0 = jnp.exp(jax.random.uniform(next(ks), (DEPTH, SSD_HEADS), jnp.float32, math.log(1e-3), math.log(1e-1)))
    dt_bias = dt0 + jnp.log(-jnp.expm1(-dt0))
    A_log = jnp.log(jax.random.uniform(next(ks), (DEPTH, SSD_HEADS), jnp.float32, 1.0, 16.0))
    D_skip = gain((DEPTH, SSD_HEADS))
    ssd_norm = gain((DEPTH, SSD_WIDTH))
    w_out = nrm((DEPTH, MIX_WIDTH, D_MODEL), MIX_WIDTH ** -0.5)
    x_norm = gain((DEPTH, D_MODEL))
    mem_norm = gain((DEPTH, D_MODEL))
    w_xq = nrm((DEPTH, D_MODEL, X_HEADS * X_HEAD_DIM), D_MODEL ** -0.5)
    w_xk = nrm((DEPTH, D_MODEL, X_HEADS * X_HEAD_DIM), D_MODEL ** -0.5)
    w_xv = nrm((DEPTH, D_MODEL, X_HEADS * X_HEAD_DIM), D_MODEL ** -0.5)
    w_xo = nrm((DEPTH, X_HEADS * X_HEAD_DIM, D_MODEL), (X_HEADS * X_HEAD_DIM) ** -0.5)
    ffn2_norm = gain((DEPTH, D_MODEL))
    ffn2_w1 = nrm((DEPTH, D_MODEL, D_FF), D_MODEL ** -0.5)
    ffn2_w3 = nrm((DEPTH, D_MODEL, D_FF), D_MODEL ** -0.5)
    ffn2_w2 = nrm((DEPTH, D_FF, D_MODEL), D_FF ** -0.5)
    final_norm = gain((D_MODEL,))
    return {'x_prompt': x_prompt, 'x_sample': x_sample, 'mem_prompt': mem_prompt,
            'state_ret': state_ret, 'state_ssm': state_ssm, 'state_conv': state_conv,
            'cache_mem_k': cache_mem_k, 'cache_mem_v': cache_mem_v,
            'ffn1_norm': ffn1_norm, 'ffn1_w1': ffn1_w1, 'ffn1_w3': ffn1_w3, 'ffn1_w2': ffn1_w2,
            'mix_norm': mix_norm, 'w_in': w_in, 'ret_gn_gain': ret_gn_gain,
            'conv_w': conv_w, 'conv_b': conv_b, 'dt_bias': dt_bias, 'A_log': A_log,
            'D_skip': D_skip, 'ssd_norm': ssd_norm, 'w_out': w_out,
            'x_norm': x_norm, 'mem_norm': mem_norm, 'w_xq': w_xq, 'w_xk': w_xk,
            'w_xv': w_xv, 'w_xo': w_xo,
            'ffn2_norm': ffn2_norm, 'ffn2_w1': ffn2_w1, 'ffn2_w3': ffn2_w3, 'ffn2_w2': ffn2_w2,
            'final_norm': final_norm}


def reference(x_prompt, x_sample, mem_prompt, state_ret, state_ssm, state_conv,
              cache_mem_k, cache_mem_v, ffn1_norm, ffn1_w1, ffn1_w3, ffn1_w2,
              mix_norm, w_in, ret_gn_gain, conv_w, conv_b, dt_bias, A_log, D_skip,
              ssd_norm, w_out, x_norm, mem_norm, w_xq, w_xk, w_xv, w_xo,
              ffn2_norm, ffn2_w1, ffn2_w3, ffn2_w2, final_norm):
    Bp, Lp, _ = x_prompt.shape
    pos_p = jnp.arange(Lp)
    pos_s = PAST_LEN + jnp.arange(x_sample.shape[1])
    dtp = x_prompt.dtype
    y_p = x_prompt
    y_s = x_sample
    ret_p, ssm_p, conv_p, memk_p, memv_p = [], [], [], [], []
    ret_s, ssm_s, conv_s = [], [], []
    for l in range(DEPTH):
        p = {'ffn1_norm': ffn1_norm[l], 'ffn1_w1': ffn1_w1[l], 'ffn1_w3': ffn1_w3[l], 'ffn1_w2': ffn1_w2[l],
             'mix_norm': mix_norm[l], 'w_in': w_in[l], 'ret_gn_gain': ret_gn_gain[l],
             'conv_w': conv_w[l], 'conv_b': conv_b[l], 'dt_bias': dt_bias[l], 'A_log': A_log[l],
             'D_skip': D_skip[l], 'ssd_norm': ssd_norm[l], 'w_out': w_out[l],
             'x_norm': x_norm[l], 'w_xq': w_xq[l], 'w_xo': w_xo[l],
             'ffn2_norm': ffn2_norm[l], 'ffn2_w1': ffn2_w1[l], 'ffn2_w3': ffn2_w3[l], 'ffn2_w2': ffn2_w2[l]}
        m = rmsnorm(mem_prompt, mem_norm[l])
        mk = (m @ w_xk[l]).reshape(Bp, MEM_TOKENS, X_HEADS, X_HEAD_DIM)
        mv = (m @ w_xv[l]).reshape(Bp, MEM_TOKENS, X_HEADS, X_HEAD_DIM)
        zr = jnp.zeros((Bp, RET_HEADS, RET_DK, RET_DV), dtp)
        zs = jnp.zeros((Bp, SSD_HEADS, SSD_HEAD_DIM, SSD_STATE), dtp)
        zc = jnp.zeros((Bp, CONV_WIDTH - 1, CONV_CH), dtp)
        y_p, rp, sp, cp = decoder_layer(y_p, pos_p, mk, mv, zr, zs, zc, p)
        y_s, rs, ss, cs = decoder_layer(y_s, pos_s, cache_mem_k[l], cache_mem_v[l],
                                        state_ret[l], state_ssm[l], state_conv[l], p)
        ret_p.append(rp)
        ssm_p.append(sp)
        conv_p.append(cp)
        memk_p.append(mk)
        memv_p.append(mv)
        ret_s.append(rs)
        ssm_s.append(ss)
        conv_s.append(cs)
    y_prompt = rmsnorm(y_p, final_norm)
    y_sample = rmsnorm(y_s, final_norm)
    return (y_prompt, y_sample, jnp.stack(ret_p), jnp.stack(ssm_p), jnp.stack(conv_p),
            jnp.stack(memk_p), jnp.stack(memv_p), jnp.stack(ret_s), jnp.stack(ssm_s), jnp.stack(conv_s))
```

```python
import functools

import jax
import jax.numpy as jnp
from jax import lax
from jax.experimental import pallas as pl
from jax.experimental.pallas import tpu as pltpu

F32 = jnp.float32
BF16 = jnp.bfloat16

D_MODEL = 1024
D_FF = 2816
PAST_LEN = 16384
RET_HEADS = 4
RET_DK = 128
RET_DV = 128
RET_WIDTH = RET_HEADS * RET_DV
SSD_HEADS = 8
SSD_HEAD_DIM = 64
SSD_WIDTH = SSD_HEADS * SSD_HEAD_DIM
SSD_GROUPS = 2
SSD_STATE = 128
SSD_PAIRS = SSD_HEADS // 2
CONV_WIDTH = 4
CONV_CH = SSD_WIDTH + 2 * SSD_GROUPS * SSD_STATE
CHUNK = 128
MEM_TOKENS = 256
X_HEADS = 4
X_HEAD_DIM = D_MODEL // X_HEADS
ROPE_BASE = 10000.0
EPS = 1e-6

Q_OFF = 0
K_OFF = Q_OFF + RET_HEADS * RET_DK
V_OFF = K_OFF + RET_HEADS * RET_DK
G_OFF = V_OFF + RET_WIDTH
Z_OFF = G_OFF + RET_WIDTH
XBC_OFF = Z_OFF + SSD_WIDTH
DT_OFF = XBC_OFF + CONV_CH
IN_PROJ_WIDTH = DT_OFF + SSD_HEADS

V7X_LANES = 128
V7X_SUBLANES = 8
V7X_VMEM_LIMIT_BYTES = 56 * 1024 * 1024
DT_PAD = V7X_LANES
IN_PROJ_PAD = DT_OFF + DT_PAD
CONV_HIST = V7X_SUBLANES


def _params(sem):
    return pltpu.CompilerParams(dimension_semantics=sem, vmem_limit_bytes=V7X_VMEM_LIMIT_BYTES)


def _resident(shape):
    zeros = (0,) * len(shape)
    return pl.BlockSpec(shape, lambda *_: zeros, pipeline_mode=pl.Buffered(1))


def _rmsnorm(x, gain):
    ms = jnp.mean(x * x, axis=-1, keepdims=True)
    return x * lax.rsqrt(ms + EPS) * gain


def _dot(a, b):
    return jnp.dot(a, b, preferred_element_type=F32)


def _dot_nt(a, b):
    return lax.dot_general(a, b, (((1,), (1,)), ((), ())), preferred_element_type=F32)


def _dot_tn(a, b):
    return lax.dot_general(a, b, (((0,), (0,)), ((), ())), preferred_element_type=F32)


def _softplus(x):
    return jnp.maximum(x, 0.0) + jnp.log1p(jnp.exp(-jnp.abs(x)))


def _rotary(x, cos2, sin2):
    return x * cos2 + pltpu.roll(x, RET_DK // 2, axis=1) * sin2


def _group_norm(y):
    mu = jnp.mean(y, axis=-1, keepdims=True)
    d = y - mu
    var = jnp.mean(d * d, axis=-1, keepdims=True)
    return d * lax.rsqrt(var + EPS)


def _ffn_kernel(*refs, final_norm):
    if final_norm:
        x_ref, g_ref, w1_ref, w3_ref, w2_ref, fg_ref, o_ref = refs
    else:
        x_ref, g_ref, w1_ref, w3_ref, w2_ref, o_ref = refs
    x = x_ref[...]
    xn = _rmsnorm(x, g_ref[...]).astype(BF16)
    a = _dot(xn, w1_ref[...])
    b = _dot(xn, w3_ref[...])
    hidden = (jax.nn.silu(a) * b).astype(BF16)
    out = x + 0.5 * _dot(hidden, w2_ref[...])
    if final_norm:
        out = _rmsnorm(out, fg_ref[...])
    o_ref[...] = out


def _ffn(x, gain, w1, w3, w2, final_gain=None, *, tm):
    t, d = x.shape
    row = pl.BlockSpec((tm, d), lambda i: (i, 0))
    ins = [x, gain, w1, w3, w2]
    specs = [row, _resident(gain.shape), _resident(w1.shape), _resident(w3.shape), _resident(w2.shape)]
    if final_gain is not None:
        ins.append(final_gain)
        specs.append(_resident(final_gain.shape))
    return pl.pallas_call(
        functools.partial(_ffn_kernel, final_norm=final_gain is not None),
        out_shape=jax.ShapeDtypeStruct((t, d), F32),
        grid=(t // tm,),
        in_specs=specs,
        out_specs=row,
        compiler_params=_params(("arbitrary",)),
        name="ffn",
    )(*ins)


def _linear_kernel(*refs, has_norm, has_res, n_w):
    refs = list(refs)
    x_ref = refs.pop(0)
    g_ref = refs.pop(0) if has_norm else None
    r_ref = refs.pop(0) if has_res else None
    w_refs, o_refs = refs[:n_w], refs[n_w:]
    x = x_ref[...]
    if has_norm:
        x = _rmsnorm(x, g_ref[...])
    xb = x.astype(BF16)
    for w_ref, o_ref in zip(w_refs, o_refs, strict=True):
        y = _dot(xb, w_ref[...])
        if has_res:
            y = r_ref[...] + y
        o_ref[...] = y


def _linear(x, weights, gain=None, res=None, *, tm):
    t, k = x.shape
    ins, specs = [x], [pl.BlockSpec((tm, k), lambda i: (i, 0))]
    if gain is not None:
        ins.append(gain)
        specs.append(_resident(gain.shape))
    if res is not None:
        ins.append(res)
        specs.append(pl.BlockSpec((tm, res.shape[1]), lambda i: (i, 0)))
    for w in weights:
        ins.append(w)
        specs.append(_resident(w.shape))
    outs = pl.pallas_call(
        functools.partial(_linear_kernel, has_norm=gain is not None, has_res=res is not None, n_w=len(weights)),
        out_shape=[jax.ShapeDtypeStruct((t, w.shape[1]), F32) for w in weights],
        grid=(t // tm,),
        in_specs=specs,
        out_specs=[pl.BlockSpec((tm, w.shape[1]), lambda i: (i, 0)) for w in weights],
        compiler_params=_params(("arbitrary",)),
        name="linear",
    )(*ins)
    return outs


def _cumsum_rows(x):
    n = x.shape[0]
    row = lax.broadcasted_iota(jnp.int32, x.shape, 0)
    step = 1
    while step < n:
        x = x + jnp.where(row >= step, pltpu.roll(x, step, axis=0), 0.0)
        step *= 2
    return x


def _mix_prompt_kernel(h_ref, g_ref, win_ref, cos_ref, sin_ref, dmat_ref, qdec_ref, kdec_ref, cdec_ref,
                       gn_ref, cw_ref, cb_ref, dtb_ref, alog_ref, dsk_ref, sn_ref, wout_ref,
                       o_ref, ret_ref, ssm_ref, conv_ref,
                       proj_s, xbc_s, xc_s, dt_s, a_s, mix_s, sret_s, sssm_s, *, tm, nt):
    t = pl.program_id(1)

    @pl.when(t == 0)
    def _():
        sret_s[...] = jnp.zeros_like(sret_s)
        sssm_s[...] = jnp.zeros_like(sssm_s)
        xbc_s[0:CONV_HIST, :] = jnp.zeros((CONV_HIST, CONV_CH), F32)

    h = h_ref[...]
    u = _rmsnorm(h, g_ref[...]).astype(BF16)
    proj_s[...] = _dot(u, win_ref[:, 0:XBC_OFF])
    xbc_s[CONV_HIST:CONV_HIST + tm, :] = _dot(u, win_ref[:, XBC_OFF:DT_OFF])
    dt_raw = _dot(u, win_ref[:, DT_OFF:IN_PROJ_PAD])

    conv = cb_ref[...]
    for i in range(CONV_WIDTH):
        off = CONV_HIST - (CONV_WIDTH - 1) + i
        conv = conv + cw_ref[i:i + 1, :] * xbc_s[off:off + tm, :]
    xc_s[...] = jax.nn.silu(conv)

    @pl.when(t == nt - 1)
    def _():
        conv_ref[0] = xbc_s[CONV_HIST + tm - (CONV_WIDTH - 1):CONV_HIST + tm, :]

    xbc_s[0:CONV_HIST, :] = xbc_s[tm:tm + CONV_HIST, :]

    dt = _softplus(dt_raw + dtb_ref[...])
    dt_s[...] = dt
    a_s[...] = dt * (-jnp.exp(alog_ref[...]))

    row_i = lax.broadcasted_iota(jnp.int32, (CHUNK, CHUNK), 0)
    col_i = lax.broadcasted_iota(jnp.int32, (CHUNK, CHUNK), 1)
    causal = row_i >= col_i
    low_lanes = col_i < SSD_HEAD_DIM
    bd_rows = lax.broadcasted_iota(jnp.int32, (2 * SSD_STATE, 2 * SSD_HEAD_DIM), 0)
    bd_cols = lax.broadcasted_iota(jnp.int32, (2 * SSD_STATE, 2 * SSD_HEAD_DIM), 1)
    bd_mask = (bd_rows < SSD_STATE) == (bd_cols < SSD_HEAD_DIM)

    def chunk_body(c, carry):
        r0 = pl.multiple_of(c * CHUNK, CHUNK)
        rows = pl.ds(r0, CHUNK)
        cos2 = cos_ref[rows, :]
        sin2 = sin_ref[rows, :]

        for hh in range(RET_HEADS):
            lanes = slice(hh * RET_DK, (hh + 1) * RET_DK)
            qr = _rotary(proj_s[rows, Q_OFF + hh * RET_DK:Q_OFF + (hh + 1) * RET_DK], cos2, sin2)
            kr = _rotary(proj_s[rows, K_OFF + hh * RET_DK:K_OFF + (hh + 1) * RET_DK], cos2, sin2) * (RET_DK ** -0.5)
            vh = proj_s[rows, V_OFF + hh * RET_DV:V_OFF + (hh + 1) * RET_DV]
            gh = proj_s[rows, G_OFF + hh * RET_DV:G_OFF + (hh + 1) * RET_DV]
            inner = _dot_nt(qr.astype(BF16), kr.astype(BF16)) * dmat_ref[hh]
            s_prev = sret_s[hh]
            lhs = jnp.concatenate([inner, qr * qdec_ref[hh]], axis=1).astype(BF16)
            rhs = jnp.concatenate([vh, s_prev], axis=0).astype(BF16)
            y = _dot(lhs, rhs)
            kv = _dot_tn((kr * kdec_ref[hh]).astype(BF16), vh.astype(BF16))
            sret_s[hh] = cdec_ref[hh:hh + 1, :] * s_prev + kv
            r = _group_norm(y) * gn_ref[:, lanes]
            mix_s[rows, lanes] = (jax.nn.silu(gh) * r).astype(BF16)

        a_cum = _cumsum_rows(a_s[rows, :])
        a_cum_t = a_cum.T
        a_last = a_cum[CHUNK - 1:CHUNK, :]
        e_cum = jnp.exp(a_cum)
        d_cum = jnp.exp(a_last - a_cum)
        c_dec = jnp.exp(a_last)
        dt_c = dt_s[rows, :]
        ys = []
        for grp in range(SSD_GROUPS):
            b_off = SSD_WIDTH + grp * SSD_STATE
            c_off = SSD_WIDTH + SSD_GROUPS * SSD_STATE + grp * SSD_STATE
            bg = xc_s[rows, b_off:b_off + SSD_STATE]
            cg = xc_s[rows, c_off:c_off + SSD_STATE]
            gmat = _dot_nt(cg.astype(BF16), bg.astype(BF16))
            for pj in range(SSD_PAIRS // SSD_GROUPS):
                j = grp * (SSD_PAIRS // SSD_GROUPS) + pj
                x_pair = xc_s[rows, j * 2 * SSD_HEAD_DIM:(j + 1) * 2 * SSD_HEAD_DIM]
                heads = (2 * j, 2 * j + 1)
                xdt = x_pair * jnp.where(low_lanes, dt_c[:, heads[0]:heads[0] + 1], dt_c[:, heads[1]:heads[1] + 1])
                lhs_parts, bd_parts, dec_parts = [], [], []
                for hd in heads:
                    seg = a_cum[:, hd:hd + 1] - a_cum_t[hd:hd + 1, :]
                    lmat = jnp.where(causal, jnp.exp(jnp.where(causal, seg, 0.0)), 0.0)
                    lhs_parts.append(gmat * lmat)
                for hd in heads:
                    lhs_parts.append(cg * e_cum[:, hd:hd + 1])
                    bd_parts.append(bg * d_cum[:, hd:hd + 1])
                    dec_parts.append(jnp.broadcast_to(c_dec[:, hd:hd + 1], (SSD_STATE, 2 * SSD_HEAD_DIM)))
                st_prev = sssm_s[j]
                x_bd = jnp.concatenate([jnp.where(low_lanes, xdt, 0.0), jnp.where(low_lanes, 0.0, xdt)], axis=0)
                lhs = jnp.concatenate(lhs_parts, axis=1).astype(BF16)
                rhs = jnp.concatenate([x_bd, st_prev], axis=0).astype(BF16)
                y = _dot(lhs, rhs)
                st = _dot_tn(jnp.concatenate(bd_parts, axis=1).astype(BF16), xdt.astype(BF16))
                sssm_s[j] = jnp.concatenate(dec_parts, axis=0) * st_prev + jnp.where(bd_mask, st, 0.0)
                ys.append(y + dsk_ref[:, j * 2 * SSD_HEAD_DIM:(j + 1) * 2 * SSD_HEAD_DIM] * x_pair)
        y_all = jnp.concatenate(ys, axis=1) * jax.nn.silu(proj_s[rows, Z_OFF:Z_OFF + SSD_WIDTH])
        mix_s[rows, RET_WIDTH:RET_WIDTH + SSD_WIDTH] = _rmsnorm(y_all, sn_ref[...]).astype(BF16)
        return carry

    lax.fori_loop(0, tm // CHUNK, chunk_body, 0)

    o_ref[...] = h + _dot(mix_s[...], wout_ref[...])

    @pl.when(t == nt - 1)
    def _():
        ret_ref[0] = sret_s[...]
        for j in range(SSD_PAIRS):
            st = sssm_s[j]
            ssm_ref[0, j * 2 * SSD_HEAD_DIM:(j + 1) * 2 * SSD_HEAD_DIM, :] = (st[0:SSD_STATE] + st[SSD_STATE:]).T


def _mix_prompt(h, gain, w_in, cos2, sin2, dmat, qdec, kdec, cdec, gn, conv_w, conv_b, dt_bias, a_log, dskip,
                ssd_gain, w_out, *, batch, seq, tm):
    nt = seq // tm
    d = h.shape[1]
    row = pl.BlockSpec((tm, d), lambda b, t: (b * nt + t, 0))
    pos = pl.BlockSpec((tm, RET_DK), lambda b, t: (t, 0))
    consts = [gain, w_in]
    tail = [dmat, qdec, kdec, cdec, gn, conv_w, conv_b, dt_bias, a_log, dskip, ssd_gain, w_out]
    return pl.pallas_call(
        functools.partial(_mix_prompt_kernel, tm=tm, nt=nt),
        out_shape=[
            jax.ShapeDtypeStruct(h.shape, F32),
            jax.ShapeDtypeStruct((batch, RET_HEADS, RET_DK, RET_DV), F32),
            jax.ShapeDtypeStruct((batch, SSD_WIDTH, SSD_STATE), F32),
            jax.ShapeDtypeStruct((batch, CONV_WIDTH - 1, CONV_CH), F32),
        ],
        grid=(batch, nt),
        in_specs=[row] + [_resident(a.shape) for a in consts] + [pos, pos] + [_resident(a.shape) for a in tail],
        out_specs=[
            row,
            pl.BlockSpec((1, RET_HEADS, RET_DK, RET_DV), lambda b, t: (b, 0, 0, 0)),
            pl.BlockSpec((1, SSD_WIDTH, SSD_STATE), lambda b, t: (b, 0, 0)),
            pl.BlockSpec((1, CONV_WIDTH - 1, CONV_CH), lambda b, t: (b, 0, 0)),
        ],
        scratch_shapes=[
            pltpu.VMEM((tm, XBC_OFF), F32),
            pltpu.VMEM((CONV_HIST + tm, CONV_CH), F32),
            pltpu.VMEM((tm, CONV_CH), F32),
            pltpu.VMEM((tm, DT_PAD), F32),
            pltpu.VMEM((tm, DT_PAD), F32),
            pltpu.VMEM((tm, RET_WIDTH + SSD_WIDTH), BF16),
            pltpu.VMEM((RET_HEADS, RET_DK, RET_DV), F32),
            pltpu.VMEM((SSD_PAIRS, 2 * SSD_STATE, 2 * SSD_HEAD_DIM), F32),
        ],
        compiler_params=_params(("arbitrary", "arbitrary")),
        name="mix_prompt",
    )(h, *consts, cos2, sin2, *tail)


def _pad_rows_t(x):
    n, w = x.shape
    return jnp.concatenate([x, jnp.zeros((V7X_LANES - n, w), x.dtype)], axis=0).T


def _mix_sample_kernel(proj_ref, sret_ref, sssm_ref, sconv_ref, cos_ref, sin_ref, gam_ref,
                       gn_ref, cw_ref, cb_ref, dtb_ref, alog_ref, dsk_ref, sn_ref,
                       mix_ref, oret_ref, ossm_ref, oconv_ref, *, bs):
    proj = proj_ref[...]
    xbc = proj[:, XBC_OFF:DT_OFF]
    hist = sconv_ref[...]
    taps = [hist[:, i * CONV_CH:(i + 1) * CONV_CH] for i in range(CONV_WIDTH - 1)] + [xbc]
    conv = cb_ref[...]
    for i in range(CONV_WIDTH):
        conv = conv + cw_ref[i:i + 1, :] * taps[i]
    oconv_ref[...] = jnp.concatenate(taps[1:], axis=1)
    xc = jax.nn.silu(conv)

    dt = _softplus(proj[:, DT_OFF:IN_PROJ_PAD] + dtb_ref[...])
    dec = jnp.exp(dt * (-jnp.exp(alog_ref[...])))

    sub_i = lax.broadcasted_iota(jnp.int32, (bs, V7X_LANES), 0)
    lane_i = lax.broadcasted_iota(jnp.int32, (SSD_HEAD_DIM, V7X_LANES), 1)
    cos2 = cos_ref[...]
    sin2 = sin_ref[...]

    for hh in range(RET_HEADS):
        lanes = slice(hh * RET_DK, (hh + 1) * RET_DK)
        qr = _rotary(proj[:, Q_OFF + hh * RET_DK:Q_OFF + (hh + 1) * RET_DK], cos2, sin2)
        kr = _rotary(proj[:, K_OFF + hh * RET_DK:K_OFF + (hh + 1) * RET_DK], cos2, sin2) * (RET_DK ** -0.5)
        vh = proj[:, V_OFF + hh * RET_DV:V_OFF + (hh + 1) * RET_DV]
        q_t = _pad_rows_t(qr)
        k_t = _pad_rows_t(kr)
        gamma = gam_ref[hh:hh + 1, :]
        y = jnp.zeros((bs, RET_DV), F32)
        for b in range(bs):
            s_new = gamma * sret_ref[b, hh] + k_t[:, b:b + 1] * vh[b:b + 1, :]
            oret_ref[b, hh] = s_new
            y_row = jnp.sum(q_t[:, b:b + 1] * s_new, axis=0, keepdims=True)
            y = jnp.where(sub_i == b, y_row, y)
        r = _group_norm(y) * gn_ref[:, lanes]
        mix_ref[:, lanes] = jax.nn.silu(proj[:, G_OFF + hh * RET_DV:G_OFF + (hh + 1) * RET_DV]) * r

    xs = xc[:, 0:SSD_WIDTH]
    head_of_lane = lax.broadcasted_iota(jnp.int32, (bs, SSD_WIDTH), 1) // SSD_HEAD_DIM
    dt_wide = jnp.zeros((bs, SSD_WIDTH), F32)
    for hd in range(SSD_HEADS):
        dt_wide = jnp.where(head_of_lane == hd, dt[:, hd:hd + 1], dt_wide)
    xdt_t = _pad_rows_t(xs * dt_wide)
    y_t = []
    for hd in range(SSD_HEADS):
        grp = hd // (SSD_HEADS // SSD_GROUPS)
        b_off = SSD_WIDTH + grp * SSD_STATE
        c_off = SSD_WIDTH + SSD_GROUPS * SSD_STATE + grp * SSD_STATE
        y_cols = jnp.zeros((SSD_HEAD_DIM, V7X_LANES), F32)
        for b in range(bs):
            s_new = (dec[b:b + 1, hd:hd + 1] * sssm_ref[b, hd]
                     + xdt_t[hd * SSD_HEAD_DIM:(hd + 1) * SSD_HEAD_DIM, b:b + 1] * xc[b:b + 1, b_off:b_off + SSD_STATE])
            ossm_ref[b, hd] = s_new
            y_col = jnp.sum(s_new * xc[b:b + 1, c_off:c_off + SSD_STATE], axis=1, keepdims=True)
            y_cols = jnp.where(lane_i == b, y_col, y_cols)
        y_t.append(y_cols)
    ys = jnp.concatenate(y_t, axis=0).T[0:bs, :]
    ys = (ys + dsk_ref[...] * xs) * jax.nn.silu(proj[:, Z_OFF:Z_OFF + SSD_WIDTH])
    mix_ref[:, RET_WIDTH:RET_WIDTH + SSD_WIDTH] = _rmsnorm(ys, sn_ref[...])


def _mix_sample(proj, s_ret, s_ssm, s_conv, cos2, sin2, gamma, gn, conv_w, conv_b, dt_bias, a_log, dskip, ssd_gain,
                *, bs):
    n = proj.shape[0]
    consts = [cos2, sin2, gamma, gn, conv_w, conv_b, dt_bias, a_log, dskip, ssd_gain]
    blk2 = lambda w: pl.BlockSpec((bs, w), lambda i: (i, 0))
    ret_blk = pl.BlockSpec((bs, RET_HEADS, RET_DK, RET_DV), lambda i: (i, 0, 0, 0))
    ssm_blk = pl.BlockSpec((bs, SSD_HEADS, SSD_HEAD_DIM, SSD_STATE), lambda i: (i, 0, 0, 0))
    return pl.pallas_call(
        functools.partial(_mix_sample_kernel, bs=bs),
        out_shape=[
            jax.ShapeDtypeStruct((n, RET_WIDTH + SSD_WIDTH), F32),
            jax.ShapeDtypeStruct(s_ret.shape, F32),
            jax.ShapeDtypeStruct(s_ssm.shape, F32),
            jax.ShapeDtypeStruct(s_conv.shape, F32),
        ],
        grid=(n // bs,),
        in_specs=[blk2(proj.shape[1]), ret_blk, ssm_blk, blk2(s_conv.shape[1])] + [_resident(a.shape) for a in consts],
        out_specs=[blk2(RET_WIDTH + SSD_WIDTH), ret_blk, ssm_blk, blk2(s_conv.shape[1])],
        compiler_params=_params(("arbitrary",)),
        name="mix_sample",
    )(proj, s_ret, s_ssm, s_conv, *consts)


def _softmax_rows(s):
    m = jnp.max(s, axis=-1, keepdims=True)
    p = jnp.exp(s - m)
    return p / jnp.sum(p, axis=-1, keepdims=True)


def _xattn_prompt_kernel(h_ref, g_ref, wq_ref, mk_ref, mv_ref, wo_ref, o_ref, att_s):
    h = h_ref[...]
    c = _rmsnorm(h, g_ref[...]).astype(BF16)
    qx = _dot(c, wq_ref[...]).astype(BF16)
    for hh in range(X_HEADS):
        lanes = slice(hh * X_HEAD_DIM, (hh + 1) * X_HEAD_DIM)
        s = _dot_nt(qx[:, lanes], mk_ref[0, :, lanes].astype(BF16)) * (X_HEAD_DIM ** -0.5)
        att = _softmax_rows(s).astype(BF16)
        att_s[:, lanes] = _dot(att, mv_ref[0, :, lanes].astype(BF16)).astype(BF16)
    o_ref[...] = h + _dot(att_s[...], wo_ref[...])


def _xattn_prompt(h, gain, w_q, mem_k, mem_v, w_o, *, batch, seq, tm):
    nt = seq // tm
    d = h.shape[1]
    row = pl.BlockSpec((tm, d), lambda b, t: (b * nt + t, 0))
    mem = pl.BlockSpec((1, MEM_TOKENS, d), lambda b, t: (b, 0, 0))
    return pl.pallas_call(
        _xattn_prompt_kernel,
        out_shape=jax.ShapeDtypeStruct(h.shape, F32),
        grid=(batch, nt),
        in_specs=[row, _resident(gain.shape), _resident(w_q.shape), mem, mem, _resident(w_o.shape)],
        out_specs=row,
        scratch_shapes=[pltpu.VMEM((tm, d), BF16)],
        compiler_params=_params(("arbitrary", "arbitrary")),
        name="xattn_prompt",
    )(h, gain, w_q, mem_k, mem_v, w_o)


def _xattn_sample_kernel(q_ref, k_ref, v_ref, o_ref, *, bs):
    q = q_ref[...]
    head_of_lane = lax.broadcasted_iota(jnp.int32, (X_HEADS, D_MODEL), 1) // X_HEAD_DIM
    head_of_row = lax.broadcasted_iota(jnp.int32, (X_HEADS, D_MODEL), 0)
    own = head_of_lane == head_of_row
    for b in range(bs):
        q_bd = jnp.where(own, q[b:b + 1, :], 0.0).astype(BF16)
        s = _dot_nt(q_bd, k_ref[b].astype(BF16)) * (X_HEAD_DIM ** -0.5)
        att = _softmax_rows(s).astype(BF16)
        o_all = _dot(att, v_ref[b].astype(BF16))
        o_ref[b:b + 1, :] = jnp.sum(jnp.where(own, o_all, 0.0), axis=0, keepdims=True)


def _xattn_sample(q, mem_k, mem_v, *, bs):
    n, d = q.shape
    row = pl.BlockSpec((bs, d), lambda i: (i, 0))
    mem = pl.BlockSpec((bs, MEM_TOKENS, d), lambda i: (i, 0, 0))
    return pl.pallas_call(
        functools.partial(_xattn_sample_kernel, bs=bs),
        out_shape=jax.ShapeDtypeStruct((n, d), F32),
        grid=(n // bs,),
        in_specs=[row, mem, mem],
        out_specs=row,
        compiler_params=_params(("arbitrary",)),
        name="xattn_sample",
    )(q, mem_k, mem_v)


def _rope_tables(pos):
    half = RET_DK // 2
    inv_freq = ROPE_BASE ** (-jnp.arange(half, dtype=F32) / half)
    ang = pos.astype(F32)[:, None] * inv_freq[None, :]
    cos, sin = jnp.cos(ang), jnp.sin(ang)
    return jnp.concatenate([cos, cos], axis=-1), jnp.concatenate([-sin, sin], axis=-1)


def _retention_decay_tables(chunk):
    log_g = jnp.log1p(-jnp.exp2(-5.0 - jnp.arange(RET_HEADS, dtype=F32)))
    idx = jnp.arange(chunk, dtype=F32)
    diff = idx[:, None] - idx[None, :]
    causal = diff >= 0
    dmat = jnp.where(causal[None], jnp.exp(log_g[:, None, None] * jnp.where(causal, diff, 0.0)[None]), 0.0)
    q_dec = jnp.exp(log_g[:, None] * (idx[None, :] + 1.0))
    k_dec = jnp.exp(log_g[:, None] * (chunk - 1.0 - idx[None, :]))
    c_dec = jnp.exp(log_g * chunk)
    wide = lambda x: jnp.broadcast_to(x[..., None], x.shape + (V7X_LANES,))
    return dmat, wide(q_dec), wide(k_dec), wide(c_dec)


def kernel(x_prompt, x_sample, mem_prompt, state_ret, state_ssm, state_conv, cache_mem_k, cache_mem_v, ffn1_norm,
           ffn1_w1, ffn1_w3, ffn1_w2, mix_norm, w_in, ret_gn_gain, conv_w, conv_b, dt_bias, A_log, D_skip, ssd_norm,
           w_out, x_norm, mem_norm, w_xq, w_xk, w_xv, w_xo, ffn2_norm, ffn2_w1, ffn2_w3, ffn2_w2, final_norm):
    bp, lp, d = x_prompt.shape
    bsz = x_sample.shape[0]
    depth = ffn1_w1.shape[0]
    row = lambda v: v.reshape(1, -1).astype(F32)
    lane_pad = lambda v: jnp.pad(row(v), ((0, 0), (0, DT_PAD - v.shape[-1])))

    cos_p, sin_p = _rope_tables(jnp.arange(lp))
    cos_s, sin_s = _rope_tables(PAST_LEN + jnp.arange(x_sample.shape[1]))
    dmat, q_dec, k_dec, c_dec = _retention_decay_tables(CHUNK)
    gamma1 = _retention_decay_tables(1)[3]

    y_p = x_prompt.reshape(bp * lp, d)
    y_s = x_sample.reshape(bsz, d)
    outs = {k: [] for k in ("ret_p", "ssm_p", "conv_p", "memk", "memv", "ret_s", "ssm_s", "conv_s")}
    for l in range(depth):
        bf = lambda w: w[l].astype(BF16)
        w_in_l = jnp.pad(w_in[l], ((0, 0), (0, IN_PROJ_PAD - IN_PROJ_WIDTH))).astype(BF16)
        f1 = (row(ffn1_norm[l]), bf(ffn1_w1), bf(ffn1_w3), bf(ffn1_w2))
        f2 = (row(ffn2_norm[l]), bf(ffn2_w1), bf(ffn2_w3), bf(ffn2_w2))
        mixer = (row(ret_gn_gain[l]), conv_w[l], row(conv_b[l]), lane_pad(dt_bias[l]), lane_pad(A_log[l]),
                 row(jnp.repeat(D_skip[l], SSD_HEAD_DIM)), row(ssd_norm[l]))
        w_out_l, w_xq_l, w_xo_l = bf(w_out), bf(w_xq), bf(w_xo)

        mk, mv = _linear(mem_prompt.reshape(bp * MEM_TOKENS, d), [bf(w_xk), bf(w_xv)], gain=row(mem_norm[l]),
                         tm=MEM_TOKENS)
        mk = mk.reshape(bp, MEM_TOKENS, d)
        mv = mv.reshape(bp, MEM_TOKENS, d)

        y_p = _ffn(y_p, *f1, tm=512)
        y_p, ret_p, ssm_p, conv_p = _mix_prompt(y_p, row(mix_norm[l]), w_in_l, cos_p, sin_p, dmat, q_dec, k_dec,
                                                c_dec, *mixer, w_out_l, batch=bp, seq=lp, tm=512)
        y_p = _xattn_prompt(y_p, row(x_norm[l]), w_xq_l, mk, mv, w_xo_l, batch=bp, seq=lp, tm=512)
        y_p = _ffn(y_p, *f2, final_gain=row(final_norm) if l == depth - 1 else None, tm=512)

        y_s = _ffn(y_s, *f1, tm=bsz)
        proj_s, = _linear(y_s, [w_in_l], gain=row(mix_norm[l]), tm=bsz)
        mix_s, ret_s, ssm_s, conv_s = _mix_sample(
            proj_s, state_ret[l], state_ssm[l], state_conv[l].reshape(bsz, (CONV_WIDTH - 1) * CONV_CH),
            cos_s, sin_s, gamma1, *mixer, bs=8)
        y_s, = _linear(mix_s, [w_out_l], res=y_s, tm=bsz)
        q_s, = _linear(y_s, [w_xq_l], gain=row(x_norm[l]), tm=bsz)
        att_s = _xattn_sample(q_s, cache_mem_k[l].reshape(bsz, MEM_TOKENS, d),
                              cache_mem_v[l].reshape(bsz, MEM_TOKENS, d), bs=8)
        y_s, = _linear(att_s, [w_xo_l], res=y_s, tm=bsz)
        y_s = _ffn(y_s, *f2, final_gain=row(final_norm) if l == depth - 1 else None, tm=bsz)

        outs["ret_p"].append(ret_p)
        outs["ssm_p"].append(ssm_p.reshape(bp, SSD_HEADS, SSD_HEAD_DIM, SSD_STATE))
        outs["conv_p"].append(conv_p)
        outs["memk"].append(mk.reshape(bp, MEM_TOKENS, X_HEADS, X_HEAD_DIM))
        outs["memv"].append(mv.reshape(bp, MEM_TOKENS, X_HEADS, X_HEAD_DIM))
        outs["ret_s"].append(ret_s)
        outs["ssm_s"].append(ssm_s)
        outs["conv_s"].append(conv_s.reshape(bsz, CONV_WIDTH - 1, CONV_CH))

    st = lambda k: jnp.stack(outs[k])
    return (y_p.reshape(bp, lp, d), y_s.reshape(bsz, x_sample.shape[1], d), st("ret_p"), st("ssm_p"), st("conv_p"),
            st("memk"), st("memv"), st("ret_s"), st("ssm_s"), st("conv_s"))
```

```python
import functools

import jax
import jax.numpy as jnp
from jax import lax
from jax.experimental import pallas as pl
from jax.experimental.pallas import tpu as pltpu

F32 = jnp.float32
BF16 = jnp.bfloat16

D_MODEL = 1024
D_FF = 2816
PAST_LEN = 16384
RET_HEADS = 4
RET_DK = 128
RET_DV = 128
RET_WIDTH = RET_HEADS * RET_DV
SSD_HEADS = 8
SSD_HEAD_DIM = 64
SSD_WIDTH = SSD_HEADS * SSD_HEAD_DIM
SSD_GROUPS = 2
SSD_STATE = 128
SSD_PAIRS = SSD_HEADS // 2
CONV_WIDTH = 4
CONV_CH = SSD_WIDTH + 2 * SSD_GROUPS * SSD_STATE
CHUNK = 128
MEM_TOKENS = 256
X_HEADS = 4
X_HEAD_DIM = D_MODEL // X_HEADS
ROPE_BASE = 10000.0
EPS = 1e-6

Q_OFF = 0
K_OFF = Q_OFF + RET_HEADS * RET_DK
V_OFF = K_OFF + RET_HEADS * RET_DK
G_OFF = V_OFF + RET_WIDTH
Z_OFF = G_OFF + RET_WIDTH
XBC_OFF = Z_OFF + SSD_WIDTH
DT_OFF = XBC_OFF + CONV_CH
IN_PROJ_WIDTH = DT_OFF + SSD_HEADS

V7X_LANES = 128
V7X_SUBLANES = 8
V7X_VMEM_LIMIT_BYTES = 56 * 1024 * 1024
DT_PAD = V7X_LANES
IN_PROJ_PAD = DT_OFF + DT_PAD
CONV_HIST = V7X_SUBLANES


def _params(sem):
    return pltpu.CompilerParams(dimension_semantics=sem, vmem_limit_bytes=V7X_VMEM_LIMIT_BYTES)


def _resident(shape):
    zeros = (0,) * len(shape)
    return pl.BlockSpec(shape, lambda *_: zeros, pipeline_mode=pl.Buffered(1))


def _rmsnorm(x, gain):
    ms = jnp.mean(x * x, axis=-1, keepdims=True)
    return x * lax.rsqrt(ms + EPS) * gain


def _dot(a, b):
    return jnp.dot(a, b, preferred_element_type=F32)


def _dot_nt(a, b):
    return lax.dot_general(a, b, (((1,), (1,)), ((), ())), preferred_element_type=F32)


def _dot_tn(a, b):
    return lax.dot_general(a, b, (((0,), (0,)), ((), ())), preferred_element_type=F32)


def _softplus(x):
    return jnp.maximum(x, 0.0) + jnp.log1p(jnp.exp(-jnp.abs(x)))


def _rotary(x, cos2, sin2):
    return x * cos2 + pltpu.roll(x, RET_DK // 2, axis=1) * sin2


def _group_norm(y):
    mu = jnp.mean(y, axis=-1, keepdims=True)
    d = y - mu
    var = jnp.mean(d * d, axis=-1, keepdims=True)
    return d * lax.rsqrt(var + EPS)


def _ffn_kernel(*refs, final_norm):
    if final_norm:
        x_ref, g_ref, w1_ref, w3_ref, w2_ref, fg_ref, o_ref = refs
    else:
        x_ref, g_ref, w1_ref, w3_ref, w2_ref, o_ref = refs
    x = x_ref[...]
    xn = _rmsnorm(x, g_ref[...]).astype(BF16)
    a = _dot(xn, w1_ref[...])
    b = _dot(xn, w3_ref[...])
    hidden = (jax.nn.silu(a) * b).astype(BF16)
    out = x + 0.5 * _dot(hidden, w2_ref[...])
    if final_norm:
        out = _rmsnorm(out, fg_ref[...])
    o_ref[...] = out


def _ffn(x, gain, w1, w3, w2, final_gain=None, *, tm):
    t, d = x.shape
    row = pl.BlockSpec((tm, d), lambda i: (i, 0))
    ins = [x, gain, w1, w3, w2]
    specs = [row, _resident(gain.shape), _resident(w1.shape), _resident(w3.shape), _resident(w2.shape)]
    if final_gain is not None:
        ins.append(final_gain)
        specs.append(_resident(final_gain.shape))
    return pl.pallas_call(
        functools.partial(_ffn_kernel, final_norm=final_gain is not None),
        out_shape=jax.ShapeDtypeStruct((t, d), F32),
        grid=(t // tm,),
        in_specs=specs,
        out_specs=row,
        compiler_params=_params(("arbitrary",)),
        name="ffn",
    )(*ins)


def _linear_kernel(*refs, has_norm, has_res, n_w):
    refs = list(refs)
    x_ref = refs.pop(0)
    g_ref = refs.pop(0) if has_norm else None
    r_ref = refs.pop(0) if has_res else None
    w_refs, o_refs = refs[:n_w], refs[n_w:]
    x = x_ref[...]
    if has_norm:
        x = _rmsnorm(x, g_ref[...])
    xb = x.astype(BF16)
    for w_ref, o_ref in zip(w_refs, o_refs, strict=True):
        y = _dot(xb, w_ref[...])
        if has_res:
            y = r_ref[...] + y
        o_ref[...] = y


def _linear(x, weights, gain=None, res=None, *, tm):
    t, k = x.shape
    ins, specs = [x], [pl.BlockSpec((tm, k), lambda i: (i, 0))]
    if gain is not None:
        ins.append(gain)
        specs.append(_resident(gain.shape))
    if res is not None:
        ins.append(res)
        specs.append(pl.BlockSpec((tm, res.shape[1]), lambda i: (i, 0)))
    for w in weights:
        ins.append(w)
        specs.append(_resident(w.shape))
    outs = pl.pallas_call(
        functools.partial(_linear_kernel, has_norm=gain is not None, has_res=res is not None, n_w=len(weights)),
        out_shape=[jax.ShapeDtypeStruct((t, w.shape[1]), F32) for w in weights],
        grid=(t // tm,),
        in_specs=specs,
        out_specs=[pl.BlockSpec((tm, w.shape[1]), lambda i: (i, 0)) for w in weights],
        compiler_params=_params(("arbitrary",)),
        name="linear",
    )(*ins)
    return outs


def _cumsum_rows(x):
    n = x.shape[0]
    row = lax.broadcasted_iota(jnp.int32, x.shape, 0)
    step = 1
    while step < n:
        x = x + jnp.where(row >= step, pltpu.roll(x, step, axis=0), 0.0)
        step *= 2
    return x


def _mix_prompt_kernel(h_ref, g_ref, win_ref, cos_ref, sin_ref, dmat_ref, qdec_ref, kdec_ref, cdec_ref,
                       gn_ref, cw_ref, cb_ref, dtb_ref, alog_ref, dsk_ref, sn_ref, wout_ref,
                       o_ref, ret_ref, ssm_ref, conv_ref,
                       proj_s, xbc_s, xc_s, dt_s, a_s, mix_s, sret_s, sssm_s, *, tm, nt):
    t = pl.program_id(1)

    @pl.when(t == 0)
    def _():
        sret_s[...] = jnp.zeros_like(sret_s)
        sssm_s[...] = jnp.zeros_like(sssm_s)
        xbc_s[0:CONV_HIST, :] = jnp.zeros((CONV_HIST, CONV_CH), F32)

    h = h_ref[...]
    u = _rmsnorm(h, g_ref[...]).astype(BF16)
    proj_s[...] = _dot(u, win_ref[:, 0:XBC_OFF])
    xbc_s[CONV_HIST:CONV_HIST + tm, :] = _dot(u, win_ref[:, XBC_OFF:DT_OFF])
    dt_raw = _dot(u, win_ref[:, DT_OFF:IN_PROJ_PAD])

    conv = cb_ref[...]
    for i in range(CONV_WIDTH):
        off = CONV_HIST - (CONV_WIDTH - 1) + i
        conv = conv + cw_ref[i:i + 1, :] * xbc_s[off:off + tm, :]
    xc_s[...] = jax.nn.silu(conv)

    @pl.when(t == nt - 1)
    def _():
        conv_ref[0] = xbc_s[CONV_HIST + tm - (CONV_WIDTH - 1):CONV_HIST + tm, :]

    xbc_s[0:CONV_HIST, :] = xbc_s[tm:tm + CONV_HIST, :]

    dt = _softplus(dt_raw + dtb_ref[...])
    dt_s[...] = dt
    a_s[...] = dt * (-jnp.exp(alog_ref[...]))

    row_i = lax.broadcasted_iota(jnp.int32, (CHUNK, CHUNK), 0)
    col_i = lax.broadcasted_iota(jnp.int32, (CHUNK, CHUNK), 1)
    causal = row_i >= col_i
    low_lanes = col_i < SSD_HEAD_DIM
    bd_rows = lax.broadcasted_iota(jnp.int32, (2 * SSD_STATE, 2 * SSD_HEAD_DIM), 0)
    bd_cols = lax.broadcasted_iota(jnp.int32, (2 * SSD_STATE, 2 * SSD_HEAD_DIM), 1)
    bd_mask = (bd_rows < SSD_STATE) == (bd_cols < SSD_HEAD_DIM)

    def chunk_body(c, carry):
        r0 = pl.multiple_of(c * CHUNK, CHUNK)
        rows = pl.ds(r0, CHUNK)
        cos2 = cos_ref[rows, :]
        sin2 = sin_ref[rows, :]

        for hh in range(RET_HEADS):
            lanes = slice(hh * RET_DK, (hh + 1) * RET_DK)
            qr = _rotary(proj_s[rows, Q_OFF + hh * RET_DK:Q_OFF + (hh + 1) * RET_DK], cos2, sin2)
            kr = _rotary(proj_s[rows, K_OFF + hh * RET_DK:K_OFF + (hh + 1) * RET_DK], cos2, sin2) * (RET_DK ** -0.5)
            vh = proj_s[rows, V_OFF + hh * RET_DV:V_OFF + (hh + 1) * RET_DV]
            gh = proj_s[rows, G_OFF + hh * RET_DV:G_OFF + (hh + 1) * RET_DV]
            inner = _dot_nt(qr.astype(BF16), kr.astype(BF16)) * dmat_ref[hh]
            s_prev = sret_s[hh]
            lhs = jnp.concatenate([inner, qr * qdec_ref[hh]], axis=1).astype(BF16)
            rhs = jnp.concatenate([vh, s_prev], axis=0).astype(BF16)
            y = _dot(lhs, rhs)
            kv = _dot_tn((kr * kdec_ref[hh]).astype(BF16), vh.astype(BF16))
            sret_s[hh] = cdec_ref[hh:hh + 1, :] * s_prev + kv
            r = _group_norm(y) * gn_ref[:, lanes]
            mix_s[rows, lanes] = (jax.nn.silu(gh) * r).astype(BF16)

        a_cum = _cumsum_rows(a_s[rows, :])
        a_cum_t = a_cum.T
        a_last = a_cum[CHUNK - 1:CHUNK, :]
        e_cum = jnp.exp(a_cum)
        d_cum = jnp.exp(a_last - a_cum)
        c_dec = jnp.exp(a_last)
        dt_c = dt_s[rows, :]
        ys = []
        for grp in range(SSD_GROUPS):
            b_off = SSD_WIDTH + grp * SSD_STATE
            c_off = SSD_WIDTH + SSD_GROUPS * SSD_STATE + grp * SSD_STATE
            bg = xc_s[rows, b_off:b_off + SSD_STATE]
            cg = xc_s[rows, c_off:c_off + SSD_STATE]
            gmat = _dot_nt(cg.astype(BF16), bg.astype(BF16))
            for pj in range(SSD_PAIRS // SSD_GROUPS):
                j = grp * (SSD_PAIRS // SSD_GROUPS) + pj
                x_pair = xc_s[rows, j * 2 * SSD_HEAD_DIM:(j + 1) * 2 * SSD_HEAD_DIM]
                heads = (2 * j, 2 * j + 1)
                xdt = x_pair * jnp.where(low_lanes, dt_c[:, heads[0]:heads[0] + 1], dt_c[:, heads[1]:heads[1] + 1])
                lhs_parts, bd_parts, dec_parts = [], [], []
                for hd in heads:
                    seg = a_cum[:, hd:hd + 1] - a_cum_t[hd:hd + 1, :]
                    lmat = jnp.where(causal, jnp.exp(jnp.where(causal, seg, 0.0)), 0.0)
                    lhs_parts.append(gmat * lmat)
                for hd in heads:
                    lhs_parts.append(cg * e_cum[:, hd:hd + 1])
                    bd_parts.append(bg * d_cum[:, hd:hd + 1])
                    dec_parts.append(jnp.broadcast_to(c_dec[:, hd:hd + 1], (SSD_STATE, 2 * SSD_HEAD_DIM)))
                st_prev = sssm_s[j]
                x_bd = jnp.concatenate([jnp.where(low_lanes, xdt, 0.0), jnp.where(low_lanes, 0.0, xdt)], axis=0)
                lhs = jnp.concatenate(lhs_parts, axis=1).astype(BF16)
                rhs = jnp.concatenate([x_bd, st_prev], axis=0).astype(BF16)
                y = _dot(lhs, rhs)
                st = _dot_tn(jnp.concatenate(bd_parts, axis=1).astype(BF16), xdt.astype(BF16))
                sssm_s[j] = jnp.concatenate(dec_parts, axis=0) * st_prev + jnp.where(bd_mask, st, 0.0)
                ys.append(y + dsk_ref[:, j * 2 * SSD_HEAD_DIM:(j + 1) * 2 * SSD_HEAD_DIM] * x_pair)
        y_all = jnp.concatenate(ys, axis=1) * jax.nn.silu(proj_s[rows, Z_OFF:Z_OFF + SSD_WIDTH])
        mix_s[rows, RET_WIDTH:RET_WIDTH + SSD_WIDTH] = _rmsnorm(y_all, sn_ref[...]).astype(BF16)
        return carry

    lax.fori_loop(0, tm // CHUNK, chunk_body, 0, unroll=True)

    o_ref[...] = h + _dot(mix_s[...], wout_ref[...])

    @pl.when(t == nt - 1)
    def _():
        ret_ref[0] = sret_s[...]
        for j in range(SSD_PAIRS):
            st = sssm_s[j]
            ssm_ref[0, j * 2 * SSD_HEAD_DIM:(j + 1) * 2 * SSD_HEAD_DIM, :] = (st[0:SSD_STATE] + st[SSD_STATE:]).T


def _mix_prompt(h, gain, w_in, cos2, sin2, dmat, qdec, kdec, cdec, gn, conv_w, conv_b, dt_bias, a_log, dskip,
                ssd_gain, w_out, *, batch, seq, tm):
    nt = seq // tm
    d = h.shape[1]
    row = pl.BlockSpec((tm, d), lambda b, t: (b * nt + t, 0))
    pos = pl.BlockSpec((tm, RET_DK), lambda b, t: (t, 0))
    consts = [gain, w_in]
    tail = [dmat, qdec, kdec, cdec, gn, conv_w, conv_b, dt_bias, a_log, dskip, ssd_gain, w_out]
    return pl.pallas_call(
        functools.partial(_mix_prompt_kernel, tm=tm, nt=nt),
        out_shape=[
            jax.ShapeDtypeStruct(h.shape, F32),
            jax.ShapeDtypeStruct((batch, RET_HEADS, RET_DK, RET_DV), F32),
            jax.ShapeDtypeStruct((batch, SSD_WIDTH, SSD_STATE), F32),
            jax.ShapeDtypeStruct((batch, CONV_WIDTH - 1, CONV_CH), F32),
        ],
        grid=(batch, nt),
        in_specs=[row] + [_resident(a.shape) for a in consts] + [pos, pos] + [_resident(a.shape) for a in tail],
        out_specs=[
            row,
            pl.BlockSpec((1, RET_HEADS, RET_DK, RET_DV), lambda b, t: (b, 0, 0, 0)),
            pl.BlockSpec((1, SSD_WIDTH, SSD_STATE), lambda b, t: (b, 0, 0)),
            pl.BlockSpec((1, CONV_WIDTH - 1, CONV_CH), lambda b, t: (b, 0, 0)),
        ],
        scratch_shapes=[
            pltpu.VMEM((tm, XBC_OFF), F32),
            pltpu.VMEM((CONV_HIST + tm, CONV_CH), F32),
            pltpu.VMEM((tm, CONV_CH), F32),
            pltpu.VMEM((tm, DT_PAD), F32),
            pltpu.VMEM((tm, DT_PAD), F32),
            pltpu.VMEM((tm, RET_WIDTH + SSD_WIDTH), BF16),
            pltpu.VMEM((RET_HEADS, RET_DK, RET_DV), F32),
            pltpu.VMEM((SSD_PAIRS, 2 * SSD_STATE, 2 * SSD_HEAD_DIM), F32),
        ],
        compiler_params=_params(("arbitrary", "arbitrary")),
        name="mix_prompt",
    )(h, *consts, cos2, sin2, *tail)


def _pad_rows_t(x):
    n, w = x.shape
    return jnp.concatenate([x, jnp.zeros((V7X_LANES - n, w), x.dtype)], axis=0).T


def _mix_sample_kernel(proj_ref, sret_ref, sssm_ref, sconv_ref, cos_ref, sin_ref, gam_ref,
                       gn_ref, cw_ref, cb_ref, dtb_ref, alog_ref, dsk_ref, sn_ref,
                       mix_ref, oret_ref, ossm_ref, oconv_ref, *, bs):
    proj = proj_ref[...]
    xbc = proj[:, XBC_OFF:DT_OFF]
    hist = sconv_ref[...]
    taps = [hist[:, i * CONV_CH:(i + 1) * CONV_CH] for i in range(CONV_WIDTH - 1)] + [xbc]
    conv = cb_ref[...]
    for i in range(CONV_WIDTH):
        conv = conv + cw_ref[i:i + 1, :] * taps[i]
    oconv_ref[...] = jnp.concatenate(taps[1:], axis=1)
    xc = jax.nn.silu(conv)

    dt = _softplus(proj[:, DT_OFF:IN_PROJ_PAD] + dtb_ref[...])
    dec = jnp.exp(dt * (-jnp.exp(alog_ref[...])))

    sub_i = lax.broadcasted_iota(jnp.int32, (bs, V7X_LANES), 0)
    lane_i = lax.broadcasted_iota(jnp.int32, (SSD_HEAD_DIM, V7X_LANES), 1)
    cos2 = cos_ref[...]
    sin2 = sin_ref[...]

    for hh in range(RET_HEADS):
        lanes = slice(hh * RET_DK, (hh + 1) * RET_DK)
        qr = _rotary(proj[:, Q_OFF + hh * RET_DK:Q_OFF + (hh + 1) * RET_DK], cos2, sin2)
        kr = _rotary(proj[:, K_OFF + hh * RET_DK:K_OFF + (hh + 1) * RET_DK], cos2, sin2) * (RET_DK ** -0.5)
        vh = proj[:, V_OFF + hh * RET_DV:V_OFF + (hh + 1) * RET_DV]
        q_t = _pad_rows_t(qr)
        k_t = _pad_rows_t(kr)
        gamma = gam_ref[hh:hh + 1, :]
        y = jnp.zeros((bs, RET_DV), F32)
        for b in range(bs):
            s_new = gamma * sret_ref[b, hh] + k_t[:, b:b + 1] * vh[b:b + 1, :]
            oret_ref[b, hh] = s_new
            y_row = jnp.sum(q_t[:, b:b + 1] * s_new, axis=0, keepdims=True)
            y = jnp.where(sub_i == b, y_row, y)
        r = _group_norm(y) * gn_ref[:, lanes]
        mix_ref[:, lanes] = jax.nn.silu(proj[:, G_OFF + hh * RET_DV:G_OFF + (hh + 1) * RET_DV]) * r

    xs = xc[:, 0:SSD_WIDTH]
    head_of_lane = lax.broadcasted_iota(jnp.int32, (bs, SSD_WIDTH), 1) // SSD_HEAD_DIM
    dt_wide = jnp.zeros((bs, SSD_WIDTH), F32)
    for hd in range(SSD_HEADS):
        dt_wide = jnp.where(head_of_lane == hd, dt[:, hd:hd + 1], dt_wide)
    xdt_t = _pad_rows_t(xs * dt_wide)
    y_t = []
    for hd in range(SSD_HEADS):
        grp = hd // (SSD_HEADS // SSD_GROUPS)
        b_off = SSD_WIDTH + grp * SSD_STATE
        c_off = SSD_WIDTH + SSD_GROUPS * SSD_STATE + grp * SSD_STATE
        y_cols = jnp.zeros((SSD_HEAD_DIM, V7X_LANES), F32)
        for b in range(bs):
            s_new = (dec[b:b + 1, hd:hd + 1] * sssm_ref[b, hd]
                     + xdt_t[hd * SSD_HEAD_DIM:(hd + 1) * SSD_HEAD_DIM, b:b + 1] * xc[b:b + 1, b_off:b_off + SSD_STATE])
            ossm_ref[b, hd] = s_new
            y_col = jnp.sum(s_new * xc[b:b + 1, c_off:c_off + SSD_STATE], axis=1, keepdims=True)
            y_cols = jnp.where(lane_i == b, y_col, y_cols)
        y_t.append(y_cols)
    ys = jnp.concatenate(y_t, axis=0).T[0:bs, :]
    ys = (ys + dsk_ref[...] * xs) * jax.nn.silu(proj[:, Z_OFF:Z_OFF + SSD_WIDTH])
    mix_ref[:, RET_WIDTH:RET_WIDTH + SSD_WIDTH] = _rmsnorm(ys, sn_ref[...])


def _mix_sample(proj, s_ret, s_ssm, s_conv, cos2, sin2, gamma, gn, conv_w, conv_b, dt_bias, a_log, dskip, ssd_gain,
                *, bs):
    n = proj.shape[0]
    consts = [cos2, sin2, gamma, gn, conv_w, conv_b, dt_bias, a_log, dskip, ssd_gain]
    blk2 = lambda w: pl.BlockSpec((bs, w), lambda i: (i, 0))
    ret_blk = pl.BlockSpec((bs, RET_HEADS, RET_DK, RET_DV), lambda i: (i, 0, 0, 0))
    ssm_blk = pl.BlockSpec((bs, SSD_HEADS, SSD_HEAD_DIM, SSD_STATE), lambda i: (i, 0, 0, 0))
    return pl.pallas_call(
        functools.partial(_mix_sample_kernel, bs=bs),
        out_shape=[
            jax.ShapeDtypeStruct((n, RET_WIDTH + SSD_WIDTH), F32),
            jax.ShapeDtypeStruct(s_ret.shape, F32),
            jax.ShapeDtypeStruct(s_ssm.shape, F32),
            jax.ShapeDtypeStruct(s_conv.shape, F32),
        ],
        grid=(n // bs,),
        in_specs=[blk2(proj.shape[1]), ret_blk, ssm_blk, blk2(s_conv.shape[1])] + [_resident(a.shape) for a in consts],
        out_specs=[blk2(RET_WIDTH + SSD_WIDTH), ret_blk, ssm_blk, blk2(s_conv.shape[1])],
        compiler_params=_params(("arbitrary",)),
        name="mix_sample",
    )(proj, s_ret, s_ssm, s_conv, *consts)


def _softmax_rows(s):
    m = jnp.max(s, axis=-1, keepdims=True)
    p = jnp.exp(s - m)
    return p / jnp.sum(p, axis=-1, keepdims=True)


def _xattn_prompt_kernel(h_ref, g_ref, wq_ref, mk_ref, mv_ref, wo_ref, o_ref, att_s):
    h = h_ref[...]
    c = _rmsnorm(h, g_ref[...]).astype(BF16)
    qx = _dot(c, wq_ref[...]).astype(BF16)
    for hh in range(X_HEADS):
        lanes = slice(hh * X_HEAD_DIM, (hh + 1) * X_HEAD_DIM)
        s = _dot_nt(qx[:, lanes], mk_ref[0, :, lanes].astype(BF16)) * (X_HEAD_DIM ** -0.5)
        att = _softmax_rows(s).astype(BF16)
        att_s[:, lanes] = _dot(att, mv_ref[0, :, lanes].astype(BF16)).astype(BF16)
    o_ref[...] = h + _dot(att_s[...], wo_ref[...])


def _xattn_prompt(h, gain, w_q, mem_k, mem_v, w_o, *, batch, seq, tm):
    nt = seq // tm
    d = h.shape[1]
    row = pl.BlockSpec((tm, d), lambda b, t: (b * nt + t, 0))
    mem = pl.BlockSpec((1, MEM_TOKENS, d), lambda b, t: (b, 0, 0))
    return pl.pallas_call(
        _xattn_prompt_kernel,
        out_shape=jax.ShapeDtypeStruct(h.shape, F32),
        grid=(batch, nt),
        in_specs=[row, _resident(gain.shape), _resident(w_q.shape), mem, mem, _resident(w_o.shape)],
        out_specs=row,
        scratch_shapes=[pltpu.VMEM((tm, d), BF16)],
        compiler_params=_params(("arbitrary", "arbitrary")),
        name="xattn_prompt",
    )(h, gain, w_q, mem_k, mem_v, w_o)


def _xattn_sample_kernel(q_ref, k_ref, v_ref, o_ref, *, bs):
    rows = MEM_TOKENS * X_HEADS
    own = (lax.broadcasted_iota(jnp.int32, (X_HEADS, rows), 1) % X_HEADS
           == lax.broadcasted_iota(jnp.int32, (X_HEADS, rows), 0))
    for b in range(bs):
        q = q_ref[b].astype(BF16)
        k = k_ref[b].reshape(rows, X_HEAD_DIM).astype(BF16)
        v = v_ref[b].reshape(rows, X_HEAD_DIM).astype(BF16)
        s = _dot_nt(q, k) * (X_HEAD_DIM ** -0.5)
        att = _softmax_rows(jnp.where(own, s, -jnp.inf)).astype(BF16)
        o_ref[b] = _dot(att, v)


def _xattn_sample(q, mem_k, mem_v, *, bs):
    n = q.shape[0]
    row = pl.BlockSpec((bs, X_HEADS, X_HEAD_DIM), lambda i: (i, 0, 0))
    mem = pl.BlockSpec((bs, MEM_TOKENS, X_HEADS, X_HEAD_DIM), lambda i: (i, 0, 0, 0))
    return pl.pallas_call(
        functools.partial(_xattn_sample_kernel, bs=bs),
        out_shape=jax.ShapeDtypeStruct((n, X_HEADS, X_HEAD_DIM), F32),
        grid=(n // bs,),
        in_specs=[row, mem, mem],
        out_specs=row,
        compiler_params=_params(("arbitrary",)),
        name="xattn_sample",
    )(q, mem_k, mem_v)


def _rope_tables(pos):
    half = RET_DK // 2
    inv_freq = ROPE_BASE ** (-jnp.arange(half, dtype=F32) / half)
    ang = pos.astype(F32)[:, None] * inv_freq[None, :]
    cos, sin = jnp.cos(ang), jnp.sin(ang)
    return jnp.concatenate([cos, cos], axis=-1), jnp.concatenate([-sin, sin], axis=-1)


def _retention_decay_tables(chunk):
    log_g = jnp.log1p(-jnp.exp2(-5.0 - jnp.arange(RET_HEADS, dtype=F32)))
    idx = jnp.arange(chunk, dtype=F32)
    diff = idx[:, None] - idx[None, :]
    causal = diff >= 0
    dmat = jnp.where(causal[None], jnp.exp(log_g[:, None, None] * jnp.where(causal, diff, 0.0)[None]), 0.0)
    q_dec = jnp.exp(log_g[:, None] * (idx[None, :] + 1.0))
    k_dec = jnp.exp(log_g[:, None] * (chunk - 1.0 - idx[None, :]))
    c_dec = jnp.exp(log_g * chunk)
    wide = lambda x: jnp.broadcast_to(x[..., None], x.shape + (V7X_LANES,))
    return dmat, wide(q_dec), wide(k_dec), wide(c_dec)


def kernel(x_prompt, x_sample, mem_prompt, state_ret, state_ssm, state_conv, cache_mem_k, cache_mem_v, ffn1_norm,
           ffn1_w1, ffn1_w3, ffn1_w2, mix_norm, w_in, ret_gn_gain, conv_w, conv_b, dt_bias, A_log, D_skip, ssd_norm,
           w_out, x_norm, mem_norm, w_xq, w_xk, w_xv, w_xo, ffn2_norm, ffn2_w1, ffn2_w3, ffn2_w2, final_norm):
    bp, lp, d = x_prompt.shape
    bsz = x_sample.shape[0]
    depth = ffn1_w1.shape[0]
    row = lambda v: v.reshape(1, -1).astype(F32)
    lane_pad = lambda v: jnp.pad(row(v), ((0, 0), (0, DT_PAD - v.shape[-1])))

    cos_p, sin_p = _rope_tables(jnp.arange(lp))
    cos_s, sin_s = _rope_tables(PAST_LEN + jnp.arange(x_sample.shape[1]))
    dmat, q_dec, k_dec, c_dec = _retention_decay_tables(CHUNK)
    gamma1 = _retention_decay_tables(1)[3]

    y_p = x_prompt.reshape(bp * lp, d)
    y_s = x_sample.reshape(bsz, d)
    outs = {k: [] for k in ("ret_p", "ssm_p", "conv_p", "memk", "memv", "ret_s", "ssm_s", "conv_s")}
    for l in range(depth):
        bf = lambda w: w[l].astype(BF16)
        w_in_l = jnp.pad(w_in[l], ((0, 0), (0, IN_PROJ_PAD - IN_PROJ_WIDTH))).astype(BF16)
        f1 = (row(ffn1_norm[l]), bf(ffn1_w1), bf(ffn1_w3), bf(ffn1_w2))
        f2 = (row(ffn2_norm[l]), bf(ffn2_w1), bf(ffn2_w3), bf(ffn2_w2))
        mixer = (row(ret_gn_gain[l]), conv_w[l], row(conv_b[l]), lane_pad(dt_bias[l]), lane_pad(A_log[l]),
                 row(jnp.repeat(D_skip[l], SSD_HEAD_DIM)), row(ssd_norm[l]))
        w_out_l, w_xq_l, w_xo_l = bf(w_out), bf(w_xq), bf(w_xo)

        mk, mv = _linear(mem_prompt.reshape(bp * MEM_TOKENS, d), [bf(w_xk), bf(w_xv)], gain=row(mem_norm[l]),
                         tm=MEM_TOKENS)
        mk = mk.reshape(bp, MEM_TOKENS, d)
        mv = mv.reshape(bp, MEM_TOKENS, d)

        y_p = _ffn(y_p, *f1, tm=512)
        y_p, ret_p, ssm_p, conv_p = _mix_prompt(y_p, row(mix_norm[l]), w_in_l, cos_p, sin_p, dmat, q_dec, k_dec,
                                                c_dec, *mixer, w_out_l, batch=bp, seq=lp, tm=512)
        y_p = _xattn_prompt(y_p, row(x_norm[l]), w_xq_l, mk, mv, w_xo_l, batch=bp, seq=lp, tm=512)
        y_p = _ffn(y_p, *f2, final_gain=row(final_norm) if l == depth - 1 else None, tm=512)

        y_s = _ffn(y_s, *f1, tm=bsz)
        proj_s, = _linear(y_s, [w_in_l], gain=row(mix_norm[l]), tm=bsz)
        mix_s, ret_s, ssm_s, conv_s = _mix_sample(
            proj_s, state_ret[l], state_ssm[l], state_conv[l].reshape(bsz, (CONV_WIDTH - 1) * CONV_CH),
            cos_s, sin_s, gamma1, *mixer, bs=8)
        y_s, = _linear(mix_s, [w_out_l], res=y_s, tm=bsz)
        q_s, = _linear(y_s, [w_xq_l], gain=row(x_norm[l]), tm=bsz)
        att_s = _xattn_sample(q_s.reshape(bsz, X_HEADS, X_HEAD_DIM), cache_mem_k[l], cache_mem_v[l], bs=8)
        y_s, = _linear(att_s.reshape(bsz, d), [w_xo_l], res=y_s, tm=bsz)
        y_s = _ffn(y_s, *f2, final_gain=row(final_norm) if l == depth - 1 else None, tm=bsz)

        outs["ret_p"].append(ret_p)
        outs["ssm_p"].append(ssm_p.reshape(bp, SSD_HEADS, SSD_HEAD_DIM, SSD_STATE))
        outs["conv_p"].append(conv_p)
        outs["memk"].append(mk.reshape(bp, MEM_TOKENS, X_HEADS, X_HEAD_DIM))
        outs["memv"].append(mv.reshape(bp, MEM_TOKENS, X_HEADS, X_HEAD_DIM))
        outs["ret_s"].append(ret_s)
        outs["ssm_s"].append(ssm_s)
        outs["conv_s"].append(conv_s.reshape(bsz, CONV_WIDTH - 1, CONV_CH))

    st = lambda k: jnp.stack(outs[k])
    return (y_p.reshape(bp, lp, d), y_s.reshape(bsz, x_sample.shape[1], d), st("ret_p"), st("ssm_p"), st("conv_p"),
            st("memk"), st("memv"), st("ret_s"), st("ssm_s"), st("conv_s"))
```

```python
import functools

import jax
import jax.numpy as jnp
from jax import lax
from jax.experimental import pallas as pl
from jax.experimental.pallas import tpu as pltpu

F32 = jnp.float32
BF16 = jnp.bfloat16

D_MODEL = 1024
D_FF = 2816
PAST_LEN = 16384
RET_HEADS = 4
RET_DK = 128
RET_DV = 128
RET_WIDTH = RET_HEADS * RET_DV
SSD_HEADS = 8
SSD_HEAD_DIM = 64
SSD_WIDTH = SSD_HEADS * SSD_HEAD_DIM
SSD_GROUPS = 2
SSD_STATE = 128
SSD_PAIRS = SSD_HEADS // 2
CONV_WIDTH = 4
CONV_CH = SSD_WIDTH + 2 * SSD_GROUPS * SSD_STATE
CHUNK = 128
MEM_TOKENS = 256
X_HEADS = 4
X_HEAD_DIM = D_MODEL // X_HEADS
ROPE_BASE = 10000.0
EPS = 1e-6

Q_OFF = 0
K_OFF = Q_OFF + RET_HEADS * RET_DK
V_OFF = K_OFF + RET_HEADS * RET_DK
G_OFF = V_OFF + RET_WIDTH
Z_OFF = G_OFF + RET_WIDTH
XBC_OFF = Z_OFF + SSD_WIDTH
DT_OFF = XBC_OFF + CONV_CH
IN_PROJ_WIDTH = DT_OFF + SSD_HEADS

PQ_OFF = 0
PV_OFF = PQ_OFF + RET_HEADS * RET_DK
PG_OFF = PV_OFF + RET_WIDTH
PZ_OFF = PG_OFF + RET_WIDTH
P_WIDTH = PZ_OFF + SSD_WIDTH

V7X_LANES = 128
V7X_SUBLANES = 8
V7X_VMEM_LIMIT_BYTES = 56 * 1024 * 1024
DT_PAD = V7X_LANES
IN_PROJ_PAD = DT_OFF + DT_PAD
CONV_HIST = V7X_SUBLANES
CONV_SLABS = CONV_CH // V7X_LANES


def _params(sem):
    return pltpu.CompilerParams(dimension_semantics=sem, vmem_limit_bytes=V7X_VMEM_LIMIT_BYTES)


def _resident(shape):
    zeros = (0,) * len(shape)
    return pl.BlockSpec(shape, lambda *_: zeros, pipeline_mode=pl.Buffered(1))


def _rmsnorm(x, gain):
    ms = jnp.mean(x * x, axis=-1, keepdims=True)
    return x * lax.rsqrt(ms + EPS) * gain


def _dot(a, b):
    return jnp.dot(a, b, preferred_element_type=F32)


def _dot_nt(a, b):
    return lax.dot_general(a, b, (((1,), (1,)), ((), ())), preferred_element_type=F32)


def _softplus(x):
    return jnp.maximum(x, 0.0) + jnp.log1p(jnp.exp(-jnp.abs(x)))


def _rotary(x, cos2, sin2):
    return x * cos2 + pltpu.roll(x, RET_DK // 2, axis=1) * sin2


def _group_norm(y):
    mu = jnp.mean(y, axis=-1, keepdims=True)
    d = y - mu
    var = jnp.mean(d * d, axis=-1, keepdims=True)
    return d * lax.rsqrt(var + EPS)


def _ffn_kernel(*refs, final_norm):
    if final_norm:
        x_ref, g_ref, w1_ref, w3_ref, w2_ref, fg_ref, o_ref = refs
    else:
        x_ref, g_ref, w1_ref, w3_ref, w2_ref, o_ref = refs
    x = x_ref[...]
    xn = _rmsnorm(x, g_ref[...]).astype(BF16)
    a = _dot(xn, w1_ref[...])
    b = _dot(xn, w3_ref[...])
    hidden = (jax.nn.silu(a) * b).astype(BF16)
    out = x + 0.5 * _dot(hidden, w2_ref[...])
    if final_norm:
        out = _rmsnorm(out, fg_ref[...])
    o_ref[...] = out


def _ffn(x, gain, w1, w3, w2, final_gain=None, *, tm):
    t, d = x.shape
    row = pl.BlockSpec((tm, d), lambda i: (i, 0))
    ins = [x, gain, w1, w3, w2]
    specs = [row, _resident(gain.shape), _resident(w1.shape), _resident(w3.shape), _resident(w2.shape)]
    if final_gain is not None:
        ins.append(final_gain)
        specs.append(_resident(final_gain.shape))
    return pl.pallas_call(
        functools.partial(_ffn_kernel, final_norm=final_gain is not None),
        out_shape=jax.ShapeDtypeStruct((t, d), F32),
        grid=(t // tm,),
        in_specs=specs,
        out_specs=row,
        compiler_params=_params(("arbitrary",)),
        name="ffn",
    )(*ins)


def _linear_kernel(*refs, has_norm, has_res, n_w):
    refs = list(refs)
    x_ref = refs.pop(0)
    g_ref = refs.pop(0) if has_norm else None
    r_ref = refs.pop(0) if has_res else None
    w_refs, o_refs = refs[:n_w], refs[n_w:]
    x = x_ref[...]
    if has_norm:
        x = _rmsnorm(x, g_ref[...])
    xb = x.astype(BF16)
    for w_ref, o_ref in zip(w_refs, o_refs, strict=True):
        y = _dot(xb, w_ref[...])
        if has_res:
            y = r_ref[...] + y
        o_ref[...] = y


def _linear(x, weights, gain=None, res=None, *, tm):
    t, k = x.shape
    ins, specs = [x], [pl.BlockSpec((tm, k), lambda i: (i, 0))]
    if gain is not None:
        ins.append(gain)
        specs.append(_resident(gain.shape))
    if res is not None:
        ins.append(res)
        specs.append(pl.BlockSpec((tm, res.shape[1]), lambda i: (i, 0)))
    for w in weights:
        ins.append(w)
        specs.append(_resident(w.shape))
    outs = pl.pallas_call(
        functools.partial(_linear_kernel, has_norm=gain is not None, has_res=res is not None, n_w=len(weights)),
        out_shape=[jax.ShapeDtypeStruct((t, w.shape[1]), F32) for w in weights],
        grid=(t // tm,),
        in_specs=specs,
        out_specs=[pl.BlockSpec((tm, w.shape[1]), lambda i: (i, 0)) for w in weights],
        compiler_params=_params(("arbitrary",)),
        name="linear",
    )(*ins)
    return outs


def _cumsum_chunks(x):
    pos = lax.broadcasted_iota(jnp.int32, x.shape, 0) % CHUNK
    step = 1
    while step < CHUNK:
        x = x + jnp.where(pos >= step, pltpu.roll(x, step, axis=0), 0.0)
        step *= 2
    return x


def _mix_prompt_kernel(h_ref, g_ref, wa_ref, wkt_ref, wxbc_ref, wdt_ref, cos_ref, sin_ref, cost_ref, sint_ref,
                       dmat_ref, qdec_ref, kdec_ref, cdec_ref, gn_ref, cw_ref, cb_ref, dtb_ref, alog_ref, dsk_ref,
                       sn_ref, wout_ref,
                       o_ref, ret_ref, ssm_ref, conv_ref,
                       xbc_s, mix_s, sret_s, sssm_s, *, tm, nt):
    nc = tm // CHUNK
    t = pl.program_id(1)

    @pl.when(t == 0)
    def _():
        sret_s[...] = jnp.zeros_like(sret_s)
        sssm_s[...] = jnp.zeros_like(sssm_s)
        xbc_s[:, 0:CONV_HIST, :] = jnp.zeros((CONV_SLABS, CONV_HIST, V7X_LANES), F32)

    h = h_ref[...]
    u = _rmsnorm(h, g_ref[...]).astype(BF16)
    qvgz = _dot(u, wa_ref[...])
    kt = _dot_nt(wkt_ref[...], u)
    xbc = _dot(u, wxbc_ref[...])
    dt_raw = _dot(u, wdt_ref[...])

    xc = []
    for sl in range(CONV_SLABS):
        lanes = slice(sl * V7X_LANES, (sl + 1) * V7X_LANES)
        xbc_s[sl, CONV_HIST:CONV_HIST + tm, :] = xbc[:, lanes]
        conv = cb_ref[:, lanes]
        for i in range(CONV_WIDTH):
            off = CONV_HIST - (CONV_WIDTH - 1) + i
            conv = conv + cw_ref[i:i + 1, lanes] * xbc_s[sl, off:off + tm, :]
        xc.append(jax.nn.silu(conv))

    @pl.when(t == nt - 1)
    def _():
        for sl in range(CONV_SLABS):
            conv_ref[0, :, sl * V7X_LANES:(sl + 1) * V7X_LANES] = (
                xbc_s[sl, CONV_HIST + tm - (CONV_WIDTH - 1):CONV_HIST + tm, :])

    xbc_s[:, 0:CONV_HIST, :] = xbc_s[:, tm:tm + CONV_HIST, :]

    dt = _softplus(dt_raw + dtb_ref[...])
    a_cum = _cumsum_chunks(dt * (-jnp.exp(alog_ref[...])))
    head_lanes = lax.broadcasted_iota(jnp.int32, (tm, DT_PAD), 1) < SSD_HEADS
    packed = jnp.where(head_lanes, a_cum, dt)

    cos2, sin2 = cos_ref[...], sin_ref[...]
    cos_t, sin_t = cost_ref[...], sint_ref[...]
    half = RET_DK // 2
    qr, krt = [], []
    for hh in range(RET_HEADS):
        qr.append(_rotary(qvgz[:, PQ_OFF + hh * RET_DK:PQ_OFF + (hh + 1) * RET_DK], cos2, sin2))
        k1 = kt[hh * RET_DK:hh * RET_DK + half, :]
        k2 = kt[hh * RET_DK + half:(hh + 1) * RET_DK, :]
        krt.append(jnp.concatenate([k1 * cos_t - k2 * sin_t, k1 * sin_t + k2 * cos_t], axis=0) * (RET_DK ** -0.5))

    row_i = lax.broadcasted_iota(jnp.int32, (CHUNK, CHUNK), 0)
    col_i = lax.broadcasted_iota(jnp.int32, (CHUNK, CHUNK), 1)
    causal = row_i >= col_i
    low_lanes = col_i < SSD_HEAD_DIM
    bd_rows = lax.broadcasted_iota(jnp.int32, (2 * SSD_STATE, 2 * SSD_HEAD_DIM), 0)
    bd_cols = lax.broadcasted_iota(jnp.int32, (2 * SSD_STATE, 2 * SSD_HEAD_DIM), 1)
    bd_mask = (bd_rows < SSD_STATE) == (bd_cols < SSD_HEAD_DIM)

    inner, kv, q_dec, v_bf = {}, {}, {}, {}
    scores, c_exp, st_inc, st_dec = {}, {}, {}, {}
    for c in range(nc):
        rows = slice(c * CHUNK, (c + 1) * CHUNK)
        for hh in range(RET_HEADS):
            q_c = qr[hh][rows, :]
            kt_c = krt[hh][:, rows]
            v_c = qvgz[rows, PV_OFF + hh * RET_DV:PV_OFF + (hh + 1) * RET_DV].astype(BF16)
            inner[c, hh] = (_dot(q_c.astype(BF16), kt_c.astype(BF16)) * dmat_ref[hh]).astype(BF16)
            kv[c, hh] = _dot((kt_c * kdec_ref[hh:hh + 1, :]).astype(BF16), v_c)
            q_dec[c, hh] = (q_c * qdec_ref[hh]).astype(BF16)
            v_bf[c, hh] = v_c

        pk = packed[rows, :]
        pk_t = pk.T
        a_t = pk_t[0:SSD_HEADS, :]
        dt_t = pk_t[SSD_HEADS:2 * SSD_HEADS, :]
        a_last = a_t[:, CHUNK - 1:CHUNK]
        w_t = jnp.exp(a_last - a_t) * dt_t
        chunk_dec = jnp.exp(a_last)
        for grp in range(SSD_GROUPS):
            cg = xc[SSD_PAIRS + SSD_GROUPS + grp][rows, :]
            b_t = xc[SSD_PAIRS + grp][rows, :].T
            gmat = _dot(cg.astype(BF16), b_t.astype(BF16))
            for pj in range(SSD_PAIRS // SSD_GROUPS):
                j = grp * (SSD_PAIRS // SSD_GROUPS) + pj
                heads = (2 * j, 2 * j + 1)
                x_pair = xc[j][rows, :].astype(BF16)
                b_w = jnp.concatenate([b_t * w_t[hd:hd + 1, :] for hd in heads], axis=0).astype(BF16)
                st_inc[c, j] = jnp.where(bd_mask, _dot(b_w, x_pair), 0.0)
                st_dec[c, j] = jnp.concatenate(
                    [jnp.broadcast_to(chunk_dec[hd:hd + 1, :], (SSD_STATE, 2 * SSD_HEAD_DIM)) for hd in heads], axis=0)
                for hd in heads:
                    a_col = jnp.broadcast_to(pk[:, hd:hd + 1], (CHUNK, CHUNK))
                    seg = a_col - a_t[hd:hd + 1, :]
                    lmat = jnp.where(causal, jnp.exp(jnp.where(causal, seg, 0.0)), 0.0)
                    scores[c, hd] = (gmat * lmat * dt_t[hd:hd + 1, :]).astype(BF16)
                    c_exp[c, hd] = (cg * jnp.exp(a_col)).astype(BF16)

    s_in, st_in = {}, {}
    for hh in range(RET_HEADS):
        s = sret_s[hh]
        for c in range(nc):
            s_in[c, hh] = s.astype(BF16)
            s = cdec_ref[hh:hh + 1, :] * s + kv[c, hh]
        sret_s[hh] = s
    for j in range(SSD_PAIRS):
        s = sssm_s[j]
        for c in range(nc):
            st_in[c, j] = s.astype(BF16)
            s = st_dec[c, j] * s + st_inc[c, j]
        sssm_s[j] = s

    for c in range(nc):
        rows = slice(c * CHUNK, (c + 1) * CHUNK)
        for hh in range(RET_HEADS):
            lanes = slice(hh * RET_DV, (hh + 1) * RET_DV)
            lhs = jnp.concatenate([inner[c, hh], q_dec[c, hh]], axis=1)
            rhs = jnp.concatenate([v_bf[c, hh], s_in[c, hh]], axis=0)
            r = _group_norm(_dot(lhs, rhs)) * gn_ref[:, lanes]
            gate = jax.nn.silu(qvgz[rows, PG_OFF + hh * RET_DV:PG_OFF + (hh + 1) * RET_DV])
            mix_s[rows, lanes] = (gate * r).astype(BF16)
        ys = []
        for j in range(SSD_PAIRS):
            heads = (2 * j, 2 * j + 1)
            x_pair = xc[j][rows, :]
            x_bd =jnp.concatenate([jnp.where(low_lanes, x_pair, 0.0), jnp.where(low_lanes, 0.0, x_pair)], axis=0)
            lhs = jnp.concatenate([scores[c, heads[0]], scores[c, heads[1]], c_exp[c, heads[0]], c_exp[c, heads[1]]],
                                  axis=1)
            rhs = jnp.concatenate([x_bd.astype(BF16), st_in[c, j]], axis=0)
            ys.append(_dot(lhs, rhs) + dsk_ref[:, j * 2 * SSD_HEAD_DIM:(j + 1) * 2 * SSD_HEAD_DIM] * x_pair)
        y_all = jnp.concatenate(ys, axis=1) * jax.nn.silu(qvgz[rows, PZ_OFF:PZ_OFF + SSD_WIDTH])
        mix_s[rows, RET_WIDTH:RET_WIDTH + SSD_WIDTH] = _rmsnorm(y_all, sn_ref[...]).astype(BF16)

    o_ref[...] = h + _dot(mix_s[...], wout_ref[...])

    @pl.when(t == nt - 1)
    def _():
        ret_ref[0] = sret_s[...]
        for j in range(SSD_PAIRS):
            st = sssm_s[j]
            ssm_ref[0, j * 2 * SSD_HEAD_DIM:(j + 1) * 2 * SSD_HEAD_DIM, :] = (st[0:SSD_STATE] + st[SSD_STATE:]).T


def _mix_prompt(h, gain, w_a, w_kt, w_xbc, w_dt, cos2, sin2, cos_t, sin_t, dmat, qdec, kdec, cdec, gn, conv_w, conv_b,
                dt_bias, a_log, dskip, ssd_gain, w_out, *, batch, seq, tm):
    nt = seq // tm
    d = h.shape[1]
    row = pl.BlockSpec((tm, d), lambda b, t: (b * nt + t, 0))
    pos = pl.BlockSpec((tm, RET_DK), lambda b, t: (t, 0))
    pos_t = pl.BlockSpec((RET_DK // 2, tm), lambda b, t: (0, t))
    consts = [gain, w_a, w_kt, w_xbc, w_dt]
    tail = [dmat, qdec, kdec, cdec, gn, conv_w, conv_b, dt_bias, a_log, dskip, ssd_gain, w_out]
    return pl.pallas_call(
        functools.partial(_mix_prompt_kernel, tm=tm, nt=nt),
        out_shape=[
            jax.ShapeDtypeStruct(h.shape, F32),
            jax.ShapeDtypeStruct((batch, RET_HEADS, RET_DK, RET_DV), F32),
            jax.ShapeDtypeStruct((batch, SSD_WIDTH, SSD_STATE), F32),
            jax.ShapeDtypeStruct((batch, CONV_WIDTH - 1, CONV_CH), F32),
        ],
        grid=(batch, nt),
        in_specs=([row] + [_resident(a.shape) for a in consts] + [pos, pos, pos_t, pos_t]
                  + [_resident(a.shape) for a in tail]),
        out_specs=[
            row,
            pl.BlockSpec((1, RET_HEADS, RET_DK, RET_DV), lambda b, t: (b, 0, 0, 0)),
            pl.BlockSpec((1, SSD_WIDTH, SSD_STATE), lambda b, t: (b, 0, 0)),
            pl.BlockSpec((1, CONV_WIDTH - 1, CONV_CH), lambda b, t: (b, 0, 0)),
        ],
        scratch_shapes=[
            pltpu.VMEM((CONV_SLABS, CONV_HIST + tm, V7X_LANES), F32),
            pltpu.VMEM((tm, RET_WIDTH + SSD_WIDTH), BF16),
            pltpu.VMEM((RET_HEADS, RET_DK, RET_DV), F32),
            pltpu.VMEM((SSD_PAIRS, 2 * SSD_STATE, 2 * SSD_HEAD_DIM), F32),
        ],
        compiler_params=_params(("arbitrary", "arbitrary")),
        name="mix_prompt",
    )(h, *consts, cos2, sin2, cos_t, sin_t, *tail)


def _pad_rows_t(x):
    n, w = x.shape
    return jnp.concatenate([x, jnp.zeros((V7X_LANES - n, w), x.dtype)], axis=0).T


def _mix_sample_kernel(proj_ref, sret_ref, sssm_ref, sconv_ref, cos_ref, sin_ref, gam_ref,
                       gn_ref, cw_ref, cb_ref, dtb_ref, alog_ref, dsk_ref, sn_ref,
                       mix_ref, oret_ref, ossm_ref, oconv_ref, *, bs):
    proj = proj_ref[...]
    xbc = proj[:, XBC_OFF:DT_OFF]
    hist = sconv_ref[...]
    taps = [hist[:, i * CONV_CH:(i + 1) * CONV_CH] for i in range(CONV_WIDTH - 1)] + [xbc]
    conv = cb_ref[...]
    for i in range(CONV_WIDTH):
        conv = conv + cw_ref[i:i + 1, :] * taps[i]
    oconv_ref[...] = jnp.concatenate(taps[1:], axis=1)
    xc = jax.nn.silu(conv)

    dt = _softplus(proj[:, DT_OFF:IN_PROJ_PAD] + dtb_ref[...])
    dec = jnp.exp(dt * (-jnp.exp(alog_ref[...])))

    sub_i = lax.broadcasted_iota(jnp.int32, (bs, V7X_LANES), 0)
    lane_i = lax.broadcasted_iota(jnp.int32, (SSD_HEAD_DIM, V7X_LANES), 1)
    cos2 = cos_ref[...]
    sin2 = sin_ref[...]

    for hh in range(RET_HEADS):
        lanes = slice(hh * RET_DK, (hh + 1) * RET_DK)
        qr = _rotary(proj[:, Q_OFF + hh * RET_DK:Q_OFF + (hh + 1) * RET_DK], cos2, sin2)
        kr = _rotary(proj[:, K_OFF + hh * RET_DK:K_OFF + (hh + 1) * RET_DK], cos2, sin2) * (RET_DK ** -0.5)
        vh = proj[:, V_OFF + hh * RET_DV:V_OFF + (hh + 1) * RET_DV]
        q_t = _pad_rows_t(qr)
        k_t = _pad_rows_t(kr)
        gamma = gam_ref[hh:hh + 1, :]
        y = jnp.zeros((bs, RET_DV), F32)
        for b in range(bs):
            s_new = gamma * sret_ref[b, hh] + k_t[:, b:b + 1] * vh[b:b + 1, :]
            oret_ref[b, hh] = s_new
            y_row = jnp.sum(q_t[:, b:b + 1] * s_new, axis=0, keepdims=True)
            y = jnp.where(sub_i == b, y_row, y)
        r = _group_norm(y) * gn_ref[:, lanes]
        mix_ref[:, lanes] = jax.nn.silu(proj[:, G_OFF + hh * RET_DV:G_OFF + (hh + 1) * RET_DV]) * r

    xs = xc[:, 0:SSD_WIDTH]
    head_of_lane = lax.broadcasted_iota(jnp.int32, (bs, SSD_WIDTH), 1) // SSD_HEAD_DIM
    dt_wide = jnp.zeros((bs, SSD_WIDTH), F32)
    for hd in range(SSD_HEADS):
        dt_wide = jnp.where(head_of_lane == hd, dt[:, hd:hd + 1], dt_wide)
    xdt_t = _pad_rows_t(xs * dt_wide)
    y_t = []
    for hd in range(SSD_HEADS):
        grp = hd // (SSD_HEADS // SSD_GROUPS)
        b_off = SSD_WIDTH + grp * SSD_STATE
        c_off = SSD_WIDTH + SSD_GROUPS * SSD_STATE + grp * SSD_STATE
        y_cols = jnp.zeros((SSD_HEAD_DIM, V7X_LANES), F32)
        for b in range(bs):
            s_new = (dec[b:b + 1, hd:hd + 1] * sssm_ref[b, hd]
                     + xdt_t[hd * SSD_HEAD_DIM:(hd + 1) * SSD_HEAD_DIM, b:b + 1] * xc[b:b + 1, b_off:b_off + SSD_STATE])
            ossm_ref[b, hd] = s_new
            y_col = jnp.sum(s_new * xc[b:b + 1, c_off:c_off + SSD_STATE], axis=1, keepdims=True)
            y_cols = jnp.where(lane_i == b, y_col, y_cols)
        y_t.append(y_cols)
    ys = jnp.concatenate(y_t, axis=0).T[0:bs, :]
    ys = (ys + dsk_ref[...] * xs) * jax.nn.silu(proj[:, Z_OFF:Z_OFF + SSD_WIDTH])
    mix_ref[:, RET_WIDTH:RET_WIDTH + SSD_WIDTH] = _rmsnorm(ys, sn_ref[...])


def _mix_sample(proj, s_ret, s_ssm, s_conv, cos2, sin2, gamma, gn, conv_w, conv_b, dt_bias, a_log, dskip, ssd_gain,
                *, bs):
    n = proj.shape[0]
    consts = [cos2, sin2, gamma, gn, conv_w, conv_b, dt_bias, a_log, dskip, ssd_gain]
    blk2 = lambda w: pl.BlockSpec((bs, w), lambda i: (i, 0))
    ret_blk = pl.BlockSpec((bs, RET_HEADS, RET_DK, RET_DV), lambda i: (i, 0, 0, 0))
    ssm_blk = pl.BlockSpec((bs, SSD_HEADS, SSD_HEAD_DIM, SSD_STATE), lambda i: (i, 0, 0, 0))
    return pl.pallas_call(
        functools.partial(_mix_sample_kernel, bs=bs),
        out_shape=[
            jax.ShapeDtypeStruct((n, RET_WIDTH + SSD_WIDTH), F32),
            jax.ShapeDtypeStruct(s_ret.shape, F32),
            jax.ShapeDtypeStruct(s_ssm.shape, F32),
            jax.ShapeDtypeStruct(s_conv.shape, F32),
        ],
        grid=(n // bs,),
        in_specs=[blk2(proj.shape[1]), ret_blk, ssm_blk, blk2(s_conv.shape[1])] + [_resident(a.shape) for a in consts],
        out_specs=[blk2(RET_WIDTH + SSD_WIDTH), ret_blk, ssm_blk, blk2(s_conv.shape[1])],
        compiler_params=_params(("arbitrary",)),
        name="mix_sample",
    )(proj, s_ret, s_ssm, s_conv, *consts)


def _softmax_rows(s):
    m = jnp.max(s, axis=-1, keepdims=True)
    p = jnp.exp(s - m)
    return p / jnp.sum(p, axis=-1, keepdims=True)


def _xattn_prompt_kernel(h_ref, g_ref, wq_ref, mk_ref, mv_ref, wo_ref, o_ref, att_s):
    h = h_ref[...]
    c = _rmsnorm(h, g_ref[...]).astype(BF16)
    qx = _dot(c, wq_ref[...]).astype(BF16)
    for hh in range(X_HEADS):
        lanes = slice(hh * X_HEAD_DIM, (hh + 1) * X_HEAD_DIM)
        s = _dot_nt(qx[:, lanes], mk_ref[0, :, lanes].astype(BF16)) * (X_HEAD_DIM ** -0.5)
        att = _softmax_rows(s).astype(BF16)
        att_s[:, lanes] = _dot(att, mv_ref[0, :, lanes].astype(BF16)).astype(BF16)
    o_ref[...] = h + _dot(att_s[...], wo_ref[...])


def _xattn_prompt(h, gain, w_q, mem_k, mem_v, w_o, *, batch, seq, tm):
    nt = seq // tm
    d = h.shape[1]
    row = pl.BlockSpec((tm, d), lambda b, t: (b * nt + t, 0))
    mem = pl.BlockSpec((1, MEM_TOKENS, d), lambda b, t: (b, 0, 0))
    return pl.pallas_call(
        _xattn_prompt_kernel,
        out_shape=jax.ShapeDtypeStruct(h.shape, F32),
        grid=(batch, nt),
        in_specs=[row, _resident(gain.shape), _resident(w_q.shape), mem, mem, _resident(w_o.shape)],
        out_specs=row,
        scratch_shapes=[pltpu.VMEM((tm, d), BF16)],
        compiler_params=_params(("arbitrary", "arbitrary")),
        name="xattn_prompt",
    )(h, gain, w_q, mem_k, mem_v, w_o)


def _xattn_sample_kernel(q_ref, k_ref, v_ref, o_ref, *, bs):
    rows = MEM_TOKENS * X_HEADS
    own = (lax.broadcasted_iota(jnp.int32, (X_HEADS, rows), 1) % X_HEADS
           == lax.broadcasted_iota(jnp.int32, (X_HEADS, rows), 0))
    for b in range(bs):
        q = q_ref[b].astype(BF16)
        k = k_ref[b].reshape(rows, X_HEAD_DIM).astype(BF16)
        v = v_ref[b].reshape(rows, X_HEAD_DIM).astype(BF16)
        s = _dot_nt(q, k) * (X_HEAD_DIM ** -0.5)
        att = _softmax_rows(jnp.where(own, s, -jnp.inf)).astype(BF16)
        o_ref[b] = _dot(att, v)


def _xattn_sample(q, mem_k, mem_v, *, bs):
    n = q.shape[0]
    row = pl.BlockSpec((bs, X_HEADS, X_HEAD_DIM), lambda i: (i, 0, 0))
    mem = pl.BlockSpec((bs, MEM_TOKENS, X_HEADS, X_HEAD_DIM), lambda i: (i, 0, 0, 0))
    return pl.pallas_call(
        functools.partial(_xattn_sample_kernel, bs=bs),
        out_shape=jax.ShapeDtypeStruct((n, X_HEADS, X_HEAD_DIM), F32),
        grid=(n // bs,),
        in_specs=[row, mem, mem],
        out_specs=row,
        compiler_params=_params(("arbitrary",)),
        name="xattn_sample",
    )(q, mem_k, mem_v)


def _rope_angles(pos):
    half = RET_DK // 2
    inv_freq = ROPE_BASE ** (-jnp.arange(half, dtype=F32) / half)
    ang = pos.astype(F32)[:, None] * inv_freq[None, :]
    return jnp.cos(ang), jnp.sin(ang)


def _rope_tables(pos):
    cos, sin = _rope_angles(pos)
    return jnp.concatenate([cos, cos], axis=-1), jnp.concatenate([-sin, sin], axis=-1)


def _retention_decay_tables(chunk):
    log_g = jnp.log1p(-jnp.exp2(-5.0 - jnp.arange(RET_HEADS, dtype=F32)))
    idx = jnp.arange(chunk, dtype=F32)
    diff = idx[:, None] - idx[None, :]
    causal = diff >= 0
    dmat = jnp.where(causal[None], jnp.exp(log_g[:, None, None] * jnp.where(causal, diff, 0.0)[None]), 0.0)
    q_dec = jnp.exp(log_g[:, None] * (idx[None, :] + 1.0))
    k_dec = jnp.exp(log_g[:, None] * (chunk - 1.0 - idx[None, :]))
    c_dec = jnp.exp(log_g * chunk)
    wide = lambda x: jnp.broadcast_to(x[..., None], x.shape + (V7X_LANES,))
    return dmat, wide(q_dec), k_dec, wide(c_dec)


def kernel(x_prompt, x_sample, mem_prompt, state_ret, state_ssm, state_conv, cache_mem_k, cache_mem_v, ffn1_norm,
           ffn1_w1, ffn1_w3, ffn1_w2, mix_norm, w_in, ret_gn_gain, conv_w, conv_b, dt_bias, A_log, D_skip, ssd_norm,
           w_out, x_norm, mem_norm, w_xq, w_xk, w_xv, w_xo, ffn2_norm, ffn2_w1, ffn2_w3, ffn2_w2, final_norm):
    bp, lp, d = x_prompt.shape
    bsz = x_sample.shape[0]
    depth = ffn1_w1.shape[0]
    row = lambda v: v.reshape(1, -1).astype(F32)
    lane_pad = lambda v: jnp.pad(row(v), ((0, 0), (0, DT_PAD - v.shape[-1])))

    cos_p, sin_p = _rope_tables(jnp.arange(lp))
    cos_pt, sin_pt = (a.T for a in _rope_angles(jnp.arange(lp)))
    cos_s, sin_s = _rope_tables(PAST_LEN + jnp.arange(x_sample.shape[1]))
    dmat, q_dec, k_dec, c_dec = _retention_decay_tables(CHUNK)
    gamma1 = _retention_decay_tables(1)[3]

    y_p = x_prompt.reshape(bp * lp, d)
    y_s = x_sample.reshape(bsz, d)
    outs = {k: [] for k in ("ret_p", "ssm_p", "conv_p", "memk", "memv", "ret_s", "ssm_s", "conv_s")}
    for l in range(depth):
        bf = lambda w: w[l].astype(BF16)
        w_in_f = w_in[l]
        w_in_l = jnp.pad(w_in_f, ((0, 0), (0, IN_PROJ_PAD - IN_PROJ_WIDTH))).astype(BF16)
        w_a = jnp.concatenate([w_in_f[:, Q_OFF:K_OFF], w_in_f[:, V_OFF:XBC_OFF]], axis=1).astype(BF16)
        w_kt = w_in_f[:, K_OFF:V_OFF].T.astype(BF16)
        w_xbc = w_in_f[:, XBC_OFF:DT_OFF].astype(BF16)
        w_dt = jnp.pad(jnp.tile(w_in_f[:, DT_OFF:], (1, 2)), ((0, 0), (0, DT_PAD - 2 * SSD_HEADS))).astype(BF16)
        f1 = (row(ffn1_norm[l]), bf(ffn1_w1), bf(ffn1_w3), bf(ffn1_w2))
        f2 = (row(ffn2_norm[l]), bf(ffn2_w1), bf(ffn2_w3), bf(ffn2_w2))
        shared = (row(ret_gn_gain[l]), conv_w[l], row(conv_b[l]))
        ssd_tail = (lane_pad(A_log[l]), row(jnp.repeat(D_skip[l], SSD_HEAD_DIM)), row(ssd_norm[l]))
        w_out_l, w_xq_l, w_xo_l = bf(w_out), bf(w_xq), bf(w_xo)

        mk, mv = _linear(mem_prompt.reshape(bp * MEM_TOKENS, d), [bf(w_xk), bf(w_xv)], gain=row(mem_norm[l]),
                         tm=MEM_TOKENS)
        mk = mk.reshape(bp, MEM_TOKENS, d)
        mv = mv.reshape(bp, MEM_TOKENS, d)

        y_p = _ffn(y_p, *f1, tm=512)
        y_p, ret_p, ssm_p, conv_p = _mix_prompt(
            y_p, row(mix_norm[l]), w_a, w_kt, w_xbc, w_dt, cos_p, sin_p, cos_pt, sin_pt, dmat, q_dec, k_dec, c_dec,
            *shared, lane_pad(jnp.tile(dt_bias[l], 2)), *ssd_tail, w_out_l, batch=bp, seq=lp, tm=512)
        y_p = _xattn_prompt(y_p, row(x_norm[l]), w_xq_l, mk, mv, w_xo_l, batch=bp, seq=lp, tm=512)
        y_p = _ffn(y_p, *f2, final_gain=row(final_norm) if l == depth - 1 else None, tm=512)

        y_s = _ffn(y_s, *f1, tm=bsz)
        proj_s, = _linear(y_s, [w_in_l], gain=row(mix_norm[l]), tm=bsz)
        mix_s, ret_s, ssm_s, conv_s = _mix_sample(
            proj_s, state_ret[l], state_ssm[l], state_conv[l].reshape(bsz, (CONV_WIDTH - 1) * CONV_CH),
            cos_s, sin_s, gamma1, *shared, lane_pad(dt_bias[l]), *ssd_tail, bs=8)
        y_s, = _linear(mix_s, [w_out_l], res=y_s, tm=bsz)
        q_s, = _linear(y_s, [w_xq_l], gain=row(x_norm[l]), tm=bsz)
        att_s = _xattn_sample(q_s.reshape(bsz, X_HEADS, X_HEAD_DIM), cache_mem_k[l], cache_mem_v[l], bs=8)
        y_s, = _linear(att_s.reshape(bsz, d), [w_xo_l], res=y_s, tm=bsz)
        y_s = _ffn(y_s, *f2, final_gain=row(final_norm) if l == depth - 1 else None, tm=bsz)

        outs["ret_p"].append(ret_p)
        outs["ssm_p"].append(ssm_p.reshape(bp, SSD_HEADS, SSD_HEAD_DIM, SSD_STATE))
        outs["conv_p"].append(conv_p)
        outs["memk"].append(mk.reshape(bp, MEM_TOKENS, X_HEADS, X_HEAD_DIM))
        outs["memv"].append(mv.reshape(bp, MEM_TOKENS, X_HEADS, X_HEAD_DIM))
        outs["ret_s"].append(ret_s)
        outs["ssm_s"].append(ssm_s)
        outs["conv_s"].append(conv_s.reshape(bsz, CONV_WIDTH - 1, CONV_CH))

    st = lambda k: jnp.stack(outs[k])
    return (y_p.reshape(bp, lp, d), y_s.reshape(bsz, x_sample.shape[1], d), st("ret_p"), st("ssm_p"), st("conv_p"),
            st("memk"), st("memv"), st("ret_s"), st("ssm_s"), st("conv_s"))
```

```python
import functools

import jax
import jax.numpy as jnp
from jax import lax
from jax.experimental import pallas as pl
from jax.experimental.pallas import tpu as pltpu

F32 = jnp.float32
BF16 = jnp.bfloat16

D_MODEL = 1024
D_FF = 2816
PAST_LEN = 16384
RET_HEADS = 4
RET_DK = 128
RET_DV = 128
RET_WIDTH = RET_HEADS * RET_DV
SSD_HEADS = 8
SSD_HEAD_DIM = 64
SSD_WIDTH = SSD_HEADS * SSD_HEAD_DIM
SSD_GROUPS = 2
SSD_STATE = 128
SSD_PAIRS = SSD_HEADS // 2
CONV_WIDTH = 4
CONV_CH = SSD_WIDTH + 2 * SSD_GROUPS * SSD_STATE
CHUNK = 128
MEM_TOKENS = 256
X_HEADS = 4
X_HEAD_DIM = D_MODEL // X_HEADS
ROPE_BASE = 10000.0
EPS = 1e-6

Q_OFF = 0
K_OFF = Q_OFF + RET_HEADS * RET_DK
V_OFF = K_OFF + RET_HEADS * RET_DK
G_OFF = V_OFF + RET_WIDTH
Z_OFF = G_OFF + RET_WIDTH
XBC_OFF = Z_OFF + SSD_WIDTH
DT_OFF = XBC_OFF + CONV_CH
IN_PROJ_WIDTH = DT_OFF + SSD_HEADS

PQ_OFF = 0
PV_OFF = PQ_OFF + RET_HEADS * RET_DK
PG_OFF = PV_OFF + RET_WIDTH
PZ_OFF = PG_OFF + RET_WIDTH
P_WIDTH = PZ_OFF + SSD_WIDTH

V7X_LANES = 128
V7X_SUBLANES = 8
V7X_VMEM_LIMIT_BYTES = 56 * 1024 * 1024
DT_PAD = V7X_LANES
IN_PROJ_PAD = DT_OFF + DT_PAD
CONV_HIST = V7X_SUBLANES
CONV_SLABS = CONV_CH // V7X_LANES


def _params(sem):
    return pltpu.CompilerParams(dimension_semantics=sem, vmem_limit_bytes=V7X_VMEM_LIMIT_BYTES)


def _resident(shape):
    zeros = (0,) * len(shape)
    return pl.BlockSpec(shape, lambda *_: zeros, pipeline_mode=pl.Buffered(1))


def _rmsnorm(x, gain):
    ms = jnp.mean(x * x, axis=-1, keepdims=True)
    return x * lax.rsqrt(ms + EPS) * gain


def _dot(a, b):
    return jnp.dot(a, b, preferred_element_type=F32)


def _dot_nt(a, b):
    return lax.dot_general(a, b, (((1,), (1,)), ((), ())), preferred_element_type=F32)


def _dot_tn(a, b):
    return lax.dot_general(a, b, (((0,), (0,)), ((), ())), preferred_element_type=F32)


def _softplus(x):
    return jnp.maximum(x, 0.0) + jnp.log1p(jnp.exp(-jnp.abs(x)))


def _rotary(x, cos2, sin2):
    return x * cos2 + pltpu.roll(x, RET_DK // 2, axis=1) * sin2


def _group_norm(y):
    mu = jnp.mean(y, axis=-1, keepdims=True)
    d = y - mu
    var = jnp.mean(d * d, axis=-1, keepdims=True)
    return d * lax.rsqrt(var + EPS)


def _ffn_kernel(*refs, final_norm):
    if final_norm:
        x_ref, g_ref, w1_ref, w3_ref, w2_ref, fg_ref, o_ref = refs
    else:
        x_ref, g_ref, w1_ref, w3_ref, w2_ref, o_ref = refs
    x = x_ref[...]
    xn = _rmsnorm(x, g_ref[...]).astype(BF16)
    a = _dot(xn, w1_ref[...])
    b = _dot(xn, w3_ref[...])
    hidden = (jax.nn.silu(a) * b).astype(BF16)
    out = x + 0.5 * _dot(hidden, w2_ref[...])
    if final_norm:
        out = _rmsnorm(out, fg_ref[...])
    o_ref[...] = out


def _ffn(x, gain, w1, w3, w2, final_gain=None, *, tm):
    t, d = x.shape
    row = pl.BlockSpec((tm, d), lambda i: (i, 0))
    ins = [x, gain, w1, w3, w2]
    specs = [row, _resident(gain.shape), _resident(w1.shape), _resident(w3.shape), _resident(w2.shape)]
    if final_gain is not None:
        ins.append(final_gain)
        specs.append(_resident(final_gain.shape))
    return pl.pallas_call(
        functools.partial(_ffn_kernel, final_norm=final_gain is not None),
        out_shape=jax.ShapeDtypeStruct((t, d), F32),
        grid=(t // tm,),
        in_specs=specs,
        out_specs=row,
        compiler_params=_params(("arbitrary",)),
        name="ffn",
    )(*ins)


def _linear_kernel(*refs, has_norm, has_res, n_w):
    refs = list(refs)
    x_ref = refs.pop(0)
    g_ref = refs.pop(0) if has_norm else None
    r_ref = refs.pop(0) if has_res else None
    w_refs, o_refs = refs[:n_w], refs[n_w:]
    x = x_ref[...]
    if has_norm:
        x = _rmsnorm(x, g_ref[...])
    xb = x.astype(BF16)
    for w_ref, o_ref in zip(w_refs, o_refs, strict=True):
        y = _dot(xb, w_ref[...])
        if has_res:
            y = r_ref[...] + y
        o_ref[...] = y


def _linear(x, weights, gain=None, res=None, *, tm):
    t, k = x.shape
    ins, specs = [x], [pl.BlockSpec((tm, k), lambda i: (i, 0))]
    if gain is not None:
        ins.append(gain)
        specs.append(_resident(gain.shape))
    if res is not None:
        ins.append(res)
        specs.append(pl.BlockSpec((tm, res.shape[1]), lambda i: (i, 0)))
    for w in weights:
        ins.append(w)
        specs.append(_resident(w.shape))
    outs = pl.pallas_call(
        functools.partial(_linear_kernel, has_norm=gain is not None, has_res=res is not None, n_w=len(weights)),
        out_shape=[jax.ShapeDtypeStruct((t, w.shape[1]), F32) for w in weights],
        grid=(t // tm,),
        in_specs=specs,
        out_specs=[pl.BlockSpec((tm, w.shape[1]), lambda i: (i, 0)) for w in weights],
        compiler_params=_params(("arbitrary",)),
        name="linear",
    )(*ins)
    return outs


def _cumsum_chunks(x):
    pos = lax.broadcasted_iota(jnp.int32, x.shape, 0) % CHUNK
    step = 1
    while step < CHUNK:
        x = x + jnp.where(pos >= step, pltpu.roll(x, step, axis=0), 0.0)
        step *= 2
    return x


def _mix_prompt_kernel(h_ref, g_ref, wa_ref, wkt_ref, wxbc_ref, wdt_ref, cos_ref, sin_ref, cost_ref, sint_ref,
                       dmat_ref, qdec_ref, kdec_ref, cdec_ref, gn_ref, cw_ref, cb_ref, dtb_ref, alog_ref, dsk_ref,
                       sn_ref, wout_ref,
                       o_ref, ret_ref, ssm_ref, conv_ref,
                       xbc_s, mix_s, sret_s, sssm_s, *, tm, nt):
    nc = tm // CHUNK
    t = pl.program_id(1)

    @pl.when(t == 0)
    def _():
        sret_s[...] = jnp.zeros_like(sret_s)
        sssm_s[...] = jnp.zeros_like(sssm_s)
        xbc_s[:, 0:CONV_HIST, :] = jnp.zeros((CONV_SLABS, CONV_HIST, V7X_LANES), F32)

    h = h_ref[...]
    u = _rmsnorm(h, g_ref[...]).astype(BF16)
    qvgz = _dot(u, wa_ref[...])
    kt = _dot_nt(wkt_ref[...], u)
    xbc = _dot(u, wxbc_ref[...])
    dt_raw = _dot(u, wdt_ref[...])

    xc = []
    for sl in range(CONV_SLABS):
        lanes = slice(sl * V7X_LANES, (sl + 1) * V7X_LANES)
        xbc_s[sl, CONV_HIST:CONV_HIST + tm, :] = xbc[:, lanes]
        conv = cb_ref[:, lanes]
        for i in range(CONV_WIDTH):
            off = CONV_HIST - (CONV_WIDTH - 1) + i
            conv = conv + cw_ref[i:i + 1, lanes] * xbc_s[sl, off:off + tm, :]
        xc.append(jax.nn.silu(conv))

    @pl.when(t == nt - 1)
    def _():
        for sl in range(CONV_SLABS):
            conv_ref[0, :, sl * V7X_LANES:(sl + 1) * V7X_LANES] = (
                xbc_s[sl, CONV_HIST + tm - (CONV_WIDTH - 1):CONV_HIST + tm, :])

    xbc_s[:, 0:CONV_HIST, :] = xbc_s[:, tm:tm + CONV_HIST, :]

    dt = _softplus(dt_raw + dtb_ref[...])
    a_cum = _cumsum_chunks(dt * (-jnp.exp(alog_ref[...])))
    head_lanes = lax.broadcasted_iota(jnp.int32, (tm, DT_PAD), 1) < SSD_HEADS
    packed = jnp.where(head_lanes, a_cum, dt)

    cos2, sin2 = cos_ref[...], sin_ref[...]
    cos_t, sin_t = cost_ref[...], sint_ref[...]
    half = RET_DK // 2
    qr, krt = [], []
    for hh in range(RET_HEADS):
        qr.append(_rotary(qvgz[:, PQ_OFF + hh * RET_DK:PQ_OFF + (hh + 1) * RET_DK], cos2, sin2))
        k1 = kt[hh * RET_DK:hh * RET_DK + half, :]
        k2 = kt[hh * RET_DK + half:(hh + 1) * RET_DK, :]
        krt.append(jnp.concatenate([k1 * cos_t - k2 * sin_t, k1 * sin_t + k2 * cos_t], axis=0) * (RET_DK ** -0.5))

    row_i = lax.broadcasted_iota(jnp.int32, (CHUNK, CHUNK), 0)
    col_i = lax.broadcasted_iota(jnp.int32, (CHUNK, CHUNK), 1)
    causal = row_i >= col_i
    low_lanes = col_i < SSD_HEAD_DIM
    bd_rows = lax.broadcasted_iota(jnp.int32, (2 * SSD_STATE, 2 * SSD_HEAD_DIM), 0)
    bd_cols = lax.broadcasted_iota(jnp.int32, (2 * SSD_STATE, 2 * SSD_HEAD_DIM), 1)
    bd_mask = (bd_rows < SSD_STATE) == (bd_cols < SSD_HEAD_DIM)

    inner, kv, q_dec, v_bf = {}, {}, {}, {}
    scores, c_exp, st_inc, st_dec = {}, {}, {}, {}
    for c in range(nc):
        rows = slice(c * CHUNK, (c + 1) * CHUNK)
        for hh in range(RET_HEADS):
            q_c = qr[hh][rows, :]
            kt_c = krt[hh][:, rows]
            v_c = qvgz[rows, PV_OFF + hh * RET_DV:PV_OFF + (hh + 1) * RET_DV].astype(BF16)
            inner[c, hh] = (_dot(q_c.astype(BF16), kt_c.astype(BF16)) * dmat_ref[hh]).astype(BF16)
            kv[c, hh] = _dot((kt_c * kdec_ref[hh:hh + 1, :]).astype(BF16), v_c)
            q_dec[c, hh] = (q_c * qdec_ref[hh]).astype(BF16)
            v_bf[c, hh] = v_c

        pk = packed[rows, :]
        pk_t = pk.T
        a_t = pk_t[0:SSD_HEADS, :]
        dt_t = pk_t[SSD_HEADS:2 * SSD_HEADS, :]
        a_last = a_t[:, CHUNK - 1:CHUNK]
        w_t = jnp.exp(a_last - a_t) * dt_t
        chunk_dec = jnp.exp(a_last)
        for grp in range(SSD_GROUPS):
            cg = xc[SSD_PAIRS + SSD_GROUPS + grp][rows, :]
            b_t = xc[SSD_PAIRS + grp][rows, :].T
            gmat = _dot(cg.astype(BF16), b_t.astype(BF16))
            for pj in range(SSD_PAIRS // SSD_GROUPS):
                j = grp * (SSD_PAIRS // SSD_GROUPS) + pj
                heads = (2 * j, 2 * j + 1)
                x_pair = xc[j][rows, :].astype(BF16)
                b_w = jnp.concatenate([b_t * w_t[hd:hd + 1, :] for hd in heads], axis=0).astype(BF16)
                st_inc[c, j] = jnp.where(bd_mask, _dot(b_w, x_pair), 0.0)
                st_dec[c, j] = jnp.concatenate(
                    [jnp.broadcast_to(chunk_dec[hd:hd + 1, :], (SSD_STATE, 2 * SSD_HEAD_DIM)) for hd in heads], axis=0)
                for hd in heads:
                    a_col = jnp.broadcast_to(pk[:, hd:hd + 1], (CHUNK, CHUNK))
                    seg = a_col - a_t[hd:hd + 1, :]
                    lmat = jnp.where(causal, jnp.exp(jnp.where(causal, seg, 0.0)), 0.0)
                    scores[c, hd] = (gmat * lmat * dt_t[hd:hd + 1, :]).astype(BF16)
                    c_exp[c, hd] = (cg * jnp.exp(a_col)).astype(BF16)

    s_in, st_in = {}, {}
    for hh in range(RET_HEADS):
        s = sret_s[hh]
        for c in range(nc):
            s_in[c, hh] = s.astype(BF16)
            s = cdec_ref[hh:hh + 1, :] * s + kv[c, hh]
        sret_s[hh] = s
    for j in range(SSD_PAIRS):
        s = sssm_s[j]
        for c in range(nc):
            st_in[c, j] = s.astype(BF16)
            s = st_dec[c, j] * s + st_inc[c, j]
        sssm_s[j] = s

    for c in range(nc):
        rows = slice(c * CHUNK, (c + 1) * CHUNK)
        for hh in range(RET_HEADS):
            lanes = slice(hh * RET_DV, (hh + 1) * RET_DV)
            lhs = jnp.concatenate([inner[c, hh], q_dec[c, hh]], axis=1)
            rhs = jnp.concatenate([v_bf[c, hh], s_in[c, hh]], axis=0)
            r = _group_norm(_dot(lhs, rhs)) * gn_ref[:, lanes]
            gate = jax.nn.silu(qvgz[rows, PG_OFF + hh * RET_DV:PG_OFF + (hh + 1) * RET_DV])
            mix_s[rows, lanes] = (gate * r).astype(BF16)
        ys = []
        for j in range(SSD_PAIRS):
            heads = (2 * j, 2 * j + 1)
            x_pair = xc[j][rows, :]
            x_bd =jnp.concatenate([jnp.where(low_lanes, x_pair, 0.0), jnp.where(low_lanes, 0.0, x_pair)], axis=0)
            lhs = jnp.concatenate([scores[c, heads[0]], scores[c, heads[1]], c_exp[c, heads[0]], c_exp[c, heads[1]]],
                                  axis=1)
            rhs = jnp.concatenate([x_bd.astype(BF16), st_in[c, j]], axis=0)
            ys.append(_dot(lhs, rhs) + dsk_ref[:, j * 2 * SSD_HEAD_DIM:(j + 1) * 2 * SSD_HEAD_DIM] * x_pair)
        y_all = jnp.concatenate(ys, axis=1) * jax.nn.silu(qvgz[rows, PZ_OFF:PZ_OFF + SSD_WIDTH])
        mix_s[rows, RET_WIDTH:RET_WIDTH + SSD_WIDTH] = _rmsnorm(y_all, sn_ref[...]).astype(BF16)

    o_ref[...] = h + _dot(mix_s[...], wout_ref[...])

    @pl.when(t == nt - 1)
    def _():
        ret_ref[0] = sret_s[...]
        for j in range(SSD_PAIRS):
            st = sssm_s[j]
            ssm_ref[0, j * 2 * SSD_HEAD_DIM:(j + 1) * 2 * SSD_HEAD_DIM, :] = (st[0:SSD_STATE] + st[SSD_STATE:]).T


def _mix_prompt(h, gain, w_a, w_kt, w_xbc, w_dt, cos2, sin2, cos_t, sin_t, dmat, qdec, kdec, cdec, gn, conv_w, conv_b,
                dt_bias, a_log, dskip, ssd_gain, w_out, *, batch, seq, tm):
    nt = seq // tm
    d = h.shape[1]
    row = pl.BlockSpec((tm, d), lambda b, t: (b * nt + t, 0))
    pos = pl.BlockSpec((tm, RET_DK), lambda b, t: (t, 0))
    pos_t = pl.BlockSpec((RET_DK // 2, tm), lambda b, t: (0, t))
    consts = [gain, w_a, w_kt, w_xbc, w_dt]
    tail = [dmat, qdec, kdec, cdec, gn, conv_w, conv_b, dt_bias, a_log, dskip, ssd_gain, w_out]
    return pl.pallas_call(
        functools.partial(_mix_prompt_kernel, tm=tm, nt=nt),
        out_shape=[
            jax.ShapeDtypeStruct(h.shape, F32),
            jax.ShapeDtypeStruct((batch, RET_HEADS, RET_DK, RET_DV), F32),
            jax.ShapeDtypeStruct((batch, SSD_WIDTH, SSD_STATE), F32),
            jax.ShapeDtypeStruct((batch, CONV_WIDTH - 1, CONV_CH), F32),
        ],
        grid=(batch, nt),
        in_specs=([row] + [_resident(a.shape) for a in consts] + [pos, pos, pos_t, pos_t]
                  + [_resident(a.shape) for a in tail]),
        out_specs=[
            row,
            pl.BlockSpec((1, RET_HEADS, RET_DK, RET_DV), lambda b, t: (b, 0, 0, 0)),
            pl.BlockSpec((1, SSD_WIDTH, SSD_STATE), lambda b, t: (b, 0, 0)),
            pl.BlockSpec((1, CONV_WIDTH - 1, CONV_CH), lambda b, t: (b, 0, 0)),
        ],
        scratch_shapes=[
            pltpu.VMEM((CONV_SLABS, CONV_HIST + tm, V7X_LANES), F32),
            pltpu.VMEM((tm, RET_WIDTH + SSD_WIDTH), BF16),
            pltpu.VMEM((RET_HEADS, RET_DK, RET_DV), F32),
            pltpu.VMEM((SSD_PAIRS, 2 * SSD_STATE, 2 * SSD_HEAD_DIM), F32),
        ],
        compiler_params=_params(("arbitrary", "arbitrary")),
        name="mix_prompt",
    )(h, *consts, cos2, sin2, cos_t, sin_t, *tail)


def _split_hi_lo(x):
    hi = x.astype(BF16).astype(F32)
    return hi, x - hi


def _mix_sample_kernel(proj_ref, sret_ref, sssm_ref, sconv_ref, cos_ref, sin_ref, gam_ref,
                       gn_ref, cw_ref, cb_ref, dtb_ref, alog_ref, dsk_ref, sn_ref,
                       mix_ref, oret_ref, ossm_ref, oconv_ref, *, bs):
    proj = proj_ref[...]
    xbc = proj[:, XBC_OFF:DT_OFF]
    hist = sconv_ref[...]
    taps = [hist[:, i * CONV_CH:(i + 1) * CONV_CH] for i in range(CONV_WIDTH - 1)] + [xbc]
    conv = cb_ref[...]
    for i in range(CONV_WIDTH):
        conv = conv + cw_ref[i:i + 1, :] * taps[i]
    oconv_ref[...] = jnp.concatenate(taps[1:], axis=1)
    xc = jax.nn.silu(conv)

    dt = _softplus(proj[:, DT_OFF:IN_PROJ_PAD] + dtb_ref[...])

    sub_i = lax.broadcasted_iota(jnp.int32, (bs, V7X_LANES), 0)
    row_of = lax.broadcasted_iota(jnp.int32, (4 * bs, V7X_LANES), 0) % bs
    cos2 = cos_ref[...]
    sin2 = sin_ref[...]

    def outer_lhs(x):
        hi, lo = _split_hi_lo(x)
        return jnp.concatenate([hi, hi, lo, lo], axis=0)

    def outer_rhs(x):
        hi, lo = _split_hi_lo(x)
        return jnp.concatenate([hi, lo, hi, lo], axis=0).astype(BF16)

    def only_sample(x4, b):
        return jnp.where(row_of == b, x4, 0.0).astype(BF16)

    for hh in range(RET_HEADS):
        lanes = slice(hh * RET_DK, (hh + 1) * RET_DK)
        qr = _rotary(proj[:, Q_OFF + hh * RET_DK:Q_OFF + (hh + 1) * RET_DK], cos2, sin2)
        kr = _rotary(proj[:, K_OFF + hh * RET_DK:K_OFF + (hh + 1) * RET_DK], cos2, sin2) * (RET_DK ** -0.5)
        vh = proj[:, V_OFF + hh * RET_DV:V_OFF + (hh + 1) * RET_DV]
        k4, v4, q_bf = outer_lhs(kr), outer_rhs(vh), qr.astype(BF16)
        gamma = gam_ref[hh:hh + 1, :]
        y = jnp.zeros((bs, RET_DV), F32)
        for b in range(bs):
            s_old = sret_ref[b, hh]
            oret_ref[b, hh] = gamma * s_old + _dot_tn(only_sample(k4, b), v4)
            y = jnp.where(sub_i == b, _dot(q_bf, s_old.astype(BF16)), y)
        y = gamma * y + jnp.sum(qr * kr, axis=-1, keepdims=True) * vh
        r = _group_norm(y) * gn_ref[:, lanes]
        mix_ref[:, lanes] = jax.nn.silu(proj[:, G_OFF + hh * RET_DV:G_OFF + (hh + 1) * RET_DV]) * r

    xs = xc[:, 0:SSD_WIDTH]
    head_of_lane = lax.broadcasted_iota(jnp.int32, (bs, SSD_WIDTH), 1) // SSD_HEAD_DIM
    dec = jnp.exp(dt * (-jnp.exp(alog_ref[...])))
    dt_wide = jnp.zeros((bs, SSD_WIDTH), F32)
    dec_wide = jnp.zeros((bs, SSD_WIDTH), F32)
    for hd in range(SSD_HEADS):
        dt_wide = jnp.where(head_of_lane == hd, dt[:, hd:hd + 1], dt_wide)
        dec_wide = jnp.where(head_of_lane == hd, dec[:, hd:hd + 1], dec_wide)
    xdt = xs * dt_wide
    x4 = outer_lhs(xdt)
    ys = []
    for j in range(SSD_PAIRS):
        grp = j // (SSD_PAIRS // SSD_GROUPS)
        lanes = slice(j * 2 * SSD_HEAD_DIM, (j + 1) * 2 * SSD_HEAD_DIM)
        bg = xc[:, SSD_WIDTH + grp * SSD_STATE:SSD_WIDTH + (grp + 1) * SSD_STATE]
        c_off = SSD_WIDTH + SSD_GROUPS * SSD_STATE + grp * SSD_STATE
        cg = xc[:, c_off:c_off + SSD_STATE]
        b4, c_bf = outer_rhs(bg), cg.astype(BF16)
        y = jnp.zeros((bs, 2 * SSD_HEAD_DIM), F32)
        for b in range(bs):
            s_old = sssm_ref[b, lanes, :]
            dec_rows = jnp.concatenate(
                [jnp.broadcast_to(dec[b:b + 1, hd:hd + 1], (SSD_HEAD_DIM, SSD_STATE)) for hd in (2 * j, 2 * j + 1)],
                axis=0)
            ossm_ref[b, lanes, :] = dec_rows * s_old + _dot_tn(only_sample(x4[:, lanes], b), b4)
            y = jnp.where(sub_i == b, _dot_nt(c_bf, s_old.astype(BF16)), y)
        ys.append(dec_wide[:, lanes] * y + jnp.sum(cg * bg, axis=-1, keepdims=True) * xdt[:, lanes])
    ys = (jnp.concatenate(ys, axis=1) + dsk_ref[...] * xs) * jax.nn.silu(proj[:, Z_OFF:Z_OFF + SSD_WIDTH])
    mix_ref[:, RET_WIDTH:RET_WIDTH + SSD_WIDTH] = _rmsnorm(ys, sn_ref[...])


def _mix_sample(proj, s_ret, s_ssm, s_conv, cos2, sin2, gamma, gn, conv_w, conv_b, dt_bias, a_log, dskip, ssd_gain,
                *, bs):
    n = proj.shape[0]
    consts = [cos2, sin2, gamma, gn, conv_w, conv_b, dt_bias, a_log, dskip, ssd_gain]
    blk2 = lambda w: pl.BlockSpec((bs, w), lambda i: (i, 0))
    ret_blk = pl.BlockSpec((bs, RET_HEADS, RET_DK, RET_DV), lambda i: (i, 0, 0, 0))
    ssm_blk = pl.BlockSpec((bs, SSD_WIDTH, SSD_STATE), lambda i: (i, 0, 0))
    return pl.pallas_call(
        functools.partial(_mix_sample_kernel, bs=bs),
        out_shape=[
            jax.ShapeDtypeStruct((n, RET_WIDTH + SSD_WIDTH), F32),
            jax.ShapeDtypeStruct(s_ret.shape, F32),
            jax.ShapeDtypeStruct(s_ssm.shape, F32),
            jax.ShapeDtypeStruct(s_conv.shape, F32),
        ],
        grid=(n // bs,),
        in_specs=[blk2(proj.shape[1]), ret_blk, ssm_blk, blk2(s_conv.shape[1])] + [_resident(a.shape) for a in consts],
        out_specs=[blk2(RET_WIDTH + SSD_WIDTH), ret_blk, ssm_blk, blk2(s_conv.shape[1])],
        compiler_params=_params(("arbitrary",)),
        name="mix_sample",
    )(proj, s_ret, s_ssm, s_conv, *consts)


def _softmax_rows(s):
    m = jnp.max(s, axis=-1, keepdims=True)
    p = jnp.exp(s - m)
    return p / jnp.sum(p, axis=-1, keepdims=True)


def _xattn_prompt_kernel(h_ref, g_ref, wq_ref, mk_ref, mv_ref, wo_ref, o_ref, att_s):
    h = h_ref[...]
    c = _rmsnorm(h, g_ref[...]).astype(BF16)
    qx = _dot(c, wq_ref[...]).astype(BF16)
    for hh in range(X_HEADS):
        lanes = slice(hh * X_HEAD_DIM, (hh + 1) * X_HEAD_DIM)
        s = _dot_nt(qx[:, lanes], mk_ref[0, :, lanes].astype(BF16)) * (X_HEAD_DIM ** -0.5)
        att = _softmax_rows(s).astype(BF16)
        att_s[:, lanes] = _dot(att, mv_ref[0, :, lanes].astype(BF16)).astype(BF16)
    o_ref[...] = h + _dot(att_s[...], wo_ref[...])


def _xattn_prompt(h, gain, w_q, mem_k, mem_v, w_o, *, batch, seq, tm):
    nt = seq // tm
    d = h.shape[1]
    row = pl.BlockSpec((tm, d), lambda b, t: (b * nt + t, 0))
    mem = pl.BlockSpec((1, MEM_TOKENS, d), lambda b, t: (b, 0, 0))
    return pl.pallas_call(
        _xattn_prompt_kernel,
        out_shape=jax.ShapeDtypeStruct(h.shape, F32),
        grid=(batch, nt),
        in_specs=[row, _resident(gain.shape), _resident(w_q.shape), mem, mem, _resident(w_o.shape)],
        out_specs=row,
        scratch_shapes=[pltpu.VMEM((tm, d), BF16)],
        compiler_params=_params(("arbitrary", "arbitrary")),
        name="xattn_prompt",
    )(h, gain, w_q, mem_k, mem_v, w_o)


def _xattn_sample_kernel(q_ref, k_ref, v_ref, o_ref, *, bs):
    rows = MEM_TOKENS * X_HEADS
    own = (lax.broadcasted_iota(jnp.int32, (X_HEADS, rows), 1) % X_HEADS
           == lax.broadcasted_iota(jnp.int32, (X_HEADS, rows), 0))
    for b in range(bs):
        q = q_ref[b].astype(BF16)
        k = k_ref[b].reshape(rows, X_HEAD_DIM).astype(BF16)
        v = v_ref[b].reshape(rows, X_HEAD_DIM).astype(BF16)
        s = _dot_nt(q, k) * (X_HEAD_DIM ** -0.5)
        att = _softmax_rows(jnp.where(own, s, -jnp.inf)).astype(BF16)
        o_ref[b] = _dot(att, v)


def _xattn_sample(q, mem_k, mem_v, *, bs):
    n = q.shape[0]
    row = pl.BlockSpec((bs, X_HEADS, X_HEAD_DIM), lambda i: (i, 0, 0))
    mem = pl.BlockSpec((bs, MEM_TOKENS, X_HEADS, X_HEAD_DIM), lambda i: (i, 0, 0, 0))
    return pl.pallas_call(
        functools.partial(_xattn_sample_kernel, bs=bs),
        out_shape=jax.ShapeDtypeStruct((n, X_HEADS, X_HEAD_DIM), F32),
        grid=(n // bs,),
        in_specs=[row, mem, mem],
        out_specs=row,
        compiler_params=_params(("arbitrary",)),
        name="xattn_sample",
    )(q, mem_k, mem_v)


def _rope_angles(pos):
    half = RET_DK // 2
    inv_freq = ROPE_BASE ** (-jnp.arange(half, dtype=F32) / half)
    ang = pos.astype(F32)[:, None] * inv_freq[None, :]
    return jnp.cos(ang), jnp.sin(ang)


def _rope_tables(pos):
    cos, sin = _rope_angles(pos)
    return jnp.concatenate([cos, cos], axis=-1), jnp.concatenate([-sin, sin], axis=-1)


def _retention_decay_tables(chunk):
    log_g = jnp.log1p(-jnp.exp2(-5.0 - jnp.arange(RET_HEADS, dtype=F32)))
    idx = jnp.arange(chunk, dtype=F32)
    diff = idx[:, None] - idx[None, :]
    causal = diff >= 0
    dmat = jnp.where(causal[None], jnp.exp(log_g[:, None, None] * jnp.where(causal, diff, 0.0)[None]), 0.0)
    q_dec = jnp.exp(log_g[:, None] * (idx[None, :] + 1.0))
    k_dec = jnp.exp(log_g[:, None] * (chunk - 1.0 - idx[None, :]))
    c_dec = jnp.exp(log_g * chunk)
    wide = lambda x: jnp.broadcast_to(x[..., None], x.shape + (V7X_LANES,))
    return dmat, wide(q_dec), k_dec, wide(c_dec)


def kernel(x_prompt, x_sample, mem_prompt, state_ret, state_ssm, state_conv, cache_mem_k, cache_mem_v, ffn1_norm,
           ffn1_w1, ffn1_w3, ffn1_w2, mix_norm, w_in, ret_gn_gain, conv_w, conv_b, dt_bias, A_log, D_skip, ssd_norm,
           w_out, x_norm, mem_norm, w_xq, w_xk, w_xv, w_xo, ffn2_norm, ffn2_w1, ffn2_w3, ffn2_w2, final_norm):
    bp, lp, d = x_prompt.shape
    bsz = x_sample.shape[0]
    depth = ffn1_w1.shape[0]
    row = lambda v: v.reshape(1, -1).astype(F32)
    lane_pad = lambda v: jnp.pad(row(v), ((0, 0), (0, DT_PAD - v.shape[-1])))

    cos_p, sin_p = _rope_tables(jnp.arange(lp))
    cos_pt, sin_pt = (a.T for a in _rope_angles(jnp.arange(lp)))
    cos_s, sin_s = _rope_tables(PAST_LEN + jnp.arange(x_sample.shape[1]))
    dmat, q_dec, k_dec, c_dec = _retention_decay_tables(CHUNK)
    gamma1 = _retention_decay_tables(1)[3]

    y_p = x_prompt.reshape(bp * lp, d)
    y_s = x_sample.reshape(bsz, d)
    outs = {k: [] for k in ("ret_p", "ssm_p", "conv_p", "memk", "memv", "ret_s", "ssm_s", "conv_s")}
    for l in range(depth):
        bf = lambda w: w[l].astype(BF16)
        w_in_f = w_in[l]
        w_in_l = jnp.pad(w_in_f, ((0, 0), (0, IN_PROJ_PAD - IN_PROJ_WIDTH))).astype(BF16)
        w_a = jnp.concatenate([w_in_f[:, Q_OFF:K_OFF], w_in_f[:, V_OFF:XBC_OFF]], axis=1).astype(BF16)
        w_kt = w_in_f[:, K_OFF:V_OFF].T.astype(BF16)
        w_xbc = w_in_f[:, XBC_OFF:DT_OFF].astype(BF16)
        w_dt = jnp.pad(jnp.tile(w_in_f[:, DT_OFF:], (1, 2)), ((0, 0), (0, DT_PAD - 2 * SSD_HEADS))).astype(BF16)
        f1 = (row(ffn1_norm[l]), bf(ffn1_w1), bf(ffn1_w3), bf(ffn1_w2))
        f2 = (row(ffn2_norm[l]), bf(ffn2_w1), bf(ffn2_w3), bf(ffn2_w2))
        shared = (row(ret_gn_gain[l]), conv_w[l], row(conv_b[l]))
        ssd_tail = (lane_pad(A_log[l]), row(jnp.repeat(D_skip[l], SSD_HEAD_DIM)), row(ssd_norm[l]))
        w_out_l, w_xq_l, w_xo_l = bf(w_out), bf(w_xq), bf(w_xo)

        mk, mv = _linear(mem_prompt.reshape(bp * MEM_TOKENS, d), [bf(w_xk), bf(w_xv)], gain=row(mem_norm[l]),
                         tm=MEM_TOKENS)
        mk = mk.reshape(bp, MEM_TOKENS, d)
        mv = mv.reshape(bp, MEM_TOKENS, d)

        y_p = _ffn(y_p, *f1, tm=512)
        y_p, ret_p, ssm_p, conv_p = _mix_prompt(
            y_p, row(mix_norm[l]), w_a, w_kt, w_xbc, w_dt, cos_p, sin_p, cos_pt, sin_pt, dmat, q_dec, k_dec, c_dec,
            *shared, lane_pad(jnp.tile(dt_bias[l], 2)), *ssd_tail, w_out_l, batch=bp, seq=lp, tm=512)
        y_p = _xattn_prompt(y_p, row(x_norm[l]), w_xq_l, mk, mv, w_xo_l, batch=bp, seq=lp, tm=512)
        y_p = _ffn(y_p, *f2, final_gain=row(final_norm) if l == depth - 1 else None, tm=512)

        y_s = _ffn(y_s, *f1, tm=bsz)
        proj_s, = _linear(y_s, [w_in_l], gain=row(mix_norm[l]), tm=bsz)
        mix_s, ret_s, ssm_s, conv_s = _mix_sample(
            proj_s, state_ret[l], state_ssm[l].reshape(bsz, SSD_WIDTH, SSD_STATE),
            state_conv[l].reshape(bsz, (CONV_WIDTH - 1) * CONV_CH),
            cos_s, sin_s, gamma1, *shared, lane_pad(dt_bias[l]), *ssd_tail, bs=8)
        y_s, = _linear(mix_s, [w_out_l], res=y_s, tm=bsz)
        q_s, = _linear(y_s, [w_xq_l], gain=row(x_norm[l]), tm=bsz)
        att_s = _xattn_sample(q_s.reshape(bsz, X_HEADS, X_HEAD_DIM), cache_mem_k[l], cache_mem_v[l], bs=8)
        y_s, = _linear(att_s.reshape(bsz, d), [w_xo_l], res=y_s, tm=bsz)
        y_s = _ffn(y_s, *f2, final_gain=row(final_norm) if l == depth - 1 else None, tm=bsz)

        outs["ret_p"].append(ret_p)
        outs["ssm_p"].append(ssm_p.reshape(bp, SSD_HEADS, SSD_HEAD_DIM, SSD_STATE))
        outs["conv_p"].append(conv_p)
        outs["memk"].append(mk.reshape(bp, MEM_TOKENS, X_HEADS, X_HEAD_DIM))
        outs["memv"].append(mv.reshape(bp, MEM_TOKENS, X_HEADS, X_HEAD_DIM))
        outs["ret_s"].append(ret_s)
        outs["ssm_s"].append(ssm_s.reshape(bsz, SSD_HEADS, SSD_HEAD_DIM, SSD_STATE))
        outs["conv_s"].append(conv_s.reshape(bsz, CONV_WIDTH - 1, CONV_CH))

    st = lambda k: jnp.stack(outs[k])
    return (y_p.reshape(bp, lp, d), y_s.reshape(bsz, x_sample.shape[1], d), st("ret_p"), st("ssm_p"), st("conv_p"),
            st("memk"), st("memv"), st("ret_s"), st("ssm_s"), st("conv_s"))
```

```python
import functools

import jax
import jax.numpy as jnp
from jax import lax
from jax.experimental import pallas as pl
from jax.experimental.pallas import tpu as pltpu

F32 = jnp.float32
BF16 = jnp.bfloat16

D_MODEL = 1024
D_FF = 2816
PAST_LEN = 16384
RET_HEADS = 4
RET_DK = 128
RET_DV = 128
RET_WIDTH = RET_HEADS * RET_DV
SSD_HEADS = 8
SSD_HEAD_DIM = 64
SSD_WIDTH = SSD_HEADS * SSD_HEAD_DIM
SSD_GROUPS = 2
SSD_STATE = 128
SSD_PAIRS = SSD_HEADS // 2
CONV_WIDTH = 4
CONV_CH = SSD_WIDTH + 2 * SSD_GROUPS * SSD_STATE
CHUNK = 128
MEM_TOKENS = 256
X_HEADS = 4
X_HEAD_DIM = D_MODEL // X_HEADS
ROPE_BASE = 10000.0
EPS = 1e-6

Q_OFF = 0
K_OFF = Q_OFF + RET_HEADS * RET_DK
V_OFF = K_OFF + RET_HEADS * RET_DK
G_OFF = V_OFF + RET_WIDTH
Z_OFF = G_OFF + RET_WIDTH
XBC_OFF = Z_OFF + SSD_WIDTH
DT_OFF = XBC_OFF + CONV_CH
IN_PROJ_WIDTH = DT_OFF + SSD_HEADS

PQ_OFF = 0
PV_OFF = PQ_OFF + RET_HEADS * RET_DK
PG_OFF = PV_OFF + RET_WIDTH
PZ_OFF = PG_OFF + RET_WIDTH
P_WIDTH = PZ_OFF + SSD_WIDTH

V7X_LANES = 128
V7X_SUBLANES = 8
V7X_VMEM_LIMIT_BYTES = 56 * 1024 * 1024
DT_PAD = V7X_LANES
IN_PROJ_PAD = DT_OFF + DT_PAD
CONV_HIST = V7X_SUBLANES
CONV_SLABS = CONV_CH // V7X_LANES


def _params(sem):
    return pltpu.CompilerParams(dimension_semantics=sem, vmem_limit_bytes=V7X_VMEM_LIMIT_BYTES)


def _resident(shape):
    zeros = (0,) * len(shape)
    return pl.BlockSpec(shape, lambda *_: zeros, pipeline_mode=pl.Buffered(1))


def _rmsnorm(x, gain):
    ms = jnp.mean(x * x, axis=-1, keepdims=True)
    return x * lax.rsqrt(ms + EPS) * gain


def _dot(a, b):
    return jnp.dot(a, b, preferred_element_type=F32)


def _dot_nt(a, b):
    return lax.dot_general(a, b, (((1,), (1,)), ((), ())), preferred_element_type=F32)


def _dot_tn(a, b):
    return lax.dot_general(a, b, (((0,), (0,)), ((), ())), preferred_element_type=F32)


def _softplus(x):
    return jnp.maximum(x, 0.0) + jnp.log1p(jnp.exp(-jnp.abs(x)))


def _rotary(x, cos2, sin2):
    return x * cos2 + pltpu.roll(x, RET_DK // 2, axis=1) * sin2


def _group_norm(y):
    mu = jnp.mean(y, axis=-1, keepdims=True)
    d = y - mu
    var = jnp.mean(d * d, axis=-1, keepdims=True)
    return d * lax.rsqrt(var + EPS)


def _ffn_kernel(*refs, final_norm):
    if final_norm:
        x_ref, g_ref, w1_ref, w3_ref, w2_ref, fg_ref, o_ref = refs
    else:
        x_ref, g_ref, w1_ref, w3_ref, w2_ref, o_ref = refs
    x = x_ref[...]
    xn = _rmsnorm(x, g_ref[...]).astype(BF16)
    a = _dot(xn, w1_ref[...])
    b = _dot(xn, w3_ref[...])
    hidden = (jax.nn.silu(a) * b).astype(BF16)
    out = x + 0.5 * _dot(hidden, w2_ref[...])
    if final_norm:
        out = _rmsnorm(out, fg_ref[...])
    o_ref[...] = out


def _ffn(x, gain, w1, w3, w2, final_gain=None, *, tm):
    t, d = x.shape
    row = pl.BlockSpec((tm, d), lambda i: (i, 0))
    ins = [x, gain, w1, w3, w2]
    specs = [row, _resident(gain.shape), _resident(w1.shape), _resident(w3.shape), _resident(w2.shape)]
    if final_gain is not None:
        ins.append(final_gain)
        specs.append(_resident(final_gain.shape))
    return pl.pallas_call(
        functools.partial(_ffn_kernel, final_norm=final_gain is not None),
        out_shape=jax.ShapeDtypeStruct((t, d), F32),
        grid=(t // tm,),
        in_specs=specs,
        out_specs=row,
        compiler_params=_params(("arbitrary",)),
        name="ffn",
    )(*ins)


def _linear_kernel(*refs, has_norm, has_res, n_w):
    refs = list(refs)
    x_ref = refs.pop(0)
    g_ref = refs.pop(0) if has_norm else None
    r_ref = refs.pop(0) if has_res else None
    w_refs, o_refs = refs[:n_w], refs[n_w:]
    x = x_ref[...]
    if has_norm:
        x = _rmsnorm(x, g_ref[...])
    xb = x.astype(BF16)
    for w_ref, o_ref in zip(w_refs, o_refs, strict=True):
        y = _dot(xb, w_ref[...])
        if has_res:
            y = r_ref[...] + y
        o_ref[...] = y


def _linear(x, weights, gain=None, res=None, *, tm):
    t, k = x.shape
    ins, specs = [x], [pl.BlockSpec((tm, k), lambda i: (i, 0))]
    if gain is not None:
        ins.append(gain)
        specs.append(_resident(gain.shape))
    if res is not None:
        ins.append(res)
        specs.append(pl.BlockSpec((tm, res.shape[1]), lambda i: (i, 0)))
    for w in weights:
        ins.append(w)
        specs.append(_resident(w.shape))
    outs = pl.pallas_call(
        functools.partial(_linear_kernel, has_norm=gain is not None, has_res=res is not None, n_w=len(weights)),
        out_shape=[jax.ShapeDtypeStruct((t, w.shape[1]), F32) for w in weights],
        grid=(t // tm,),
        in_specs=specs,
        out_specs=[pl.BlockSpec((tm, w.shape[1]), lambda i: (i, 0)) for w in weights],
        compiler_params=_params(("arbitrary",)),
        name="linear",
    )(*ins)
    return outs


def _cumsum_chunks(x):
    pos = lax.broadcasted_iota(jnp.int32, x.shape, 0) % CHUNK
    step = 1
    while step < CHUNK:
        x = x + jnp.where(pos >= step, pltpu.roll(x, step, axis=0), 0.0)
        step *= 2
    return x


def _mixp_project(tile, h_ref, g_ref, wa_ref, wkt_ref, wxbc_ref, wdt_ref, cos_ref, sin_ref, cost_ref, sint_ref,
                  qdec_ref, kdec_ref, cw_ref, cb_ref, dtb_ref, alog_ref, conv_ref, xbc_s, stage, *, tm, nt):
    q_s, qd_s, kr_s, kd_s, v_s, g_s, z_s, xc_s, pk_s = stage
    row_start = (tile % nt) == 0

    u = _rmsnorm(h_ref[...], g_ref[...]).astype(BF16)
    qvgz = _dot(u, wa_ref[...])
    kt = _dot_nt(wkt_ref[...], u)
    xbc = _dot(u, wxbc_ref[...])
    dt_raw = _dot(u, wdt_ref[...])

    for sl in range(CONV_SLABS):
        lanes = slice(sl * V7X_LANES, (sl + 1) * V7X_LANES)
        xbc_s[sl, 0:CONV_HIST, :] = jnp.where(row_start, 0.0, xbc_s[sl, 0:CONV_HIST, :])
        xbc_s[sl, CONV_HIST:CONV_HIST + tm, :] = xbc[:, lanes]
        conv = cb_ref[:, lanes]
        for i in range(CONV_WIDTH):
            off = CONV_HIST - (CONV_WIDTH - 1) + i
            conv = conv + cw_ref[i:i + 1, lanes] * xbc_s[sl, off:off + tm, :]
        xc_s[sl] = jax.nn.silu(conv)
        conv_ref[0, :, lanes] = xbc_s[sl, CONV_HIST + tm - (CONV_WIDTH - 1):CONV_HIST + tm, :]
        xbc_s[sl, 0:CONV_HIST, :] = xbc_s[sl, tm:tm + CONV_HIST, :]

    dt = _softplus(dt_raw + dtb_ref[...])
    a_cum = _cumsum_chunks(dt * (-jnp.exp(alog_ref[...])))
    head_lanes = lax.broadcasted_iota(jnp.int32, (tm, DT_PAD), 1) < SSD_HEADS
    pk_s[...] = jnp.where(head_lanes, a_cum, dt)

    cos2, sin2 = cos_ref[...], sin_ref[...]
    cos_t, sin_t = cost_ref[...], sint_ref[...]
    half = RET_DK // 2
    for hh in range(RET_HEADS):
        lanes = slice(hh * RET_DK, (hh + 1) * RET_DK)
        qr = _rotary(qvgz[:, PQ_OFF + hh * RET_DK:PQ_OFF + (hh + 1) * RET_DK], cos2, sin2)
        q_s[:, lanes] = qr.astype(BF16)
        qd_s[:, lanes] = (qr * qdec_ref[hh]).astype(BF16)
        k1 = kt[hh * RET_DK:hh * RET_DK + half, :]
        k2 = kt[hh * RET_DK + half:(hh + 1) * RET_DK, :]
        kr = jnp.concatenate([k1 * cos_t - k2 * sin_t, k1 * sin_t + k2 * cos_t], axis=0) * (RET_DK ** -0.5)
        kr_s[lanes, :] = kr.astype(BF16)
        kd_s[lanes, :] = (kr * kdec_ref[hh:hh + 1, :]).astype(BF16)
    v_s[...] = qvgz[:, PV_OFF:PV_OFF + RET_WIDTH].astype(BF16)
    g_s[...] = jax.nn.silu(qvgz[:, PG_OFF:PG_OFF + RET_WIDTH])
    z_s[...] = jax.nn.silu(qvgz[:, PZ_OFF:PZ_OFF + SSD_WIDTH])


def _mixp_heads(tile, h_ref, dmat_ref, cdec_ref, gn_ref, dsk_ref, sn_ref, wout_ref, o_ref, ret_ref, ssm_ref,
                mix_s, sret_s, sssm_s, stage, *, tm, nt):
    q_s, qd_s, kr_s, kd_s, v_s, g_s, z_s, xc_s, pk_s = stage
    nc = tm // CHUNK
    row_start = (tile % nt) == 0

    row_i = lax.broadcasted_iota(jnp.int32, (CHUNK, CHUNK), 0)
    col_i = lax.broadcasted_iota(jnp.int32, (CHUNK, CHUNK), 1)
    causal = row_i >= col_i
    low_lanes = col_i < SSD_HEAD_DIM
    bd_rows = lax.broadcasted_iota(jnp.int32, (2 * SSD_STATE, 2 * SSD_HEAD_DIM), 0)
    bd_cols = lax.broadcasted_iota(jnp.int32, (2 * SSD_STATE, 2 * SSD_HEAD_DIM), 1)
    bd_mask = (bd_rows < SSD_STATE) == (bd_cols < SSD_HEAD_DIM)

    inner, kv = {}, {}
    scores, c_exp, st_inc, st_dec = {}, {}, {}, {}
    for c in range(nc):
        rows = slice(c * CHUNK, (c + 1) * CHUNK)
        for hh in range(RET_HEADS):
            lanes = slice(hh * RET_DK, (hh + 1) * RET_DK)
            inner[c, hh] = (_dot(q_s[rows, lanes], kr_s[lanes, rows]) * dmat_ref[hh]).astype(BF16)
            kv[c, hh] = _dot(kd_s[lanes, rows], v_s[rows, lanes])

        pk = pk_s[rows, :]
        pk_t = pk.T
        a_t = pk_t[0:SSD_HEADS, :]
        dt_t = pk_t[SSD_HEADS:2 * SSD_HEADS, :]
        a_last = a_t[:, CHUNK - 1:CHUNK]
        w_t = jnp.exp(a_last - a_t) * dt_t
        chunk_dec = jnp.exp(a_last)
        for grp in range(SSD_GROUPS):
            cg = xc_s[SSD_PAIRS + SSD_GROUPS + grp, rows, :]
            b_t = xc_s[SSD_PAIRS + grp, rows, :].T
            gmat = _dot(cg.astype(BF16), b_t.astype(BF16))
            for pj in range(SSD_PAIRS // SSD_GROUPS):
                j = grp * (SSD_PAIRS // SSD_GROUPS) + pj
                heads = (2 * j, 2 * j + 1)
                x_pair = xc_s[j, rows, :].astype(BF16)
                b_w = jnp.concatenate([b_t * w_t[hd:hd + 1, :] for hd in heads], axis=0).astype(BF16)
                st_inc[c, j] = jnp.where(bd_mask, _dot(b_w, x_pair), 0.0)
                st_dec[c, j] = jnp.concatenate(
                    [jnp.broadcast_to(chunk_dec[hd:hd + 1, :], (SSD_STATE, 2 * SSD_HEAD_DIM)) for hd in heads], axis=0)
                for hd in heads:
                    a_col = jnp.broadcast_to(pk[:, hd:hd + 1], (CHUNK, CHUNK))
                    seg = a_col - a_t[hd:hd + 1, :]
                    lmat = jnp.where(causal, jnp.exp(jnp.where(causal, seg, 0.0)), 0.0)
                    scores[c, hd] = (gmat * lmat * dt_t[hd:hd + 1, :]).astype(BF16)
                    c_exp[c, hd] = (cg * jnp.exp(a_col)).astype(BF16)

    s_in, st_in = {}, {}
    for hh in range(RET_HEADS):
        s = jnp.where(row_start, 0.0, sret_s[hh])
        for c in range(nc):
            s_in[c, hh] = s.astype(BF16)
            s = cdec_ref[hh:hh + 1, :] * s + kv[c, hh]
        sret_s[hh] = s
        ret_ref[0, hh] = s
    for j in range(SSD_PAIRS):
        s = jnp.where(row_start, 0.0, sssm_s[j])
        for c in range(nc):
            st_in[c, j] = s.astype(BF16)
            s = st_dec[c, j] * s + st_inc[c, j]
        sssm_s[j] = s
        ssm_ref[0, j * 2 * SSD_HEAD_DIM:(j + 1) * 2 * SSD_HEAD_DIM, :] = (s[0:SSD_STATE] + s[SSD_STATE:]).T

    for c in range(nc):
        rows = slice(c * CHUNK, (c + 1) * CHUNK)
        for hh in range(RET_HEADS):
            lanes = slice(hh * RET_DV, (hh + 1) * RET_DV)
            lhs = jnp.concatenate([inner[c, hh], qd_s[rows, lanes]], axis=1)
            rhs = jnp.concatenate([v_s[rows, lanes], s_in[c, hh]], axis=0)
            r = _group_norm(_dot(lhs, rhs)) * gn_ref[:, lanes]
            mix_s[rows, lanes] = (g_s[rows, lanes] * r).astype(BF16)
        ys = []
        for j in range(SSD_PAIRS):
            heads = (2 * j, 2 * j + 1)
            x_pair = xc_s[j, rows, :]
            x_bd = jnp.concatenate([jnp.where(low_lanes, x_pair, 0.0), jnp.where(low_lanes, 0.0, x_pair)], axis=0)
            lhs = jnp.concatenate([scores[c, heads[0]], scores[c, heads[1]], c_exp[c, heads[0]], c_exp[c, heads[1]]],
                                  axis=1)
            rhs = jnp.concatenate([x_bd.astype(BF16), st_in[c, j]], axis=0)
            ys.append(_dot(lhs, rhs) + dsk_ref[:, j * 2 * SSD_HEAD_DIM:(j + 1) * 2 * SSD_HEAD_DIM] * x_pair)
        y_all = jnp.concatenate(ys, axis=1) * z_s[rows, :]
        mix_s[rows, RET_WIDTH:RET_WIDTH + SSD_WIDTH] = _rmsnorm(y_all, sn_ref[...]).astype(BF16)

    o_ref[...] = h_ref[...] + _dot(mix_s[...], wout_ref[...])


def _mix_prompt_kernel(h_ref, hp_ref, g_ref, wa_ref, wkt_ref, wxbc_ref, wdt_ref, cos_ref, sin_ref, cost_ref, sint_ref,
                       dmat_ref, qdec_ref, kdec_ref, cdec_ref, gn_ref, cw_ref, cb_ref, dtb_ref, alog_ref, dsk_ref,
                       sn_ref, wout_ref,
                       o_ref, ret_ref, ssm_ref, conv_ref,
                       xbc_s, mix_s, sret_s, sssm_s, *stage_s, tm, nt, n_tiles):
    s = pl.program_id(0)
    sets = (tuple(r.at[0] for r in stage_s), tuple(r.at[1] for r in stage_s))

    @pl.when(s == 0)
    def _():
        for r in stage_s:
            r[...] = jnp.zeros_like(r)
        sret_s[...] = jnp.zeros_like(sret_s)
        sssm_s[...] = jnp.zeros_like(sssm_s)
        xbc_s[...] = jnp.zeros_like(xbc_s)

    def step(write_set, read_set):
        _mixp_project(jnp.minimum(s, n_tiles - 1), h_ref, g_ref, wa_ref, wkt_ref, wxbc_ref, wdt_ref, cos_ref, sin_ref,
                      cost_ref, sint_ref, qdec_ref, kdec_ref, cw_ref, cb_ref, dtb_ref, alog_ref, conv_ref, xbc_s,
                      write_set, tm=tm, nt=nt)
        _mixp_heads(jnp.maximum(s - 1, 0), hp_ref, dmat_ref, cdec_ref, gn_ref, dsk_ref, sn_ref, wout_ref, o_ref, ret_ref,
                    ssm_ref, mix_s, sret_s, sssm_s, read_set, tm=tm, nt=nt)

    @pl.when(s % 2 == 0)
    def _():
        step(sets[0], sets[1])

    @pl.when(s % 2 == 1)
    def _():
        step(sets[1], sets[0])


def _mix_prompt(h, gain, w_a, w_kt, w_xbc, w_dt, cos2, sin2, cos_t, sin_t, dmat, qdec, kdec, cdec, gn, conv_w, conv_b,
                dt_bias, a_log, dskip, ssd_gain, w_out, *, batch, seq, tm):
    nt = seq // tm
    n_tiles = batch * nt
    d = h.shape[1]
    cur = lambda s: jnp.minimum(s, n_tiles - 1)
    prev = lambda s: jnp.maximum(s - 1, 0)
    row_cur = pl.BlockSpec((tm, d), lambda s: (cur(s), 0))
    row_prev = pl.BlockSpec((tm, d), lambda s: (prev(s), 0))
    pos = pl.BlockSpec((tm, RET_DK), lambda s: (cur(s) % nt, 0))
    pos_t = pl.BlockSpec((RET_DK // 2, tm), lambda s: (0, cur(s) % nt))
    consts = [gain, w_a, w_kt, w_xbc, w_dt]
    tail = [dmat, qdec, kdec, cdec, gn, conv_w, conv_b, dt_bias, a_log, dskip, ssd_gain, w_out]
    stage = [
        pltpu.VMEM((2, tm, RET_HEADS * RET_DK), BF16),
        pltpu.VMEM((2, tm, RET_HEADS * RET_DK), BF16),
        pltpu.VMEM((2, RET_HEADS * RET_DK, tm), BF16),
        pltpu.VMEM((2, RET_HEADS * RET_DK, tm), BF16),
        pltpu.VMEM((2, tm, RET_WIDTH), BF16),
        pltpu.VMEM((2, tm, RET_WIDTH), F32),
        pltpu.VMEM((2, tm, SSD_WIDTH), F32),
        pltpu.VMEM((2, CONV_SLABS, tm, V7X_LANES), F32),
        pltpu.VMEM((2, tm, DT_PAD), F32),
    ]
    return pl.pallas_call(
        functools.partial(_mix_prompt_kernel, tm=tm, nt=nt, n_tiles=n_tiles),
        out_shape=[
            jax.ShapeDtypeStruct(h.shape, F32),
            jax.ShapeDtypeStruct((batch, RET_HEADS, RET_DK, RET_DV), F32),
            jax.ShapeDtypeStruct((batch, SSD_WIDTH, SSD_STATE), F32),
            jax.ShapeDtypeStruct((batch, CONV_WIDTH - 1, CONV_CH), F32),
        ],
        grid=(n_tiles + 1,),
        in_specs=([row_cur, row_prev] + [_resident(a.shape) for a in consts] + [pos, pos, pos_t, pos_t]
                  + [_resident(a.shape) for a in tail]),
        out_specs=[
            row_prev,
            pl.BlockSpec((1, RET_HEADS, RET_DK, RET_DV), lambda s: (prev(s) // nt, 0, 0, 0)),
            pl.BlockSpec((1, SSD_WIDTH, SSD_STATE), lambda s: (prev(s) // nt, 0, 0)),
            pl.BlockSpec((1, CONV_WIDTH - 1, CONV_CH), lambda s: (cur(s) // nt, 0, 0)),
        ],
        scratch_shapes=[
            pltpu.VMEM((CONV_SLABS, CONV_HIST + tm, V7X_LANES), F32),
            pltpu.VMEM((tm, RET_WIDTH + SSD_WIDTH), BF16),
            pltpu.VMEM((RET_HEADS, RET_DK, RET_DV), F32),
            pltpu.VMEM((SSD_PAIRS, 2 * SSD_STATE, 2 * SSD_HEAD_DIM), F32),
        ] + stage,
        compiler_params=_params(("arbitrary",)),
        name="mix_prompt",
    )(h, h, *consts, cos2, sin2, cos_t, sin_t, *tail)


def _split_hi_lo(x):
    hi = x.astype(BF16).astype(F32)
    return hi, x - hi


def _mix_sample_kernel(proj_ref, sret_ref, sssm_ref, sconv_ref, cos_ref, sin_ref, gam_ref,
                       gn_ref, cw_ref, cb_ref, dtb_ref, alog_ref, dsk_ref, sn_ref,
                       mix_ref, oret_ref, ossm_ref, oconv_ref, *, bs):
    proj = proj_ref[...]
    xbc = proj[:, XBC_OFF:DT_OFF]
    hist = sconv_ref[...]
    taps = [hist[:, i * CONV_CH:(i + 1) * CONV_CH] for i in range(CONV_WIDTH - 1)] + [xbc]
    conv = cb_ref[...]
    for i in range(CONV_WIDTH):
        conv = conv + cw_ref[i:i + 1, :] * taps[i]
    oconv_ref[...] = jnp.concatenate(taps[1:], axis=1)
    xc = jax.nn.silu(conv)

    dt = _softplus(proj[:, DT_OFF:IN_PROJ_PAD] + dtb_ref[...])

    sub_i = lax.broadcasted_iota(jnp.int32, (bs, V7X_LANES), 0)
    row_of = lax.broadcasted_iota(jnp.int32, (4 * bs, V7X_LANES), 0) % bs
    cos2 = cos_ref[...]
    sin2 = sin_ref[...]

    def outer_lhs(x):
        hi, lo = _split_hi_lo(x)
        return jnp.concatenate([hi, hi, lo, lo], axis=0)

    def outer_rhs(x):
        hi, lo = _split_hi_lo(x)
        return jnp.concatenate([hi, lo, hi, lo], axis=0).astype(BF16)

    def only_sample(x4, b):
        return jnp.where(row_of == b, x4, 0.0).astype(BF16)

    for hh in range(RET_HEADS):
        lanes = slice(hh * RET_DK, (hh + 1) * RET_DK)
        qr = _rotary(proj[:, Q_OFF + hh * RET_DK:Q_OFF + (hh + 1) * RET_DK], cos2, sin2)
        kr = _rotary(proj[:, K_OFF + hh * RET_DK:K_OFF + (hh + 1) * RET_DK], cos2, sin2) * (RET_DK ** -0.5)
        vh = proj[:, V_OFF + hh * RET_DV:V_OFF + (hh + 1) * RET_DV]
        k4, v4, q_bf = outer_lhs(kr), outer_rhs(vh), qr.astype(BF16)
        gamma = gam_ref[hh:hh + 1, :]
        y = jnp.zeros((bs, RET_DV), F32)
        for b in range(bs):
            s_old = sret_ref[b, hh]
            oret_ref[b, hh] = gamma * s_old + _dot_tn(only_sample(k4, b), v4)
            y = jnp.where(sub_i == b, _dot(q_bf, s_old.astype(BF16)), y)
        y = gamma * y + jnp.sum(qr * kr, axis=-1, keepdims=True) * vh
        r = _group_norm(y) * gn_ref[:, lanes]
        mix_ref[:, lanes] = jax.nn.silu(proj[:, G_OFF + hh * RET_DV:G_OFF + (hh + 1) * RET_DV]) * r

    xs = xc[:, 0:SSD_WIDTH]
    head_of_lane = lax.broadcasted_iota(jnp.int32, (bs, SSD_WIDTH), 1) // SSD_HEAD_DIM
    dec = jnp.exp(dt * (-jnp.exp(alog_ref[...])))
    dt_wide = jnp.zeros((bs, SSD_WIDTH), F32)
    dec_wide = jnp.zeros((bs, SSD_WIDTH), F32)
    for hd in range(SSD_HEADS):
        dt_wide = jnp.where(head_of_lane == hd, dt[:, hd:hd + 1], dt_wide)
        dec_wide = jnp.where(head_of_lane == hd, dec[:, hd:hd + 1], dec_wide)
    xdt = xs * dt_wide
    x4 = outer_lhs(xdt)
    ys = []
    for j in range(SSD_PAIRS):
        grp = j // (SSD_PAIRS // SSD_GROUPS)
        lanes = slice(j * 2 * SSD_HEAD_DIM, (j + 1) * 2 * SSD_HEAD_DIM)
        bg = xc[:, SSD_WIDTH + grp * SSD_STATE:SSD_WIDTH + (grp + 1) * SSD_STATE]
        c_off = SSD_WIDTH + SSD_GROUPS * SSD_STATE + grp * SSD_STATE
        cg = xc[:, c_off:c_off + SSD_STATE]
        b4, c_bf = outer_rhs(bg), cg.astype(BF16)
        y = jnp.zeros((bs, 2 * SSD_HEAD_DIM), F32)
        for b in range(bs):
            s_old = sssm_ref[b, lanes, :]
            dec_rows = jnp.concatenate(
                [jnp.broadcast_to(dec[b:b + 1, hd:hd + 1], (SSD_HEAD_DIM, SSD_STATE)) for hd in (2 * j, 2 * j + 1)],
                axis=0)
            ossm_ref[b, lanes, :] = dec_rows * s_old + _dot_tn(only_sample(x4[:, lanes], b), b4)
            y = jnp.where(sub_i == b, _dot_nt(c_bf, s_old.astype(BF16)), y)
        ys.append(dec_wide[:, lanes] * y + jnp.sum(cg * bg, axis=-1, keepdims=True) * xdt[:, lanes])
    ys = (jnp.concatenate(ys, axis=1) + dsk_ref[...] * xs) * jax.nn.silu(proj[:, Z_OFF:Z_OFF + SSD_WIDTH])
    mix_ref[:, RET_WIDTH:RET_WIDTH + SSD_WIDTH] = _rmsnorm(ys, sn_ref[...])


def _mix_sample(proj, s_ret, s_ssm, s_conv, cos2, sin2, gamma, gn, conv_w, conv_b, dt_bias, a_log, dskip, ssd_gain,
                *, bs):
    n = proj.shape[0]
    consts = [cos2, sin2, gamma, gn, conv_w, conv_b, dt_bias, a_log, dskip, ssd_gain]
    blk2 = lambda w: pl.BlockSpec((bs, w), lambda i: (i, 0))
    ret_blk = pl.BlockSpec((bs, RET_HEADS, RET_DK, RET_DV), lambda i: (i, 0, 0, 0))
    ssm_blk = pl.BlockSpec((bs, SSD_WIDTH, SSD_STATE), lambda i: (i, 0, 0))
    return pl.pallas_call(
        functools.partial(_mix_sample_kernel, bs=bs),
        out_shape=[
            jax.ShapeDtypeStruct((n, RET_WIDTH + SSD_WIDTH), F32),
            jax.ShapeDtypeStruct(s_ret.shape, F32),
            jax.ShapeDtypeStruct(s_ssm.shape, F32),
            jax.ShapeDtypeStruct(s_conv.shape, F32),
        ],
        grid=(n // bs,),
        in_specs=[blk2(proj.shape[1]), ret_blk, ssm_blk, blk2(s_conv.shape[1])] + [_resident(a.shape) for a in consts],
        out_specs=[blk2(RET_WIDTH + SSD_WIDTH), ret_blk, ssm_blk, blk2(s_conv.shape[1])],
        compiler_params=_params(("arbitrary",)),
        name="mix_sample",
    )(proj, s_ret, s_ssm, s_conv, *consts)


def _softmax_rows(s):
    m = jnp.max(s, axis=-1, keepdims=True)
    p = jnp.exp(s - m)
    return p / jnp.sum(p, axis=-1, keepdims=True)


def _xattn_prompt_kernel(h_ref, g_ref, wq_ref, mk_ref, mv_ref, wo_ref, o_ref, att_s):
    h = h_ref[...]
    c = _rmsnorm(h, g_ref[...]).astype(BF16)
    qx = _dot(c, wq_ref[...]).astype(BF16)
    for hh in range(X_HEADS):
        lanes = slice(hh * X_HEAD_DIM, (hh + 1) * X_HEAD_DIM)
        s = _dot_nt(qx[:, lanes], mk_ref[0, :, lanes].astype(BF16)) * (X_HEAD_DIM ** -0.5)
        att = _softmax_rows(s).astype(BF16)
        att_s[:, lanes] = _dot(att, mv_ref[0, :, lanes].astype(BF16)).astype(BF16)
    o_ref[...] = h + _dot(att_s[...], wo_ref[...])


def _xattn_prompt(h, gain, w_q, mem_k, mem_v, w_o, *, batch, seq, tm):
    nt = seq // tm
    d = h.shape[1]
    row = pl.BlockSpec((tm, d), lambda b, t: (b * nt + t, 0))
    mem = pl.BlockSpec((1, MEM_TOKENS, d), lambda b, t: (b, 0, 0))
    return pl.pallas_call(
        _xattn_prompt_kernel,
        out_shape=jax.ShapeDtypeStruct(h.shape, F32),
        grid=(batch, nt),
        in_specs=[row, _resident(gain.shape), _resident(w_q.shape), mem, mem, _resident(w_o.shape)],
        out_specs=row,
        scratch_shapes=[pltpu.VMEM((tm, d), BF16)],
        compiler_params=_params(("arbitrary", "arbitrary")),
        name="xattn_prompt",
    )(h, gain, w_q, mem_k, mem_v, w_o)


def _xattn_sample_kernel(q_ref, k_ref, v_ref, o_ref, *, bs):
    rows = MEM_TOKENS * X_HEADS
    own = (lax.broadcasted_iota(jnp.int32, (X_HEADS, rows), 1) % X_HEADS
           == lax.broadcasted_iota(jnp.int32, (X_HEADS, rows), 0))
    for b in range(bs):
        q = q_ref[b].astype(BF16)
        k = k_ref[b].reshape(rows, X_HEAD_DIM).astype(BF16)
        v = v_ref[b].reshape(rows, X_HEAD_DIM).astype(BF16)
        s = _dot_nt(q, k) * (X_HEAD_DIM ** -0.5)
        att = _softmax_rows(jnp.where(own, s, -jnp.inf)).astype(BF16)
        o_ref[b] = _dot(att, v)


def _xattn_sample(q, mem_k, mem_v, *, bs):
    n = q.shape[0]
    row = pl.BlockSpec((bs, X_HEADS, X_HEAD_DIM), lambda i: (i, 0, 0))
    mem = pl.BlockSpec((bs, MEM_TOKENS, X_HEADS, X_HEAD_DIM), lambda i: (i, 0, 0, 0))
    return pl.pallas_call(
        functools.partial(_xattn_sample_kernel, bs=bs),
        out_shape=jax.ShapeDtypeStruct((n, X_HEADS, X_HEAD_DIM), F32),
        grid=(n // bs,),
        in_specs=[row, mem, mem],
        out_specs=row,
        compiler_params=_params(("arbitrary",)),
        name="xattn_sample",
    )(q, mem_k, mem_v)


def _rope_angles(pos):
    half = RET_DK // 2
    inv_freq = ROPE_BASE ** (-jnp.arange(half, dtype=F32) / half)
    ang = pos.astype(F32)[:, None] * inv_freq[None, :]
    return jnp.cos(ang), jnp.sin(ang)


def _rope_tables(pos):
    cos, sin = _rope_angles(pos)
    return jnp.concatenate([cos, cos], axis=-1), jnp.concatenate([-sin, sin], axis=-1)


def _retention_decay_tables(chunk):
    log_g = jnp.log1p(-jnp.exp2(-5.0 - jnp.arange(RET_HEADS, dtype=F32)))
    idx = jnp.arange(chunk, dtype=F32)
    diff = idx[:, None] - idx[None, :]
    causal = diff >= 0
    dmat = jnp.where(causal[None], jnp.exp(log_g[:, None, None] * jnp.where(causal, diff, 0.0)[None]), 0.0)
    q_dec = jnp.exp(log_g[:, None] * (idx[None, :] + 1.0))
    k_dec = jnp.exp(log_g[:, None] * (chunk - 1.0 - idx[None, :]))
    c_dec = jnp.exp(log_g * chunk)
    wide = lambda x: jnp.broadcast_to(x[..., None], x.shape + (V7X_LANES,))
    return dmat, wide(q_dec), k_dec, wide(c_dec)


def kernel(x_prompt, x_sample, mem_prompt, state_ret, state_ssm, state_conv, cache_mem_k, cache_mem_v, ffn1_norm,
           ffn1_w1, ffn1_w3, ffn1_w2, mix_norm, w_in, ret_gn_gain, conv_w, conv_b, dt_bias, A_log, D_skip, ssd_norm,
           w_out, x_norm, mem_norm, w_xq, w_xk, w_xv, w_xo, ffn2_norm, ffn2_w1, ffn2_w3, ffn2_w2, final_norm):
    bp, lp, d = x_prompt.shape
    bsz = x_sample.shape[0]
    depth = ffn1_w1.shape[0]
    row = lambda v: v.reshape(1, -1).astype(F32)
    lane_pad = lambda v: jnp.pad(row(v), ((0, 0), (0, DT_PAD - v.shape[-1])))

    cos_p, sin_p = _rope_tables(jnp.arange(lp))
    cos_pt, sin_pt = (a.T for a in _rope_angles(jnp.arange(lp)))
    cos_s, sin_s = _rope_tables(PAST_LEN + jnp.arange(x_sample.shape[1]))
    dmat, q_dec, k_dec, c_dec = _retention_decay_tables(CHUNK)
    gamma1 = _retention_decay_tables(1)[3]

    y_p = x_prompt.reshape(bp * lp, d)
    y_s = x_sample.reshape(bsz, d)
    outs = {k: [] for k in ("ret_p", "ssm_p", "conv_p", "memk", "memv", "ret_s", "ssm_s", "conv_s")}
    for l in range(depth):
        bf = lambda w: w[l].astype(BF16)
        w_in_f = w_in[l]
        w_in_l = jnp.pad(w_in_f, ((0, 0), (0, IN_PROJ_PAD - IN_PROJ_WIDTH))).astype(BF16)
        w_a = jnp.concatenate([w_in_f[:, Q_OFF:K_OFF], w_in_f[:, V_OFF:XBC_OFF]], axis=1).astype(BF16)
        w_kt = w_in_f[:, K_OFF:V_OFF].T.astype(BF16)
        w_xbc = w_in_f[:, XBC_OFF:DT_OFF].astype(BF16)
        w_dt = jnp.pad(jnp.tile(w_in_f[:, DT_OFF:], (1, 2)), ((0, 0), (0, DT_PAD - 2 * SSD_HEADS))).astype(BF16)
        f1 = (row(ffn1_norm[l]), bf(ffn1_w1), bf(ffn1_w3), bf(ffn1_w2))
        f2 = (row(ffn2_norm[l]), bf(ffn2_w1), bf(ffn2_w3), bf(ffn2_w2))
        shared = (row(ret_gn_gain[l]), conv_w[l], row(conv_b[l]))
        ssd_tail = (lane_pad(A_log[l]), row(jnp.repeat(D_skip[l], SSD_HEAD_DIM)), row(ssd_norm[l]))
        w_out_l, w_xq_l, w_xo_l = bf(w_out), bf(w_xq), bf(w_xo)

        mk, mv = _linear(mem_prompt.reshape(bp * MEM_TOKENS, d), [bf(w_xk), bf(w_xv)], gain=row(mem_norm[l]),
                         tm=MEM_TOKENS)
        mk = mk.reshape(bp, MEM_TOKENS, d)
        mv = mv.reshape(bp, MEM_TOKENS, d)

        y_p = _ffn(y_p, *f1, tm=512)
        tm_p = 512
        y_p, ret_p, ssm_p, conv_p = _mix_prompt(
            y_p, row(mix_norm[l]), w_a, w_kt, w_xbc, w_dt, cos_p, sin_p, cos_pt, sin_pt, dmat,
            jnp.tile(q_dec, (1, tm_p // CHUNK, 1)), jnp.tile(k_dec, (1, tm_p // CHUNK)), c_dec,
            *shared, lane_pad(jnp.tile(dt_bias[l], 2)), *ssd_tail, w_out_l, batch=bp, seq=lp, tm=tm_p)
        y_p = _xattn_prompt(y_p, row(x_norm[l]), w_xq_l, mk, mv, w_xo_l, batch=bp, seq=lp, tm=512)
        y_p = _ffn(y_p, *f2, final_gain=row(final_norm) if l == depth - 1 else None, tm=512)

        y_s = _ffn(y_s, *f1, tm=bsz)
        proj_s, = _linear(y_s, [w_in_l], gain=row(mix_norm[l]), tm=bsz)
        mix_s, ret_s, ssm_s, conv_s = _mix_sample(
            proj_s, state_ret[l], state_ssm[l].reshape(bsz, SSD_WIDTH, SSD_STATE),
            state_conv[l].reshape(bsz, (CONV_WIDTH - 1) * CONV_CH),
            cos_s, sin_s, gamma1, *shared, lane_pad(dt_bias[l]), *ssd_tail, bs=8)
        y_s, = _linear(mix_s, [w_out_l], res=y_s, tm=bsz)
        q_s, = _linear(y_s, [w_xq_l], gain=row(x_norm[l]), tm=bsz)
        att_s = _xattn_sample(q_s.reshape(bsz, X_HEADS, X_HEAD_DIM), cache_mem_k[l], cache_mem_v[l], bs=8)
        y_s, = _linear(att_s.reshape(bsz, d), [w_xo_l], res=y_s, tm=bsz)
        y_s = _ffn(y_s, *f2, final_gain=row(final_norm) if l == depth - 1 else None, tm=bsz)

        outs["ret_p"].append(ret_p)
        outs["ssm_p"].append(ssm_p.reshape(bp, SSD_HEADS, SSD_HEAD_DIM, SSD_STATE))
        outs["conv_p"].append(conv_p)
        outs["memk"].append(mk.reshape(bp, MEM_TOKENS, X_HEADS, X_HEAD_DIM))
        outs["memv"].append(mv.reshape(bp, MEM_TOKENS, X_HEADS, X_HEAD_DIM))
        outs["ret_s"].append(ret_s)
        outs["ssm_s"].append(ssm_s.reshape(bsz, SSD_HEADS, SSD_HEAD_DIM, SSD_STATE))
        outs["conv_s"].append(conv_s.reshape(bsz, CONV_WIDTH - 1, CONV_CH))

    st = lambda k: jnp.stack(outs[k])
    return (y_p.reshape(bp, lp, d), y_s.reshape(bsz, x_sample.shape[1], d), st("ret_p"), st("ssm_p"), st("conv_p"),
            st("memk"), st("memv"), st("ret_s"), st("ssm_s"), st("conv_s"))
```

```python
import functools

import jax
import jax.numpy as jnp
from jax import lax
from jax.experimental import pallas as pl
from jax.experimental.pallas import tpu as pltpu

F32 = jnp.float32
BF16 = jnp.bfloat16

D_MODEL = 1024
D_FF = 2816
PAST_LEN = 16384
RET_HEADS = 4
RET_DK = 128
RET_DV = 128
RET_WIDTH = RET_HEADS * RET_DV
SSD_HEADS = 8
SSD_HEAD_DIM = 64
SSD_WIDTH = SSD_HEADS * SSD_HEAD_DIM
SSD_GROUPS = 2
SSD_STATE = 128
SSD_PAIRS = SSD_HEADS // 2
CONV_WIDTH = 4
CONV_CH = SSD_WIDTH + 2 * SSD_GROUPS * SSD_STATE
CHUNK = 128
MEM_TOKENS = 256
X_HEADS = 4
X_HEAD_DIM = D_MODEL // X_HEADS
ROPE_BASE = 10000.0
EPS = 1e-6

Q_OFF = 0
K_OFF = Q_OFF + RET_HEADS * RET_DK
V_OFF = K_OFF + RET_HEADS * RET_DK
G_OFF = V_OFF + RET_WIDTH
Z_OFF = G_OFF + RET_WIDTH
XBC_OFF = Z_OFF + SSD_WIDTH
DT_OFF = XBC_OFF + CONV_CH
IN_PROJ_WIDTH = DT_OFF + SSD_HEADS

PQ_OFF = 0
PV_OFF = PQ_OFF + RET_HEADS * RET_DK
PG_OFF = PV_OFF + RET_WIDTH
PZ_OFF = PG_OFF + RET_WIDTH
P_WIDTH = PZ_OFF + SSD_WIDTH

V7X_LANES = 128
V7X_SUBLANES = 8
V7X_VMEM_LIMIT_BYTES = 56 * 1024 * 1024
DT_PAD = V7X_LANES
IN_PROJ_PAD = DT_OFF + DT_PAD
CONV_HIST = V7X_SUBLANES
CONV_SLABS = CONV_CH // V7X_LANES


def _params(sem):
    return pltpu.CompilerParams(dimension_semantics=sem, vmem_limit_bytes=V7X_VMEM_LIMIT_BYTES)


def _resident(shape):
    zeros = (0,) * len(shape)
    return pl.BlockSpec(shape, lambda *_: zeros, pipeline_mode=pl.Buffered(1))


def _rmsnorm(x, gain):
    ms = jnp.mean(x * x, axis=-1, keepdims=True)
    return x * lax.rsqrt(ms + EPS) * gain


def _dot(a, b):
    return jnp.dot(a, b, preferred_element_type=F32)


def _dot_nt(a, b):
    return lax.dot_general(a, b, (((1,), (1,)), ((), ())), preferred_element_type=F32)


def _dot_tn(a, b):
    return lax.dot_general(a, b, (((0,), (0,)), ((), ())), preferred_element_type=F32)


def _softplus(x):
    return jnp.maximum(x, 0.0) + jnp.log1p(jnp.exp(-jnp.abs(x)))


def _rotary(x, cos2, sin2):
    return x * cos2 + pltpu.roll(x, RET_DK // 2, axis=1) * sin2


def _group_norm(y):
    mu = jnp.mean(y, axis=-1, keepdims=True)
    d = y - mu
    var = jnp.mean(d * d, axis=-1, keepdims=True)
    return d * lax.rsqrt(var + EPS)


def _ffn_kernel(*refs, final_norm):
    if final_norm:
        x_ref, g_ref, w1_ref, w3_ref, w2_ref, fg_ref, o_ref = refs
    else:
        x_ref, g_ref, w1_ref, w3_ref, w2_ref, o_ref = refs
    x = x_ref[...]
    xn = _rmsnorm(x, g_ref[...]).astype(BF16)
    a = _dot(xn, w1_ref[...])
    b = _dot(xn, w3_ref[...])
    hidden = (jax.nn.silu(a) * b).astype(BF16)
    out = x + 0.5 * _dot(hidden, w2_ref[...])
    if final_norm:
        out = _rmsnorm(out, fg_ref[...])
    o_ref[...] = out


def _ffn(x, gain, w1, w3, w2, final_gain=None, *, tm):
    t, d = x.shape
    row = pl.BlockSpec((tm, d), lambda i: (i, 0))
    ins = [x, gain, w1, w3, w2]
    specs = [row, _resident(gain.shape), _resident(w1.shape), _resident(w3.shape), _resident(w2.shape)]
    if final_gain is not None:
        ins.append(final_gain)
        specs.append(_resident(final_gain.shape))
    return pl.pallas_call(
        functools.partial(_ffn_kernel, final_norm=final_gain is not None),
        out_shape=jax.ShapeDtypeStruct((t, d), F32),
        grid=(t // tm,),
        in_specs=specs,
        out_specs=row,
        compiler_params=_params(("arbitrary",)),
        name="ffn",
    )(*ins)


def _linear_kernel(*refs, has_norm, has_res, n_w):
    refs = list(refs)
    x_ref = refs.pop(0)
    g_ref = refs.pop(0) if has_norm else None
    r_ref = refs.pop(0) if has_res else None
    w_refs, o_refs = refs[:n_w], refs[n_w:]
    x = x_ref[...]
    if has_norm:
        x = _rmsnorm(x, g_ref[...])
    xb = x.astype(BF16)
    for w_ref, o_ref in zip(w_refs, o_refs, strict=True):
        y = _dot(xb, w_ref[...])
        if has_res:
            y = r_ref[...] + y
        o_ref[...] = y


def _linear(x, weights, gain=None, res=None, *, tm):
    t, k = x.shape
    ins, specs = [x], [pl.BlockSpec((tm, k), lambda i: (i, 0))]
    if gain is not None:
        ins.append(gain)
        specs.append(_resident(gain.shape))
    if res is not None:
        ins.append(res)
        specs.append(pl.BlockSpec((tm, res.shape[1]), lambda i: (i, 0)))
    for w in weights:
        ins.append(w)
        specs.append(_resident(w.shape))
    outs = pl.pallas_call(
        functools.partial(_linear_kernel, has_norm=gain is not None, has_res=res is not None, n_w=len(weights)),
        out_shape=[jax.ShapeDtypeStruct((t, w.shape[1]), F32) for w in weights],
        grid=(t // tm,),
        in_specs=specs,
        out_specs=[pl.BlockSpec((tm, w.shape[1]), lambda i: (i, 0)) for w in weights],
        compiler_params=_params(("arbitrary",)),
        name="linear",
    )(*ins)
    return outs


def _cumsum_chunks(x):
    pos = lax.broadcasted_iota(jnp.int32, x.shape, 0) % CHUNK
    step = 1
    while step < CHUNK:
        x = x + jnp.where(pos >= step, pltpu.roll(x, step, axis=0), 0.0)
        step *= 2
    return x


def _mixp_project(tile, h_ref, g_ref, wa_ref, wkt_ref, wxbc_ref, wdt_ref, cos_ref, sin_ref, cost_ref, sint_ref,
                  qdec_ref, kdec_ref, cw_ref, cb_ref, dtb_ref, alog_ref, conv_ref, xbc_s, stage, *, tm, nt):
    q_s, qd_s, kr_s, kd_s, v_s, g_s, z_s, xc_s, pk_s = stage
    row_start = (tile % nt) == 0

    u = _rmsnorm(h_ref[...], g_ref[...]).astype(BF16)
    qvgz = _dot(u, wa_ref[...])
    kt = _dot_nt(wkt_ref[...], u)
    xbc = _dot(u, wxbc_ref[...])
    dt_raw = _dot(u, wdt_ref[...])

    for sl in range(CONV_SLABS):
        lanes = slice(sl * V7X_LANES, (sl + 1) * V7X_LANES)
        xbc_s[sl, 0:CONV_HIST, :] = jnp.where(row_start, 0.0, xbc_s[sl, 0:CONV_HIST, :])
        xbc_s[sl, CONV_HIST:CONV_HIST + tm, :] = xbc[:, lanes]
        conv = cb_ref[:, lanes]
        for i in range(CONV_WIDTH):
            off = CONV_HIST - (CONV_WIDTH - 1) + i
            conv = conv + cw_ref[i:i + 1, lanes] * xbc_s[sl, off:off + tm, :]
        xc_s[sl] = jax.nn.silu(conv)
        conv_ref[0, :, lanes] = xbc_s[sl, CONV_HIST + tm - (CONV_WIDTH - 1):CONV_HIST + tm, :]
        xbc_s[sl, 0:CONV_HIST, :] = xbc_s[sl, tm:tm + CONV_HIST, :]

    dt = _softplus(dt_raw + dtb_ref[...])
    a_cum = _cumsum_chunks(dt * (-jnp.exp(alog_ref[...])))
    head_lanes = lax.broadcasted_iota(jnp.int32, (tm, DT_PAD), 1) < SSD_HEADS
    pk_s[...] = jnp.where(head_lanes, a_cum, dt)

    cos2, sin2 = cos_ref[...], sin_ref[...]
    cos_t, sin_t = cost_ref[...], sint_ref[...]
    half = RET_DK // 2
    for hh in range(RET_HEADS):
        lanes = slice(hh * RET_DK, (hh + 1) * RET_DK)
        qr = _rotary(qvgz[:, PQ_OFF + hh * RET_DK:PQ_OFF + (hh + 1) * RET_DK], cos2, sin2)
        q_s[:, lanes] = qr.astype(BF16)
        qd_s[:, lanes] = (qr * qdec_ref[hh]).astype(BF16)
        k1 = kt[hh * RET_DK:hh * RET_DK + half, :]
        k2 = kt[hh * RET_DK + half:(hh + 1) * RET_DK, :]
        kr = jnp.concatenate([k1 * cos_t - k2 * sin_t, k1 * sin_t + k2 * cos_t], axis=0) * (RET_DK ** -0.5)
        kr_s[lanes, :] = kr.astype(BF16)
        kd_s[lanes, :] = (kr * kdec_ref[hh:hh + 1, :]).astype(BF16)
    v_s[...] = qvgz[:, PV_OFF:PV_OFF + RET_WIDTH].astype(BF16)
    g_s[...] = jax.nn.silu(qvgz[:, PG_OFF:PG_OFF + RET_WIDTH])
    z_s[...] = jax.nn.silu(qvgz[:, PZ_OFF:PZ_OFF + SSD_WIDTH])


def _mixp_heads(tile, h_ref, dmat_ref, cdec_ref, gn_ref, dsk_ref, sn_ref, wout_ref, o_ref, ret_ref, ssm_ref,
                mix_s, sret_s, sssm_s, stage, *, tm, nt):
    q_s, qd_s, kr_s, kd_s, v_s, g_s, z_s, xc_s, pk_s = stage
    nc = tm // CHUNK
    row_start = (tile % nt) == 0

    row_i = lax.broadcasted_iota(jnp.int32, (CHUNK, CHUNK), 0)
    col_i = lax.broadcasted_iota(jnp.int32, (CHUNK, CHUNK), 1)
    causal = row_i >= col_i
    low_lanes = col_i < SSD_HEAD_DIM
    bd_rows = lax.broadcasted_iota(jnp.int32, (2 * SSD_STATE, 2 * SSD_HEAD_DIM), 0)
    bd_cols = lax.broadcasted_iota(jnp.int32, (2 * SSD_STATE, 2 * SSD_HEAD_DIM), 1)
    bd_mask = (bd_rows < SSD_STATE) == (bd_cols < SSD_HEAD_DIM)

    inner, kv = {}, {}
    scores, c_exp, st_inc, st_dec = {}, {}, {}, {}
    for c in range(nc):
        rows = slice(c * CHUNK, (c + 1) * CHUNK)
        for hh in range(RET_HEADS):
            lanes = slice(hh * RET_DK, (hh + 1) * RET_DK)
            inner[c, hh] = (_dot(q_s[rows, lanes], kr_s[lanes, rows]) * dmat_ref[hh]).astype(BF16)
            kv[c, hh] = _dot(kd_s[lanes, rows], v_s[rows, lanes])

        pk = pk_s[rows, :]
        pk_t = pk.T
        a_t = pk_t[0:SSD_HEADS, :]
        dt_t = pk_t[SSD_HEADS:2 * SSD_HEADS, :]
        a_last = a_t[:, CHUNK - 1:CHUNK]
        w_t = jnp.exp(a_last - a_t) * dt_t
        chunk_dec = jnp.exp(a_last)
        for grp in range(SSD_GROUPS):
            cg = xc_s[SSD_PAIRS + SSD_GROUPS + grp, rows, :]
            b_t = xc_s[SSD_PAIRS + grp, rows, :].T
            gmat = _dot(cg.astype(BF16), b_t.astype(BF16))
            for pj in range(SSD_PAIRS // SSD_GROUPS):
                j = grp * (SSD_PAIRS // SSD_GROUPS) + pj
                heads = (2 * j, 2 * j + 1)
                x_pair = xc_s[j, rows, :].astype(BF16)
                b_w = jnp.concatenate([b_t * w_t[hd:hd + 1, :] for hd in heads], axis=0).astype(BF16)
                st_inc[c, j] = jnp.where(bd_mask, _dot(b_w, x_pair), 0.0)
                st_dec[c, j] = jnp.concatenate(
                    [jnp.broadcast_to(chunk_dec[hd:hd + 1, :], (SSD_STATE, 2 * SSD_HEAD_DIM)) for hd in heads], axis=0)
                for hd in heads:
                    a_col = jnp.broadcast_to(pk[:, hd:hd + 1], (CHUNK, CHUNK))
                    seg = a_col - a_t[hd:hd + 1, :]
                    lmat = jnp.where(causal, jnp.exp(jnp.where(causal, seg, 0.0)), 0.0)
                    scores[c, hd] = (gmat * lmat * dt_t[hd:hd + 1, :]).astype(BF16)
                    c_exp[c, hd] = (cg * jnp.exp(a_col)).astype(BF16)

    s_in, st_in = {}, {}
    for hh in range(RET_HEADS):
        s = jnp.where(row_start, 0.0, sret_s[hh])
        for c in range(nc):
            s_in[c, hh] = s.astype(BF16)
            s = cdec_ref[hh:hh + 1, :] * s + kv[c, hh]
        sret_s[hh] = s
        ret_ref[0, hh] = s
    for j in range(SSD_PAIRS):
        s = jnp.where(row_start, 0.0, sssm_s[j])
        for c in range(nc):
            st_in[c, j] = s.astype(BF16)
            s = st_dec[c, j] * s + st_inc[c, j]
        sssm_s[j] = s
        ssm_ref[0, j * 2 * SSD_HEAD_DIM:(j + 1) * 2 * SSD_HEAD_DIM, :] = (s[0:SSD_STATE] + s[SSD_STATE:]).T

    for c in range(nc):
        rows = slice(c * CHUNK, (c + 1) * CHUNK)
        for hh in range(RET_HEADS):
            lanes = slice(hh * RET_DV, (hh + 1) * RET_DV)
            lhs = jnp.concatenate([inner[c, hh], qd_s[rows, lanes]], axis=1)
            rhs = jnp.concatenate([v_s[rows, lanes], s_in[c, hh]], axis=0)
            r = _group_norm(_dot(lhs, rhs)) * gn_ref[:, lanes]
            mix_s[rows, lanes] = (g_s[rows, lanes] * r).astype(BF16)
        ys = []
        for j in range(SSD_PAIRS):
            heads = (2 * j, 2 * j + 1)
            x_pair = xc_s[j, rows, :]
            x_bd = jnp.concatenate([jnp.where(low_lanes, x_pair, 0.0), jnp.where(low_lanes, 0.0, x_pair)], axis=0)
            lhs = jnp.concatenate([scores[c, heads[0]], scores[c, heads[1]], c_exp[c, heads[0]], c_exp[c, heads[1]]],
                                  axis=1)
            rhs = jnp.concatenate([x_bd.astype(BF16), st_in[c, j]], axis=0)
            ys.append(_dot(lhs, rhs) + dsk_ref[:, j * 2 * SSD_HEAD_DIM:(j + 1) * 2 * SSD_HEAD_DIM] * x_pair)
        y_all = jnp.concatenate(ys, axis=1) * z_s[rows, :]
        mix_s[rows, RET_WIDTH:RET_WIDTH + SSD_WIDTH] = _rmsnorm(y_all, sn_ref[...]).astype(BF16)

    o_ref[...] = h_ref[...] + _dot(mix_s[...], wout_ref[...])


def _mix_prompt_kernel(h_ref, hp_ref, g_ref, wa_ref, wkt_ref, wxbc_ref, wdt_ref, cos_ref, sin_ref, cost_ref, sint_ref,
                       dmat_ref, qdec_ref, kdec_ref, cdec_ref, gn_ref, cw_ref, cb_ref, dtb_ref, alog_ref, dsk_ref,
                       sn_ref, wout_ref,
                       o_ref, ret_ref, ssm_ref, conv_ref,
                       xbc_s, mix_s, sret_s, sssm_s, *stage_s, tm, nt, n_tiles):
    s = pl.program_id(0)
    sets = (tuple(r.at[0] for r in stage_s), tuple(r.at[1] for r in stage_s))

    @pl.when(s == 0)
    def _():
        for r in stage_s:
            r[...] = jnp.zeros_like(r)
        sret_s[...] = jnp.zeros_like(sret_s)
        sssm_s[...] = jnp.zeros_like(sssm_s)
        xbc_s[...] = jnp.zeros_like(xbc_s)

    def step(write_set, read_set):
        _mixp_project(jnp.minimum(s, n_tiles - 1), h_ref, g_ref, wa_ref, wkt_ref, wxbc_ref, wdt_ref, cos_ref, sin_ref,
                      cost_ref, sint_ref, qdec_ref, kdec_ref, cw_ref, cb_ref, dtb_ref, alog_ref, conv_ref, xbc_s,
                      write_set, tm=tm, nt=nt)
        _mixp_heads(jnp.maximum(s - 1, 0), hp_ref, dmat_ref, cdec_ref, gn_ref, dsk_ref, sn_ref, wout_ref, o_ref, ret_ref,
                    ssm_ref, mix_s, sret_s, sssm_s, read_set, tm=tm, nt=nt)

    @pl.when(s % 2 == 0)
    def _():
        step(sets[0], sets[1])

    @pl.when(s % 2 == 1)
    def _():
        step(sets[1], sets[0])


def _mix_prompt(h, gain, w_a, w_kt, w_xbc, w_dt, cos2, sin2, cos_t, sin_t, dmat, qdec, kdec, cdec, gn, conv_w, conv_b,
                dt_bias, a_log, dskip, ssd_gain, w_out, *, batch, seq, tm):
    nt = seq // tm
    n_tiles = batch * nt
    d = h.shape[1]
    cur = lambda s: jnp.minimum(s, n_tiles - 1)
    prev = lambda s: jnp.maximum(s - 1, 0)
    row_cur = pl.BlockSpec((tm, d), lambda s: (cur(s), 0))
    row_prev = pl.BlockSpec((tm, d), lambda s: (prev(s), 0))
    pos = pl.BlockSpec((tm, RET_DK), lambda s: (cur(s) % nt, 0))
    pos_t = pl.BlockSpec((RET_DK // 2, tm), lambda s: (0, cur(s) % nt))
    consts = [gain, w_a, w_kt, w_xbc, w_dt]
    tail = [dmat, qdec, kdec, cdec, gn, conv_w, conv_b, dt_bias, a_log, dskip, ssd_gain, w_out]
    stage = [
        pltpu.VMEM((2, tm, RET_HEADS * RET_DK), BF16),
        pltpu.VMEM((2, tm, RET_HEADS * RET_DK), BF16),
        pltpu.VMEM((2, RET_HEADS * RET_DK, tm), BF16),
        pltpu.VMEM((2, RET_HEADS * RET_DK, tm), BF16),
        pltpu.VMEM((2, tm, RET_WIDTH), BF16),
        pltpu.VMEM((2, tm, RET_WIDTH), F32),
        pltpu.VMEM((2, tm, SSD_WIDTH), F32),
        pltpu.VMEM((2, CONV_SLABS, tm, V7X_LANES), F32),
        pltpu.VMEM((2, tm, DT_PAD), F32),
    ]
    return pl.pallas_call(
        functools.partial(_mix_prompt_kernel, tm=tm, nt=nt, n_tiles=n_tiles),
        out_shape=[
            jax.ShapeDtypeStruct(h.shape, F32),
            jax.ShapeDtypeStruct((batch, RET_HEADS, RET_DK, RET_DV), F32),
            jax.ShapeDtypeStruct((batch, SSD_WIDTH, SSD_STATE), F32),
            jax.ShapeDtypeStruct((batch, CONV_WIDTH - 1, CONV_CH), F32),
        ],
        grid=(n_tiles + 1,),
        in_specs=([row_cur, row_prev] + [_resident(a.shape) for a in consts] + [pos, pos, pos_t, pos_t]
                  + [_resident(a.shape) for a in tail]),
        out_specs=[
            row_prev,
            pl.BlockSpec((1, RET_HEADS, RET_DK, RET_DV), lambda s: (prev(s) // nt, 0, 0, 0)),
            pl.BlockSpec((1, SSD_WIDTH, SSD_STATE), lambda s: (prev(s) // nt, 0, 0)),
            pl.BlockSpec((1, CONV_WIDTH - 1, CONV_CH), lambda s: (cur(s) // nt, 0, 0)),
        ],
        scratch_shapes=[
            pltpu.VMEM((CONV_SLABS, CONV_HIST + tm, V7X_LANES), F32),
            pltpu.VMEM((tm, RET_WIDTH + SSD_WIDTH), BF16),
            pltpu.VMEM((RET_HEADS, RET_DK, RET_DV), F32),
            pltpu.VMEM((SSD_PAIRS, 2 * SSD_STATE, 2 * SSD_HEAD_DIM), F32),
        ] + stage,
        compiler_params=_params(("arbitrary",)),
        name="mix_prompt",
    )(h, h, *consts, cos2, sin2, cos_t, sin_t, *tail)


def _split_hi_lo(x):
    hi = x.astype(BF16).astype(F32)
    return hi, x - hi


def _mix_sample_kernel(proj_ref, sret_ref, sssm_ref, sconv_ref, cos_ref, sin_ref, gam_ref,
                       gn_ref, cw_ref, cb_ref, dtb_ref, alog_ref, dsk_ref, sn_ref,
                       mix_ref, oret_ref, ossm_ref, oconv_ref, *, bs):
    proj = proj_ref[...]
    xbc = proj[:, XBC_OFF:DT_OFF]
    hist = sconv_ref[...]
    taps = [hist[:, i * CONV_CH:(i + 1) * CONV_CH] for i in range(CONV_WIDTH - 1)] + [xbc]
    conv = cb_ref[...]
    for i in range(CONV_WIDTH):
        conv = conv + cw_ref[i:i + 1, :] * taps[i]
    oconv_ref[...] = jnp.concatenate(taps[1:], axis=1)
    xc = jax.nn.silu(conv)

    dt = _softplus(proj[:, DT_OFF:IN_PROJ_PAD] + dtb_ref[...])

    sub_i = lax.broadcasted_iota(jnp.int32, (bs, V7X_LANES), 0)
    row_of = lax.broadcasted_iota(jnp.int32, (4 * bs, V7X_LANES), 0) % bs
    cos2 = cos_ref[...]
    sin2 = sin_ref[...]

    def outer_lhs(x):
        hi, lo = _split_hi_lo(x)
        return jnp.concatenate([hi, hi, lo, lo], axis=0)

    def outer_rhs(x):
        hi, lo = _split_hi_lo(x)
        return jnp.concatenate([hi, lo, hi, lo], axis=0).astype(BF16)

    def only_sample(x4, b):
        return jnp.where(row_of == b, x4, 0.0).astype(BF16)

    for hh in range(RET_HEADS):
        lanes = slice(hh * RET_DK, (hh + 1) * RET_DK)
        qr = _rotary(proj[:, Q_OFF + hh * RET_DK:Q_OFF + (hh + 1) * RET_DK], cos2, sin2)
        kr = _rotary(proj[:, K_OFF + hh * RET_DK:K_OFF + (hh + 1) * RET_DK], cos2, sin2) * (RET_DK ** -0.5)
        vh = proj[:, V_OFF + hh * RET_DV:V_OFF + (hh + 1) * RET_DV]
        k4, v4, q_bf = outer_lhs(kr), outer_rhs(vh), qr.astype(BF16)
        gamma = gam_ref[hh:hh + 1, :]
        y = jnp.zeros((bs, RET_DV), F32)
        for b in range(bs):
            s_old = sret_ref[b, hh]
            oret_ref[b, hh] = gamma * s_old + _dot_tn(only_sample(k4, b), v4)
            y = jnp.where(sub_i == b, _dot(q_bf, s_old.astype(BF16)), y)
        y = gamma * y + jnp.sum(qr * kr, axis=-1, keepdims=True) * vh
        r = _group_norm(y) * gn_ref[:, lanes]
        mix_ref[:, lanes] = jax.nn.silu(proj[:, G_OFF + hh * RET_DV:G_OFF + (hh + 1) * RET_DV]) * r

    xs = xc[:, 0:SSD_WIDTH]
    head_of_lane = lax.broadcasted_iota(jnp.int32, (bs, SSD_WIDTH), 1) // SSD_HEAD_DIM
    dec = jnp.exp(dt * (-jnp.exp(alog_ref[...])))
    dt_wide = jnp.zeros((bs, SSD_WIDTH), F32)
    dec_wide = jnp.zeros((bs, SSD_WIDTH), F32)
    for hd in range(SSD_HEADS):
        dt_wide = jnp.where(head_of_lane == hd, dt[:, hd:hd + 1], dt_wide)
        dec_wide = jnp.where(head_of_lane == hd, dec[:, hd:hd + 1], dec_wide)
    xdt = xs * dt_wide
    x4 = outer_lhs(xdt)
    ys = []
    for j in range(SSD_PAIRS):
        grp = j // (SSD_PAIRS // SSD_GROUPS)
        lanes = slice(j * 2 * SSD_HEAD_DIM, (j + 1) * 2 * SSD_HEAD_DIM)
        bg = xc[:, SSD_WIDTH + grp * SSD_STATE:SSD_WIDTH + (grp + 1) * SSD_STATE]
        c_off = SSD_WIDTH + SSD_GROUPS * SSD_STATE + grp * SSD_STATE
        cg = xc[:, c_off:c_off + SSD_STATE]
        b4, c_bf = outer_rhs(bg), cg.astype(BF16)
        y = jnp.zeros((bs, 2 * SSD_HEAD_DIM), F32)
        for b in range(bs):
            s_old = sssm_ref[b, lanes, :]
            dec_rows = jnp.concatenate(
                [jnp.broadcast_to(dec[b:b + 1, hd:hd + 1], (SSD_HEAD_DIM, SSD_STATE)) for hd in (2 * j, 2 * j + 1)],
                axis=0)
            ossm_ref[b, lanes, :] = dec_rows * s_old + _dot_tn(only_sample(x4[:, lanes], b), b4)
            y = jnp.where(sub_i == b, _dot_nt(c_bf, s_old.astype(BF16)), y)
        ys.append(dec_wide[:, lanes] * y + jnp.sum(cg * bg, axis=-1, keepdims=True) * xdt[:, lanes])
    ys = (jnp.concatenate(ys, axis=1) + dsk_ref[...] * xs) * jax.nn.silu(proj[:, Z_OFF:Z_OFF + SSD_WIDTH])
    mix_ref[:, RET_WIDTH:RET_WIDTH + SSD_WIDTH] = _rmsnorm(ys, sn_ref[...])


def _mix_sample(proj, s_ret, s_ssm, s_conv, cos2, sin2, gamma, gn, conv_w, conv_b, dt_bias, a_log, dskip, ssd_gain,
                *, bs):
    n = proj.shape[0]
    consts = [cos2, sin2, gamma, gn, conv_w, conv_b, dt_bias, a_log, dskip, ssd_gain]
    blk2 = lambda w: pl.BlockSpec((bs, w), lambda i: (i, 0))
    ret_blk = pl.BlockSpec((bs, RET_HEADS, RET_DK, RET_DV), lambda i: (i, 0, 0, 0))
    ssm_blk = pl.BlockSpec((bs, SSD_WIDTH, SSD_STATE), lambda i: (i, 0, 0))
    return pl.pallas_call(
        functools.partial(_mix_sample_kernel, bs=bs),
        out_shape=[
            jax.ShapeDtypeStruct((n, RET_WIDTH + SSD_WIDTH), F32),
            jax.ShapeDtypeStruct(s_ret.shape, F32),
            jax.ShapeDtypeStruct(s_ssm.shape, F32),
            jax.ShapeDtypeStruct(s_conv.shape, F32),
        ],
        grid=(n // bs,),
        in_specs=[blk2(proj.shape[1]), ret_blk, ssm_blk, blk2(s_conv.shape[1])] + [_resident(a.shape) for a in consts],
        out_specs=[blk2(RET_WIDTH + SSD_WIDTH), ret_blk, ssm_blk, blk2(s_conv.shape[1])],
        compiler_params=_params(("arbitrary",)),
        name="mix_sample",
    )(proj, s_ret, s_ssm, s_conv, *consts)


def _softmax_rows(s):
    m = jnp.max(s, axis=-1, keepdims=True)
    p = jnp.exp(s - m)
    return p / jnp.sum(p, axis=-1, keepdims=True)


def _xattn_kernel(h_ref, g_ref, wq_ref, mk_ref, mv_ref, wo_ref, qs_ref, ks_ref, vs_ref, o_ref, os_ref, att_s, *, bs):
    h = h_ref[...]
    c = _rmsnorm(h, g_ref[...]).astype(BF16)
    qx = _dot(c, wq_ref[...]).astype(BF16)
    for hh in range(X_HEADS):
        lanes = slice(hh * X_HEAD_DIM, (hh + 1) * X_HEAD_DIM)
        s = _dot_nt(qx[:, lanes], mk_ref[0, :, lanes].astype(BF16)) * (X_HEAD_DIM ** -0.5)
        att = _softmax_rows(s).astype(BF16)
        att_s[:, lanes] = _dot(att, mv_ref[0, :, lanes].astype(BF16)).astype(BF16)
    o_ref[...] = h + _dot(att_s[...], wo_ref[...])

    rows = MEM_TOKENS * X_HEADS
    own = (lax.broadcasted_iota(jnp.int32, (X_HEADS, rows), 1) % X_HEADS
           == lax.broadcasted_iota(jnp.int32, (X_HEADS, rows), 0))
    for b in range(bs):
        q = qs_ref[b].astype(BF16)
        k = ks_ref[b].reshape(rows, X_HEAD_DIM).astype(BF16)
        v = vs_ref[b].reshape(rows, X_HEAD_DIM).astype(BF16)
        s = _dot_nt(q, k) * (X_HEAD_DIM ** -0.5)
        att = _softmax_rows(jnp.where(own, s, -jnp.inf)).astype(BF16)
        os_ref[b] = _dot(att, v)


def _xattn(h, gain, w_q, mem_k, mem_v, w_o, q_s, cache_k, cache_v, *, batch, seq, tm):
    nt = seq // tm
    d = h.shape[1]
    n = q_s.shape[0]
    bs = n // (batch * nt)
    row = pl.BlockSpec((tm, d), lambda b, t: (b * nt + t, 0))
    mem = pl.BlockSpec((1, MEM_TOKENS, d), lambda b, t: (b, 0, 0))
    row_s = pl.BlockSpec((bs, X_HEADS, X_HEAD_DIM), lambda b, t: (b * nt + t, 0, 0))
    mem_s = pl.BlockSpec((bs, MEM_TOKENS, X_HEADS, X_HEAD_DIM), lambda b, t: (b * nt + t, 0, 0, 0))
    return pl.pallas_call(
        functools.partial(_xattn_kernel, bs=bs),
        out_shape=[jax.ShapeDtypeStruct(h.shape, F32), jax.ShapeDtypeStruct((n, X_HEADS, X_HEAD_DIM), F32)],
        grid=(batch, nt),
        in_specs=[row, _resident(gain.shape), _resident(w_q.shape), mem, mem, _resident(w_o.shape),
                  row_s, mem_s, mem_s],
        out_specs=[row, row_s],
        scratch_shapes=[pltpu.VMEM((tm, d), BF16)],
        compiler_params=_params(("arbitrary", "arbitrary")),
        name="xattn",
    )(h, gain, w_q, mem_k, mem_v, w_o, q_s, cache_k, cache_v)


def _rope_angles(pos):
    half = RET_DK // 2
    inv_freq = ROPE_BASE ** (-jnp.arange(half, dtype=F32) / half)
    ang = pos.astype(F32)[:, None] * inv_freq[None, :]
    return jnp.cos(ang), jnp.sin(ang)


def _rope_tables(pos):
    cos, sin = _rope_angles(pos)
    return jnp.concatenate([cos, cos], axis=-1), jnp.concatenate([-sin, sin], axis=-1)


def _retention_decay_tables(chunk):
    log_g = jnp.log1p(-jnp.exp2(-5.0 - jnp.arange(RET_HEADS, dtype=F32)))
    idx = jnp.arange(chunk, dtype=F32)
    diff = idx[:, None] - idx[None, :]
    causal = diff >= 0
    dmat = jnp.where(causal[None], jnp.exp(log_g[:, None, None] * jnp.where(causal, diff, 0.0)[None]), 0.0)
    q_dec = jnp.exp(log_g[:, None] * (idx[None, :] + 1.0))
    k_dec = jnp.exp(log_g[:, None] * (chunk - 1.0 - idx[None, :]))
    c_dec = jnp.exp(log_g * chunk)
    wide = lambda x: jnp.broadcast_to(x[..., None], x.shape + (V7X_LANES,))
    return dmat, wide(q_dec), k_dec, wide(c_dec)


def kernel(x_prompt, x_sample, mem_prompt, state_ret, state_ssm, state_conv, cache_mem_k, cache_mem_v, ffn1_norm,
           ffn1_w1, ffn1_w3, ffn1_w2, mix_norm, w_in, ret_gn_gain, conv_w, conv_b, dt_bias, A_log, D_skip, ssd_norm,
           w_out, x_norm, mem_norm, w_xq, w_xk, w_xv, w_xo, ffn2_norm, ffn2_w1, ffn2_w3, ffn2_w2, final_norm):
    bp, lp, d = x_prompt.shape
    bsz = x_sample.shape[0]
    depth = ffn1_w1.shape[0]
    row = lambda v: v.reshape(1, -1).astype(F32)
    lane_pad = lambda v: jnp.pad(row(v), ((0, 0), (0, DT_PAD - v.shape[-1])))

    cos_p, sin_p = _rope_tables(jnp.arange(lp))
    cos_pt, sin_pt = (a.T for a in _rope_angles(jnp.arange(lp)))
    cos_s, sin_s = _rope_tables(PAST_LEN + jnp.arange(x_sample.shape[1]))
    dmat, q_dec, k_dec, c_dec = _retention_decay_tables(CHUNK)
    gamma1 = _retention_decay_tables(1)[3]

    y_p = x_prompt.reshape(bp * lp, d)
    y_s = x_sample.reshape(bsz, d)
    outs = {k: [] for k in ("ret_p", "ssm_p", "conv_p", "memk", "memv", "ret_s", "ssm_s", "conv_s")}
    for l in range(depth):
        bf = lambda w: w[l].astype(BF16)
        w_in_f = w_in[l]
        w_in_l = jnp.pad(w_in_f, ((0, 0), (0, IN_PROJ_PAD - IN_PROJ_WIDTH))).astype(BF16)
        w_a = jnp.concatenate([w_in_f[:, Q_OFF:K_OFF], w_in_f[:, V_OFF:XBC_OFF]], axis=1).astype(BF16)
        w_kt = w_in_f[:, K_OFF:V_OFF].T.astype(BF16)
        w_xbc = w_in_f[:, XBC_OFF:DT_OFF].astype(BF16)
        w_dt = jnp.pad(jnp.tile(w_in_f[:, DT_OFF:], (1, 2)), ((0, 0), (0, DT_PAD - 2 * SSD_HEADS))).astype(BF16)
        f1 = (row(ffn1_norm[l]), bf(ffn1_w1), bf(ffn1_w3), bf(ffn1_w2))
        f2 = (row(ffn2_norm[l]), bf(ffn2_w1), bf(ffn2_w3), bf(ffn2_w2))
        shared = (row(ret_gn_gain[l]), conv_w[l], row(conv_b[l]))
        ssd_tail = (lane_pad(A_log[l]), row(jnp.repeat(D_skip[l], SSD_HEAD_DIM)), row(ssd_norm[l]))
        w_out_l, w_xq_l, w_xo_l = bf(w_out), bf(w_xq), bf(w_xo)

        mk, mv = _linear(mem_prompt.reshape(bp * MEM_TOKENS, d), [bf(w_xk), bf(w_xv)], gain=row(mem_norm[l]),
                         tm=MEM_TOKENS)
        mk = mk.reshape(bp, MEM_TOKENS, d)
        mv = mv.reshape(bp, MEM_TOKENS, d)

        y_p = _ffn(y_p, *f1, tm=512)
        tm_p = 512
        y_p, ret_p, ssm_p, conv_p = _mix_prompt(
            y_p, row(mix_norm[l]), w_a, w_kt, w_xbc, w_dt, cos_p, sin_p, cos_pt, sin_pt, dmat,
            jnp.tile(q_dec, (1, tm_p // CHUNK, 1)), jnp.tile(k_dec, (1, tm_p // CHUNK)), c_dec,
            *shared, lane_pad(jnp.tile(dt_bias[l], 2)), *ssd_tail, w_out_l, batch=bp, seq=lp, tm=tm_p)

        y_s = _ffn(y_s, *f1, tm=bsz)
        proj_s, = _linear(y_s, [w_in_l], gain=row(mix_norm[l]), tm=bsz)
        mix_s, ret_s, ssm_s, conv_s = _mix_sample(
            proj_s, state_ret[l], state_ssm[l].reshape(bsz, SSD_WIDTH, SSD_STATE),
            state_conv[l].reshape(bsz, (CONV_WIDTH - 1) * CONV_CH),
            cos_s, sin_s, gamma1, *shared, lane_pad(dt_bias[l]), *ssd_tail, bs=8)
        y_s, = _linear(mix_s, [w_out_l], res=y_s, tm=bsz)
        q_s, = _linear(y_s, [w_xq_l], gain=row(x_norm[l]), tm=bsz)

        y_p, att_s = _xattn(y_p, row(x_norm[l]), w_xq_l, mk, mv, w_xo_l, q_s.reshape(bsz, X_HEADS, X_HEAD_DIM),
                            cache_mem_k[l], cache_mem_v[l], batch=bp, seq=lp, tm=512)

        y_p = _ffn(y_p, *f2, final_gain=row(final_norm) if l == depth - 1 else None, tm=512)
        y_s, = _linear(att_s.reshape(bsz, d), [w_xo_l], res=y_s, tm=bsz)
        y_s = _ffn(y_s, *f2, final_gain=row(final_norm) if l == depth - 1 else None, tm=bsz)

        outs["ret_p"].append(ret_p)
        outs["ssm_p"].append(ssm_p.reshape(bp, SSD_HEADS, SSD_HEAD_DIM, SSD_STATE))
        outs["conv_p"].append(conv_p)
        outs["memk"].append(mk.reshape(bp, MEM_TOKENS, X_HEADS, X_HEAD_DIM))
        outs["memv"].append(mv.reshape(bp, MEM_TOKENS, X_HEADS, X_HEAD_DIM))
        outs["ret_s"].append(ret_s)
        outs["ssm_s"].append(ssm_s.reshape(bsz, SSD_HEADS, SSD_HEAD_DIM, SSD_STATE))
        outs["conv_s"].append(conv_s.reshape(bsz, CONV_WIDTH - 1, CONV_CH))

    st = lambda k: jnp.stack(outs[k])
    return (y_p.reshape(bp, lp, d), y_s.reshape(bsz, x_sample.shape[1], d), st("ret_p"), st("ssm_p"), st("conv_p"),
            st("memk"), st("memv"), st("ret_s"), st("ssm_s"), st("conv_s"))
```

```python
import functools

import jax
import jax.numpy as jnp
from jax import lax
from jax.experimental import pallas as pl
from jax.experimental.pallas import tpu as pltpu

F32 = jnp.float32
BF16 = jnp.bfloat16

D_MODEL = 1024
D_FF = 2816
PAST_LEN = 16384
RET_HEADS = 4
RET_DK = 128
RET_DV = 128
RET_WIDTH = RET_HEADS * RET_DV
SSD_HEADS = 8
SSD_HEAD_DIM = 64
SSD_WIDTH = SSD_HEADS * SSD_HEAD_DIM
SSD_GROUPS = 2
SSD_STATE = 128
SSD_PAIRS = SSD_HEADS // 2
CONV_WIDTH = 4
CONV_CH = SSD_WIDTH + 2 * SSD_GROUPS * SSD_STATE
CHUNK = 128
MEM_TOKENS = 256
X_HEADS = 4
X_HEAD_DIM = D_MODEL // X_HEADS
ROPE_BASE = 10000.0
EPS = 1e-6

Q_OFF = 0
K_OFF = Q_OFF + RET_HEADS * RET_DK
V_OFF = K_OFF + RET_HEADS * RET_DK
G_OFF = V_OFF + RET_WIDTH
Z_OFF = G_OFF + RET_WIDTH
XBC_OFF = Z_OFF + SSD_WIDTH
DT_OFF = XBC_OFF + CONV_CH
IN_PROJ_WIDTH = DT_OFF + SSD_HEADS

PQ_OFF = 0
PV_OFF = PQ_OFF + RET_HEADS * RET_DK
PG_OFF = PV_OFF + RET_WIDTH
PZ_OFF = PG_OFF + RET_WIDTH
P_WIDTH = PZ_OFF + SSD_WIDTH

V7X_LANES = 128
V7X_SUBLANES = 8
V7X_VMEM_LIMIT_BYTES = 56 * 1024 * 1024
DT_PAD = V7X_LANES
IN_PROJ_PAD = DT_OFF + DT_PAD
CONV_HIST = V7X_SUBLANES
CONV_SLABS = CONV_CH // V7X_LANES


def _params(sem):
    return pltpu.CompilerParams(dimension_semantics=sem, vmem_limit_bytes=V7X_VMEM_LIMIT_BYTES)


def _resident(shape):
    zeros = (0,) * len(shape)
    return pl.BlockSpec(shape, lambda *_: zeros, pipeline_mode=pl.Buffered(1))


def _rmsnorm(x, gain):
    ms = jnp.mean(x * x, axis=-1, keepdims=True)
    return x * lax.rsqrt(ms + EPS) * gain


def _dot(a, b):
    return jnp.dot(a, b, preferred_element_type=F32)


def _dot_nt(a, b):
    return lax.dot_general(a, b, (((1,), (1,)), ((), ())), preferred_element_type=F32)


def _dot_tn(a, b):
    return lax.dot_general(a, b, (((0,), (0,)), ((), ())), preferred_element_type=F32)


def _softplus(x):
    return jnp.maximum(x, 0.0) + jnp.log1p(jnp.exp(-jnp.abs(x)))


def _rotary(x, cos2, sin2):
    return x * cos2 + pltpu.roll(x, RET_DK // 2, axis=1) * sin2


def _group_norm(y):
    mu = jnp.mean(y, axis=-1, keepdims=True)
    d = y - mu
    var = jnp.mean(d * d, axis=-1, keepdims=True)
    return d * lax.rsqrt(var + EPS)


def _sample_attention(q_ref, k_ref, v_ref, o_ref):
    slabs = MEM_TOKENS * X_HEADS // V7X_SUBLANES
    for b in range(q_ref.shape[0]):
        k3 = k_ref[b].reshape(slabs, V7X_SUBLANES, X_HEAD_DIM)
        v3 = v_ref[b].reshape(slabs, V7X_SUBLANES, X_HEAD_DIM)
        q4 = q_ref[b]
        q8 = jnp.concatenate([q4] * (V7X_SUBLANES // X_HEADS), axis=0)
        s = jnp.sum(k3 * q8[None], axis=-1, keepdims=True) * (X_HEAD_DIM ** -0.5)
        m8 = jnp.max(s, axis=0)
        m4 = jnp.maximum(m8[0:X_HEADS], m8[X_HEADS:])
        p = jnp.exp(s - jnp.concatenate([m4, m4], axis=0)[None])
        acc = jnp.sum(p * v3, axis=0)
        den = jnp.sum(p, axis=0)
        o_ref[b] = (acc[0:X_HEADS] + acc[X_HEADS:]) / (den[0:X_HEADS] + den[X_HEADS:])


def _ffn_kernel(*refs, final_norm, with_attention):
    refs = list(refs)
    x_ref, g_ref, w1_ref, w3_ref, w2_ref = refs[:5]
    del refs[:5]
    fg_ref = refs.pop(0) if final_norm else None
    attn_in = [refs.pop(0) for _ in range(3)] if with_attention else None
    o_ref = refs.pop(0)
    x = x_ref[...]
    xn = _rmsnorm(x, g_ref[...]).astype(BF16)
    a = _dot(xn, w1_ref[...])
    b = _dot(xn, w3_ref[...])
    hidden = (jax.nn.silu(a) * b).astype(BF16)
    out = x + 0.5 * _dot(hidden, w2_ref[...])
    if final_norm:
        out = _rmsnorm(out, fg_ref[...])
    o_ref[...] = out
    if with_attention:
        _sample_attention(*attn_in, refs.pop(0))


def _ffn(x, gain, w1, w3, w2, final_gain=None, attention=None, *, tm):
    t, d = x.shape
    steps = t // tm
    row = pl.BlockSpec((tm, d), lambda i: (i, 0))
    ins = [x, gain, w1, w3, w2]
    specs = [row, _resident(gain.shape), _resident(w1.shape), _resident(w3.shape), _resident(w2.shape)]
    out_shape, out_specs = [jax.ShapeDtypeStruct((t, d), F32)], [row]
    if final_gain is not None:
        ins.append(final_gain)
        specs.append(_resident(final_gain.shape))
    if attention is not None:
        q, cache_k, cache_v = attention
        bs = q.shape[0] // steps
        row_s = pl.BlockSpec((bs, X_HEADS, X_HEAD_DIM), lambda i: (i, 0, 0))
        mem_s = pl.BlockSpec((bs, MEM_TOKENS, X_HEADS, X_HEAD_DIM), lambda i: (i, 0, 0, 0))
        ins += [q, cache_k, cache_v]
        specs += [row_s, mem_s, mem_s]
        out_shape.append(jax.ShapeDtypeStruct(q.shape, F32))
        out_specs.append(row_s)
    outs = pl.pallas_call(
        functools.partial(_ffn_kernel, final_norm=final_gain is not None, with_attention=attention is not None),
        out_shape=out_shape,
        grid=(steps,),
        in_specs=specs,
        out_specs=out_specs,
        compiler_params=_params(("arbitrary",)),
        name="ffn",
    )(*ins)
    return outs if attention is not None else outs[0]


def _linear_kernel(*refs, has_norm, has_res, n_w):
    refs = list(refs)
    x_ref = refs.pop(0)
    g_ref = refs.pop(0) if has_norm else None
    r_ref = refs.pop(0) if has_res else None
    w_refs, o_refs = refs[:n_w], refs[n_w:]
    x = x_ref[...]
    if has_norm:
        x = _rmsnorm(x, g_ref[...])
    xb = x.astype(BF16)
    for w_ref, o_ref in zip(w_refs, o_refs, strict=True):
        y = _dot(xb, w_ref[...])
        if has_res:
            y = r_ref[...] + y
        o_ref[...] = y


def _linear(x, weights, gain=None, res=None, *, tm):
    t, k = x.shape
    ins, specs = [x], [pl.BlockSpec((tm, k), lambda i: (i, 0))]
    if gain is not None:
        ins.append(gain)
        specs.append(_resident(gain.shape))
    if res is not None:
        ins.append(res)
        specs.append(pl.BlockSpec((tm, res.shape[1]), lambda i: (i, 0)))
    for w in weights:
        ins.append(w)
        specs.append(_resident(w.shape))
    outs = pl.pallas_call(
        functools.partial(_linear_kernel, has_norm=gain is not None, has_res=res is not None, n_w=len(weights)),
        out_shape=[jax.ShapeDtypeStruct((t, w.shape[1]), F32) for w in weights],
        grid=(t // tm,),
        in_specs=specs,
        out_specs=[pl.BlockSpec((tm, w.shape[1]), lambda i: (i, 0)) for w in weights],
        compiler_params=_params(("arbitrary",)),
        name="linear",
    )(*ins)
    return outs


def _memkv_kernel(x_ref, g_ref, wk_ref, wv_ref, k_ref, v_ref, k4_ref, v4_ref):
    xb = _rmsnorm(x_ref[...], g_ref[...]).astype(BF16)
    for w_ref, o_ref, o4_ref in ((wk_ref, k_ref, k4_ref), (wv_ref, v_ref, v4_ref)):
        y = _dot(xb, w_ref[...])
        o_ref[0] = y
        o4_ref[0] = y.reshape(MEM_TOKENS, X_HEADS, X_HEAD_DIM)


def _memkv(mem, gain, w_k, w_v):
    batch, m, d = mem.shape
    flat = jax.ShapeDtypeStruct((batch, m, d), F32)
    split = jax.ShapeDtypeStruct((batch, m, X_HEADS, X_HEAD_DIM), F32)
    flat_blk = pl.BlockSpec((1, m, d), lambda b: (b, 0, 0))
    split_blk = pl.BlockSpec((1, m, X_HEADS, X_HEAD_DIM), lambda b: (b, 0, 0, 0))
    return pl.pallas_call(
        _memkv_kernel,
        out_shape=[flat, flat, split, split],
        grid=(batch,),
        in_specs=[pl.BlockSpec((m, d), lambda b: (b, 0)), _resident(gain.shape), _resident(w_k.shape),
                  _resident(w_v.shape)],
        out_specs=[flat_blk, flat_blk, split_blk, split_blk],
        compiler_params=_params(("arbitrary",)),
        name="memkv",
    )(mem.reshape(batch * m, d), gain, w_k, w_v)


def _cumsum_chunks(x):
    pos = lax.broadcasted_iota(jnp.int32, x.shape, 0) % CHUNK
    step = 1
    while step < CHUNK:
        x = x + jnp.where(pos >= step, pltpu.roll(x, step, axis=0), 0.0)
        step *= 2
    return x


def _mixp_project(tile, h_ref, g_ref, wa_ref, wkt_ref, wxbc_ref, wdt_ref, cos_ref, sin_ref, cost_ref, sint_ref,
                  qdec_ref, kdec_ref, cw_ref, cb_ref, dtb_ref, alog_ref, conv_ref, xbc_s, stage, *, tm, nt):
    q_s, qd_s, kr_s, kd_s, v_s, g_s, z_s, xc_s, pk_s = stage
    row_start = (tile % nt) == 0

    u = _rmsnorm(h_ref[...], g_ref[...]).astype(BF16)
    qvgz = _dot(u, wa_ref[...])
    kt = _dot_nt(wkt_ref[...], u)
    xbc = _dot(u, wxbc_ref[...])
    dt_raw = _dot(u, wdt_ref[...])

    for sl in range(CONV_SLABS):
        lanes = slice(sl * V7X_LANES, (sl + 1) * V7X_LANES)
        xbc_s[sl, 0:CONV_HIST, :] = jnp.where(row_start, 0.0, xbc_s[sl, 0:CONV_HIST, :])
        xbc_s[sl, CONV_HIST:CONV_HIST + tm, :] = xbc[:, lanes]
        conv = cb_ref[:, lanes]
        for i in range(CONV_WIDTH):
            off = CONV_HIST - (CONV_WIDTH - 1) + i
            conv = conv + cw_ref[i:i + 1, lanes] * xbc_s[sl, off:off + tm, :]
        xc_s[sl] = jax.nn.silu(conv)
        conv_ref[0, :, lanes] = xbc_s[sl, CONV_HIST + tm - (CONV_WIDTH - 1):CONV_HIST + tm, :]
        xbc_s[sl, 0:CONV_HIST, :] = xbc_s[sl, tm:tm + CONV_HIST, :]

    dt = _softplus(dt_raw + dtb_ref[...])
    a_cum = _cumsum_chunks(dt * (-jnp.exp(alog_ref[...])))
    head_lanes = lax.broadcasted_iota(jnp.int32, (tm, DT_PAD), 1) < SSD_HEADS
    pk_s[...] = jnp.where(head_lanes, a_cum, dt)

    cos2, sin2 = cos_ref[...], sin_ref[...]
    cos_t, sin_t = cost_ref[...], sint_ref[...]
    half = RET_DK // 2
    for hh in range(RET_HEADS):
        lanes = slice(hh * RET_DK, (hh + 1) * RET_DK)
        qr = _rotary(qvgz[:, PQ_OFF + hh * RET_DK:PQ_OFF + (hh + 1) * RET_DK], cos2, sin2)
        q_s[:, lanes] = qr.astype(BF16)
        qd_s[:, lanes] = (qr * qdec_ref[hh]).astype(BF16)
        k1 = kt[hh * RET_DK:hh * RET_DK + half, :]
        k2 = kt[hh * RET_DK + half:(hh + 1) * RET_DK, :]
        kr = jnp.concatenate([k1 * cos_t - k2 * sin_t, k1 * sin_t + k2 * cos_t], axis=0) * (RET_DK ** -0.5)
        kr_s[lanes, :] = kr.astype(BF16)
        kd_s[lanes, :] = (kr * kdec_ref[hh:hh + 1, :]).astype(BF16)
    v_s[...] = qvgz[:, PV_OFF:PV_OFF + RET_WIDTH].astype(BF16)
    g_s[...] = jax.nn.silu(qvgz[:, PG_OFF:PG_OFF + RET_WIDTH])
    z_s[...] = jax.nn.silu(qvgz[:, PZ_OFF:PZ_OFF + SSD_WIDTH])


def _mixp_heads(tile, h_ref, dmat_ref, cdec_ref, gn_ref, dsk_ref, sn_ref, wout_ref, o_ref, ret_ref, ssm_ref,
                mix_s, sret_s, sssm_s, stage, *, tm, nt):
    q_s, qd_s, kr_s, kd_s, v_s, g_s, z_s, xc_s, pk_s = stage
    nc = tm // CHUNK
    row_start = (tile % nt) == 0

    row_i = lax.broadcasted_iota(jnp.int32, (CHUNK, CHUNK), 0)
    col_i = lax.broadcasted_iota(jnp.int32, (CHUNK, CHUNK), 1)
    causal = row_i >= col_i
    low_lanes = col_i < SSD_HEAD_DIM
    bd_rows = lax.broadcasted_iota(jnp.int32, (2 * SSD_STATE, 2 * SSD_HEAD_DIM), 0)
    bd_cols = lax.broadcasted_iota(jnp.int32, (2 * SSD_STATE, 2 * SSD_HEAD_DIM), 1)
    bd_mask = (bd_rows < SSD_STATE) == (bd_cols < SSD_HEAD_DIM)

    inner, kv = {}, {}
    scores, c_exp, st_inc, st_dec = {}, {}, {}, {}
    for c in range(nc):
        rows = slice(c * CHUNK, (c + 1) * CHUNK)
        for hh in range(RET_HEADS):
            lanes = slice(hh * RET_DK, (hh + 1) * RET_DK)
            inner[c, hh] = (_dot(q_s[rows, lanes], kr_s[lanes, rows]) * dmat_ref[hh]).astype(BF16)
            kv[c, hh] = _dot(kd_s[lanes, rows], v_s[rows, lanes])

        pk = pk_s[rows, :]
        pk_t = pk.T
        a_t = pk_t[0:SSD_HEADS, :]
        dt_t = pk_t[SSD_HEADS:2 * SSD_HEADS, :]
        a_last = a_t[:, CHUNK - 1:CHUNK]
        w_t = jnp.exp(a_last - a_t) * dt_t
        chunk_dec = jnp.exp(a_last)
        for grp in range(SSD_GROUPS):
            cg = xc_s[SSD_PAIRS + SSD_GROUPS + grp, rows, :]
            b_t = xc_s[SSD_PAIRS + grp, rows, :].T
            gmat = _dot(cg.astype(BF16), b_t.astype(BF16))
            for pj in range(SSD_PAIRS // SSD_GROUPS):
                j = grp * (SSD_PAIRS // SSD_GROUPS) + pj
                heads = (2 * j, 2 * j + 1)
                x_pair = xc_s[j, rows, :].astype(BF16)
                b_w = jnp.concatenate([b_t * w_t[hd:hd + 1, :] for hd in heads], axis=0).astype(BF16)
                st_inc[c, j] = jnp.where(bd_mask, _dot(b_w, x_pair), 0.0)
                st_dec[c, j] = jnp.concatenate(
                    [jnp.broadcast_to(chunk_dec[hd:hd + 1, :], (SSD_STATE, 2 * SSD_HEAD_DIM)) for hd in heads], axis=0)
                for hd in heads:
                    a_col = jnp.broadcast_to(pk[:, hd:hd + 1], (CHUNK, CHUNK))
                    seg = a_col - a_t[hd:hd + 1, :]
                    lmat = jnp.where(causal, jnp.exp(jnp.where(causal, seg, 0.0)), 0.0)
                    scores[c, hd] = (gmat * lmat * dt_t[hd:hd + 1, :]).astype(BF16)
                    c_exp[c, hd] = (cg * jnp.exp(a_col)).astype(BF16)

    s_in, st_in = {}, {}
    for hh in range(RET_HEADS):
        s = jnp.where(row_start, 0.0, sret_s[hh])
        for c in range(nc):
            s_in[c, hh] = s.astype(BF16)
            s = cdec_ref[hh:hh + 1, :] * s + kv[c, hh]
        sret_s[hh] = s
        ret_ref[0, hh] = s
    for j in range(SSD_PAIRS):
        s = jnp.where(row_start, 0.0, sssm_s[j])
        for c in range(nc):
            st_in[c, j] = s.astype(BF16)
            s = st_dec[c, j] * s + st_inc[c, j]
        sssm_s[j] = s
        ssm_ref[0, j * 2 * SSD_HEAD_DIM:(j + 1) * 2 * SSD_HEAD_DIM, :] = (s[0:SSD_STATE] + s[SSD_STATE:]).T

    for c in range(nc):
        rows = slice(c * CHUNK, (c + 1) * CHUNK)
        for hh in range(RET_HEADS):
            lanes = slice(hh * RET_DV, (hh + 1) * RET_DV)
            lhs = jnp.concatenate([inner[c, hh], qd_s[rows, lanes]], axis=1)
            rhs = jnp.concatenate([v_s[rows, lanes], s_in[c, hh]], axis=0)
            r = _group_norm(_dot(lhs, rhs)) * gn_ref[:, lanes]
            mix_s[rows, lanes] = (g_s[rows, lanes] * r).astype(BF16)
        ys = []
        for j in range(SSD_PAIRS):
            heads = (2 * j, 2 * j + 1)
            x_pair = xc_s[j, rows, :]
            x_bd = jnp.concatenate([jnp.where(low_lanes, x_pair, 0.0), jnp.where(low_lanes, 0.0, x_pair)], axis=0)
            lhs = jnp.concatenate([scores[c, heads[0]], scores[c, heads[1]], c_exp[c, heads[0]], c_exp[c, heads[1]]],
                                  axis=1)
            rhs = jnp.concatenate([x_bd.astype(BF16), st_in[c, j]], axis=0)
            ys.append(_dot(lhs, rhs) + dsk_ref[:, j * 2 * SSD_HEAD_DIM:(j + 1) * 2 * SSD_HEAD_DIM] * x_pair)
        y_all = jnp.concatenate(ys, axis=1) * z_s[rows, :]
        mix_s[rows, RET_WIDTH:RET_WIDTH + SSD_WIDTH] = _rmsnorm(y_all, sn_ref[...]).astype(BF16)

    o_ref[...] = h_ref[...] + _dot(mix_s[...], wout_ref[...])


def _mix_prompt_kernel(h_ref, hp_ref, g_ref, wa_ref, wkt_ref, wxbc_ref, wdt_ref, cos_ref, sin_ref, cost_ref, sint_ref,
                       dmat_ref, qdec_ref, kdec_ref, cdec_ref, gn_ref, cw_ref, cb_ref, dtb_ref, alog_ref, dsk_ref,
                       sn_ref, wout_ref,
                       o_ref, ret_ref, ssm_ref, conv_ref,
                       xbc_s, mix_s, sret_s, sssm_s, *stage_s, tm, nt, n_tiles):
    s = pl.program_id(0)
    sets = (tuple(r.at[0] for r in stage_s), tuple(r.at[1] for r in stage_s))

    @pl.when(s == 0)
    def _():
        for r in stage_s:
            r[...] = jnp.zeros_like(r)
        sret_s[...] = jnp.zeros_like(sret_s)
        sssm_s[...] = jnp.zeros_like(sssm_s)
        xbc_s[...] = jnp.zeros_like(xbc_s)

    def step(write_set, read_set):
        _mixp_project(jnp.minimum(s, n_tiles - 1), h_ref, g_ref, wa_ref, wkt_ref, wxbc_ref, wdt_ref, cos_ref, sin_ref,
                      cost_ref, sint_ref, qdec_ref, kdec_ref, cw_ref, cb_ref, dtb_ref, alog_ref, conv_ref, xbc_s,
                      write_set, tm=tm, nt=nt)
        _mixp_heads(jnp.maximum(s - 1, 0), hp_ref, dmat_ref, cdec_ref, gn_ref, dsk_ref, sn_ref, wout_ref, o_ref, ret_ref,
                    ssm_ref, mix_s, sret_s, sssm_s, read_set, tm=tm, nt=nt)

    @pl.when(s % 2 == 0)
    def _():
        step(sets[0], sets[1])

    @pl.when(s % 2 == 1)
    def _():
        step(sets[1], sets[0])


def _mix_prompt(h, gain, w_a, w_kt, w_xbc, w_dt, cos2, sin2, cos_t, sin_t, dmat, qdec, kdec, cdec, gn, conv_w, conv_b,
                dt_bias, a_log, dskip, ssd_gain, w_out, *, batch, seq, tm):
    nt = seq // tm
    n_tiles = batch * nt
    d = h.shape[1]
    cur = lambda s: jnp.minimum(s, n_tiles - 1)
    prev = lambda s: jnp.maximum(s - 1, 0)
    row_cur = pl.BlockSpec((tm, d), lambda s: (cur(s), 0))
    row_prev = pl.BlockSpec((tm, d), lambda s: (prev(s), 0))
    pos = pl.BlockSpec((tm, RET_DK), lambda s: (cur(s) % nt, 0))
    pos_t = pl.BlockSpec((RET_DK // 2, tm), lambda s: (0, cur(s) % nt))
    consts = [gain, w_a, w_kt, w_xbc, w_dt]
    tail = [dmat, qdec, kdec, cdec, gn, conv_w, conv_b, dt_bias, a_log, dskip, ssd_gain, w_out]
    stage = [
        pltpu.VMEM((2, tm, RET_HEADS * RET_DK), BF16),
        pltpu.VMEM((2, tm, RET_HEADS * RET_DK), BF16),
        pltpu.VMEM((2, RET_HEADS * RET_DK, tm), BF16),
        pltpu.VMEM((2, RET_HEADS * RET_DK, tm), BF16),
        pltpu.VMEM((2, tm, RET_WIDTH), BF16),
        pltpu.VMEM((2, tm, RET_WIDTH), F32),
        pltpu.VMEM((2, tm, SSD_WIDTH), F32),
        pltpu.VMEM((2, CONV_SLABS, tm, V7X_LANES), F32),
        pltpu.VMEM((2, tm, DT_PAD), F32),
    ]
    return pl.pallas_call(
        functools.partial(_mix_prompt_kernel, tm=tm, nt=nt, n_tiles=n_tiles),
        out_shape=[
            jax.ShapeDtypeStruct(h.shape, F32),
            jax.ShapeDtypeStruct((batch, RET_HEADS, RET_DK, RET_DV), F32),
            jax.ShapeDtypeStruct((batch, SSD_WIDTH, SSD_STATE), F32),
            jax.ShapeDtypeStruct((batch, CONV_WIDTH - 1, CONV_CH), F32),
        ],
        grid=(n_tiles + 1,),
        in_specs=([row_cur, row_prev] + [_resident(a.shape) for a in consts] + [pos, pos, pos_t, pos_t]
                  + [_resident(a.shape) for a in tail]),
        out_specs=[
            row_prev,
            pl.BlockSpec((1, RET_HEADS, RET_DK, RET_DV), lambda s: (prev(s) // nt, 0, 0, 0)),
            pl.BlockSpec((1, SSD_WIDTH, SSD_STATE), lambda s: (prev(s) // nt, 0, 0)),
            pl.BlockSpec((1, CONV_WIDTH - 1, CONV_CH), lambda s: (cur(s) // nt, 0, 0)),
        ],
        scratch_shapes=[
            pltpu.VMEM((CONV_SLABS, CONV_HIST + tm, V7X_LANES), F32),
            pltpu.VMEM((tm, RET_WIDTH + SSD_WIDTH), BF16),
            pltpu.VMEM((RET_HEADS, RET_DK, RET_DV), F32),
            pltpu.VMEM((SSD_PAIRS, 2 * SSD_STATE, 2 * SSD_HEAD_DIM), F32),
        ] + stage,
        compiler_params=_params(("arbitrary",)),
        name="mix_prompt",
    )(h, h, *consts, cos2, sin2, cos_t, sin_t, *tail)


def _split_hi_lo(x):
    hi = x.astype(BF16).astype(F32)
    return hi, x - hi


def _mix_sample_kernel(proj_ref, sret_ref, sssm_ref, sconv_ref, cos_ref, sin_ref, gam_ref,
                       gn_ref, cw_ref, cb_ref, dtb_ref, alog_ref, dsk_ref, sn_ref,
                       mix_ref, oret_ref, ossm_ref, oconv_ref, *, bs):
    proj = proj_ref[...]
    xbc = proj[:, XBC_OFF:DT_OFF]
    hist = sconv_ref[...]
    taps = [hist[:, i * CONV_CH:(i + 1) * CONV_CH] for i in range(CONV_WIDTH - 1)] + [xbc]
    conv = cb_ref[...]
    for i in range(CONV_WIDTH):
        conv = conv + cw_ref[i:i + 1, :] * taps[i]
    oconv_ref[...] = jnp.concatenate(taps[1:], axis=1)
    xc = jax.nn.silu(conv)

    dt = _softplus(proj[:, DT_OFF:IN_PROJ_PAD] + dtb_ref[...])

    sub_i = lax.broadcasted_iota(jnp.int32, (bs, V7X_LANES), 0)
    row_of = lax.broadcasted_iota(jnp.int32, (4 * bs, V7X_LANES), 0) % bs
    cos2 = cos_ref[...]
    sin2 = sin_ref[...]

    def outer_lhs(x):
        hi, lo = _split_hi_lo(x)
        return jnp.concatenate([hi, hi, lo, lo], axis=0)

    def outer_rhs(x):
        hi, lo = _split_hi_lo(x)
        return jnp.concatenate([hi, lo, hi, lo], axis=0).astype(BF16)

    def only_sample(x4, b):
        return jnp.where(row_of == b, x4, 0.0).astype(BF16)

    for hh in range(RET_HEADS):
        lanes = slice(hh * RET_DK, (hh + 1) * RET_DK)
        qr = _rotary(proj[:, Q_OFF + hh * RET_DK:Q_OFF + (hh + 1) * RET_DK], cos2, sin2)
        kr = _rotary(proj[:, K_OFF + hh * RET_DK:K_OFF + (hh + 1) * RET_DK], cos2, sin2) * (RET_DK ** -0.5)
        vh = proj[:, V_OFF + hh * RET_DV:V_OFF + (hh + 1) * RET_DV]
        k4, v4, q_bf = outer_lhs(kr), outer_rhs(vh), qr.astype(BF16)
        gamma = gam_ref[hh:hh + 1, :]
        y = jnp.zeros((bs, RET_DV), F32)
        for b in range(bs):
            s_old = sret_ref[b, hh]
            oret_ref[b, hh] = gamma * s_old + _dot_tn(only_sample(k4, b), v4)
            y = jnp.where(sub_i == b, _dot(q_bf, s_old.astype(BF16)), y)
        y = gamma * y + jnp.sum(qr * kr, axis=-1, keepdims=True) * vh
        r = _group_norm(y) * gn_ref[:, lanes]
        mix_ref[:, lanes] = jax.nn.silu(proj[:, G_OFF + hh * RET_DV:G_OFF + (hh + 1) * RET_DV]) * r

    xs = xc[:, 0:SSD_WIDTH]
    head_of_lane = lax.broadcasted_iota(jnp.int32, (bs, SSD_WIDTH), 1) // SSD_HEAD_DIM
    dec = jnp.exp(dt * (-jnp.exp(alog_ref[...])))
    dt_wide = jnp.zeros((bs, SSD_WIDTH), F32)
    dec_wide = jnp.zeros((bs, SSD_WIDTH), F32)
    for hd in range(SSD_HEADS):
        dt_wide = jnp.where(head_of_lane == hd, dt[:, hd:hd + 1], dt_wide)
        dec_wide = jnp.where(head_of_lane == hd, dec[:, hd:hd + 1], dec_wide)
    xdt = xs * dt_wide
    x4 = outer_lhs(xdt)
    ys = []
    for j in range(SSD_PAIRS):
        grp = j // (SSD_PAIRS // SSD_GROUPS)
        lanes = slice(j * 2 * SSD_HEAD_DIM, (j + 1) * 2 * SSD_HEAD_DIM)
        bg = xc[:, SSD_WIDTH + grp * SSD_STATE:SSD_WIDTH + (grp + 1) * SSD_STATE]
        c_off = SSD_WIDTH + SSD_GROUPS * SSD_STATE + grp * SSD_STATE
        cg = xc[:, c_off:c_off + SSD_STATE]
        b4, c_bf = outer_rhs(bg), cg.astype(BF16)
        y = jnp.zeros((bs, 2 * SSD_HEAD_DIM), F32)
        for b in range(bs):
            s_old = sssm_ref[b, lanes, :]
            dec_rows = jnp.concatenate(
                [jnp.broadcast_to(dec[b:b + 1, hd:hd + 1], (SSD_HEAD_DIM, SSD_STATE)) for hd in (2 * j, 2 * j + 1)],
                axis=0)
            ossm_ref[b, lanes, :] = dec_rows * s_old + _dot_tn(only_sample(x4[:, lanes], b), b4)
            y = jnp.where(sub_i == b, _dot_nt(c_bf, s_old.astype(BF16)), y)
        ys.append(dec_wide[:, lanes] * y + jnp.sum(cg * bg, axis=-1, keepdims=True) * xdt[:, lanes])
    ys = (jnp.concatenate(ys, axis=1) + dsk_ref[...] * xs) * jax.nn.silu(proj[:, Z_OFF:Z_OFF + SSD_WIDTH])
    mix_ref[:, RET_WIDTH:RET_WIDTH + SSD_WIDTH] = _rmsnorm(ys, sn_ref[...])


def _mix_sample(proj, s_ret, s_ssm, s_conv, cos2, sin2, gamma, gn, conv_w, conv_b, dt_bias, a_log, dskip, ssd_gain,
                *, bs):
    n = proj.shape[0]
    consts = [cos2, sin2, gamma, gn, conv_w, conv_b, dt_bias, a_log, dskip, ssd_gain]
    blk2 = lambda w: pl.BlockSpec((bs, w), lambda i: (i, 0))
    ret_blk = pl.BlockSpec((bs, RET_HEADS, RET_DK, RET_DV), lambda i: (i, 0, 0, 0))
    ssm_blk = pl.BlockSpec((bs, SSD_WIDTH, SSD_STATE), lambda i: (i, 0, 0))
    return pl.pallas_call(
        functools.partial(_mix_sample_kernel, bs=bs),
        out_shape=[
            jax.ShapeDtypeStruct((n, RET_WIDTH + SSD_WIDTH), F32),
            jax.ShapeDtypeStruct(s_ret.shape, F32),
            jax.ShapeDtypeStruct(s_ssm.shape, F32),
            jax.ShapeDtypeStruct(s_conv.shape, F32),
        ],
        grid=(n // bs,),
        in_specs=[blk2(proj.shape[1]), ret_blk, ssm_blk, blk2(s_conv.shape[1])] + [_resident(a.shape) for a in consts],
        out_specs=[blk2(RET_WIDTH + SSD_WIDTH), ret_blk, ssm_blk, blk2(s_conv.shape[1])],
        compiler_params=_params(("arbitrary",)),
        name="mix_sample",
    )(proj, s_ret, s_ssm, s_conv, *consts)


def _softmax_rows(s):
    m = jnp.max(s, axis=-1, keepdims=True)
    p = jnp.exp(s - m)
    return p / jnp.sum(p, axis=-1, keepdims=True)


def _xattn_prompt_kernel(h_ref, g_ref, wq_ref, mk_ref, mv_ref, wo_ref, o_ref, att_s):
    h = h_ref[...]
    c = _rmsnorm(h, g_ref[...]).astype(BF16)
    qx = _dot(c, wq_ref[...]).astype(BF16)
    for hh in range(X_HEADS):
        lanes = slice(hh * X_HEAD_DIM, (hh + 1) * X_HEAD_DIM)
        s = _dot_nt(qx[:, lanes], mk_ref[0, :, lanes].astype(BF16)) * (X_HEAD_DIM ** -0.5)
        att = _softmax_rows(s).astype(BF16)
        att_s[:, lanes] = _dot(att, mv_ref[0, :, lanes].astype(BF16)).astype(BF16)
    o_ref[...] = h + _dot(att_s[...], wo_ref[...])


def _xattn_prompt(h, gain, w_q, mem_k, mem_v, w_o, *, batch, seq, tm):
    nt = seq // tm
    d = h.shape[1]
    row = pl.BlockSpec((tm, d), lambda b, t: (b * nt + t, 0))
    mem = pl.BlockSpec((1, MEM_TOKENS, d), lambda b, t: (b, 0, 0))
    return pl.pallas_call(
        _xattn_prompt_kernel,
        out_shape=jax.ShapeDtypeStruct(h.shape, F32),
        grid=(batch, nt),
        in_specs=[row, _resident(gain.shape), _resident(w_q.shape), mem, mem, _resident(w_o.shape)],
        out_specs=row,
        scratch_shapes=[pltpu.VMEM((tm, d), BF16)],
        compiler_params=_params(("arbitrary", "arbitrary")),
        name="xattn_prompt",
    )(h, gain, w_q, mem_k, mem_v, w_o)


def _rope_angles(pos):
    half = RET_DK // 2
    inv_freq = ROPE_BASE ** (-jnp.arange(half, dtype=F32) / half)
    ang = pos.astype(F32)[:, None] * inv_freq[None, :]
    return jnp.cos(ang), jnp.sin(ang)


def _rope_tables(pos):
    cos, sin = _rope_angles(pos)
    return jnp.concatenate([cos, cos], axis=-1), jnp.concatenate([-sin, sin], axis=-1)


def _retention_decay_tables(chunk):
    log_g = jnp.log1p(-jnp.exp2(-5.0 - jnp.arange(RET_HEADS, dtype=F32)))
    idx = jnp.arange(chunk, dtype=F32)
    diff = idx[:, None] - idx[None, :]
    causal = diff >= 0
    dmat = jnp.where(causal[None], jnp.exp(log_g[:, None, None] * jnp.where(causal, diff, 0.0)[None]), 0.0)
    q_dec = jnp.exp(log_g[:, None] * (idx[None, :] + 1.0))
    k_dec = jnp.exp(log_g[:, None] * (chunk - 1.0 - idx[None, :]))
    c_dec = jnp.exp(log_g * chunk)
    wide = lambda x: jnp.broadcast_to(x[..., None], x.shape + (V7X_LANES,))
    return dmat, wide(q_dec), k_dec, wide(c_dec)


def kernel(x_prompt, x_sample, mem_prompt, state_ret, state_ssm, state_conv, cache_mem_k, cache_mem_v, ffn1_norm,
           ffn1_w1, ffn1_w3, ffn1_w2, mix_norm, w_in, ret_gn_gain, conv_w, conv_b, dt_bias, A_log, D_skip, ssd_norm,
           w_out, x_norm, mem_norm, w_xq, w_xk, w_xv, w_xo, ffn2_norm, ffn2_w1, ffn2_w3, ffn2_w2, final_norm):
    bp, lp, d = x_prompt.shape
    bsz = x_sample.shape[0]
    depth = ffn1_w1.shape[0]
    row = lambda v: v.reshape(1, -1).astype(F32)
    lane_pad = lambda v: jnp.pad(row(v), ((0, 0), (0, DT_PAD - v.shape[-1])))

    cos_p, sin_p = _rope_tables(jnp.arange(lp))
    cos_pt, sin_pt = (a.T for a in _rope_angles(jnp.arange(lp)))
    cos_s, sin_s = _rope_tables(PAST_LEN + jnp.arange(x_sample.shape[1]))
    dmat, q_dec, k_dec, c_dec = _retention_decay_tables(CHUNK)
    gamma1 = _retention_decay_tables(1)[3]

    y_p = x_prompt.reshape(bp * lp, d)
    y_s = x_sample.reshape(bsz, d)
    outs = {k: [] for k in ("ret_p", "ssm_p", "conv_p", "memk", "memv", "ret_s", "ssm_s", "conv_s")}
    for l in range(depth):
        bf = lambda w: w[l].astype(BF16)
        w_in_f = w_in[l]
        w_in_l = jnp.pad(w_in_f, ((0, 0), (0, IN_PROJ_PAD - IN_PROJ_WIDTH))).astype(BF16)
        w_a = jnp.concatenate([w_in_f[:, Q_OFF:K_OFF], w_in_f[:, V_OFF:XBC_OFF]], axis=1).astype(BF16)
        w_kt = w_in_f[:, K_OFF:V_OFF].T.astype(BF16)
        w_xbc = w_in_f[:, XBC_OFF:DT_OFF].astype(BF16)
        w_dt = jnp.pad(jnp.tile(w_in_f[:, DT_OFF:], (1, 2)), ((0, 0), (0, DT_PAD - 2 * SSD_HEADS))).astype(BF16)
        f1 = (row(ffn1_norm[l]), bf(ffn1_w1), bf(ffn1_w3), bf(ffn1_w2))
        f2 = (row(ffn2_norm[l]), bf(ffn2_w1), bf(ffn2_w3), bf(ffn2_w2))
        shared = (row(ret_gn_gain[l]), conv_w[l], row(conv_b[l]))
        ssd_tail = (lane_pad(A_log[l]), row(jnp.repeat(D_skip[l], SSD_HEAD_DIM)), row(ssd_norm[l]))
        w_out_l, w_xq_l, w_xo_l = bf(w_out), bf(w_xq), bf(w_xo)

        mk, mv, mk4, mv4 = _memkv(mem_prompt, row(mem_norm[l]), bf(w_xk), bf(w_xv))

        y_p = _ffn(y_p, *f1, tm=512)
        tm_p = 512
        y_p, ret_p, ssm_p, conv_p = _mix_prompt(
            y_p, row(mix_norm[l]), w_a, w_kt, w_xbc, w_dt, cos_p, sin_p, cos_pt, sin_pt, dmat,
            jnp.tile(q_dec, (1, tm_p // CHUNK, 1)), jnp.tile(k_dec, (1, tm_p // CHUNK)), c_dec,
            *shared, lane_pad(jnp.tile(dt_bias[l], 2)), *ssd_tail, w_out_l, batch=bp, seq=lp, tm=tm_p)

        y_s = _ffn(y_s, *f1, tm=bsz)
        proj_s, = _linear(y_s, [w_in_l], gain=row(mix_norm[l]), tm=bsz)
        mix_s, ret_s, ssm_s, conv_s = _mix_sample(
            proj_s, state_ret[l], state_ssm[l].reshape(bsz, SSD_WIDTH, SSD_STATE),
            state_conv[l].reshape(bsz, (CONV_WIDTH - 1) * CONV_CH),
            cos_s, sin_s, gamma1, *shared, lane_pad(dt_bias[l]), *ssd_tail, bs=8)
        y_s, = _linear(mix_s, [w_out_l], res=y_s, tm=bsz)
        q_s, = _linear(y_s, [w_xq_l], gain=row(x_norm[l]), tm=bsz)

        y_p = _xattn_prompt(y_p, row(x_norm[l]), w_xq_l, mk, mv, w_xo_l, batch=bp, seq=lp, tm=512)
        y_p, att_s = _ffn(y_p, *f2, final_gain=row(final_norm) if l == depth - 1 else None,
                          attention=(q_s.reshape(bsz, X_HEADS, X_HEAD_DIM), cache_mem_k[l], cache_mem_v[l]), tm=512)
        y_s, = _linear(att_s.reshape(bsz, d), [w_xo_l], res=y_s, tm=bsz)
        y_s = _ffn(y_s, *f2, final_gain=row(final_norm) if l == depth - 1 else None, tm=bsz)

        outs["ret_p"].append(ret_p)
        outs["ssm_p"].append(ssm_p.reshape(bp, SSD_HEADS, SSD_HEAD_DIM, SSD_STATE))
        outs["conv_p"].append(conv_p)
        outs["memk"].append(mk4)
        outs["memv"].append(mv4)
        outs["ret_s"].append(ret_s)
        outs["ssm_s"].append(ssm_s.reshape(bsz, SSD_HEADS, SSD_HEAD_DIM, SSD_STATE))
        outs["conv_s"].append(conv_s.reshape(bsz, CONV_WIDTH - 1, CONV_CH))

    st = lambda k: jnp.stack(outs[k])
    return (y_p.reshape(bp, lp, d), y_s.reshape(bsz, x_sample.shape[1], d), st("ret_p"), st("ssm_p"), st("conv_p"),
            st("memk"), st("memv"), st("ret_s"), st("ssm_s"), st("conv_s"))
```

```python
import functools

import jax
import jax.numpy as jnp
from jax import lax
from jax.experimental import pallas as pl
from jax.experimental.pallas import tpu as pltpu

F32 = jnp.float32
BF16 = jnp.bfloat16

D_MODEL = 1024
D_FF = 2816
PAST_LEN = 16384
RET_HEADS = 4
RET_DK = 128
RET_DV = 128
RET_WIDTH = RET_HEADS * RET_DV
SSD_HEADS = 8
SSD_HEAD_DIM = 64
SSD_WIDTH = SSD_HEADS * SSD_HEAD_DIM
SSD_GROUPS = 2
SSD_STATE = 128
SSD_PAIRS = SSD_HEADS // 2
CONV_WIDTH = 4
CONV_CH = SSD_WIDTH + 2 * SSD_GROUPS * SSD_STATE
CHUNK = 128
MEM_TOKENS = 256
X_HEADS = 4
X_HEAD_DIM = D_MODEL // X_HEADS
ROPE_BASE = 10000.0
EPS = 1e-6

Q_OFF = 0
K_OFF = Q_OFF + RET_HEADS * RET_DK
V_OFF = K_OFF + RET_HEADS * RET_DK
G_OFF = V_OFF + RET_WIDTH
Z_OFF = G_OFF + RET_WIDTH
XBC_OFF = Z_OFF + SSD_WIDTH
DT_OFF = XBC_OFF + CONV_CH
IN_PROJ_WIDTH = DT_OFF + SSD_HEADS

PQ_OFF = 0
PV_OFF = PQ_OFF + RET_HEADS * RET_DK
PG_OFF = PV_OFF + RET_WIDTH
PZ_OFF = PG_OFF + RET_WIDTH
P_WIDTH = PZ_OFF + SSD_WIDTH

V7X_LANES = 128
V7X_SUBLANES = 8
V7X_VMEM_LIMIT_BYTES = 56 * 1024 * 1024
DT_PAD = V7X_LANES
IN_PROJ_PAD = DT_OFF + DT_PAD
CONV_HIST = V7X_SUBLANES
CONV_SLABS = CONV_CH // V7X_LANES


def _params(sem):
    return pltpu.CompilerParams(dimension_semantics=sem, vmem_limit_bytes=V7X_VMEM_LIMIT_BYTES)


def _resident(shape):
    zeros = (0,) * len(shape)
    return pl.BlockSpec(shape, lambda *_: zeros, pipeline_mode=pl.Buffered(1))


def _rmsnorm(x, gain):
    ms = jnp.mean(x * x, axis=-1, keepdims=True)
    return x * lax.rsqrt(ms + EPS) * gain


def _dot(a, b):
    return jnp.dot(a, b, preferred_element_type=F32)


def _dot_nt(a, b):
    return lax.dot_general(a, b, (((1,), (1,)), ((), ())), preferred_element_type=F32)


def _dot_tn(a, b):
    return lax.dot_general(a, b, (((0,), (0,)), ((), ())), preferred_element_type=F32)


def _softplus(x):
    return jnp.maximum(x, 0.0) + jnp.log1p(jnp.exp(-jnp.abs(x)))


def _rotary(x, cos2, sin2):
    return x * cos2 + pltpu.roll(x, RET_DK // 2, axis=1) * sin2


def _group_norm(y):
    mu = jnp.mean(y, axis=-1, keepdims=True)
    d = y - mu
    var = jnp.mean(d * d, axis=-1, keepdims=True)
    return d * lax.rsqrt(var + EPS)


def _sample_attention(q_ref, k_ref, v_ref, o_ref):
    slabs = MEM_TOKENS * X_HEADS // V7X_SUBLANES
    for b in range(q_ref.shape[0]):
        k3 = k_ref[b].reshape(slabs, V7X_SUBLANES, X_HEAD_DIM)
        v3 = v_ref[b].reshape(slabs, V7X_SUBLANES, X_HEAD_DIM)
        q4 = q_ref[b]
        q8 = jnp.concatenate([q4] * (V7X_SUBLANES // X_HEADS), axis=0)
        s = jnp.sum(k3 * q8[None], axis=-1, keepdims=True) * (X_HEAD_DIM ** -0.5)
        m8 = jnp.max(s, axis=0)
        m4 = jnp.maximum(m8[0:X_HEADS], m8[X_HEADS:])
        p = jnp.exp(s - jnp.concatenate([m4, m4], axis=0)[None])
        acc = jnp.sum(p * v3, axis=0)
        den = jnp.sum(p, axis=0)
        o_ref[b] = (acc[0:X_HEADS] + acc[X_HEADS:]) / (den[0:X_HEADS] + den[X_HEADS:])


def _ffn_kernel(*refs, final_norm, with_attention, with_mixers):
    refs = list(refs)
    x_ref, g_ref, w1_ref, w3_ref, w2_ref = refs[:5]
    del refs[:5]
    fg_ref = refs.pop(0) if final_norm else None
    attn_in = [refs.pop(0) for _ in range(3)] if with_attention else None
    mixer_in = [refs.pop(0) for _ in range(N_MIXER_IN)] if with_mixers else None
    o_ref = refs.pop(0)
    x = x_ref[...]
    xn = _rmsnorm(x, g_ref[...]).astype(BF16)
    a = _dot(xn, w1_ref[...])
    b = _dot(xn, w3_ref[...])
    hidden = (jax.nn.silu(a) * b).astype(BF16)
    out = x + 0.5 * _dot(hidden, w2_ref[...])
    if final_norm:
        out = _rmsnorm(out, fg_ref[...])
    o_ref[...] = out
    if with_attention:
        _sample_attention(*attn_in, refs.pop(0))
    if with_mixers:
        _sample_mixers(*mixer_in, *refs[:N_MIXER_OUT])


def _ffn(x, gain, w1, w3, w2, final_gain=None, attention=None, mixers=None, *, tm):
    t, d = x.shape
    steps = t // tm
    row = pl.BlockSpec((tm, d), lambda i: (i, 0))
    ins = [x, gain, w1, w3, w2]
    specs = [row, _resident(gain.shape), _resident(w1.shape), _resident(w3.shape), _resident(w2.shape)]
    out_shape, out_specs = [jax.ShapeDtypeStruct((t, d), F32)], [row]
    if final_gain is not None:
        ins.append(final_gain)
        specs.append(_resident(final_gain.shape))
    if attention is not None:
        q, cache_k, cache_v = attention
        bs = q.shape[0] // steps
        row_s = pl.BlockSpec((bs, X_HEADS, X_HEAD_DIM), lambda i: (i, 0, 0))
        mem_s = pl.BlockSpec((bs, MEM_TOKENS, X_HEADS, X_HEAD_DIM), lambda i: (i, 0, 0, 0))
        ins += [q, cache_k, cache_v]
        specs += [row_s, mem_s, mem_s]
        out_shape.append(jax.ShapeDtypeStruct(q.shape, F32))
        out_specs.append(row_s)
    if mixers is not None:
        m_ins, m_specs, m_shape, m_out_specs = _sample_mixer_specs(*mixers, steps)
        ins += m_ins
        specs += m_specs
        out_shape += m_shape
        out_specs += m_out_specs
    outs = pl.pallas_call(
        functools.partial(_ffn_kernel, final_norm=final_gain is not None, with_attention=attention is not None,
                          with_mixers=mixers is not None),
        out_shape=out_shape,
        grid=(steps,),
        in_specs=specs,
        out_specs=out_specs,
        compiler_params=_params(("arbitrary",)),
        name="ffn",
    )(*ins)
    return outs if len(outs) > 1 else outs[0]


def _linear_kernel(*refs, has_norm, has_res, n_w):
    refs = list(refs)
    x_ref = refs.pop(0)
    g_ref = refs.pop(0) if has_norm else None
    r_ref = refs.pop(0) if has_res else None
    w_refs, o_refs = refs[:n_w], refs[n_w:]
    x = x_ref[...]
    if has_norm:
        x = _rmsnorm(x, g_ref[...])
    xb = x.astype(BF16)
    for w_ref, o_ref in zip(w_refs, o_refs, strict=True):
        y = _dot(xb, w_ref[...])
        if has_res:
            y = r_ref[...] + y
        o_ref[...] = y


def _linear(x, weights, gain=None, res=None, *, tm):
    t, k = x.shape
    ins, specs = [x], [pl.BlockSpec((tm, k), lambda i: (i, 0))]
    if gain is not None:
        ins.append(gain)
        specs.append(_resident(gain.shape))
    if res is not None:
        ins.append(res)
        specs.append(pl.BlockSpec((tm, res.shape[1]), lambda i: (i, 0)))
    for w in weights:
        ins.append(w)
        specs.append(_resident(w.shape))
    outs = pl.pallas_call(
        functools.partial(_linear_kernel, has_norm=gain is not None, has_res=res is not None, n_w=len(weights)),
        out_shape=[jax.ShapeDtypeStruct((t, w.shape[1]), F32) for w in weights],
        grid=(t // tm,),
        in_specs=specs,
        out_specs=[pl.BlockSpec((tm, w.shape[1]), lambda i: (i, 0)) for w in weights],
        compiler_params=_params(("arbitrary",)),
        name="linear",
    )(*ins)
    return outs


def _memkv_kernel(x_ref, g_ref, wk_ref, wv_ref, k_ref, v_ref, k4_ref, v4_ref):
    xb = _rmsnorm(x_ref[...], g_ref[...]).astype(BF16)
    for w_ref, o_ref, o4_ref in ((wk_ref, k_ref, k4_ref), (wv_ref, v_ref, v4_ref)):
        y = _dot(xb, w_ref[...])
        o_ref[0] = y
        o4_ref[0] = y.reshape(MEM_TOKENS, X_HEADS, X_HEAD_DIM)


def _memkv(mem, gain, w_k, w_v):
    batch, m, d = mem.shape
    flat = jax.ShapeDtypeStruct((batch, m, d), F32)
    split = jax.ShapeDtypeStruct((batch, m, X_HEADS, X_HEAD_DIM), F32)
    flat_blk = pl.BlockSpec((1, m, d), lambda b: (b, 0, 0))
    split_blk = pl.BlockSpec((1, m, X_HEADS, X_HEAD_DIM), lambda b: (b, 0, 0, 0))
    return pl.pallas_call(
        _memkv_kernel,
        out_shape=[flat, flat, split, split],
        grid=(batch,),
        in_specs=[pl.BlockSpec((m, d), lambda b: (b, 0)), _resident(gain.shape), _resident(w_k.shape),
                  _resident(w_v.shape)],
        out_specs=[flat_blk, flat_blk, split_blk, split_blk],
        compiler_params=_params(("arbitrary",)),
        name="memkv",
    )(mem.reshape(batch * m, d), gain, w_k, w_v)


def _cumsum_chunks(x):
    pos = lax.broadcasted_iota(jnp.int32, x.shape, 0) % CHUNK
    step = 1
    while step < CHUNK:
        x = x + jnp.where(pos >= step, pltpu.roll(x, step, axis=0), 0.0)
        step *= 2
    return x


def _mixp_project(tile, h_ref, g_ref, wa_ref, wkt_ref, wxbc_ref, wdt_ref, cos_ref, sin_ref, cost_ref, sint_ref,
                  qdec_ref, kdec_ref, cw_ref, cb_ref, dtb_ref, alog_ref, conv_ref, xbc_s, stage, *, tm, nt):
    q_s, qd_s, kr_s, kd_s, v_s, g_s, z_s, xc_s, pk_s = stage
    row_start = (tile % nt) == 0

    u = _rmsnorm(h_ref[...], g_ref[...]).astype(BF16)
    qvgz = _dot(u, wa_ref[...])
    kt = _dot_nt(wkt_ref[...], u)
    xbc = _dot(u, wxbc_ref[...])
    dt_raw = _dot(u, wdt_ref[...])

    for sl in range(CONV_SLABS):
        lanes = slice(sl * V7X_LANES, (sl + 1) * V7X_LANES)
        xbc_s[sl, 0:CONV_HIST, :] = jnp.where(row_start, 0.0, xbc_s[sl, 0:CONV_HIST, :])
        xbc_s[sl, CONV_HIST:CONV_HIST + tm, :] = xbc[:, lanes]
        conv = cb_ref[:, lanes]
        for i in range(CONV_WIDTH):
            off = CONV_HIST - (CONV_WIDTH - 1) + i
            conv = conv + cw_ref[i:i + 1, lanes] * xbc_s[sl, off:off + tm, :]
        xc_s[sl] = jax.nn.silu(conv)
        conv_ref[0, :, lanes] = xbc_s[sl, CONV_HIST + tm - (CONV_WIDTH - 1):CONV_HIST + tm, :]
        xbc_s[sl, 0:CONV_HIST, :] = xbc_s[sl, tm:tm + CONV_HIST, :]

    dt = _softplus(dt_raw + dtb_ref[...])
    a_cum = _cumsum_chunks(dt * (-jnp.exp(alog_ref[...])))
    head_lanes = lax.broadcasted_iota(jnp.int32, (tm, DT_PAD), 1) < SSD_HEADS
    pk_s[...] = jnp.where(head_lanes, a_cum, dt)

    cos2, sin2 = cos_ref[...], sin_ref[...]
    cos_t, sin_t = cost_ref[...], sint_ref[...]
    half = RET_DK // 2
    for hh in range(RET_HEADS):
        lanes = slice(hh * RET_DK, (hh + 1) * RET_DK)
        qr = _rotary(qvgz[:, PQ_OFF + hh * RET_DK:PQ_OFF + (hh + 1) * RET_DK], cos2, sin2)
        q_s[:, lanes] = qr.astype(BF16)
        qd_s[:, lanes] = (qr * qdec_ref[hh]).astype(BF16)
        k1 = kt[hh * RET_DK:hh * RET_DK + half, :]
        k2 = kt[hh * RET_DK + half:(hh + 1) * RET_DK, :]
        kr = jnp.concatenate([k1 * cos_t - k2 * sin_t, k1 * sin_t + k2 * cos_t], axis=0) * (RET_DK ** -0.5)
        kr_s[lanes, :] = kr.astype(BF16)
        kd_s[lanes, :] = (kr * kdec_ref[hh:hh + 1, :]).astype(BF16)
    v_s[...] = qvgz[:, PV_OFF:PV_OFF + RET_WIDTH].astype(BF16)
    g_s[...] = jax.nn.silu(qvgz[:, PG_OFF:PG_OFF + RET_WIDTH])
    z_s[...] = jax.nn.silu(qvgz[:, PZ_OFF:PZ_OFF + SSD_WIDTH])


def _mixp_heads(tile, h_ref, dmat_ref, cdec_ref, gn_ref, dsk_ref, sn_ref, wout_ref, o_ref, ret_ref, ssm_ref,
                mix_s, sret_s, sssm_s, stage, *, tm, nt):
    q_s, qd_s, kr_s, kd_s, v_s, g_s, z_s, xc_s, pk_s = stage
    nc = tm // CHUNK
    row_start = (tile % nt) == 0

    row_i = lax.broadcasted_iota(jnp.int32, (CHUNK, CHUNK), 0)
    col_i = lax.broadcasted_iota(jnp.int32, (CHUNK, CHUNK), 1)
    causal = row_i >= col_i
    low_lanes = col_i < SSD_HEAD_DIM
    bd_rows = lax.broadcasted_iota(jnp.int32, (2 * SSD_STATE, 2 * SSD_HEAD_DIM), 0)
    bd_cols = lax.broadcasted_iota(jnp.int32, (2 * SSD_STATE, 2 * SSD_HEAD_DIM), 1)
    bd_mask = (bd_rows < SSD_STATE) == (bd_cols < SSD_HEAD_DIM)

    inner, kv = {}, {}
    scores, c_exp, st_inc, st_dec = {}, {}, {}, {}
    for c in range(nc):
        rows = slice(c * CHUNK, (c + 1) * CHUNK)
        for hh in range(RET_HEADS):
            lanes = slice(hh * RET_DK, (hh + 1) * RET_DK)
            inner[c, hh] = (_dot(q_s[rows, lanes], kr_s[lanes, rows]) * dmat_ref[hh]).astype(BF16)
            kv[c, hh] = _dot(kd_s[lanes, rows], v_s[rows, lanes])

        pk = pk_s[rows, :]
        pk_t = pk.T
        a_t = pk_t[0:SSD_HEADS, :]
        dt_t = pk_t[SSD_HEADS:2 * SSD_HEADS, :]
        a_last = a_t[:, CHUNK - 1:CHUNK]
        w_t = jnp.exp(a_last - a_t) * dt_t
        chunk_dec = jnp.exp(a_last)
        for grp in range(SSD_GROUPS):
            cg = xc_s[SSD_PAIRS + SSD_GROUPS + grp, rows, :]
            b_t = xc_s[SSD_PAIRS + grp, rows, :].T
            gmat = _dot(cg.astype(BF16), b_t.astype(BF16))
            for pj in range(SSD_PAIRS // SSD_GROUPS):
                j = grp * (SSD_PAIRS // SSD_GROUPS) + pj
                heads = (2 * j, 2 * j + 1)
                x_pair = xc_s[j, rows, :].astype(BF16)
                b_w = jnp.concatenate([b_t * w_t[hd:hd + 1, :] for hd in heads], axis=0).astype(BF16)
                st_inc[c, j] = jnp.where(bd_mask, _dot(b_w, x_pair), 0.0)
                st_dec[c, j] = jnp.concatenate(
                    [jnp.broadcast_to(chunk_dec[hd:hd + 1, :], (SSD_STATE, 2 * SSD_HEAD_DIM)) for hd in heads], axis=0)
                for hd in heads:
                    a_col = jnp.broadcast_to(pk[:, hd:hd + 1], (CHUNK, CHUNK))
                    seg = a_col - a_t[hd:hd + 1, :]
                    lmat = jnp.where(causal, jnp.exp(jnp.where(causal, seg, 0.0)), 0.0)
                    scores[c, hd] = (gmat * lmat * dt_t[hd:hd + 1, :]).astype(BF16)
                    c_exp[c, hd] = (cg * jnp.exp(a_col)).astype(BF16)

    s_in, st_in = {}, {}
    for hh in range(RET_HEADS):
        s = jnp.where(row_start, 0.0, sret_s[hh])
        for c in range(nc):
            s_in[c, hh] = s.astype(BF16)
            s = cdec_ref[hh:hh + 1, :] * s + kv[c, hh]
        sret_s[hh] = s
        ret_ref[0, hh] = s
    for j in range(SSD_PAIRS):
        s = jnp.where(row_start, 0.0, sssm_s[j])
        for c in range(nc):
            st_in[c, j] = s.astype(BF16)
            s = st_dec[c, j] * s + st_inc[c, j]
        sssm_s[j] = s
        ssm_ref[0, j * 2 * SSD_HEAD_DIM:(j + 1) * 2 * SSD_HEAD_DIM, :] = (s[0:SSD_STATE] + s[SSD_STATE:]).T

    for c in range(nc):
        rows = slice(c * CHUNK, (c + 1) * CHUNK)
        for hh in range(RET_HEADS):
            lanes = slice(hh * RET_DV, (hh + 1) * RET_DV)
            lhs = jnp.concatenate([inner[c, hh], qd_s[rows, lanes]], axis=1)
            rhs = jnp.concatenate([v_s[rows, lanes], s_in[c, hh]], axis=0)
            r = _group_norm(_dot(lhs, rhs)) * gn_ref[:, lanes]
            mix_s[rows, lanes] = (g_s[rows, lanes] * r).astype(BF16)
        ys = []
        for j in range(SSD_PAIRS):
            heads = (2 * j, 2 * j + 1)
            x_pair = xc_s[j, rows, :]
            x_bd = jnp.concatenate([jnp.where(low_lanes, x_pair, 0.0), jnp.where(low_lanes, 0.0, x_pair)], axis=0)
            lhs = jnp.concatenate([scores[c, heads[0]], scores[c, heads[1]], c_exp[c, heads[0]], c_exp[c, heads[1]]],
                                  axis=1)
            rhs = jnp.concatenate([x_bd.astype(BF16), st_in[c, j]], axis=0)
            ys.append(_dot(lhs, rhs) + dsk_ref[:, j * 2 * SSD_HEAD_DIM:(j + 1) * 2 * SSD_HEAD_DIM] * x_pair)
        y_all = jnp.concatenate(ys, axis=1) * z_s[rows, :]
        mix_s[rows, RET_WIDTH:RET_WIDTH + SSD_WIDTH] = _rmsnorm(y_all, sn_ref[...]).astype(BF16)

    o_ref[...] = h_ref[...] + _dot(mix_s[...], wout_ref[...])


def _mix_prompt_kernel(h_ref, hp_ref, g_ref, wa_ref, wkt_ref, wxbc_ref, wdt_ref, cos_ref, sin_ref, cost_ref, sint_ref,
                       dmat_ref, qdec_ref, kdec_ref, cdec_ref, gn_ref, cw_ref, cb_ref, dtb_ref, alog_ref, dsk_ref,
                       sn_ref, wout_ref,
                       o_ref, ret_ref, ssm_ref, conv_ref,
                       xbc_s, mix_s, sret_s, sssm_s, *stage_s, tm, nt, n_tiles):
    s = pl.program_id(0)
    sets = (tuple(r.at[0] for r in stage_s), tuple(r.at[1] for r in stage_s))

    @pl.when(s == 0)
    def _():
        for r in stage_s:
            r[...] = jnp.zeros_like(r)
        sret_s[...] = jnp.zeros_like(sret_s)
        sssm_s[...] = jnp.zeros_like(sssm_s)
        xbc_s[...] = jnp.zeros_like(xbc_s)

    def step(write_set, read_set):
        _mixp_project(jnp.minimum(s, n_tiles - 1), h_ref, g_ref, wa_ref, wkt_ref, wxbc_ref, wdt_ref, cos_ref, sin_ref,
                      cost_ref, sint_ref, qdec_ref, kdec_ref, cw_ref, cb_ref, dtb_ref, alog_ref, conv_ref, xbc_s,
                      write_set, tm=tm, nt=nt)
        _mixp_heads(jnp.maximum(s - 1, 0), hp_ref, dmat_ref, cdec_ref, gn_ref, dsk_ref, sn_ref, wout_ref, o_ref, ret_ref,
                    ssm_ref, mix_s, sret_s, sssm_s, read_set, tm=tm, nt=nt)

    @pl.when(s % 2 == 0)
    def _():
        step(sets[0], sets[1])

    @pl.when(s % 2 == 1)
    def _():
        step(sets[1], sets[0])


def _mix_prompt(h, gain, w_a, w_kt, w_xbc, w_dt, cos2, sin2, cos_t, sin_t, dmat, qdec, kdec, cdec, gn, conv_w, conv_b,
                dt_bias, a_log, dskip, ssd_gain, w_out, *, batch, seq, tm):
    nt = seq // tm
    n_tiles = batch * nt
    d = h.shape[1]
    cur = lambda s: jnp.minimum(s, n_tiles - 1)
    prev = lambda s: jnp.maximum(s - 1, 0)
    row_cur = pl.BlockSpec((tm, d), lambda s: (cur(s), 0))
    row_prev = pl.BlockSpec((tm, d), lambda s: (prev(s), 0))
    pos = pl.BlockSpec((tm, RET_DK), lambda s: (cur(s) % nt, 0))
    pos_t = pl.BlockSpec((RET_DK // 2, tm), lambda s: (0, cur(s) % nt))
    consts = [gain, w_a, w_kt, w_xbc, w_dt]
    tail = [dmat, qdec, kdec, cdec, gn, conv_w, conv_b, dt_bias, a_log, dskip, ssd_gain, w_out]
    stage = [
        pltpu.VMEM((2, tm, RET_HEADS * RET_DK), BF16),
        pltpu.VMEM((2, tm, RET_HEADS * RET_DK), BF16),
        pltpu.VMEM((2, RET_HEADS * RET_DK, tm), BF16),
        pltpu.VMEM((2, RET_HEADS * RET_DK, tm), BF16),
        pltpu.VMEM((2, tm, RET_WIDTH), BF16),
        pltpu.VMEM((2, tm, RET_WIDTH), F32),
        pltpu.VMEM((2, tm, SSD_WIDTH), F32),
        pltpu.VMEM((2, CONV_SLABS, tm, V7X_LANES), F32),
        pltpu.VMEM((2, tm, DT_PAD), F32),
    ]
    return pl.pallas_call(
        functools.partial(_mix_prompt_kernel, tm=tm, nt=nt, n_tiles=n_tiles),
        out_shape=[
            jax.ShapeDtypeStruct(h.shape, F32),
            jax.ShapeDtypeStruct((batch, RET_HEADS, RET_DK, RET_DV), F32),
            jax.ShapeDtypeStruct((batch, SSD_WIDTH, SSD_STATE), F32),
            jax.ShapeDtypeStruct((batch, CONV_WIDTH - 1, CONV_CH), F32),
        ],
        grid=(n_tiles + 1,),
        in_specs=([row_cur, row_prev] + [_resident(a.shape) for a in consts] + [pos, pos, pos_t, pos_t]
                  + [_resident(a.shape) for a in tail]),
        out_specs=[
            row_prev,
            pl.BlockSpec((1, RET_HEADS, RET_DK, RET_DV), lambda s: (prev(s) // nt, 0, 0, 0)),
            pl.BlockSpec((1, SSD_WIDTH, SSD_STATE), lambda s: (prev(s) // nt, 0, 0)),
            pl.BlockSpec((1, CONV_WIDTH - 1, CONV_CH), lambda s: (cur(s) // nt, 0, 0)),
        ],
        scratch_shapes=[
            pltpu.VMEM((CONV_SLABS, CONV_HIST + tm, V7X_LANES), F32),
            pltpu.VMEM((tm, RET_WIDTH + SSD_WIDTH), BF16),
            pltpu.VMEM((RET_HEADS, RET_DK, RET_DV), F32),
            pltpu.VMEM((SSD_PAIRS, 2 * SSD_STATE, 2 * SSD_HEAD_DIM), F32),
        ] + stage,
        compiler_params=_params(("arbitrary",)),
        name="mix_prompt",
    )(h, h, *consts, cos2, sin2, cos_t, sin_t, *tail)


def _split_hi_lo(x):
    hi = x.astype(BF16).astype(F32)
    return hi, x - hi


def _sample_mixers(proj_ref, sret_ref, sssm_ref, sconv_ref, cos_ref, sin_ref, gam_ref,
                   gn_ref, cw_ref, cb_ref, dtb_ref, alog_ref, dsk_ref, sn_ref,
                   mix_ref, oret_ref, ossm_ref, oconv_ref):
    bs = proj_ref.shape[1]
    proj = proj_ref[0]
    xbc = proj[:, XBC_OFF:DT_OFF]
    hist = sconv_ref[0]
    taps = [hist[:, i * CONV_CH:(i + 1) * CONV_CH] for i in range(CONV_WIDTH - 1)] + [xbc]
    conv = cb_ref[...]
    for i in range(CONV_WIDTH):
        conv = conv + cw_ref[i:i + 1, :] * taps[i]
    oconv_ref[0] = jnp.concatenate(taps[1:], axis=1)
    xc = jax.nn.silu(conv)

    dt = _softplus(proj[:, DT_OFF:IN_PROJ_PAD] + dtb_ref[...])

    sub_i = lax.broadcasted_iota(jnp.int32, (bs, V7X_LANES), 0)
    row_of = lax.broadcasted_iota(jnp.int32, (4 * bs, V7X_LANES), 0) % bs
    cos2 = cos_ref[...]
    sin2 = sin_ref[...]

    def outer_lhs(x):
        hi, lo = _split_hi_lo(x)
        return jnp.concatenate([hi, hi, lo, lo], axis=0)

    def outer_rhs(x):
        hi, lo = _split_hi_lo(x)
        return jnp.concatenate([hi, lo, hi, lo], axis=0).astype(BF16)

    def only_sample(x4, b):
        return jnp.where(row_of == b, x4, 0.0).astype(BF16)

    for hh in range(RET_HEADS):
        lanes = slice(hh * RET_DK, (hh + 1) * RET_DK)
        qr = _rotary(proj[:, Q_OFF + hh * RET_DK:Q_OFF + (hh + 1) * RET_DK], cos2, sin2)
        kr = _rotary(proj[:, K_OFF + hh * RET_DK:K_OFF + (hh + 1) * RET_DK], cos2, sin2) * (RET_DK ** -0.5)
        vh = proj[:, V_OFF + hh * RET_DV:V_OFF + (hh + 1) * RET_DV]
        k4, v4, q_bf = outer_lhs(kr), outer_rhs(vh), qr.astype(BF16)
        gamma = gam_ref[hh:hh + 1, :]
        y = jnp.zeros((bs, RET_DV), F32)
        for b in range(bs):
            s_old = sret_ref[b, hh]
            oret_ref[b, hh] = gamma * s_old + _dot_tn(only_sample(k4, b), v4)
            y = jnp.where(sub_i == b, _dot(q_bf, s_old.astype(BF16)), y)
        y = gamma * y + jnp.sum(qr * kr, axis=-1, keepdims=True) * vh
        r = _group_norm(y) * gn_ref[:, lanes]
        mix_ref[0, :, lanes] = jax.nn.silu(proj[:, G_OFF + hh * RET_DV:G_OFF + (hh + 1) * RET_DV]) * r

    xs = xc[:, 0:SSD_WIDTH]
    head_of_lane = lax.broadcasted_iota(jnp.int32, (bs, SSD_WIDTH), 1) // SSD_HEAD_DIM
    dec = jnp.exp(dt * (-jnp.exp(alog_ref[...])))
    dt_wide = jnp.zeros((bs, SSD_WIDTH), F32)
    dec_wide = jnp.zeros((bs, SSD_WIDTH), F32)
    for hd in range(SSD_HEADS):
        dt_wide = jnp.where(head_of_lane == hd, dt[:, hd:hd + 1], dt_wide)
        dec_wide = jnp.where(head_of_lane == hd, dec[:, hd:hd + 1], dec_wide)
    xdt = xs * dt_wide
    x4 = outer_lhs(xdt)
    ys = []
    for j in range(SSD_PAIRS):
        grp = j // (SSD_PAIRS // SSD_GROUPS)
        lanes = slice(j * 2 * SSD_HEAD_DIM, (j + 1) * 2 * SSD_HEAD_DIM)
        bg = xc[:, SSD_WIDTH + grp * SSD_STATE:SSD_WIDTH + (grp + 1) * SSD_STATE]
        c_off = SSD_WIDTH + SSD_GROUPS * SSD_STATE + grp * SSD_STATE
        cg = xc[:, c_off:c_off + SSD_STATE]
        b4, c_bf = outer_rhs(bg), cg.astype(BF16)
        y = jnp.zeros((bs, 2 * SSD_HEAD_DIM), F32)
        for b in range(bs):
            s_old = sssm_ref[b, lanes, :]
            dec_rows = jnp.concatenate(
                [jnp.broadcast_to(dec[b:b + 1, hd:hd + 1], (SSD_HEAD_DIM, SSD_STATE)) for hd in (2 * j, 2 * j + 1)],
                axis=0)
            ossm_ref[b, lanes, :] = dec_rows * s_old + _dot_tn(only_sample(x4[:, lanes], b), b4)
            y = jnp.where(sub_i == b, _dot_nt(c_bf, s_old.astype(BF16)), y)
        ys.append(dec_wide[:, lanes] * y + jnp.sum(cg * bg, axis=-1, keepdims=True) * xdt[:, lanes])
    ys = (jnp.concatenate(ys, axis=1) + dsk_ref[...] * xs) * jax.nn.silu(proj[:, Z_OFF:Z_OFF + SSD_WIDTH])
    mix_ref[0, :, RET_WIDTH:RET_WIDTH + SSD_WIDTH] = _rmsnorm(ys, sn_ref[...])


N_MIXER_IN = 14
N_MIXER_OUT = 4


def _sample_mixer_specs(proj, s_ret, s_ssm, s_conv, consts, steps):
    n = proj.shape[0]
    bs = n // steps
    grouped = lambda a: a.reshape(steps, bs, a.shape[1])
    blk3 = lambda w: pl.BlockSpec((1, bs, w), lambda i: (i, 0, 0))
    ret_blk = pl.BlockSpec((bs, RET_HEADS, RET_DK, RET_DV), lambda i: (i, 0, 0, 0))
    ssm_blk = pl.BlockSpec((bs, SSD_WIDTH, SSD_STATE), lambda i: (i, 0, 0))
    ins = [grouped(proj), s_ret, s_ssm, grouped(s_conv), *consts]
    in_specs = [blk3(proj.shape[1]), ret_blk, ssm_blk, blk3(s_conv.shape[1])] + [_resident(a.shape) for a in consts]
    out_shape = [jax.ShapeDtypeStruct((steps, bs, RET_WIDTH + SSD_WIDTH), F32), jax.ShapeDtypeStruct(s_ret.shape, F32),
                 jax.ShapeDtypeStruct(s_ssm.shape, F32), jax.ShapeDtypeStruct((steps, bs, s_conv.shape[1]), F32)]
    out_specs = [blk3(RET_WIDTH + SSD_WIDTH), ret_blk, ssm_blk, blk3(s_conv.shape[1])]
    return ins, in_specs, out_shape, out_specs


def _softmax_rows(s):
    m = jnp.max(s, axis=-1, keepdims=True)
    p = jnp.exp(s - m)
    return p / jnp.sum(p, axis=-1, keepdims=True)


def _xattn_prompt_kernel(h_ref, g_ref, wq_ref, mk_ref, mv_ref, wo_ref, o_ref, att_s):
    h = h_ref[...]
    c = _rmsnorm(h, g_ref[...]).astype(BF16)
    qx = _dot(c, wq_ref[...]).astype(BF16)
    for hh in range(X_HEADS):
        lanes = slice(hh * X_HEAD_DIM, (hh + 1) * X_HEAD_DIM)
        s = _dot_nt(qx[:, lanes], mk_ref[0, :, lanes].astype(BF16)) * (X_HEAD_DIM ** -0.5)
        att = _softmax_rows(s).astype(BF16)
        att_s[:, lanes] = _dot(att, mv_ref[0, :, lanes].astype(BF16)).astype(BF16)
    o_ref[...] = h + _dot(att_s[...], wo_ref[...])


def _xattn_prompt(h, gain, w_q, mem_k, mem_v, w_o, *, batch, seq, tm):
    nt = seq // tm
    d = h.shape[1]
    row = pl.BlockSpec((tm, d), lambda b, t: (b * nt + t, 0))
    mem = pl.BlockSpec((1, MEM_TOKENS, d), lambda b, t: (b, 0, 0))
    return pl.pallas_call(
        _xattn_prompt_kernel,
        out_shape=jax.ShapeDtypeStruct(h.shape, F32),
        grid=(batch, nt),
        in_specs=[row, _resident(gain.shape), _resident(w_q.shape), mem, mem, _resident(w_o.shape)],
        out_specs=row,
        scratch_shapes=[pltpu.VMEM((tm, d), BF16)],
        compiler_params=_params(("arbitrary", "arbitrary")),
        name="xattn_prompt",
    )(h, gain, w_q, mem_k, mem_v, w_o)


def _rope_angles(pos):
    half = RET_DK // 2
    inv_freq = ROPE_BASE ** (-jnp.arange(half, dtype=F32) / half)
    ang = pos.astype(F32)[:, None] * inv_freq[None, :]
    return jnp.cos(ang), jnp.sin(ang)


def _rope_tables(pos):
    cos, sin = _rope_angles(pos)
    return jnp.concatenate([cos, cos], axis=-1), jnp.concatenate([-sin, sin], axis=-1)


def _retention_decay_tables(chunk):
    log_g = jnp.log1p(-jnp.exp2(-5.0 - jnp.arange(RET_HEADS, dtype=F32)))
    idx = jnp.arange(chunk, dtype=F32)
    diff = idx[:, None] - idx[None, :]
    causal = diff >= 0
    dmat = jnp.where(causal[None], jnp.exp(log_g[:, None, None] * jnp.where(causal, diff, 0.0)[None]), 0.0)
    q_dec = jnp.exp(log_g[:, None] * (idx[None, :] + 1.0))
    k_dec = jnp.exp(log_g[:, None] * (chunk - 1.0 - idx[None, :]))
    c_dec = jnp.exp(log_g * chunk)
    wide = lambda x: jnp.broadcast_to(x[..., None], x.shape + (V7X_LANES,))
    return dmat, wide(q_dec), k_dec, wide(c_dec)


def kernel(x_prompt, x_sample, mem_prompt, state_ret, state_ssm, state_conv, cache_mem_k, cache_mem_v, ffn1_norm,
           ffn1_w1, ffn1_w3, ffn1_w2, mix_norm, w_in, ret_gn_gain, conv_w, conv_b, dt_bias, A_log, D_skip, ssd_norm,
           w_out, x_norm, mem_norm, w_xq, w_xk, w_xv, w_xo, ffn2_norm, ffn2_w1, ffn2_w3, ffn2_w2, final_norm):
    bp, lp, d = x_prompt.shape
    bsz = x_sample.shape[0]
    depth = ffn1_w1.shape[0]
    row = lambda v: v.reshape(1, -1).astype(F32)
    lane_pad = lambda v: jnp.pad(row(v), ((0, 0), (0, DT_PAD - v.shape[-1])))

    cos_p, sin_p = _rope_tables(jnp.arange(lp))
    cos_pt, sin_pt = (a.T for a in _rope_angles(jnp.arange(lp)))
    cos_s, sin_s = _rope_tables(PAST_LEN + jnp.arange(x_sample.shape[1]))
    dmat, q_dec, k_dec, c_dec = _retention_decay_tables(CHUNK)
    gamma1 = _retention_decay_tables(1)[3]

    y_p = x_prompt.reshape(bp * lp, d)
    y_s = x_sample.reshape(bsz, d)
    outs = {k: [] for k in ("ret_p", "ssm_p", "conv_p", "memk", "memv", "ret_s", "ssm_s", "conv_s")}
    for l in range(depth):
        bf = lambda w: w[l].astype(BF16)
        w_in_f = w_in[l]
        w_in_l = jnp.pad(w_in_f, ((0, 0), (0, IN_PROJ_PAD - IN_PROJ_WIDTH))).astype(BF16)
        w_a = jnp.concatenate([w_in_f[:, Q_OFF:K_OFF], w_in_f[:, V_OFF:XBC_OFF]], axis=1).astype(BF16)
        w_kt = w_in_f[:, K_OFF:V_OFF].T.astype(BF16)
        w_xbc = w_in_f[:, XBC_OFF:DT_OFF].astype(BF16)
        w_dt = jnp.pad(jnp.tile(w_in_f[:, DT_OFF:], (1, 2)), ((0, 0), (0, DT_PAD - 2 * SSD_HEADS))).astype(BF16)
        f1 = (row(ffn1_norm[l]), bf(ffn1_w1), bf(ffn1_w3), bf(ffn1_w2))
        f2 = (row(ffn2_norm[l]), bf(ffn2_w1), bf(ffn2_w3), bf(ffn2_w2))
        shared = (row(ret_gn_gain[l]), conv_w[l], row(conv_b[l]))
        ssd_tail = (lane_pad(A_log[l]), row(jnp.repeat(D_skip[l], SSD_HEAD_DIM)), row(ssd_norm[l]))
        w_out_l, w_xq_l, w_xo_l = bf(w_out), bf(w_xq), bf(w_xo)

        mk, mv, mk4, mv4 = _memkv(mem_prompt, row(mem_norm[l]), bf(w_xk), bf(w_xv))

        y_s = _ffn(y_s, *f1, tm=bsz)
        proj_s, = _linear(y_s, [w_in_l], gain=row(mix_norm[l]), tm=bsz)

        mixer_consts = (cos_s, sin_s, gamma1, *shared, lane_pad(dt_bias[l]), *ssd_tail)
        y_p, mix_s, ret_s, ssm_s, conv_s = _ffn(
            y_p, *f1, mixers=(proj_s, state_ret[l], state_ssm[l].reshape(bsz, SSD_WIDTH, SSD_STATE),
                              state_conv[l].reshape(bsz, (CONV_WIDTH - 1) * CONV_CH), mixer_consts), tm=512)
        y_s, = _linear(mix_s.reshape(bsz, d), [w_out_l], res=y_s, tm=bsz)
        q_s, = _linear(y_s, [w_xq_l], gain=row(x_norm[l]), tm=bsz)

        tm_p = 512
        y_p, ret_p, ssm_p, conv_p = _mix_prompt(
            y_p, row(mix_norm[l]), w_a, w_kt, w_xbc, w_dt, cos_p, sin_p, cos_pt, sin_pt, dmat,
            jnp.tile(q_dec, (1, tm_p // CHUNK, 1)), jnp.tile(k_dec, (1, tm_p // CHUNK)), c_dec,
            *shared, lane_pad(jnp.tile(dt_bias[l], 2)), *ssd_tail, w_out_l, batch=bp, seq=lp, tm=tm_p)

        y_p = _xattn_prompt(y_p, row(x_norm[l]), w_xq_l, mk, mv, w_xo_l, batch=bp, seq=lp, tm=512)
        y_p, att_s = _ffn(y_p, *f2, final_gain=row(final_norm) if l == depth - 1 else None,
                          attention=(q_s.reshape(bsz, X_HEADS, X_HEAD_DIM), cache_mem_k[l], cache_mem_v[l]), tm=512)
        y_s, = _linear(att_s.reshape(bsz, d), [w_xo_l], res=y_s, tm=bsz)
        y_s = _ffn(y_s, *f2, final_gain=row(final_norm) if l == depth - 1 else None, tm=bsz)

        outs["ret_p"].append(ret_p)
        outs["ssm_p"].append(ssm_p.reshape(bp, SSD_HEADS, SSD_HEAD_DIM, SSD_STATE))
        outs["conv_p"].append(conv_p)
        outs["memk"].append(mk4)
        outs["memv"].append(mv4)
        outs["ret_s"].append(ret_s)
        outs["ssm_s"].append(ssm_s.reshape(bsz, SSD_HEADS, SSD_HEAD_DIM, SSD_STATE))
        outs["conv_s"].append(conv_s.reshape(bsz, CONV_WIDTH - 1, CONV_CH))

    st = lambda k: jnp.stack(outs[k])
    return (y_p.reshape(bp, lp, d), y_s.reshape(bsz, x_sample.shape[1], d), st("ret_p"), st("ssm_p"), st("conv_p"),
            st("memk"), st("memv"), st("ret_s"), st("ssm_s"), st("conv_s"))
```

```python
import functools

import jax
import jax.numpy as jnp
from jax import lax
from jax.experimental import pallas as pl
from jax.experimental.pallas import tpu as pltpu

F32 = jnp.float32
BF16 = jnp.bfloat16

D_MODEL = 1024
D_FF = 2816
PAST_LEN = 16384
RET_HEADS = 4
RET_DK = 128
RET_DV = 128
RET_WIDTH = RET_HEADS * RET_DV
SSD_HEADS = 8
SSD_HEAD_DIM = 64
SSD_WIDTH = SSD_HEADS * SSD_HEAD_DIM
SSD_GROUPS = 2
SSD_STATE = 128
SSD_PAIRS = SSD_HEADS // 2
CONV_WIDTH = 4
CONV_CH = SSD_WIDTH + 2 * SSD_GROUPS * SSD_STATE
CHUNK = 128
MEM_TOKENS = 256
X_HEADS = 4
X_HEAD_DIM = D_MODEL // X_HEADS
ROPE_BASE = 10000.0
EPS = 1e-6

Q_OFF = 0
K_OFF = Q_OFF + RET_HEADS * RET_DK
V_OFF = K_OFF + RET_HEADS * RET_DK
G_OFF = V_OFF + RET_WIDTH
Z_OFF = G_OFF + RET_WIDTH
XBC_OFF = Z_OFF + SSD_WIDTH
DT_OFF = XBC_OFF + CONV_CH
IN_PROJ_WIDTH = DT_OFF + SSD_HEADS

PQ_OFF = 0
PV_OFF = PQ_OFF + RET_HEADS * RET_DK
PG_OFF = PV_OFF + RET_WIDTH
PZ_OFF = PG_OFF + RET_WIDTH
P_WIDTH = PZ_OFF + SSD_WIDTH

V7X_LANES = 128
V7X_SUBLANES = 8
V7X_VMEM_LIMIT_BYTES = 56 * 1024 * 1024
DT_PAD = V7X_LANES
IN_PROJ_PAD = DT_OFF + DT_PAD
CONV_HIST = V7X_SUBLANES
CONV_SLABS = CONV_CH // V7X_LANES


def _params(sem):
    return pltpu.CompilerParams(dimension_semantics=sem, vmem_limit_bytes=V7X_VMEM_LIMIT_BYTES)


def _resident(shape):
    zeros = (0,) * len(shape)
    return pl.BlockSpec(shape, lambda *_: zeros, pipeline_mode=pl.Buffered(1))


def _rmsnorm(x, gain):
    ms = jnp.mean(x * x, axis=-1, keepdims=True)
    return x * lax.rsqrt(ms + EPS) * gain


def _dot(a, b):
    return jnp.dot(a, b, preferred_element_type=F32)


def _dot_nt(a, b):
    return lax.dot_general(a, b, (((1,), (1,)), ((), ())), preferred_element_type=F32)


def _dot_tn(a, b):
    return lax.dot_general(a, b, (((0,), (0,)), ((), ())), preferred_element_type=F32)


def _softplus(x):
    return jnp.maximum(x, 0.0) + jnp.log1p(jnp.exp(-jnp.abs(x)))


def _rotary(x, cos2, sin2):
    return x * cos2 + pltpu.roll(x, RET_DK // 2, axis=1) * sin2


def _group_norm(y):
    mu = jnp.mean(y, axis=-1, keepdims=True)
    d = y - mu
    var = jnp.mean(d * d, axis=-1, keepdims=True)
    return d * lax.rsqrt(var + EPS)


def _sample_attention(q_ref, k_ref, v_ref, o_ref):
    slabs = MEM_TOKENS * X_HEADS // V7X_SUBLANES
    for b in range(q_ref.shape[0]):
        k3 = k_ref[b].reshape(slabs, V7X_SUBLANES, X_HEAD_DIM)
        v3 = v_ref[b].reshape(slabs, V7X_SUBLANES, X_HEAD_DIM)
        q4 = q_ref[b]
        q8 = jnp.concatenate([q4] * (V7X_SUBLANES // X_HEADS), axis=0)
        s = jnp.sum(k3 * q8[None], axis=-1, keepdims=True) * (X_HEAD_DIM ** -0.5)
        m8 = jnp.max(s, axis=0)
        m4 = jnp.maximum(m8[0:X_HEADS], m8[X_HEADS:])
        p = jnp.exp(s - jnp.concatenate([m4, m4], axis=0)[None])
        acc = jnp.sum(p * v3, axis=0)
        den = jnp.sum(p, axis=0)
        o_ref[b] = (acc[0:X_HEADS] + acc[X_HEADS:]) / (den[0:X_HEADS] + den[X_HEADS:])


def _ffn_kernel(*refs, final_norm, with_attention, with_mixers):
    refs = list(refs)
    x_ref, g_ref, w1_ref, w3_ref, w2_ref = refs[:5]
    del refs[:5]
    fg_ref = refs.pop(0) if final_norm else None
    attn_in = [refs.pop(0) for _ in range(3)] if with_attention else None
    mixer_in = [refs.pop(0) for _ in range(N_MIXER_IN)] if with_mixers else None
    o_ref = refs.pop(0)
    x = x_ref[...]
    xn = _rmsnorm(x, g_ref[...]).astype(BF16)
    a = _dot(xn, w1_ref[...])
    b = _dot(xn, w3_ref[...])
    hidden = (jax.nn.silu(a) * b).astype(BF16)
    out = x + 0.5 * _dot(hidden, w2_ref[...])
    if final_norm:
        out = _rmsnorm(out, fg_ref[...])
    o_ref[...] = out
    if with_attention:
        _sample_attention(*attn_in, refs.pop(0))
    if with_mixers:
        _sample_mixers(*mixer_in, *refs[:N_MIXER_OUT])


def _ffn(x, gain, w1, w3, w2, final_gain=None, attention=None, mixers=None, *, tm):
    t, d = x.shape
    steps = t // tm
    row = pl.BlockSpec((tm, d), lambda i: (i, 0))
    ins = [x, gain, w1, w3, w2]
    specs = [row, _resident(gain.shape), _resident(w1.shape), _resident(w3.shape), _resident(w2.shape)]
    out_shape, out_specs = [jax.ShapeDtypeStruct((t, d), F32)], [row]
    if final_gain is not None:
        ins.append(final_gain)
        specs.append(_resident(final_gain.shape))
    if attention is not None:
        q, cache_k, cache_v = attention
        bs = q.shape[0] // steps
        row_s = pl.BlockSpec((bs, X_HEADS, X_HEAD_DIM), lambda i: (i, 0, 0))
        mem_s = pl.BlockSpec((bs, MEM_TOKENS, X_HEADS, X_HEAD_DIM), lambda i: (i, 0, 0, 0))
        ins += [q, cache_k, cache_v]
        specs += [row_s, mem_s, mem_s]
        out_shape.append(jax.ShapeDtypeStruct(q.shape, F32))
        out_specs.append(row_s)
    if mixers is not None:
        m_ins, m_specs, m_shape, m_out_specs = _sample_mixer_specs(*mixers, steps)
        ins += m_ins
        specs += m_specs
        out_shape += m_shape
        out_specs += m_out_specs
    outs = pl.pallas_call(
        functools.partial(_ffn_kernel, final_norm=final_gain is not None, with_attention=attention is not None,
                          with_mixers=mixers is not None),
        out_shape=out_shape,
        grid=(steps,),
        in_specs=specs,
        out_specs=out_specs,
        compiler_params=_params(("arbitrary",)),
        name="ffn",
    )(*ins)
    return outs if len(outs) > 1 else outs[0]


def _linear_kernel(*refs, has_norm, has_res, transposed):
    refs = list(refs)
    x_ref = refs.pop(0)
    g_ref = refs.pop(0) if has_norm else None
    r_ref = refs.pop(0) if has_res else None
    n_w = len(transposed)
    w_refs, o_refs = refs[:n_w], refs[n_w:]
    x = x_ref[...]
    if has_norm:
        x = _rmsnorm(x, g_ref[...])
    xb = x.astype(BF16)
    for w_ref, o_ref, w_is_t in zip(w_refs, o_refs, transposed, strict=True):
        y = _dot_nt(xb, w_ref[...]) if w_is_t else _dot(xb, w_ref[...])
        if has_res:
            y = r_ref[...] + y
        o_ref[...] = y


def _linear(x, weights, gain=None, res=None, transposed=None, *, tm):
    t, k = x.shape
    transposed = tuple(transposed) if transposed is not None else (False,) * len(weights)
    widths = [w.shape[0] if w_is_t else w.shape[1] for w, w_is_t in zip(weights, transposed)]
    ins, specs = [x], [pl.BlockSpec((tm, k), lambda i: (i, 0))]
    if gain is not None:
        ins.append(gain)
        specs.append(_resident(gain.shape))
    if res is not None:
        ins.append(res)
        specs.append(pl.BlockSpec((tm, res.shape[1]), lambda i: (i, 0)))
    for w in weights:
        ins.append(w)
        specs.append(_resident(w.shape))
    outs = pl.pallas_call(
        functools.partial(_linear_kernel, has_norm=gain is not None, has_res=res is not None, transposed=transposed),
        out_shape=[jax.ShapeDtypeStruct((t, n), F32) for n in widths],
        grid=(t // tm,),
        in_specs=specs,
        out_specs=[pl.BlockSpec((tm, n), lambda i: (i, 0)) for n in widths],
        compiler_params=_params(("arbitrary",)),
        name="linear",
    )(*ins)
    return outs


def _memkv_kernel(x_ref, g_ref, wk_ref, wkt_ref, wv_ref, kt_ref, v_ref, k4_ref, v4_ref):
    xb = _rmsnorm(x_ref[...], g_ref[...]).astype(BF16)
    k4_ref[0] = _dot(xb, wk_ref[...]).reshape(MEM_TOKENS, X_HEADS, X_HEAD_DIM)
    kt_ref[0] = _dot_nt(wkt_ref[...], xb).astype(BF16)
    v = _dot(xb, wv_ref[...])
    v4_ref[0] = v.reshape(MEM_TOKENS, X_HEADS, X_HEAD_DIM)
    v_ref[0] = v.astype(BF16)


def _memkv(mem, gain, w_k, w_kt, w_v):
    batch, m, d = mem.shape
    split = jax.ShapeDtypeStruct((batch, m, X_HEADS, X_HEAD_DIM), F32)
    split_blk = pl.BlockSpec((1, m, X_HEADS, X_HEAD_DIM), lambda b: (b, 0, 0, 0))
    return pl.pallas_call(
        _memkv_kernel,
        out_shape=[jax.ShapeDtypeStruct((batch, d, m), BF16), jax.ShapeDtypeStruct((batch, m, d), BF16), split, split],
        grid=(batch,),
        in_specs=[pl.BlockSpec((m, d), lambda b: (b, 0)), _resident(gain.shape), _resident(w_k.shape),
                  _resident(w_kt.shape), _resident(w_v.shape)],
        out_specs=[pl.BlockSpec((1, d, m), lambda b: (b, 0, 0)), pl.BlockSpec((1, m, d), lambda b: (b, 0, 0)),
                   split_blk, split_blk],
        compiler_params=_params(("arbitrary",)),
        name="memkv",
    )(mem.reshape(batch * m, d), gain, w_k, w_kt, w_v)


def _cumsum_chunks(x):
    pos = lax.broadcasted_iota(jnp.int32, x.shape, 0) % CHUNK
    step = 1
    while step < CHUNK:
        x = x + jnp.where(pos >= step, pltpu.roll(x, step, axis=0), 0.0)
        step *= 2
    return x


def _mixp_project(tile, h_ref, g_ref, wa_ref, wkt_ref, wxbc_ref, wdt_ref, cos_ref, sin_ref, cost_ref, sint_ref,
                  qdec_ref, kdec_ref, cw_ref, cb_ref, dtb_ref, alog_ref, conv_ref, xbc_s, stage, *, tm, nt):
    q_s, qd_s, kr_s, kd_s, v_s, g_s, z_s, xc_s, pk_s = stage
    row_start = (tile % nt) == 0

    u = _rmsnorm(h_ref[...], g_ref[...]).astype(BF16)
    qvgz = _dot(u, wa_ref[...])
    kt = _dot_nt(wkt_ref[...], u)
    xbc = _dot(u, wxbc_ref[...])
    dt_raw = _dot(u, wdt_ref[...])

    for sl in range(CONV_SLABS):
        lanes = slice(sl * V7X_LANES, (sl + 1) * V7X_LANES)
        xbc_s[sl, 0:CONV_HIST, :] = jnp.where(row_start, 0.0, xbc_s[sl, 0:CONV_HIST, :])
        xbc_s[sl, CONV_HIST:CONV_HIST + tm, :] = xbc[:, lanes]
        conv = cb_ref[:, lanes]
        for i in range(CONV_WIDTH):
            off = CONV_HIST - (CONV_WIDTH - 1) + i
            conv = conv + cw_ref[i:i + 1, lanes] * xbc_s[sl, off:off + tm, :]
        xc_s[sl] = jax.nn.silu(conv)
        conv_ref[0, :, lanes] = xbc_s[sl, CONV_HIST + tm - (CONV_WIDTH - 1):CONV_HIST + tm, :]
        xbc_s[sl, 0:CONV_HIST, :] = xbc_s[sl, tm:tm + CONV_HIST, :]

    dt = _softplus(dt_raw + dtb_ref[...])
    a_cum = _cumsum_chunks(dt * (-jnp.exp(alog_ref[...])))
    head_lanes = lax.broadcasted_iota(jnp.int32, (tm, DT_PAD), 1) < SSD_HEADS
    pk_s[...] = jnp.where(head_lanes, a_cum, dt)

    cos2, sin2 = cos_ref[...], sin_ref[...]
    cos_t, sin_t = cost_ref[...], sint_ref[...]
    half = RET_DK // 2
    for hh in range(RET_HEADS):
        lanes = slice(hh * RET_DK, (hh + 1) * RET_DK)
        qr = _rotary(qvgz[:, PQ_OFF + hh * RET_DK:PQ_OFF + (hh + 1) * RET_DK], cos2, sin2)
        q_s[:, lanes] = qr.astype(BF16)
        qd_s[:, lanes] = (qr * qdec_ref[hh]).astype(BF16)
        k1 = kt[hh * RET_DK:hh * RET_DK + half, :]
        k2 = kt[hh * RET_DK + half:(hh + 1) * RET_DK, :]
        kr = jnp.concatenate([k1 * cos_t - k2 * sin_t, k1 * sin_t + k2 * cos_t], axis=0) * (RET_DK ** -0.5)
        kr_s[lanes, :] = kr.astype(BF16)
        kd_s[lanes, :] = (kr * kdec_ref[hh:hh + 1, :]).astype(BF16)
    v_s[...] = qvgz[:, PV_OFF:PV_OFF + RET_WIDTH].astype(BF16)
    g_s[...] = jax.nn.silu(qvgz[:, PG_OFF:PG_OFF + RET_WIDTH])
    z_s[...] = jax.nn.silu(qvgz[:, PZ_OFF:PZ_OFF + SSD_WIDTH])


def _mixp_heads(tile, h_ref, dmat_ref, cdec_ref, gn_ref, dsk_ref, sn_ref, wout_ref, o_ref, ret_ref, ssm_ref,
                mix_s, sret_s, sssm_s, stage, *, tm, nt):
    q_s, qd_s, kr_s, kd_s, v_s, g_s, z_s, xc_s, pk_s = stage
    nc = tm // CHUNK
    row_start = (tile % nt) == 0

    row_i = lax.broadcasted_iota(jnp.int32, (CHUNK, CHUNK), 0)
    col_i = lax.broadcasted_iota(jnp.int32, (CHUNK, CHUNK), 1)
    causal = row_i >= col_i
    low_lanes = col_i < SSD_HEAD_DIM
    bd_rows = lax.broadcasted_iota(jnp.int32, (2 * SSD_STATE, 2 * SSD_HEAD_DIM), 0)
    bd_cols = lax.broadcasted_iota(jnp.int32, (2 * SSD_STATE, 2 * SSD_HEAD_DIM), 1)
    bd_mask = (bd_rows < SSD_STATE) == (bd_cols < SSD_HEAD_DIM)

    inner, kv = {}, {}
    scores, c_exp, st_inc, st_dec = {}, {}, {}, {}
    for c in range(nc):
        rows = slice(c * CHUNK, (c + 1) * CHUNK)
        for hh in range(RET_HEADS):
            lanes = slice(hh * RET_DK, (hh + 1) * RET_DK)
            inner[c, hh] = (_dot(q_s[rows, lanes], kr_s[lanes, rows]) * dmat_ref[hh]).astype(BF16)
            kv[c, hh] = _dot(kd_s[lanes, rows], v_s[rows, lanes])

        pk = pk_s[rows, :]
        pk_t = pk.T
        a_t = pk_t[0:SSD_HEADS, :]
        dt_t = pk_t[SSD_HEADS:2 * SSD_HEADS, :]
        a_last = a_t[:, CHUNK - 1:CHUNK]
        w_t = jnp.exp(a_last - a_t) * dt_t
        chunk_dec = jnp.exp(a_last)
        for grp in range(SSD_GROUPS):
            cg = xc_s[SSD_PAIRS + SSD_GROUPS + grp, rows, :]
            b_t = xc_s[SSD_PAIRS + grp, rows, :].T
            gmat = _dot(cg.astype(BF16), b_t.astype(BF16))
            for pj in range(SSD_PAIRS // SSD_GROUPS):
                j = grp * (SSD_PAIRS // SSD_GROUPS) + pj
                heads = (2 * j, 2 * j + 1)
                x_pair = xc_s[j, rows, :].astype(BF16)
                b_w = jnp.concatenate([b_t * w_t[hd:hd + 1, :] for hd in heads], axis=0).astype(BF16)
                st_inc[c, j] = jnp.where(bd_mask, _dot(b_w, x_pair), 0.0)
                st_dec[c, j] = jnp.concatenate(
                    [jnp.broadcast_to(chunk_dec[hd:hd + 1, :], (SSD_STATE, 2 * SSD_HEAD_DIM)) for hd in heads], axis=0)
                for hd in heads:
                    a_col = jnp.broadcast_to(pk[:, hd:hd + 1], (CHUNK, CHUNK))
                    seg = a_col - a_t[hd:hd + 1, :]
                    lmat = jnp.where(causal, jnp.exp(jnp.where(causal, seg, 0.0)), 0.0)
                    scores[c, hd] = (gmat * lmat * dt_t[hd:hd + 1, :]).astype(BF16)
                    c_exp[c, hd] = (cg * jnp.exp(a_col)).astype(BF16)

    s_in, st_in = {}, {}
    for hh in range(RET_HEADS):
        s = jnp.where(row_start, 0.0, sret_s[hh])
        for c in range(nc):
            s_in[c, hh] = s.astype(BF16)
            s = cdec_ref[hh:hh + 1, :] * s + kv[c, hh]
        sret_s[hh] = s
        ret_ref[0, hh] = s
    for j in range(SSD_PAIRS):
        s = jnp.where(row_start, 0.0, sssm_s[j])
        for c in range(nc):
            st_in[c, j] = s.astype(BF16)
            s = st_dec[c, j] * s + st_inc[c, j]
        sssm_s[j] = s
        ssm_ref[0, j * 2 * SSD_HEAD_DIM:(j + 1) * 2 * SSD_HEAD_DIM, :] = (s[0:SSD_STATE] + s[SSD_STATE:]).T

    for c in range(nc):
        rows = slice(c * CHUNK, (c + 1) * CHUNK)
        for hh in range(RET_HEADS):
            lanes = slice(hh * RET_DV, (hh + 1) * RET_DV)
            lhs = jnp.concatenate([inner[c, hh], qd_s[rows, lanes]], axis=1)
            rhs = jnp.concatenate([v_s[rows, lanes], s_in[c, hh]], axis=0)
            r = _group_norm(_dot(lhs, rhs)) * gn_ref[:, lanes]
            mix_s[rows, lanes] = (g_s[rows, lanes] * r).astype(BF16)
        ys = []
        for j in range(SSD_PAIRS):
            heads = (2 * j, 2 * j + 1)
            x_pair = xc_s[j, rows, :]
            x_bd = jnp.concatenate([jnp.where(low_lanes, x_pair, 0.0), jnp.where(low_lanes, 0.0, x_pair)], axis=0)
            lhs = jnp.concatenate([scores[c, heads[0]], scores[c, heads[1]], c_exp[c, heads[0]], c_exp[c, heads[1]]],
                                  axis=1)
            rhs = jnp.concatenate([x_bd.astype(BF16), st_in[c, j]], axis=0)
            ys.append(_dot(lhs, rhs) + dsk_ref[:, j * 2 * SSD_HEAD_DIM:(j + 1) * 2 * SSD_HEAD_DIM] * x_pair)
        y_all = jnp.concatenate(ys, axis=1) * z_s[rows, :]
        mix_s[rows, RET_WIDTH:RET_WIDTH + SSD_WIDTH] = _rmsnorm(y_all, sn_ref[...]).astype(BF16)

    o_ref[...] = h_ref[...] + _dot(mix_s[...], wout_ref[...])


def _mix_prompt_kernel(h_ref, hp_ref, g_ref, wa_ref, wkt_ref, wxbc_ref, wdt_ref, cos_ref, sin_ref, cost_ref, sint_ref,
                       dmat_ref, qdec_ref, kdec_ref, cdec_ref, gn_ref, cw_ref, cb_ref, dtb_ref, alog_ref, dsk_ref,
                       sn_ref, wout_ref,
                       o_ref, ret_ref, ssm_ref, conv_ref,
                       xbc_s, mix_s, sret_s, sssm_s, *stage_s, tm, nt, n_tiles):
    s = pl.program_id(0)
    sets = (tuple(r.at[0] for r in stage_s), tuple(r.at[1] for r in stage_s))

    @pl.when(s == 0)
    def _():
        for r in stage_s:
            r[...] = jnp.zeros_like(r)
        sret_s[...] = jnp.zeros_like(sret_s)
        sssm_s[...] = jnp.zeros_like(sssm_s)
        xbc_s[...] = jnp.zeros_like(xbc_s)

    def step(write_set, read_set):
        _mixp_project(jnp.minimum(s, n_tiles - 1), h_ref, g_ref, wa_ref, wkt_ref, wxbc_ref, wdt_ref, cos_ref, sin_ref,
                      cost_ref, sint_ref, qdec_ref, kdec_ref, cw_ref, cb_ref, dtb_ref, alog_ref, conv_ref, xbc_s,
                      write_set, tm=tm, nt=nt)
        _mixp_heads(jnp.maximum(s - 1, 0), hp_ref, dmat_ref, cdec_ref, gn_ref, dsk_ref, sn_ref, wout_ref, o_ref, ret_ref,
                    ssm_ref, mix_s, sret_s, sssm_s, read_set, tm=tm, nt=nt)

    @pl.when(s % 2 == 0)
    def _():
        step(sets[0], sets[1])

    @pl.when(s % 2 == 1)
    def _():
        step(sets[1], sets[0])


def _mix_prompt(h, gain, w_a, w_kt, w_xbc, w_dt, cos2, sin2, cos_t, sin_t, dmat, qdec, kdec, cdec, gn, conv_w, conv_b,
                dt_bias, a_log, dskip, ssd_gain, w_out, *, batch, seq, tm):
    nt = seq // tm
    n_tiles = batch * nt
    d = h.shape[1]
    cur = lambda s: jnp.minimum(s, n_tiles - 1)
    prev = lambda s: jnp.maximum(s - 1, 0)
    row_cur = pl.BlockSpec((tm, d), lambda s: (cur(s), 0))
    row_prev = pl.BlockSpec((tm, d), lambda s: (prev(s), 0))
    pos = pl.BlockSpec((tm, RET_DK), lambda s: (cur(s) % nt, 0))
    pos_t = pl.BlockSpec((RET_DK // 2, tm), lambda s: (0, cur(s) % nt))
    consts = [gain, w_a, w_kt, w_xbc, w_dt]
    tail = [dmat, qdec, kdec, cdec, gn, conv_w, conv_b, dt_bias, a_log, dskip, ssd_gain, w_out]
    stage = [
        pltpu.VMEM((2, tm, RET_HEADS * RET_DK), BF16),
        pltpu.VMEM((2, tm, RET_HEADS * RET_DK), BF16),
        pltpu.VMEM((2, RET_HEADS * RET_DK, tm), BF16),
        pltpu.VMEM((2, RET_HEADS * RET_DK, tm), BF16),
        pltpu.VMEM((2, tm, RET_WIDTH), BF16),
        pltpu.VMEM((2, tm, RET_WIDTH), F32),
        pltpu.VMEM((2, tm, SSD_WIDTH), F32),
        pltpu.VMEM((2, CONV_SLABS, tm, V7X_LANES), F32),
        pltpu.VMEM((2, tm, DT_PAD), F32),
    ]
    return pl.pallas_call(
        functools.partial(_mix_prompt_kernel, tm=tm, nt=nt, n_tiles=n_tiles),
        out_shape=[
            jax.ShapeDtypeStruct(h.shape, F32),
            jax.ShapeDtypeStruct((batch, RET_HEADS, RET_DK, RET_DV), F32),
            jax.ShapeDtypeStruct((batch, SSD_WIDTH, SSD_STATE), F32),
            jax.ShapeDtypeStruct((batch, CONV_WIDTH - 1, CONV_CH), F32),
        ],
        grid=(n_tiles + 1,),
        in_specs=([row_cur, row_prev] + [_resident(a.shape) for a in consts] + [pos, pos, pos_t, pos_t]
                  + [_resident(a.shape) for a in tail]),
        out_specs=[
            row_prev,
            pl.BlockSpec((1, RET_HEADS, RET_DK, RET_DV), lambda s: (prev(s) // nt, 0, 0, 0)),
            pl.BlockSpec((1, SSD_WIDTH, SSD_STATE), lambda s: (prev(s) // nt, 0, 0)),
            pl.BlockSpec((1, CONV_WIDTH - 1, CONV_CH), lambda s: (cur(s) // nt, 0, 0)),
        ],
        scratch_shapes=[
            pltpu.VMEM((CONV_SLABS, CONV_HIST + tm, V7X_LANES), F32),
            pltpu.VMEM((tm, RET_WIDTH + SSD_WIDTH), BF16),
            pltpu.VMEM((RET_HEADS, RET_DK, RET_DV), F32),
            pltpu.VMEM((SSD_PAIRS, 2 * SSD_STATE, 2 * SSD_HEAD_DIM), F32),
        ] + stage,
        compiler_params=_params(("arbitrary",)),
        name="mix_prompt",
    )(h, h, *consts, cos2, sin2, cos_t, sin_t, *tail)


def _split_hi_lo(x):
    hi = x.astype(BF16).astype(F32)
    return hi, x - hi


def _sample_mixers(proj_ref, sret_ref, sssm_ref, sconv_ref, cos_ref, sin_ref, gam_ref,
                   gn_ref, cw_ref, cb_ref, dtb_ref, alog_ref, dsk_ref, sn_ref,
                   mix_ref, oret_ref, ossm_ref, oconv_ref):
    bs = proj_ref.shape[1]
    proj = proj_ref[0]
    xbc = proj[:, XBC_OFF:DT_OFF]
    hist = sconv_ref[0]
    taps = [hist[:, i * CONV_CH:(i + 1) * CONV_CH] for i in range(CONV_WIDTH - 1)] + [xbc]
    conv = cb_ref[...]
    for i in range(CONV_WIDTH):
        conv = conv + cw_ref[i:i + 1, :] * taps[i]
    oconv_ref[0] = jnp.concatenate(taps[1:], axis=1)
    xc = jax.nn.silu(conv)

    dt = _softplus(proj[:, DT_OFF:IN_PROJ_PAD] + dtb_ref[...])

    sub_i = lax.broadcasted_iota(jnp.int32, (bs, V7X_LANES), 0)
    row_of = lax.broadcasted_iota(jnp.int32, (4 * bs, V7X_LANES), 0) % bs
    cos2 = cos_ref[...]
    sin2 = sin_ref[...]

    def outer_lhs(x):
        hi, lo = _split_hi_lo(x)
        return jnp.concatenate([hi, hi, lo, lo], axis=0)

    def outer_rhs(x):
        hi, lo = _split_hi_lo(x)
        return jnp.concatenate([hi, lo, hi, lo], axis=0).astype(BF16)

    def only_sample(x4, b):
        return jnp.where(row_of == b, x4, 0.0).astype(BF16)

    for hh in range(RET_HEADS):
        lanes = slice(hh * RET_DK, (hh + 1) * RET_DK)
        qr = _rotary(proj[:, Q_OFF + hh * RET_DK:Q_OFF + (hh + 1) * RET_DK], cos2, sin2)
        kr = _rotary(proj[:, K_OFF + hh * RET_DK:K_OFF + (hh + 1) * RET_DK], cos2, sin2) * (RET_DK ** -0.5)
        vh = proj[:, V_OFF + hh * RET_DV:V_OFF + (hh + 1) * RET_DV]
        k4, v4, q_bf = outer_lhs(kr), outer_rhs(vh), qr.astype(BF16)
        gamma = gam_ref[hh:hh + 1, :]
        y = jnp.zeros((bs, RET_DV), F32)
        for b in range(bs):
            s_old = sret_ref[b, hh]
            oret_ref[b, hh] = gamma * s_old + _dot_tn(only_sample(k4, b), v4)
            y = jnp.where(sub_i == b, _dot(q_bf, s_old.astype(BF16)), y)
        y = gamma * y + jnp.sum(qr * kr, axis=-1, keepdims=True) * vh
        r = _group_norm(y) * gn_ref[:, lanes]
        mix_ref[0, :, lanes] = jax.nn.silu(proj[:, G_OFF + hh * RET_DV:G_OFF + (hh + 1) * RET_DV]) * r

    xs = xc[:, 0:SSD_WIDTH]
    head_of_lane = lax.broadcasted_iota(jnp.int32, (bs, SSD_WIDTH), 1) // SSD_HEAD_DIM
    dec = jnp.exp(dt * (-jnp.exp(alog_ref[...])))
    dt_wide = jnp.zeros((bs, SSD_WIDTH), F32)
    dec_wide = jnp.zeros((bs, SSD_WIDTH), F32)
    for hd in range(SSD_HEADS):
        dt_wide = jnp.where(head_of_lane == hd, dt[:, hd:hd + 1], dt_wide)
        dec_wide = jnp.where(head_of_lane == hd, dec[:, hd:hd + 1], dec_wide)
    xdt = xs * dt_wide
    x4 = outer_lhs(xdt)
    ys = []
    for j in range(SSD_PAIRS):
        grp = j // (SSD_PAIRS // SSD_GROUPS)
        lanes = slice(j * 2 * SSD_HEAD_DIM, (j + 1) * 2 * SSD_HEAD_DIM)
        bg = xc[:, SSD_WIDTH + grp * SSD_STATE:SSD_WIDTH + (grp + 1) * SSD_STATE]
        c_off = SSD_WIDTH + SSD_GROUPS * SSD_STATE + grp * SSD_STATE
        cg = xc[:, c_off:c_off + SSD_STATE]
        b4, c_bf = outer_rhs(bg), cg.astype(BF16)
        y = jnp.zeros((bs, 2 * SSD_HEAD_DIM), F32)
        for b in range(bs):
            s_old = sssm_ref[b, lanes, :]
            dec_rows = jnp.concatenate(
                [jnp.broadcast_to(dec[b:b + 1, hd:hd + 1], (SSD_HEAD_DIM, SSD_STATE)) for hd in (2 * j, 2 * j + 1)],
                axis=0)
            ossm_ref[b, lanes, :] = dec_rows * s_old + _dot_tn(only_sample(x4[:, lanes], b), b4)
            y = jnp.where(sub_i == b, _dot_nt(c_bf, s_old.astype(BF16)), y)
        ys.append(dec_wide[:, lanes] * y + jnp.sum(cg * bg, axis=-1, keepdims=True) * xdt[:, lanes])
    ys = (jnp.concatenate(ys, axis=1) + dsk_ref[...] * xs) * jax.nn.silu(proj[:, Z_OFF:Z_OFF + SSD_WIDTH])
    mix_ref[0, :, RET_WIDTH:RET_WIDTH + SSD_WIDTH] = _rmsnorm(ys, sn_ref[...])


N_MIXER_IN = 14
N_MIXER_OUT = 4


def _sample_mixer_specs(proj, s_ret, s_ssm, s_conv, consts, steps):
    n = proj.shape[0]
    bs = n // steps
    grouped = lambda a: a.reshape(steps, bs, a.shape[1])
    blk3 = lambda w: pl.BlockSpec((1, bs, w), lambda i: (i, 0, 0))
    ret_blk = pl.BlockSpec((bs, RET_HEADS, RET_DK, RET_DV), lambda i: (i, 0, 0, 0))
    ssm_blk = pl.BlockSpec((bs, SSD_WIDTH, SSD_STATE), lambda i: (i, 0, 0))
    ins = [grouped(proj), s_ret, s_ssm, grouped(s_conv), *consts]
    in_specs = [blk3(proj.shape[1]), ret_blk, ssm_blk, blk3(s_conv.shape[1])] + [_resident(a.shape) for a in consts]
    out_shape = [jax.ShapeDtypeStruct((steps, bs, RET_WIDTH + SSD_WIDTH), F32), jax.ShapeDtypeStruct(s_ret.shape, F32),
                 jax.ShapeDtypeStruct(s_ssm.shape, F32), jax.ShapeDtypeStruct((steps, bs, s_conv.shape[1]), F32)]
    out_specs = [blk3(RET_WIDTH + SSD_WIDTH), ret_blk, ssm_blk, blk3(s_conv.shape[1])]
    return ins, in_specs, out_shape, out_specs


def _softmax_rows(s):
    m = jnp.max(s, axis=-1, keepdims=True)
    p = jnp.exp(s - m)
    return p / jnp.sum(p, axis=-1, keepdims=True)


def _xattn_prompt_kernel(h_ref, g_ref, wq_ref, kt_ref, v_ref, wo_ref, o_ref, att_s):
    h = h_ref[...]
    c = _rmsnorm(h, g_ref[...]).astype(BF16)
    qx = _dot(c, wq_ref[...]).astype(BF16)
    for hh in range(X_HEADS):
        lanes = slice(hh * X_HEAD_DIM, (hh + 1) * X_HEAD_DIM)
        s = _dot(qx[:, lanes], kt_ref[0, lanes, :]) * (X_HEAD_DIM ** -0.5)
        att = _softmax_rows(s).astype(BF16)
        att_s[:, lanes] = _dot(att, v_ref[0, :, lanes]).astype(BF16)
    o_ref[...] = h + _dot(att_s[...], wo_ref[...])


def _xattn_prompt(h, gain, w_q, mem_kt, mem_v, w_o, *, batch, seq, tm):
    nt = seq // tm
    d = h.shape[1]
    row = pl.BlockSpec((tm, d), lambda b, t: (b * nt + t, 0))
    return pl.pallas_call(
        _xattn_prompt_kernel,
        out_shape=jax.ShapeDtypeStruct(h.shape, F32),
        grid=(batch, nt),
        in_specs=[row, _resident(gain.shape), _resident(w_q.shape),
                  pl.BlockSpec((1, d, MEM_TOKENS), lambda b, t: (b, 0, 0)),
                  pl.BlockSpec((1, MEM_TOKENS, d), lambda b, t: (b, 0, 0)), _resident(w_o.shape)],
        out_specs=row,
        scratch_shapes=[pltpu.VMEM((tm, d), BF16)],
        compiler_params=_params(("arbitrary", "arbitrary")),
        name="xattn_prompt",
    )(h, gain, w_q, mem_kt, mem_v, w_o)


def _rope_angles(pos):
    half = RET_DK // 2
    inv_freq = ROPE_BASE ** (-jnp.arange(half, dtype=F32) / half)
    ang = pos.astype(F32)[:, None] * inv_freq[None, :]
    return jnp.cos(ang), jnp.sin(ang)


def _rope_tables(pos):
    cos, sin = _rope_angles(pos)
    return jnp.concatenate([cos, cos], axis=-1), jnp.concatenate([-sin, sin], axis=-1)


def _retention_decay_tables(chunk):
    log_g = jnp.log1p(-jnp.exp2(-5.0 - jnp.arange(RET_HEADS, dtype=F32)))
    idx = jnp.arange(chunk, dtype=F32)
    diff = idx[:, None] - idx[None, :]
    causal = diff >= 0
    dmat = jnp.where(causal[None], jnp.exp(log_g[:, None, None] * jnp.where(causal, diff, 0.0)[None]), 0.0)
    q_dec = jnp.exp(log_g[:, None] * (idx[None, :] + 1.0))
    k_dec = jnp.exp(log_g[:, None] * (chunk - 1.0 - idx[None, :]))
    c_dec = jnp.exp(log_g * chunk)
    wide = lambda x: jnp.broadcast_to(x[..., None], x.shape + (V7X_LANES,))
    return dmat, wide(q_dec), k_dec, wide(c_dec)


def kernel(x_prompt, x_sample, mem_prompt, state_ret, state_ssm, state_conv, cache_mem_k, cache_mem_v, ffn1_norm,
           ffn1_w1, ffn1_w3, ffn1_w2, mix_norm, w_in, ret_gn_gain, conv_w, conv_b, dt_bias, A_log, D_skip, ssd_norm,
           w_out, x_norm, mem_norm, w_xq, w_xk, w_xv, w_xo, ffn2_norm, ffn2_w1, ffn2_w3, ffn2_w2, final_norm):
    bp, lp, d = x_prompt.shape
    bsz = x_sample.shape[0]
    depth = ffn1_w1.shape[0]
    row = lambda v: v.reshape(1, -1).astype(F32)
    lane_pad = lambda v: jnp.pad(row(v), ((0, 0), (0, DT_PAD - v.shape[-1])))

    cos_p, sin_p = _rope_tables(jnp.arange(lp))
    cos_pt, sin_pt = (a.T for a in _rope_angles(jnp.arange(lp)))
    cos_s, sin_s = _rope_tables(PAST_LEN + jnp.arange(x_sample.shape[1]))
    dmat, q_dec, k_dec, c_dec = _retention_decay_tables(CHUNK)
    gamma1 = _retention_decay_tables(1)[3]

    y_p = x_prompt.reshape(bp * lp, d)
    y_s = x_sample.reshape(bsz, d)
    outs = {k: [] for k in ("ret_p", "ssm_p", "conv_p", "memk", "memv", "ret_s", "ssm_s", "conv_s")}
    for l in range(depth):
        bf = lambda w: w[l].astype(BF16)
        w_in_f = w_in[l]
        w_a = jnp.concatenate([w_in_f[:, Q_OFF:K_OFF], w_in_f[:, V_OFF:XBC_OFF]], axis=1).astype(BF16)
        w_kt = w_in_f[:, K_OFF:V_OFF].T.astype(BF16)
        w_xbc = w_in_f[:, XBC_OFF:DT_OFF].astype(BF16)
        w_dt = jnp.pad(jnp.tile(w_in_f[:, DT_OFF:], (1, 2)), ((0, 0), (0, DT_PAD - 2 * SSD_HEADS))).astype(BF16)
        f1 = (row(ffn1_norm[l]), bf(ffn1_w1), bf(ffn1_w3), bf(ffn1_w2))
        f2 = (row(ffn2_norm[l]), bf(ffn2_w1), bf(ffn2_w3), bf(ffn2_w2))
        shared = (row(ret_gn_gain[l]), conv_w[l], row(conv_b[l]))
        ssd_tail = (lane_pad(A_log[l]), row(jnp.repeat(D_skip[l], SSD_HEAD_DIM)), row(ssd_norm[l]))
        w_out_l, w_xq_l, w_xo_l = bf(w_out), bf(w_xq), bf(w_xo)

        mkt, mv, mk4, mv4 = _memkv(mem_prompt, row(mem_norm[l]), bf(w_xk), w_xk[l].T.astype(BF16), bf(w_xv))

        y_s = _ffn(y_s, *f1, tm=bsz)
        qvgz_s, k_s, xbc_s, dt_s = _linear(y_s, [w_a, w_kt, w_xbc, w_dt], gain=row(mix_norm[l]),
                                           transposed=(False, True, False, False), tm=bsz)
        proj_s = jnp.concatenate([qvgz_s[:, :PV_OFF], k_s, qvgz_s[:, PV_OFF:], xbc_s, dt_s], axis=1)

        mixer_consts = (cos_s, sin_s, gamma1, *shared, lane_pad(dt_bias[l]), *ssd_tail)
        y_p, mix_s, ret_s, ssm_s, conv_s = _ffn(
            y_p, *f1, mixers=(proj_s, state_ret[l], state_ssm[l].reshape(bsz, SSD_WIDTH, SSD_STATE),
                              state_conv[l].reshape(bsz, (CONV_WIDTH - 1) * CONV_CH), mixer_consts), tm=512)
        y_s, = _linear(mix_s.reshape(bsz, d), [w_out_l], res=y_s, tm=bsz)
        q_s, = _linear(y_s, [w_xq_l], gain=row(x_norm[l]), tm=bsz)

        tm_p = 512
        y_p, ret_p, ssm_p, conv_p = _mix_prompt(
            y_p, row(mix_norm[l]), w_a, w_kt, w_xbc, w_dt, cos_p, sin_p, cos_pt, sin_pt, dmat,
            jnp.tile(q_dec, (1, tm_p // CHUNK, 1)), jnp.tile(k_dec, (1, tm_p // CHUNK)), c_dec,
            *shared, lane_pad(jnp.tile(dt_bias[l], 2)), *ssd_tail, w_out_l, batch=bp, seq=lp, tm=tm_p)

        y_p = _xattn_prompt(y_p, row(x_norm[l]), w_xq_l, mkt, mv, w_xo_l, batch=bp, seq=lp, tm=512)
        y_p, att_s = _ffn(y_p, *f2, final_gain=row(final_norm) if l == depth - 1 else None,
                          attention=(q_s.reshape(bsz, X_HEADS, X_HEAD_DIM), cache_mem_k[l], cache_mem_v[l]), tm=512)
        y_s, = _linear(att_s.reshape(bsz, d), [w_xo_l], res=y_s, tm=bsz)
        y_s = _ffn(y_s, *f2, final_gain=row(final_norm) if l == depth - 1 else None, tm=bsz)

        outs["ret_p"].append(ret_p)
        outs["ssm_p"].append(ssm_p.reshape(bp, SSD_HEADS, SSD_HEAD_DIM, SSD_STATE))
        outs["conv_p"].append(conv_p)
        outs["memk"].append(mk4)
        outs["memv"].append(mv4)
        outs["ret_s"].append(ret_s)
        outs["ssm_s"].append(ssm_s.reshape(bsz, SSD_HEADS, SSD_HEAD_DIM, SSD_STATE))
        outs["conv_s"].append(conv_s.reshape(bsz, CONV_WIDTH - 1, CONV_CH))

    st = lambda k: jnp.stack(outs[k])
    return (y_p.reshape(bp, lp, d), y_s.reshape(bsz, x_sample.shape[1], d), st("ret_p"), st("ssm_p"), st("conv_p"),
            st("memk"), st("memv"), st("ret_s"), st("ssm_s"), st("conv_s"))
```

```python
import functools

import jax
import jax.numpy as jnp
from jax import lax
from jax.experimental import pallas as pl
from jax.experimental.pallas import tpu as pltpu

F32 = jnp.float32
BF16 = jnp.bfloat16

D_MODEL = 1024
D_FF = 2816
PAST_LEN = 16384
RET_HEADS = 4
RET_DK = 128
RET_DV = 128
RET_WIDTH = RET_HEADS * RET_DV
SSD_HEADS = 8
SSD_HEAD_DIM = 64
SSD_WIDTH = SSD_HEADS * SSD_HEAD_DIM
SSD_GROUPS = 2
SSD_STATE = 128
SSD_PAIRS = SSD_HEADS // 2
CONV_WIDTH = 4
CONV_CH = SSD_WIDTH + 2 * SSD_GROUPS * SSD_STATE
CHUNK = 128
MEM_TOKENS = 256
X_HEADS = 4
X_HEAD_DIM = D_MODEL // X_HEADS
ROPE_BASE = 10000.0
EPS = 1e-6

Q_OFF = 0
K_OFF = Q_OFF + RET_HEADS * RET_DK
V_OFF = K_OFF + RET_HEADS * RET_DK
G_OFF = V_OFF + RET_WIDTH
Z_OFF = G_OFF + RET_WIDTH
XBC_OFF = Z_OFF + SSD_WIDTH
DT_OFF = XBC_OFF + CONV_CH
IN_PROJ_WIDTH = DT_OFF + SSD_HEADS

PQ_OFF = 0
PV_OFF = PQ_OFF + RET_HEADS * RET_DK
PG_OFF = PV_OFF + RET_WIDTH
PZ_OFF = PG_OFF + RET_WIDTH
P_WIDTH = PZ_OFF + SSD_WIDTH

V7X_LANES = 128
V7X_SUBLANES = 8
V7X_BF16_SUBLANES = 16
V7X_VMEM_LIMIT_BYTES = 56 * 1024 * 1024
DT_PAD = V7X_LANES
IN_PROJ_PAD = DT_OFF + DT_PAD
CONV_HIST = V7X_SUBLANES
CONV_SLABS = CONV_CH // V7X_LANES


def _params(sem):
    return pltpu.CompilerParams(dimension_semantics=sem, vmem_limit_bytes=V7X_VMEM_LIMIT_BYTES)


def _resident(shape):
    zeros = (0,) * len(shape)
    return pl.BlockSpec(shape, lambda *_: zeros, pipeline_mode=pl.Buffered(1))


def _rmsnorm(x, gain):
    ms = jnp.mean(x * x, axis=-1, keepdims=True)
    return x * lax.rsqrt(ms + EPS) * gain


def _dot(a, b):
    return jnp.dot(a, b, preferred_element_type=F32)


def _dot_nt(a, b):
    return lax.dot_general(a, b, (((1,), (1,)), ((), ())), preferred_element_type=F32)


def _dot_tn(a, b):
    return lax.dot_general(a, b, (((0,), (0,)), ((), ())), preferred_element_type=F32)


def _softplus(x):
    return jnp.maximum(x, 0.0) + jnp.log1p(jnp.exp(-jnp.abs(x)))


def _rotary(x, cos2, sin2):
    return x * cos2 + pltpu.roll(x, RET_DK // 2, axis=1) * sin2


def _group_norm(y):
    mu = jnp.mean(y, axis=-1, keepdims=True)
    d = y - mu
    var = jnp.mean(d * d, axis=-1, keepdims=True)
    return d * lax.rsqrt(var + EPS)


def _sample_attention(q_ref, k_ref, v_ref, o_ref):
    slabs = MEM_TOKENS * X_HEADS // V7X_SUBLANES
    for b in range(q_ref.shape[0]):
        k3 = k_ref[b].reshape(slabs, V7X_SUBLANES, X_HEAD_DIM)
        v3 = v_ref[b].reshape(slabs, V7X_SUBLANES, X_HEAD_DIM)
        q4 = q_ref[b]
        q8 = jnp.concatenate([q4] * (V7X_SUBLANES // X_HEADS), axis=0)
        s = jnp.sum(k3 * q8[None], axis=-1, keepdims=True) * (X_HEAD_DIM ** -0.5)
        m8 = jnp.max(s, axis=0)
        m4 = jnp.maximum(m8[0:X_HEADS], m8[X_HEADS:])
        p = jnp.exp(s - jnp.concatenate([m4, m4], axis=0)[None])
        acc = jnp.sum(p * v3, axis=0)
        den = jnp.sum(p, axis=0)
        o_ref[b] = (acc[0:X_HEADS] + acc[X_HEADS:]) / (den[0:X_HEADS] + den[X_HEADS:])


def _ffn_kernel(*refs, final_norm, with_attention, with_mixers, n_cast):
    refs = list(refs)
    x_ref, g_ref, w1_ref, w3_ref, w2_ref = refs[:5]
    del refs[:5]
    fg_ref = refs.pop(0) if final_norm else None
    attn_in = [refs.pop(0) for _ in range(3)] if with_attention else None
    mixer_in = [refs.pop(0) for _ in range(N_MIXER_IN)] if with_mixers else None
    cast_in = [refs.pop(0) for _ in range(n_cast)]
    o_ref = refs.pop(0)
    x = x_ref[...]
    xn = _rmsnorm(x, g_ref[...]).astype(BF16)
    a = _dot(xn, w1_ref[...])
    b = _dot(xn, w3_ref[...])
    hidden = (jax.nn.silu(a) * b).astype(BF16)
    out = x + 0.5 * _dot(hidden, w2_ref[...])
    if final_norm:
        out = _rmsnorm(out, fg_ref[...])
    o_ref[...] = out
    if with_attention:
        _sample_attention(*attn_in, refs.pop(0))
    if with_mixers:
        _sample_mixers(*mixer_in, *refs[:N_MIXER_OUT])
        del refs[:N_MIXER_OUT]
    _cast_blocks(cast_in, refs)


def _ffn(x, gain, w1, w3, w2, final_gain=None, attention=None, mixers=None, casts=(), *, tm):
    t, d = x.shape
    steps = t // tm
    row = pl.BlockSpec((tm, d), lambda i: (i, 0))
    ins = [x, gain, w1, w3, w2]
    specs = [row, _resident(gain.shape), _resident(w1.shape), _resident(w3.shape), _resident(w2.shape)]
    out_shape, out_specs = [jax.ShapeDtypeStruct((t, d), F32)], [row]
    if final_gain is not None:
        ins.append(final_gain)
        specs.append(_resident(final_gain.shape))
    if attention is not None:
        q, cache_k, cache_v = attention
        bs = q.shape[0] // steps
        row_s = pl.BlockSpec((bs, X_HEADS, X_HEAD_DIM), lambda i: (i, 0, 0))
        mem_s = pl.BlockSpec((bs, MEM_TOKENS, X_HEADS, X_HEAD_DIM), lambda i: (i, 0, 0, 0))
        ins += [q, cache_k, cache_v]
        specs += [row_s, mem_s, mem_s]
        out_shape.append(jax.ShapeDtypeStruct(q.shape, F32))
        out_specs.append(row_s)
    if mixers is not None:
        m_ins, m_specs, m_shape, m_out_specs = _sample_mixer_specs(*mixers, steps)
        ins += m_ins
        specs += m_specs
        out_shape += m_shape
        out_specs += m_out_specs
    c_in, c_shape, c_out = _cast_specs(casts, steps)
    ins += list(casts)
    specs += c_in
    out_shape += c_shape
    out_specs += c_out
    outs = pl.pallas_call(
        functools.partial(_ffn_kernel, final_norm=final_gain is not None, with_attention=attention is not None,
                          with_mixers=mixers is not None, n_cast=len(casts)),
        out_shape=out_shape,
        grid=(steps,),
        in_specs=specs,
        out_specs=out_specs,
        compiler_params=_params(("arbitrary",)),
        name="ffn",
    )(*ins)
    return outs if len(outs) > 1 else outs[0]


def _linear_kernel(*refs, has_norm, has_res, n_w):
    refs = list(refs)
    x_ref = refs.pop(0)
    g_ref = refs.pop(0) if has_norm else None
    r_ref = refs.pop(0) if has_res else None
    w_refs, o_refs = refs[:n_w], refs[n_w:]
    x = x_ref[...]
    if has_norm:
        x = _rmsnorm(x, g_ref[...])
    xb = x.astype(BF16)
    for w_ref, o_ref in zip(w_refs, o_refs, strict=True):
        y = _dot(xb, w_ref[...])
        if has_res:
            y = r_ref[...] + y
        o_ref[...] = y


def _linear(x, weights, gain=None, res=None, *, tm):
    t, k = x.shape
    ins, specs = [x], [pl.BlockSpec((tm, k), lambda i: (i, 0))]
    if gain is not None:
        ins.append(gain)
        specs.append(_resident(gain.shape))
    if res is not None:
        ins.append(res)
        specs.append(pl.BlockSpec((tm, res.shape[1]), lambda i: (i, 0)))
    for w in weights:
        ins.append(w)
        specs.append(_resident(w.shape))
    outs = pl.pallas_call(
        functools.partial(_linear_kernel, has_norm=gain is not None, has_res=res is not None, n_w=len(weights)),
        out_shape=[jax.ShapeDtypeStruct((t, w.shape[1]), F32) for w in weights],
        grid=(t // tm,),
        in_specs=specs,
        out_specs=[pl.BlockSpec((tm, w.shape[1]), lambda i: (i, 0)) for w in weights],
        compiler_params=_params(("arbitrary",)),
        name="linear",
    )(*ins)
    return outs


def _cast_specs(arrays, steps):
    in_specs, out_shape, out_specs = [], [], []
    for a in arrays:
        rows = pl.cdiv(pl.cdiv(a.shape[0], steps), V7X_BF16_SUBLANES) * V7X_BF16_SUBLANES
        last = pl.cdiv(a.shape[0], rows) - 1
        blk = pl.BlockSpec((rows, a.shape[1]), lambda i, last=last: (jnp.minimum(i, last), 0))
        in_specs.append(blk)
        out_specs.append(blk)
        out_shape.append(jax.ShapeDtypeStruct(a.shape, BF16))
    return in_specs, out_shape, out_specs


def _cast_blocks(in_refs, out_refs):
    for i_ref, o_ref in zip(in_refs, out_refs, strict=True):
        o_ref[...] = i_ref[...].astype(BF16)


def _memkv_kernel(*refs, n_cast):
    x_ref, g_ref, wk_ref, wv_ref = refs[:4]
    cast_in = refs[4:4 + n_cast]
    k_ref, v_ref, k4_ref, v4_ref = refs[4 + n_cast:8 + n_cast]
    xb = _rmsnorm(x_ref[...], g_ref[...]).astype(BF16)
    for w_ref, o_ref, o4_ref in ((wk_ref, k_ref, k4_ref), (wv_ref, v_ref, v4_ref)):
        y = _dot(xb, w_ref[...])
        o_ref[0] = y
        o4_ref[0] = y.reshape(MEM_TOKENS, X_HEADS, X_HEAD_DIM)
    _cast_blocks(cast_in, refs[8 + n_cast:])


def _memkv(mem, gain, w_k, w_v, casts=()):
    batch, m, d = mem.shape
    flat = jax.ShapeDtypeStruct((batch, m, d), F32)
    split = jax.ShapeDtypeStruct((batch, m, X_HEADS, X_HEAD_DIM), F32)
    flat_blk = pl.BlockSpec((1, m, d), lambda b: (b, 0, 0))
    split_blk = pl.BlockSpec((1, m, X_HEADS, X_HEAD_DIM), lambda b: (b, 0, 0, 0))
    c_in, c_shape, c_out = _cast_specs(casts, batch)
    return pl.pallas_call(
        functools.partial(_memkv_kernel, n_cast=len(casts)),
        out_shape=[flat, flat, split, split] + c_shape,
        grid=(batch,),
        in_specs=[pl.BlockSpec((m, d), lambda b: (b, 0)), _resident(gain.shape), _resident(w_k.shape),
                  _resident(w_v.shape)] + c_in,
        out_specs=[flat_blk, flat_blk, split_blk, split_blk] + c_out,
        compiler_params=_params(("arbitrary",)),
        name="memkv",
    )(mem.reshape(batch * m, d), gain, w_k, w_v, *casts)


def _cumsum_chunks(x):
    pos = lax.broadcasted_iota(jnp.int32, x.shape, 0) % CHUNK
    step = 1
    while step < CHUNK:
        x = x + jnp.where(pos >= step, pltpu.roll(x, step, axis=0), 0.0)
        step *= 2
    return x


def _mixp_project(tile, h_ref, g_ref, wa_ref, wkt_ref, wxbc_ref, wdt_ref, cos_ref, sin_ref, cost_ref, sint_ref,
                  qdec_ref, kdec_ref, cw_ref, cb_ref, dtb_ref, alog_ref, conv_ref, xbc_s, stage, *, tm, nt):
    q_s, qd_s, kr_s, kd_s, v_s, g_s, z_s, xc_s, pk_s = stage
    row_start = (tile % nt) == 0

    u = _rmsnorm(h_ref[...], g_ref[...]).astype(BF16)
    qvgz = _dot(u, wa_ref[...])
    kt = _dot_nt(wkt_ref[...], u)
    xbc = _dot(u, wxbc_ref[...])
    dt_raw = _dot(u, wdt_ref[...])

    for sl in range(CONV_SLABS):
        lanes = slice(sl * V7X_LANES, (sl + 1) * V7X_LANES)
        xbc_s[sl, 0:CONV_HIST, :] = jnp.where(row_start, 0.0, xbc_s[sl, 0:CONV_HIST, :])
        xbc_s[sl, CONV_HIST:CONV_HIST + tm, :] = xbc[:, lanes]
        conv = cb_ref[:, lanes]
        for i in range(CONV_WIDTH):
            off = CONV_HIST - (CONV_WIDTH - 1) + i
            conv = conv + cw_ref[i:i + 1, lanes] * xbc_s[sl, off:off + tm, :]
        xc_s[sl] = jax.nn.silu(conv)
        conv_ref[0, :, lanes] = xbc_s[sl, CONV_HIST + tm - (CONV_WIDTH - 1):CONV_HIST + tm, :]
        xbc_s[sl, 0:CONV_HIST, :] = xbc_s[sl, tm:tm + CONV_HIST, :]

    dt = _softplus(dt_raw + dtb_ref[...])
    a_cum = _cumsum_chunks(dt * (-jnp.exp(alog_ref[...])))
    head_lanes = lax.broadcasted_iota(jnp.int32, (tm, DT_PAD), 1) < SSD_HEADS
    pk_s[...] = jnp.where(head_lanes, a_cum, dt)

    cos2, sin2 = cos_ref[...], sin_ref[...]
    cos_t, sin_t = cost_ref[...], sint_ref[...]
    half = RET_DK // 2
    for hh in range(RET_HEADS):
        lanes = slice(hh * RET_DK, (hh + 1) * RET_DK)
        qr = _rotary(qvgz[:, PQ_OFF + hh * RET_DK:PQ_OFF + (hh + 1) * RET_DK], cos2, sin2)
        q_s[:, lanes] = qr.astype(BF16)
        qd_s[:, lanes] = (qr * qdec_ref[hh]).astype(BF16)
        k1 = kt[hh * RET_DK:hh * RET_DK + half, :]
        k2 = kt[hh * RET_DK + half:(hh + 1) * RET_DK, :]
        kr = jnp.concatenate([k1 * cos_t - k2 * sin_t, k1 * sin_t + k2 * cos_t], axis=0) * (RET_DK ** -0.5)
        kr_s[lanes, :] = kr.astype(BF16)
        kd_s[lanes, :] = (kr * kdec_ref[hh:hh + 1, :]).astype(BF16)
    v_s[...] = qvgz[:, PV_OFF:PV_OFF + RET_WIDTH].astype(BF16)
    g_s[...] = jax.nn.silu(qvgz[:, PG_OFF:PG_OFF + RET_WIDTH])
    z_s[...] = jax.nn.silu(qvgz[:, PZ_OFF:PZ_OFF + SSD_WIDTH])


def _mixp_heads(tile, h_ref, dmat_ref, cdec_ref, gn_ref, dsk_ref, sn_ref, wout_ref, o_ref, ret_ref, ssm_ref,
                mix_s, sret_s, sssm_s, stage, *, tm, nt):
    q_s, qd_s, kr_s, kd_s, v_s, g_s, z_s, xc_s, pk_s = stage
    nc = tm // CHUNK
    row_start = (tile % nt) == 0

    row_i = lax.broadcasted_iota(jnp.int32, (CHUNK, CHUNK), 0)
    col_i = lax.broadcasted_iota(jnp.int32, (CHUNK, CHUNK), 1)
    causal = row_i >= col_i
    low_lanes = col_i < SSD_HEAD_DIM
    bd_rows = lax.broadcasted_iota(jnp.int32, (2 * SSD_STATE, 2 * SSD_HEAD_DIM), 0)
    bd_cols = lax.broadcasted_iota(jnp.int32, (2 * SSD_STATE, 2 * SSD_HEAD_DIM), 1)
    bd_mask = (bd_rows < SSD_STATE) == (bd_cols < SSD_HEAD_DIM)

    inner, kv = {}, {}
    scores, c_exp, st_inc, st_dec = {}, {}, {}, {}
    for c in range(nc):
        rows = slice(c * CHUNK, (c + 1) * CHUNK)
        for hh in range(RET_HEADS):
            lanes = slice(hh * RET_DK, (hh + 1) * RET_DK)
            inner[c, hh] = (_dot(q_s[rows, lanes], kr_s[lanes, rows]) * dmat_ref[hh]).astype(BF16)
            kv[c, hh] = _dot(kd_s[lanes, rows], v_s[rows, lanes])

        pk = pk_s[rows, :]
        pk_t = pk.T
        a_t = pk_t[0:SSD_HEADS, :]
        dt_t = pk_t[SSD_HEADS:2 * SSD_HEADS, :]
        a_last = a_t[:, CHUNK - 1:CHUNK]
        w_t = jnp.exp(a_last - a_t) * dt_t
        chunk_dec = jnp.exp(a_last)
        for grp in range(SSD_GROUPS):
            cg = xc_s[SSD_PAIRS + SSD_GROUPS + grp, rows, :]
            b_t = xc_s[SSD_PAIRS + grp, rows, :].T
            gmat = _dot(cg.astype(BF16), b_t.astype(BF16))
            for pj in range(SSD_PAIRS // SSD_GROUPS):
                j = grp * (SSD_PAIRS // SSD_GROUPS) + pj
                heads = (2 * j, 2 * j + 1)
                x_pair = xc_s[j, rows, :].astype(BF16)
                b_w = jnp.concatenate([b_t * w_t[hd:hd + 1, :] for hd in heads], axis=0).astype(BF16)
                st_inc[c, j] = jnp.where(bd_mask, _dot(b_w, x_pair), 0.0)
                st_dec[c, j] = jnp.concatenate(
                    [jnp.broadcast_to(chunk_dec[hd:hd + 1, :], (SSD_STATE, 2 * SSD_HEAD_DIM)) for hd in heads], axis=0)
                for hd in heads:
                    a_col = jnp.broadcast_to(pk[:, hd:hd + 1], (CHUNK, CHUNK))
                    seg = a_col - a_t[hd:hd + 1, :]
                    lmat = jnp.where(causal, jnp.exp(jnp.where(causal, seg, 0.0)), 0.0)
                    scores[c, hd] = (gmat * lmat * dt_t[hd:hd + 1, :]).astype(BF16)
                    c_exp[c, hd] = (cg * jnp.exp(a_col)).astype(BF16)

    s_in, st_in = {}, {}
    for hh in range(RET_HEADS):
        s = jnp.where(row_start, 0.0, sret_s[hh])
        for c in range(nc):
            s_in[c, hh] = s.astype(BF16)
            s = cdec_ref[hh:hh + 1, :] * s + kv[c, hh]
        sret_s[hh] = s
        ret_ref[0, hh] = s
    for j in range(SSD_PAIRS):
        s = jnp.where(row_start, 0.0, sssm_s[j])
        for c in range(nc):
            st_in[c, j] = s.astype(BF16)
            s = st_dec[c, j] * s + st_inc[c, j]
        sssm_s[j] = s
        ssm_ref[0, j * 2 * SSD_HEAD_DIM:(j + 1) * 2 * SSD_HEAD_DIM, :] = (s[0:SSD_STATE] + s[SSD_STATE:]).T

    for c in range(nc):
        rows = slice(c * CHUNK, (c + 1) * CHUNK)
        for hh in range(RET_HEADS):
            lanes = slice(hh * RET_DV, (hh + 1) * RET_DV)
            lhs = jnp.concatenate([inner[c, hh], qd_s[rows, lanes]], axis=1)
            rhs = jnp.concatenate([v_s[rows, lanes], s_in[c, hh]], axis=0)
            r = _group_norm(_dot(lhs, rhs)) * gn_ref[:, lanes]
            mix_s[rows, lanes] = (g_s[rows, lanes] * r).astype(BF16)
        ys = []
        for j in range(SSD_PAIRS):
            heads = (2 * j, 2 * j + 1)
            x_pair = xc_s[j, rows, :]
            x_bd = jnp.concatenate([jnp.where(low_lanes, x_pair, 0.0), jnp.where(low_lanes, 0.0, x_pair)], axis=0)
            lhs = jnp.concatenate([scores[c, heads[0]], scores[c, heads[1]], c_exp[c, heads[0]], c_exp[c, heads[1]]],
                                  axis=1)
            rhs = jnp.concatenate([x_bd.astype(BF16), st_in[c, j]], axis=0)
            ys.append(_dot(lhs, rhs) + dsk_ref[:, j * 2 * SSD_HEAD_DIM:(j + 1) * 2 * SSD_HEAD_DIM] * x_pair)
        y_all = jnp.concatenate(ys, axis=1) * z_s[rows, :]
        mix_s[rows, RET_WIDTH:RET_WIDTH + SSD_WIDTH] = _rmsnorm(y_all, sn_ref[...]).astype(BF16)

    o_ref[...] = h_ref[...] + _dot(mix_s[...], wout_ref[...])


def _mix_prompt_kernel(h_ref, hp_ref, g_ref, wa_ref, wkt_ref, wxbc_ref, wdt_ref, cos_ref, sin_ref, cost_ref, sint_ref,
                       dmat_ref, qdec_ref, kdec_ref, cdec_ref, gn_ref, cw_ref, cb_ref, dtb_ref, alog_ref, dsk_ref,
                       sn_ref, wout_ref,
                       o_ref, ret_ref, ssm_ref, conv_ref,
                       xbc_s, mix_s, sret_s, sssm_s, *stage_s, tm, nt, n_tiles):
    s = pl.program_id(0)
    sets = (tuple(r.at[0] for r in stage_s), tuple(r.at[1] for r in stage_s))

    @pl.when(s == 0)
    def _():
        for r in stage_s:
            r[...] = jnp.zeros_like(r)
        sret_s[...] = jnp.zeros_like(sret_s)
        sssm_s[...] = jnp.zeros_like(sssm_s)
        xbc_s[...] = jnp.zeros_like(xbc_s)

    def step(write_set, read_set):
        _mixp_project(jnp.minimum(s, n_tiles - 1), h_ref, g_ref, wa_ref, wkt_ref, wxbc_ref, wdt_ref, cos_ref, sin_ref,
                      cost_ref, sint_ref, qdec_ref, kdec_ref, cw_ref, cb_ref, dtb_ref, alog_ref, conv_ref, xbc_s,
                      write_set, tm=tm, nt=nt)
        _mixp_heads(jnp.maximum(s - 1, 0), hp_ref, dmat_ref, cdec_ref, gn_ref, dsk_ref, sn_ref, wout_ref, o_ref, ret_ref,
                    ssm_ref, mix_s, sret_s, sssm_s, read_set, tm=tm, nt=nt)

    @pl.when(s % 2 == 0)
    def _():
        step(sets[0], sets[1])

    @pl.when(s % 2 == 1)
    def _():
        step(sets[1], sets[0])


def _mix_prompt(h, gain, w_a, w_kt, w_xbc, w_dt, cos2, sin2, cos_t, sin_t, dmat, qdec, kdec, cdec, gn, conv_w, conv_b,
                dt_bias, a_log, dskip, ssd_gain, w_out, *, batch, seq, tm):
    nt = seq // tm
    n_tiles = batch * nt
    d = h.shape[1]
    cur = lambda s: jnp.minimum(s, n_tiles - 1)
    prev = lambda s: jnp.maximum(s - 1, 0)
    row_cur = pl.BlockSpec((tm, d), lambda s: (cur(s), 0))
    row_prev = pl.BlockSpec((tm, d), lambda s: (prev(s), 0))
    pos = pl.BlockSpec((tm, RET_DK), lambda s: (cur(s) % nt, 0))
    pos_t = pl.BlockSpec((RET_DK // 2, tm), lambda s: (0, cur(s) % nt))
    consts = [gain, w_a, w_kt, w_xbc, w_dt]
    tail = [dmat, qdec, kdec, cdec, gn, conv_w, conv_b, dt_bias, a_log, dskip, ssd_gain, w_out]
    stage = [
        pltpu.VMEM((2, tm, RET_HEADS * RET_DK), BF16),
        pltpu.VMEM((2, tm, RET_HEADS * RET_DK), BF16),
        pltpu.VMEM((2, RET_HEADS * RET_DK, tm), BF16),
        pltpu.VMEM((2, RET_HEADS * RET_DK, tm), BF16),
        pltpu.VMEM((2, tm, RET_WIDTH), BF16),
        pltpu.VMEM((2, tm, RET_WIDTH), F32),
        pltpu.VMEM((2, tm, SSD_WIDTH), F32),
        pltpu.VMEM((2, CONV_SLABS, tm, V7X_LANES), F32),
        pltpu.VMEM((2, tm, DT_PAD), F32),
    ]
    return pl.pallas_call(
        functools.partial(_mix_prompt_kernel, tm=tm, nt=nt, n_tiles=n_tiles),
        out_shape=[
            jax.ShapeDtypeStruct(h.shape, F32),
            jax.ShapeDtypeStruct((batch, RET_HEADS, RET_DK, RET_DV), F32),
            jax.ShapeDtypeStruct((batch, SSD_WIDTH, SSD_STATE), F32),
            jax.ShapeDtypeStruct((batch, CONV_WIDTH - 1, CONV_CH), F32),
        ],
        grid=(n_tiles + 1,),
        in_specs=([row_cur, row_prev] + [_resident(a.shape) for a in consts] + [pos, pos, pos_t, pos_t]
                  + [_resident(a.shape) for a in tail]),
        out_specs=[
            row_prev,
            pl.BlockSpec((1, RET_HEADS, RET_DK, RET_DV), lambda s: (prev(s) // nt, 0, 0, 0)),
            pl.BlockSpec((1, SSD_WIDTH, SSD_STATE), lambda s: (prev(s) // nt, 0, 0)),
            pl.BlockSpec((1, CONV_WIDTH - 1, CONV_CH), lambda s: (cur(s) // nt, 0, 0)),
        ],
        scratch_shapes=[
            pltpu.VMEM((CONV_SLABS, CONV_HIST + tm, V7X_LANES), F32),
            pltpu.VMEM((tm, RET_WIDTH + SSD_WIDTH), BF16),
            pltpu.VMEM((RET_HEADS, RET_DK, RET_DV), F32),
            pltpu.VMEM((SSD_PAIRS, 2 * SSD_STATE, 2 * SSD_HEAD_DIM), F32),
        ] + stage,
        compiler_params=_params(("arbitrary",)),
        name="mix_prompt",
    )(h, h, *consts, cos2, sin2, cos_t, sin_t, *tail)


def _split_hi_lo(x):
    hi = x.astype(BF16).astype(F32)
    return hi, x - hi


def _sample_mixers(proj_ref, sret_ref, sssm_ref, sconv_ref, cos_ref, sin_ref, gam_ref,
                   gn_ref, cw_ref, cb_ref, dtb_ref, alog_ref, dsk_ref, sn_ref,
                   mix_ref, oret_ref, ossm_ref, oconv_ref):
    bs = proj_ref.shape[1]
    proj = proj_ref[0]
    xbc = proj[:, XBC_OFF:DT_OFF]
    hist = sconv_ref[0]
    taps = [hist[:, i * CONV_CH:(i + 1) * CONV_CH] for i in range(CONV_WIDTH - 1)] + [xbc]
    conv = cb_ref[...]
    for i in range(CONV_WIDTH):
        conv = conv + cw_ref[i:i + 1, :] * taps[i]
    oconv_ref[0] = jnp.concatenate(taps[1:], axis=1)
    xc = jax.nn.silu(conv)

    dt = _softplus(proj[:, DT_OFF:IN_PROJ_PAD] + dtb_ref[...])

    sub_i = lax.broadcasted_iota(jnp.int32, (bs, V7X_LANES), 0)
    row_of = lax.broadcasted_iota(jnp.int32, (4 * bs, V7X_LANES), 0) % bs
    cos2 = cos_ref[...]
    sin2 = sin_ref[...]

    def outer_lhs(x):
        hi, lo = _split_hi_lo(x)
        return jnp.concatenate([hi, hi, lo, lo], axis=0)

    def outer_rhs(x):
        hi, lo = _split_hi_lo(x)
        return jnp.concatenate([hi, lo, hi, lo], axis=0).astype(BF16)

    def only_sample(x4, b):
        return jnp.where(row_of == b, x4, 0.0).astype(BF16)

    for hh in range(RET_HEADS):
        lanes = slice(hh * RET_DK, (hh + 1) * RET_DK)
        qr = _rotary(proj[:, Q_OFF + hh * RET_DK:Q_OFF + (hh + 1) * RET_DK], cos2, sin2)
        kr = _rotary(proj[:, K_OFF + hh * RET_DK:K_OFF + (hh + 1) * RET_DK], cos2, sin2) * (RET_DK ** -0.5)
        vh = proj[:, V_OFF + hh * RET_DV:V_OFF + (hh + 1) * RET_DV]
        k4, v4, q_bf = outer_lhs(kr), outer_rhs(vh), qr.astype(BF16)
        gamma = gam_ref[hh:hh + 1, :]
        y = jnp.zeros((bs, RET_DV), F32)
        for b in range(bs):
            s_old = sret_ref[b, hh]
            oret_ref[b, hh] = gamma * s_old + _dot_tn(only_sample(k4, b), v4)
            y = jnp.where(sub_i == b, _dot(q_bf, s_old.astype(BF16)), y)
        y = gamma * y + jnp.sum(qr * kr, axis=-1, keepdims=True) * vh
        r = _group_norm(y) * gn_ref[:, lanes]
        mix_ref[0, :, lanes] = jax.nn.silu(proj[:, G_OFF + hh * RET_DV:G_OFF + (hh + 1) * RET_DV]) * r

    xs = xc[:, 0:SSD_WIDTH]
    head_of_lane = lax.broadcasted_iota(jnp.int32, (bs, SSD_WIDTH), 1) // SSD_HEAD_DIM
    dec = jnp.exp(dt * (-jnp.exp(alog_ref[...])))
    dt_wide = jnp.zeros((bs, SSD_WIDTH), F32)
    dec_wide = jnp.zeros((bs, SSD_WIDTH), F32)
    for hd in range(SSD_HEADS):
        dt_wide = jnp.where(head_of_lane == hd, dt[:, hd:hd + 1], dt_wide)
        dec_wide = jnp.where(head_of_lane == hd, dec[:, hd:hd + 1], dec_wide)
    xdt = xs * dt_wide
    x4 = outer_lhs(xdt)
    ys = []
    for j in range(SSD_PAIRS):
        grp = j // (SSD_PAIRS // SSD_GROUPS)
        lanes = slice(j * 2 * SSD_HEAD_DIM, (j + 1) * 2 * SSD_HEAD_DIM)
        bg = xc[:, SSD_WIDTH + grp * SSD_STATE:SSD_WIDTH + (grp + 1) * SSD_STATE]
        c_off = SSD_WIDTH + SSD_GROUPS * SSD_STATE + grp * SSD_STATE
        cg = xc[:, c_off:c_off + SSD_STATE]
        b4, c_bf = outer_rhs(bg), cg.astype(BF16)
        y = jnp.zeros((bs, 2 * SSD_HEAD_DIM), F32)
        for b in range(bs):
            s_old = sssm_ref[b, lanes, :]
            dec_rows = jnp.concatenate(
                [jnp.broadcast_to(dec[b:b + 1, hd:hd + 1], (SSD_HEAD_DIM, SSD_STATE)) for hd in (2 * j, 2 * j + 1)],
                axis=0)
            ossm_ref[b, lanes, :] = dec_rows * s_old + _dot_tn(only_sample(x4[:, lanes], b), b4)
            y = jnp.where(sub_i == b, _dot_nt(c_bf, s_old.astype(BF16)), y)
        ys.append(dec_wide[:, lanes] * y + jnp.sum(cg * bg, axis=-1, keepdims=True) * xdt[:, lanes])
    ys = (jnp.concatenate(ys, axis=1) + dsk_ref[...] * xs) * jax.nn.silu(proj[:, Z_OFF:Z_OFF + SSD_WIDTH])
    mix_ref[0, :, RET_WIDTH:RET_WIDTH + SSD_WIDTH] = _rmsnorm(ys, sn_ref[...])


N_MIXER_IN = 14
N_MIXER_OUT = 4


def _sample_mixer_specs(proj, s_ret, s_ssm, s_conv, consts, steps):
    n = proj.shape[0]
    bs = n // steps
    grouped = lambda a: a.reshape(steps, bs, a.shape[1])
    blk3 = lambda w: pl.BlockSpec((1, bs, w), lambda i: (i, 0, 0))
    ret_blk = pl.BlockSpec((bs, RET_HEADS, RET_DK, RET_DV), lambda i: (i, 0, 0, 0))
    ssm_blk = pl.BlockSpec((bs, SSD_WIDTH, SSD_STATE), lambda i: (i, 0, 0))
    ins = [grouped(proj), s_ret, s_ssm, grouped(s_conv), *consts]
    in_specs = [blk3(proj.shape[1]), ret_blk, ssm_blk, blk3(s_conv.shape[1])] + [_resident(a.shape) for a in consts]
    out_shape = [jax.ShapeDtypeStruct((steps, bs, RET_WIDTH + SSD_WIDTH), F32), jax.ShapeDtypeStruct(s_ret.shape, F32),
                 jax.ShapeDtypeStruct(s_ssm.shape, F32), jax.ShapeDtypeStruct((steps, bs, s_conv.shape[1]), F32)]
    out_specs = [blk3(RET_WIDTH + SSD_WIDTH), ret_blk, ssm_blk, blk3(s_conv.shape[1])]
    return ins, in_specs, out_shape, out_specs


def _softmax_rows(s):
    m = jnp.max(s, axis=-1, keepdims=True)
    p = jnp.exp(s - m)
    return p / jnp.sum(p, axis=-1, keepdims=True)


def _xattn_prompt_kernel(h_ref, g_ref, wq_ref, mk_ref, mv_ref, wo_ref, o_ref, att_s):
    h = h_ref[...]
    c = _rmsnorm(h, g_ref[...]).astype(BF16)
    qx = _dot(c, wq_ref[...]).astype(BF16)
    for hh in range(X_HEADS):
        lanes = slice(hh * X_HEAD_DIM, (hh + 1) * X_HEAD_DIM)
        s = _dot_nt(qx[:, lanes], mk_ref[0, :, lanes].astype(BF16)) * (X_HEAD_DIM ** -0.5)
        att = _softmax_rows(s).astype(BF16)
        att_s[:, lanes] = _dot(att, mv_ref[0, :, lanes].astype(BF16)).astype(BF16)
    o_ref[...] = h + _dot(att_s[...], wo_ref[...])


def _xattn_prompt(h, gain, w_q, mem_k, mem_v, w_o, *, batch, seq, tm):
    nt = seq // tm
    d = h.shape[1]
    row = pl.BlockSpec((tm, d), lambda b, t: (b * nt + t, 0))
    mem = pl.BlockSpec((1, MEM_TOKENS, d), lambda b, t: (b, 0, 0))
    return pl.pallas_call(
        _xattn_prompt_kernel,
        out_shape=jax.ShapeDtypeStruct(h.shape, F32),
        grid=(batch, nt),
        in_specs=[row, _resident(gain.shape), _resident(w_q.shape), mem, mem, _resident(w_o.shape)],
        out_specs=row,
        scratch_shapes=[pltpu.VMEM((tm, d), BF16)],
        compiler_params=_params(("arbitrary", "arbitrary")),
        name="xattn_prompt",
    )(h, gain, w_q, mem_k, mem_v, w_o)


def _rope_angles(pos):
    half = RET_DK // 2
    inv_freq = ROPE_BASE ** (-jnp.arange(half, dtype=F32) / half)
    ang = pos.astype(F32)[:, None] * inv_freq[None, :]
    return jnp.cos(ang), jnp.sin(ang)


def _rope_tables(pos):
    cos, sin = _rope_angles(pos)
    return jnp.concatenate([cos, cos], axis=-1), jnp.concatenate([-sin, sin], axis=-1)


def _retention_decay_tables(chunk):
    log_g = jnp.log1p(-jnp.exp2(-5.0 - jnp.arange(RET_HEADS, dtype=F32)))
    idx = jnp.arange(chunk, dtype=F32)
    diff = idx[:, None] - idx[None, :]
    causal = diff >= 0
    dmat = jnp.where(causal[None], jnp.exp(log_g[:, None, None] * jnp.where(causal, diff, 0.0)[None]), 0.0)
    q_dec = jnp.exp(log_g[:, None] * (idx[None, :] + 1.0))
    k_dec = jnp.exp(log_g[:, None] * (chunk - 1.0 - idx[None, :]))
    c_dec = jnp.exp(log_g * chunk)
    wide = lambda x: jnp.broadcast_to(x[..., None], x.shape + (V7X_LANES,))
    return dmat, wide(q_dec), k_dec, wide(c_dec)


def kernel(x_prompt, x_sample, mem_prompt, state_ret, state_ssm, state_conv, cache_mem_k, cache_mem_v, ffn1_norm,
           ffn1_w1, ffn1_w3, ffn1_w2, mix_norm, w_in, ret_gn_gain, conv_w, conv_b, dt_bias, A_log, D_skip, ssd_norm,
           w_out, x_norm, mem_norm, w_xq, w_xk, w_xv, w_xo, ffn2_norm, ffn2_w1, ffn2_w3, ffn2_w2, final_norm):
    bp, lp, d = x_prompt.shape
    bsz = x_sample.shape[0]
    depth = ffn1_w1.shape[0]
    row = lambda v: v.reshape(1, -1).astype(F32)
    lane_pad = lambda v: jnp.pad(row(v), ((0, 0), (0, DT_PAD - v.shape[-1])))

    cos_p, sin_p = _rope_tables(jnp.arange(lp))
    cos_pt, sin_pt = (a.T for a in _rope_angles(jnp.arange(lp)))
    cos_s, sin_s = _rope_tables(PAST_LEN + jnp.arange(x_sample.shape[1]))
    dmat, q_dec, k_dec, c_dec = _retention_decay_tables(CHUNK)
    gamma1 = _retention_decay_tables(1)[3]

    y_p = x_prompt.reshape(bp * lp, d)
    y_s = x_sample.reshape(bsz, d)
    outs = {k: [] for k in ("ret_p", "ssm_p", "conv_p", "memk", "memv", "ret_s", "ssm_s", "conv_s")}
    for l in range(depth):
        bf = lambda w: w[l].astype(BF16)
        w_in_f = w_in[l]
        w_in_l = jnp.pad(w_in_f, ((0, 0), (0, IN_PROJ_PAD - IN_PROJ_WIDTH))).astype(BF16)
        w_a = jnp.concatenate([w_in_f[:, Q_OFF:K_OFF], w_in_f[:, V_OFF:XBC_OFF]], axis=1).astype(BF16)
        w_kt = w_in_f[:, K_OFF:V_OFF].T.astype(BF16)
        w_xbc = w_in_f[:, XBC_OFF:DT_OFF].astype(BF16)
        w_dt = jnp.pad(jnp.tile(w_in_f[:, DT_OFF:], (1, 2)), ((0, 0), (0, DT_PAD - 2 * SSD_HEADS))).astype(BF16)
        shared = (row(ret_gn_gain[l]), conv_w[l], row(conv_b[l]))
        ssd_tail = (lane_pad(A_log[l]), row(jnp.repeat(D_skip[l], SSD_HEAD_DIM)), row(ssd_norm[l]))
        w_out_l, w_xq_l, w_xo_l = bf(w_out), bf(w_xq), bf(w_xo)

        mk, mv, mk4, mv4, f1_w1, f1_w3, f1_w2 = _memkv(
            mem_prompt, row(mem_norm[l]), bf(w_xk), bf(w_xv), casts=(ffn1_w1[l], ffn1_w3[l], ffn1_w2[l]))
        f1 = (row(ffn1_norm[l]), f1_w1, f1_w3, f1_w2)

        y_s = _ffn(y_s, *f1, tm=bsz)
        proj_s, = _linear(y_s, [w_in_l], gain=row(mix_norm[l]), tm=bsz)

        mixer_consts = (cos_s, sin_s, gamma1, *shared, lane_pad(dt_bias[l]), *ssd_tail)
        y_p, mix_s, ret_s, ssm_s, conv_s, f2_w1, f2_w3, f2_w2 = _ffn(
            y_p, *f1, mixers=(proj_s, state_ret[l], state_ssm[l].reshape(bsz, SSD_WIDTH, SSD_STATE),
                              state_conv[l].reshape(bsz, (CONV_WIDTH - 1) * CONV_CH), mixer_consts),
            casts=(ffn2_w1[l], ffn2_w3[l], ffn2_w2[l]), tm=512)
        f2 = (row(ffn2_norm[l]), f2_w1, f2_w3, f2_w2)
        y_s, = _linear(mix_s.reshape(bsz, d), [w_out_l], res=y_s, tm=bsz)
        q_s, = _linear(y_s, [w_xq_l], gain=row(x_norm[l]), tm=bsz)

        tm_p = 512
        y_p, ret_p, ssm_p, conv_p = _mix_prompt(
            y_p, row(mix_norm[l]), w_a, w_kt, w_xbc, w_dt, cos_p, sin_p, cos_pt, sin_pt, dmat,
            jnp.tile(q_dec, (1, tm_p // CHUNK, 1)), jnp.tile(k_dec, (1, tm_p // CHUNK)), c_dec,
            *shared, lane_pad(jnp.tile(dt_bias[l], 2)), *ssd_tail, w_out_l, batch=bp, seq=lp, tm=tm_p)

        y_p = _xattn_prompt(y_p, row(x_norm[l]), w_xq_l, mk, mv, w_xo_l, batch=bp, seq=lp, tm=512)
        y_p, att_s = _ffn(y_p, *f2, final_gain=row(final_norm) if l == depth - 1 else None,
                          attention=(q_s.reshape(bsz, X_HEADS, X_HEAD_DIM), cache_mem_k[l], cache_mem_v[l]), tm=512)
        y_s, = _linear(att_s.reshape(bsz, d), [w_xo_l], res=y_s, tm=bsz)
        y_s = _ffn(y_s, *f2, final_gain=row(final_norm) if l == depth - 1 else None, tm=bsz)

        outs["ret_p"].append(ret_p)
        outs["ssm_p"].append(ssm_p.reshape(bp, SSD_HEADS, SSD_HEAD_DIM, SSD_STATE))
        outs["conv_p"].append(conv_p)
        outs["memk"].append(mk4)
        outs["memv"].append(mv4)
        outs["ret_s"].append(ret_s)
        outs["ssm_s"].append(ssm_s.reshape(bsz, SSD_HEADS, SSD_HEAD_DIM, SSD_STATE))
        outs["conv_s"].append(conv_s.reshape(bsz, CONV_WIDTH - 1, CONV_CH))

    st = lambda k: jnp.stack(outs[k])
    return (y_p.reshape(bp, lp, d), y_s.reshape(bsz, x_sample.shape[1], d), st("ret_p"), st("ssm_p"), st("conv_p"),
            st("memk"), st("memv"), st("ret_s"), st("ssm_s"), st("conv_s"))
```

```python
import functools

import jax
import jax.numpy as jnp
from jax import lax
from jax.experimental import pallas as pl
from jax.experimental.pallas import tpu as pltpu

F32 = jnp.float32
BF16 = jnp.bfloat16

D_MODEL = 1024
PAST_LEN = 16384
RET_HEADS = 4
RET_DK = 128
RET_DV = 128
RET_WIDTH = RET_HEADS * RET_DV
SSD_HEADS = 8
SSD_HEAD_DIM = 64
SSD_WIDTH = SSD_HEADS * SSD_HEAD_DIM
SSD_GROUPS = 2
SSD_STATE = 128
SSD_PAIRS = SSD_HEADS // 2
CONV_WIDTH = 4
CONV_CH = SSD_WIDTH + 2 * SSD_GROUPS * SSD_STATE
CHUNK = 128
MEM_TOKENS = 256
X_HEADS = 4
X_HEAD_DIM = D_MODEL // X_HEADS
ROPE_BASE = 10000.0
EPS = 1e-6

Q_OFF = 0
K_OFF = Q_OFF + RET_HEADS * RET_DK
V_OFF = K_OFF + RET_HEADS * RET_DK
G_OFF = V_OFF + RET_WIDTH
Z_OFF = G_OFF + RET_WIDTH
XBC_OFF = Z_OFF + SSD_WIDTH
DT_OFF = XBC_OFF + CONV_CH
IN_PROJ_WIDTH = DT_OFF + SSD_HEADS

PQ_OFF = 0
PV_OFF = PQ_OFF + RET_HEADS * RET_DK
PG_OFF = PV_OFF + RET_WIDTH
PZ_OFF = PG_OFF + RET_WIDTH
P_WIDTH = PZ_OFF + SSD_WIDTH

V7X_LANES = 128
V7X_SUBLANES = 8
V7X_BF16_SUBLANES = 16
V7X_VMEM_LIMIT_BYTES = 56 * 1024 * 1024
DT_PAD = V7X_LANES
IN_PROJ_PAD = DT_OFF + DT_PAD
CONV_HIST = V7X_SUBLANES
CONV_SLABS = CONV_CH // V7X_LANES


def _params(sem):
    return pltpu.CompilerParams(dimension_semantics=sem, vmem_limit_bytes=V7X_VMEM_LIMIT_BYTES)


def _resident(shape):
    zeros = (0,) * len(shape)
    return pl.BlockSpec(shape, lambda *_: zeros, pipeline_mode=pl.Buffered(1))


def _rmsnorm(x, gain):
    ms = jnp.mean(x * x, axis=-1, keepdims=True)
    return x * lax.rsqrt(ms + EPS) * gain


def _dot(a, b):
    return jnp.dot(a, b, preferred_element_type=F32)


def _dot_nt(a, b):
    return lax.dot_general(a, b, (((1,), (1,)), ((), ())), preferred_element_type=F32)


def _dot_tn(a, b):
    return lax.dot_general(a, b, (((0,), (0,)), ((), ())), preferred_element_type=F32)


def _softplus(x):
    return jnp.maximum(x, 0.0) + jnp.log1p(jnp.exp(-jnp.abs(x)))


def _rotary(x, cos2, sin2):
    return x * cos2 + pltpu.roll(x, RET_DK // 2, axis=1) * sin2


def _group_norm(y):
    mu = jnp.mean(y, axis=-1, keepdims=True)
    d = y - mu
    var = jnp.mean(d * d, axis=-1, keepdims=True)
    return d * lax.rsqrt(var + EPS)


def _sample_attention(q_ref, k_ref, v_ref, o_ref):
    slabs = MEM_TOKENS * X_HEADS // V7X_SUBLANES
    for b in range(q_ref.shape[0]):
        k3 = k_ref[b].reshape(slabs, V7X_SUBLANES, X_HEAD_DIM)
        v3 = v_ref[b].reshape(slabs, V7X_SUBLANES, X_HEAD_DIM)
        q4 = q_ref[b]
        q8 = jnp.concatenate([q4] * (V7X_SUBLANES // X_HEADS), axis=0)
        s = jnp.sum(k3 * q8[None], axis=-1, keepdims=True) * (X_HEAD_DIM ** -0.5)
        m8 = jnp.max(s, axis=0)
        m4 = jnp.maximum(m8[0:X_HEADS], m8[X_HEADS:])
        p = jnp.exp(s - jnp.concatenate([m4, m4], axis=0)[None])
        acc = jnp.sum(p * v3, axis=0)
        den = jnp.sum(p, axis=0)
        o_ref[b] = (acc[0:X_HEADS] + acc[X_HEADS:]) / (den[0:X_HEADS] + den[X_HEADS:])


def _ffn_kernel(*refs, final_norm, with_attention, with_mixers, n_cast):
    refs = list(refs)
    x_ref, g_ref, w1_ref, w3_ref, w2_ref = refs[:5]
    del refs[:5]
    fg_ref = refs.pop(0) if final_norm else None
    attn_in = [refs.pop(0) for _ in range(3)] if with_attention else None
    mixer_in = [refs.pop(0) for _ in range(N_MIXER_IN)] if with_mixers else None
    cast_in = [refs.pop(0) for _ in range(n_cast)]
    o_ref = refs.pop(0)
    x = x_ref[...]
    xn = _rmsnorm(x, g_ref[...]).astype(BF16)
    a = _dot(xn, w1_ref[...])
    b = _dot(xn, w3_ref[...])
    hidden = (jax.nn.silu(a) * b).astype(BF16)
    out = x + 0.5 * _dot(hidden, w2_ref[...])
    if final_norm:
        out = _rmsnorm(out, fg_ref[...])
    o_ref[...] = out
    if with_attention:
        _sample_attention(*attn_in, refs.pop(0))
    if with_mixers:
        _sample_mixers(*mixer_in, *refs[:N_MIXER_OUT])
        del refs[:N_MIXER_OUT]
    _cast_blocks(cast_in, refs)


def _ffn(x, gain, w1, w3, w2, final_gain=None, attention=None, mixers=None, casts=(), *, tm):
    t, d = x.shape
    steps = t // tm
    row = pl.BlockSpec((tm, d), lambda i: (i, 0))
    ins = [x, gain, w1, w3, w2]
    specs = [row, _resident(gain.shape), _resident(w1.shape), _resident(w3.shape), _resident(w2.shape)]
    out_shape, out_specs = [jax.ShapeDtypeStruct((t, d), F32)], [row]
    if final_gain is not None:
        ins.append(final_gain)
        specs.append(_resident(final_gain.shape))
    if attention is not None:
        q, cache_k, cache_v = attention
        bs = q.shape[0] // steps
        row_s = pl.BlockSpec((bs, X_HEADS, X_HEAD_DIM), lambda i: (i, 0, 0))
        mem_s = pl.BlockSpec((bs, MEM_TOKENS, X_HEADS, X_HEAD_DIM), lambda i: (i, 0, 0, 0))
        ins += [q, cache_k, cache_v]
        specs += [row_s, mem_s, mem_s]
        out_shape.append(jax.ShapeDtypeStruct(q.shape, F32))
        out_specs.append(row_s)
    if mixers is not None:
        m_ins, m_specs, m_shape, m_out_specs = _sample_mixer_specs(*mixers, steps)
        ins += m_ins
        specs += m_specs
        out_shape += m_shape
        out_specs += m_out_specs
    c_in, c_shape, c_out = _cast_specs(casts, steps)
    ins += list(casts)
    specs += c_in
    out_shape += c_shape
    out_specs += c_out
    outs = pl.pallas_call(
        functools.partial(_ffn_kernel, final_norm=final_gain is not None, with_attention=attention is not None,
                          with_mixers=mixers is not None, n_cast=len(casts)),
        out_shape=out_shape,
        grid=(steps,),
        in_specs=specs,
        out_specs=out_specs,
        compiler_params=_params(("arbitrary",)),
        name="ffn",
    )(*ins)
    return outs if len(outs) > 1 else outs[0]


def _linear_kernel(*refs, has_norm, has_res, n_w):
    refs = list(refs)
    x_ref = refs.pop(0)
    g_ref = refs.pop(0) if has_norm else None
    r_ref = refs.pop(0) if has_res else None
    w_refs, o_refs = refs[:n_w], refs[n_w:]
    x = x_ref[...]
    if has_norm:
        x = _rmsnorm(x, g_ref[...])
    xb = x.astype(BF16)
    for w_ref, o_ref in zip(w_refs, o_refs, strict=True):
        y = _dot(xb, w_ref[...])
        if has_res:
            y = r_ref[...] + y
        o_ref[...] = y


def _linear(x, weights, gain=None, res=None, *, tm):
    t, k = x.shape
    ins, specs = [x], [pl.BlockSpec((tm, k), lambda i: (i, 0))]
    if gain is not None:
        ins.append(gain)
        specs.append(_resident(gain.shape))
    if res is not None:
        ins.append(res)
        specs.append(pl.BlockSpec((tm, res.shape[1]), lambda i: (i, 0)))
    for w in weights:
        ins.append(w)
        specs.append(_resident(w.shape))
    outs = pl.pallas_call(
        functools.partial(_linear_kernel, has_norm=gain is not None, has_res=res is not None, n_w=len(weights)),
        out_shape=[jax.ShapeDtypeStruct((t, w.shape[1]), F32) for w in weights],
        grid=(t // tm,),
        in_specs=specs,
        out_specs=[pl.BlockSpec((tm, w.shape[1]), lambda i: (i, 0)) for w in weights],
        compiler_params=_params(("arbitrary",)),
        name="linear",
    )(*ins)
    return outs


def _cast_specs(arrays, steps):
    in_specs, out_shape, out_specs = [], [], []
    for a in arrays:
        rows = pl.cdiv(pl.cdiv(a.shape[0], steps), V7X_BF16_SUBLANES) * V7X_BF16_SUBLANES
        last = pl.cdiv(a.shape[0], rows) - 1
        blk = pl.BlockSpec((rows, a.shape[1]), lambda i, last=last: (jnp.minimum(i, last), 0))
        in_specs.append(blk)
        out_specs.append(blk)
        out_shape.append(jax.ShapeDtypeStruct(a.shape, BF16))
    return in_specs, out_shape, out_specs


def _cast_blocks(in_refs, out_refs):
    for i_ref, o_ref in zip(in_refs, out_refs, strict=True):
        o_ref[...] = i_ref[...].astype(BF16)


def _memkv_kernel(*refs, n_cast):
    x_ref, g_ref, wk_ref, wv_ref = refs[:4]
    cast_in = refs[4:4 + n_cast]
    k_ref, v_ref, k4_ref, v4_ref = refs[4 + n_cast:8 + n_cast]
    xb = _rmsnorm(x_ref[...], g_ref[...]).astype(BF16)
    for w_ref, o_ref, o4_ref in ((wk_ref, k_ref, k4_ref), (wv_ref, v_ref, v4_ref)):
        y = _dot(xb, w_ref[...])
        o_ref[0] = y
        o4_ref[0] = y.reshape(MEM_TOKENS, X_HEADS, X_HEAD_DIM)
    _cast_blocks(cast_in, refs[8 + n_cast:])


def _memkv(mem, gain, w_k, w_v, casts=()):
    batch, m, d = mem.shape
    flat = jax.ShapeDtypeStruct((batch, m, d), F32)
    split = jax.ShapeDtypeStruct((batch, m, X_HEADS, X_HEAD_DIM), F32)
    flat_blk = pl.BlockSpec((1, m, d), lambda b: (b, 0, 0))
    split_blk = pl.BlockSpec((1, m, X_HEADS, X_HEAD_DIM), lambda b: (b, 0, 0, 0))
    c_in, c_shape, c_out = _cast_specs(casts, batch)
    return pl.pallas_call(
        functools.partial(_memkv_kernel, n_cast=len(casts)),
        out_shape=[flat, flat, split, split] + c_shape,
        grid=(batch,),
        in_specs=[pl.BlockSpec((m, d), lambda b: (b, 0)), _resident(gain.shape), _resident(w_k.shape),
                  _resident(w_v.shape)] + c_in,
        out_specs=[flat_blk, flat_blk, split_blk, split_blk] + c_out,
        compiler_params=_params(("arbitrary",)),
        name="memkv",
    )(mem.reshape(batch * m, d), gain, w_k, w_v, *casts)


def _cumsum_chunks(x):
    pos = lax.broadcasted_iota(jnp.int32, x.shape, 0) % CHUNK
    step = 1
    while step < CHUNK:
        x = x + jnp.where(pos >= step, pltpu.roll(x, step, axis=0), 0.0)
        step *= 2
    return x


def _mixp_project(tile, h_ref, g_ref, wa_ref, wkt_ref, wxbc_ref, wdt_ref, cos_ref, sin_ref, cost_ref, sint_ref,
                  qdec_ref, kdec_ref, cw_ref, cb_ref, dtb_ref, alog_ref, conv_ref, xbc_s, stage, *, tm, nt):
    q_s, qd_s, kr_s, kd_s, v_s, g_s, z_s, xc_s, pk_s = stage
    row_start = (tile % nt) == 0

    u = _rmsnorm(h_ref[...], g_ref[...]).astype(BF16)
    qvgz = _dot(u, wa_ref[...])
    kt = _dot_nt(wkt_ref[...], u)
    xbc = _dot(u, wxbc_ref[...])
    dt_raw = _dot(u, wdt_ref[...])

    for sl in range(CONV_SLABS):
        lanes = slice(sl * V7X_LANES, (sl + 1) * V7X_LANES)
        xbc_s[sl, 0:CONV_HIST, :] = jnp.where(row_start, 0.0, xbc_s[sl, 0:CONV_HIST, :])
        xbc_s[sl, CONV_HIST:CONV_HIST + tm, :] = xbc[:, lanes]
        conv = cb_ref[:, lanes]
        for i in range(CONV_WIDTH):
            off = CONV_HIST - (CONV_WIDTH - 1) + i
            conv = conv + cw_ref[i:i + 1, lanes] * xbc_s[sl, off:off + tm, :]
        xc_s[sl] = jax.nn.silu(conv)
        conv_ref[0, :, lanes] = xbc_s[sl, CONV_HIST + tm - (CONV_WIDTH - 1):CONV_HIST + tm, :]
        xbc_s[sl, 0:CONV_HIST, :] = xbc_s[sl, tm:tm + CONV_HIST, :]

    dt = _softplus(dt_raw + dtb_ref[...])
    a_cum = _cumsum_chunks(dt * (-jnp.exp(alog_ref[...])))
    head_lanes = lax.broadcasted_iota(jnp.int32, (tm, DT_PAD), 1) < SSD_HEADS
    pk_s[...] = jnp.where(head_lanes, a_cum, dt)

    cos2, sin2 = cos_ref[...], sin_ref[...]
    cos_t, sin_t = cost_ref[...], sint_ref[...]
    half = RET_DK // 2
    for hh in range(RET_HEADS):
        lanes = slice(hh * RET_DK, (hh + 1) * RET_DK)
        qr = _rotary(qvgz[:, PQ_OFF + hh * RET_DK:PQ_OFF + (hh + 1) * RET_DK], cos2, sin2)
        q_s[:, lanes] = qr.astype(BF16)
        qd_s[:, lanes] = (qr * qdec_ref[hh]).astype(BF16)
        k1 = kt[hh * RET_DK:hh * RET_DK + half, :]
        k2 = kt[hh * RET_DK + half:(hh + 1) * RET_DK, :]
        kr = jnp.concatenate([k1 * cos_t - k2 * sin_t, k1 * sin_t + k2 * cos_t], axis=0) * (RET_DK ** -0.5)
        kr_s[lanes, :] = kr.astype(BF16)
        kd_s[lanes, :] = (kr * kdec_ref[hh:hh + 1, :]).astype(BF16)
    v_s[...] = qvgz[:, PV_OFF:PV_OFF + RET_WIDTH].astype(BF16)
    g_s[...] = jax.nn.silu(qvgz[:, PG_OFF:PG_OFF + RET_WIDTH])
    z_s[...] = jax.nn.silu(qvgz[:, PZ_OFF:PZ_OFF + SSD_WIDTH])


def _mixp_heads(tile, h_ref, dmat_ref, cdec_ref, gn_ref, dsk_ref, sn_ref, wout_ref, o_ref, ret_ref, ssm_ref,
                mix_s, sret_s, sssm_s, stage, *, tm, nt):
    q_s, qd_s, kr_s, kd_s, v_s, g_s, z_s, xc_s, pk_s = stage
    nc = tm // CHUNK
    row_start = (tile % nt) == 0

    row_i = lax.broadcasted_iota(jnp.int32, (CHUNK, CHUNK), 0)
    col_i = lax.broadcasted_iota(jnp.int32, (CHUNK, CHUNK), 1)
    causal = row_i >= col_i
    low_lanes = col_i < SSD_HEAD_DIM
    bd_rows = lax.broadcasted_iota(jnp.int32, (2 * SSD_STATE, 2 * SSD_HEAD_DIM), 0)
    bd_cols = lax.broadcasted_iota(jnp.int32, (2 * SSD_STATE, 2 * SSD_HEAD_DIM), 1)
    bd_mask = (bd_rows < SSD_STATE) == (bd_cols < SSD_HEAD_DIM)

    inner, kv = {}, {}
    scores, c_exp, st_inc, st_dec = {}, {}, {}, {}
    for c in range(nc):
        rows = slice(c * CHUNK, (c + 1) * CHUNK)
        for hh in range(RET_HEADS):
            lanes = slice(hh * RET_DK, (hh + 1) * RET_DK)
            inner[c, hh] = (_dot(q_s[rows, lanes], kr_s[lanes, rows]) * dmat_ref[hh]).astype(BF16)
            kv[c, hh] = _dot(kd_s[lanes, rows], v_s[rows, lanes])

        pk = pk_s[rows, :]
        pk_t = pk.T
        a_t = pk_t[0:SSD_HEADS, :]
        dt_t = pk_t[SSD_HEADS:2 * SSD_HEADS, :]
        a_last = a_t[:, CHUNK - 1:CHUNK]
        w_t = jnp.exp(a_last - a_t) * dt_t
        chunk_dec = jnp.exp(a_last)
        for grp in range(SSD_GROUPS):
            cg = xc_s[SSD_PAIRS + SSD_GROUPS + grp, rows, :]
            b_t = xc_s[SSD_PAIRS + grp, rows, :].T
            gmat = _dot(cg.astype(BF16), b_t.astype(BF16))
            for pj in range(SSD_PAIRS // SSD_GROUPS):
                j = grp * (SSD_PAIRS // SSD_GROUPS) + pj
                heads = (2 * j, 2 * j + 1)
                x_pair = xc_s[j, rows, :].astype(BF16)
                b_w = jnp.concatenate([b_t * w_t[hd:hd + 1, :] for hd in heads], axis=0).astype(BF16)
                st_inc[c, j] = jnp.where(bd_mask, _dot(b_w, x_pair), 0.0)
                st_dec[c, j] = jnp.concatenate(
                    [jnp.broadcast_to(chunk_dec[hd:hd + 1, :], (SSD_STATE, 2 * SSD_HEAD_DIM)) for hd in heads], axis=0)
                for hd in heads:
                    a_col = jnp.broadcast_to(pk[:, hd:hd + 1], (CHUNK, CHUNK))
                    seg = a_col - a_t[hd:hd + 1, :]
                    lmat = jnp.where(causal, jnp.exp(jnp.where(causal, seg, 0.0)), 0.0)
                    scores[c, hd] = (gmat * lmat * dt_t[hd:hd + 1, :]).astype(BF16)
                    c_exp[c, hd] = (cg * jnp.exp(a_col)).astype(BF16)

    s_in, st_in = {}, {}
    for hh in range(RET_HEADS):
        s = jnp.where(row_start, 0.0, sret_s[hh])
        for c in range(nc):
            s_in[c, hh] = s.astype(BF16)
            s = cdec_ref[hh:hh + 1, :] * s + kv[c, hh]
        sret_s[hh] = s
        ret_ref[0, hh] = s
    for j in range(SSD_PAIRS):
        s = jnp.where(row_start, 0.0, sssm_s[j])
        for c in range(nc):
            st_in[c, j] = s.astype(BF16)
            s = st_dec[c, j] * s + st_inc[c, j]
        sssm_s[j] = s
        ssm_ref[0, j * 2 * SSD_HEAD_DIM:(j + 1) * 2 * SSD_HEAD_DIM, :] = (s[0:SSD_STATE] + s[SSD_STATE:]).T

    for c in range(nc):
        rows = slice(c * CHUNK, (c + 1) * CHUNK)
        for hh in range(RET_HEADS):
            lanes = slice(hh * RET_DV, (hh + 1) * RET_DV)
            lhs = jnp.concatenate([inner[c, hh], qd_s[rows, lanes]], axis=1)
            rhs = jnp.concatenate([v_s[rows, lanes], s_in[c, hh]], axis=0)
            r = _group_norm(_dot(lhs, rhs)) * gn_ref[:, lanes]
            mix_s[rows, lanes] = (g_s[rows, lanes] * r).astype(BF16)
        ys = []
        for j in range(SSD_PAIRS):
            heads = (2 * j, 2 * j + 1)
            x_pair = xc_s[j, rows, :]
            x_bd = jnp.concatenate([jnp.where(low_lanes, x_pair, 0.0), jnp.where(low_lanes, 0.0, x_pair)], axis=0)
            lhs = jnp.concatenate([scores[c, heads[0]], scores[c, heads[1]], c_exp[c, heads[0]], c_exp[c, heads[1]]],
                                  axis=1)
            rhs = jnp.concatenate([x_bd.astype(BF16), st_in[c, j]], axis=0)
            ys.append(_dot(lhs, rhs) + dsk_ref[:, j * 2 * SSD_HEAD_DIM:(j + 1) * 2 * SSD_HEAD_DIM] * x_pair)
        y_all = jnp.concatenate(ys, axis=1) * z_s[rows, :]
        mix_s[rows, RET_WIDTH:RET_WIDTH + SSD_WIDTH] = _rmsnorm(y_all, sn_ref[...]).astype(BF16)

    o_ref[...] = h_ref[...] + _dot(mix_s[...], wout_ref[...])


def _mix_prompt_kernel(h_ref, hp_ref, g_ref, wa_ref, wkt_ref, wxbc_ref, wdt_ref, cos_ref, sin_ref, cost_ref, sint_ref,
                       dmat_ref, qdec_ref, kdec_ref, cdec_ref, gn_ref, cw_ref, cb_ref, dtb_ref, alog_ref, dsk_ref,
                       sn_ref, wout_ref,
                       o_ref, ret_ref, ssm_ref, conv_ref,
                       xbc_s, mix_s, sret_s, sssm_s, *stage_s, tm, nt, n_tiles):
    s = pl.program_id(0)
    sets = (tuple(r.at[0] for r in stage_s), tuple(r.at[1] for r in stage_s))

    @pl.when(s == 0)
    def _():
        for r in stage_s:
            r[...] = jnp.zeros_like(r)
        sret_s[...] = jnp.zeros_like(sret_s)
        sssm_s[...] = jnp.zeros_like(sssm_s)
        xbc_s[...] = jnp.zeros_like(xbc_s)

    def step(write_set, read_set):
        _mixp_project(jnp.minimum(s, n_tiles - 1), h_ref, g_ref, wa_ref, wkt_ref, wxbc_ref, wdt_ref, cos_ref, sin_ref,
                      cost_ref, sint_ref, qdec_ref, kdec_ref, cw_ref, cb_ref, dtb_ref, alog_ref, conv_ref, xbc_s,
                      write_set, tm=tm, nt=nt)
        _mixp_heads(jnp.maximum(s - 1, 0), hp_ref, dmat_ref, cdec_ref, gn_ref, dsk_ref, sn_ref, wout_ref, o_ref, ret_ref,
                    ssm_ref, mix_s, sret_s, sssm_s, read_set, tm=tm, nt=nt)

    @pl.when(s % 2 == 0)
    def _():
        step(sets[0], sets[1])

    @pl.when(s % 2 == 1)
    def _():
        step(sets[1], sets[0])


def _mix_prompt(h, gain, w_a, w_kt, w_xbc, w_dt, cos2, sin2, cos_t, sin_t, dmat, qdec, kdec, cdec, gn, conv_w, conv_b,
                dt_bias, a_log, dskip, ssd_gain, w_out, *, batch, seq, tm):
    nt = seq // tm
    n_tiles = batch * nt
    d = h.shape[1]
    cur = lambda s: jnp.minimum(s, n_tiles - 1)
    prev = lambda s: jnp.maximum(s - 1, 0)
    row_cur = pl.BlockSpec((tm, d), lambda s: (cur(s), 0))
    row_prev = pl.BlockSpec((tm, d), lambda s: (prev(s), 0))
    pos = pl.BlockSpec((tm, RET_DK), lambda s: (cur(s) % nt, 0))
    pos_t = pl.BlockSpec((RET_DK // 2, tm), lambda s: (0, cur(s) % nt))
    consts = [gain, w_a, w_kt, w_xbc, w_dt]
    tail = [dmat, qdec, kdec, cdec, gn, conv_w, conv_b, dt_bias, a_log, dskip, ssd_gain, w_out]
    stage = [
        pltpu.VMEM((2, tm, RET_HEADS * RET_DK), BF16),
        pltpu.VMEM((2, tm, RET_HEADS * RET_DK), BF16),
        pltpu.VMEM((2, RET_HEADS * RET_DK, tm), BF16),
        pltpu.VMEM((2, RET_HEADS * RET_DK, tm), BF16),
        pltpu.VMEM((2, tm, RET_WIDTH), BF16),
        pltpu.VMEM((2, tm, RET_WIDTH), F32),
        pltpu.VMEM((2, tm, SSD_WIDTH), F32),
        pltpu.VMEM((2, CONV_SLABS, tm, V7X_LANES), F32),
        pltpu.VMEM((2, tm, DT_PAD), F32),
    ]
    return pl.pallas_call(
        functools.partial(_mix_prompt_kernel, tm=tm, nt=nt, n_tiles=n_tiles),
        out_shape=[
            jax.ShapeDtypeStruct(h.shape, F32),
            jax.ShapeDtypeStruct((batch, RET_HEADS, RET_DK, RET_DV), F32),
            jax.ShapeDtypeStruct((batch, SSD_WIDTH, SSD_STATE), F32),
            jax.ShapeDtypeStruct((batch, CONV_WIDTH - 1, CONV_CH), F32),
        ],
        grid=(n_tiles + 1,),
        in_specs=([row_cur, row_prev] + [_resident(a.shape) for a in consts] + [pos, pos, pos_t, pos_t]
                  + [_resident(a.shape) for a in tail]),
        out_specs=[
            row_prev,
            pl.BlockSpec((1, RET_HEADS, RET_DK, RET_DV), lambda s: (prev(s) // nt, 0, 0, 0)),
            pl.BlockSpec((1, SSD_WIDTH, SSD_STATE), lambda s: (prev(s) // nt, 0, 0)),
            pl.BlockSpec((1, CONV_WIDTH - 1, CONV_CH), lambda s: (cur(s) // nt, 0, 0)),
        ],
        scratch_shapes=[
            pltpu.VMEM((CONV_SLABS, CONV_HIST + tm, V7X_LANES), F32),
            pltpu.VMEM((tm, RET_WIDTH + SSD_WIDTH), BF16),
            pltpu.VMEM((RET_HEADS, RET_DK, RET_DV), F32),
            pltpu.VMEM((SSD_PAIRS, 2 * SSD_STATE, 2 * SSD_HEAD_DIM), F32),
        ] + stage,
        compiler_params=_params(("arbitrary",)),
        name="mix_prompt",
    )(h, h, *consts, cos2, sin2, cos_t, sin_t, *tail)


def _split_hi_lo(x):
    hi = x.astype(BF16).astype(F32)
    return hi, x - hi


def _sample_mixers(proj_ref, sret_ref, sssm_ref, sconv_ref, cos_ref, sin_ref, gam_ref,
                   gn_ref, cw_ref, cb_ref, dtb_ref, alog_ref, dsk_ref, sn_ref,
                   mix_ref, oret_ref, ossm_ref, oconv_ref):
    bs = proj_ref.shape[1]
    proj = proj_ref[0]
    xbc = proj[:, XBC_OFF:DT_OFF]
    hist = sconv_ref[0]
    taps = [hist[:, i * CONV_CH:(i + 1) * CONV_CH] for i in range(CONV_WIDTH - 1)] + [xbc]
    conv = cb_ref[...]
    for i in range(CONV_WIDTH):
        conv = conv + cw_ref[i:i + 1, :] * taps[i]
    oconv_ref[0] = jnp.concatenate(taps[1:], axis=1)
    xc = jax.nn.silu(conv)

    dt = _softplus(proj[:, DT_OFF:IN_PROJ_PAD] + dtb_ref[...])

    sub_i = lax.broadcasted_iota(jnp.int32, (bs, V7X_LANES), 0)
    row_of = lax.broadcasted_iota(jnp.int32, (4 * bs, V7X_LANES), 0) % bs
    cos2 = cos_ref[...]
    sin2 = sin_ref[...]

    def outer_lhs(x):
        hi, lo = _split_hi_lo(x)
        return jnp.concatenate([hi, hi, lo, lo], axis=0)

    def outer_rhs(x):
        hi, lo = _split_hi_lo(x)
        return jnp.concatenate([hi, lo, hi, lo], axis=0).astype(BF16)

    def only_sample(x4, b):
        return jnp.where(row_of == b, x4, 0.0).astype(BF16)

    for hh in range(RET_HEADS):
        lanes = slice(hh * RET_DK, (hh + 1) * RET_DK)
        qr = _rotary(proj[:, Q_OFF + hh * RET_DK:Q_OFF + (hh + 1) * RET_DK], cos2, sin2)
        kr = _rotary(proj[:, K_OFF + hh * RET_DK:K_OFF + (hh + 1) * RET_DK], cos2, sin2) * (RET_DK ** -0.5)
        vh = proj[:, V_OFF + hh * RET_DV:V_OFF + (hh + 1) * RET_DV]
        k4, v4, q_bf = outer_lhs(kr), outer_rhs(vh), qr.astype(BF16)
        gamma = gam_ref[hh:hh + 1, :]
        y = jnp.zeros((bs, RET_DV), F32)
        for b in range(bs):
            s_old = sret_ref[b, hh]
            oret_ref[b, hh] = gamma * s_old + _dot_tn(only_sample(k4, b), v4)
            y = jnp.where(sub_i == b, _dot(q_bf, s_old.astype(BF16)), y)
        y = gamma * y + jnp.sum(qr * kr, axis=-1, keepdims=True) * vh
        r = _group_norm(y) * gn_ref[:, lanes]
        mix_ref[0, :, lanes] = jax.nn.silu(proj[:, G_OFF + hh * RET_DV:G_OFF + (hh + 1) * RET_DV]) * r

    xs = xc[:, 0:SSD_WIDTH]
    head_of_lane = lax.broadcasted_iota(jnp.int32, (bs, SSD_WIDTH), 1) // SSD_HEAD_DIM
    dec = jnp.exp(dt * (-jnp.exp(alog_ref[...])))
    dt_wide = jnp.zeros((bs, SSD_WIDTH), F32)
    dec_wide = jnp.zeros((bs, SSD_WIDTH), F32)
    for hd in range(SSD_HEADS):
        dt_wide = jnp.where(head_of_lane == hd, dt[:, hd:hd + 1], dt_wide)
        dec_wide = jnp.where(head_of_lane == hd, dec[:, hd:hd + 1], dec_wide)
    xdt = xs * dt_wide
    x4 = outer_lhs(xdt)
    ys = []
    for j in range(SSD_PAIRS):
        grp = j // (SSD_PAIRS // SSD_GROUPS)
        lanes = slice(j * 2 * SSD_HEAD_DIM, (j + 1) * 2 * SSD_HEAD_DIM)
        bg = xc[:, SSD_WIDTH + grp * SSD_STATE:SSD_WIDTH + (grp + 1) * SSD_STATE]
        c_off = SSD_WIDTH + SSD_GROUPS * SSD_STATE + grp * SSD_STATE
        cg = xc[:, c_off:c_off + SSD_STATE]
        b4, c_bf = outer_rhs(bg), cg.astype(BF16)
        y = jnp.zeros((bs, 2 * SSD_HEAD_DIM), F32)
        for b in range(bs):
            s_old = sssm_ref[b, lanes, :]
            dec_rows = jnp.concatenate(
                [jnp.broadcast_to(dec[b:b + 1, hd:hd + 1], (SSD_HEAD_DIM, SSD_STATE)) for hd in (2 * j, 2 * j + 1)],
                axis=0)
            ossm_ref[b, lanes, :] = dec_rows * s_old + _dot_tn(only_sample(x4[:, lanes], b), b4)
            y = jnp.where(sub_i == b, _dot_nt(c_bf, s_old.astype(BF16)), y)
        ys.append(dec_wide[:, lanes] * y + jnp.sum(cg * bg, axis=-1, keepdims=True) * xdt[:, lanes])
    ys = (jnp.concatenate(ys, axis=1) + dsk_ref[...] * xs) * jax.nn.silu(proj[:, Z_OFF:Z_OFF + SSD_WIDTH])
    mix_ref[0, :, RET_WIDTH:RET_WIDTH + SSD_WIDTH] = _rmsnorm(ys, sn_ref[...])


N_MIXER_IN = 14
N_MIXER_OUT = 4


def _sample_mixer_specs(proj, s_ret, s_ssm, s_conv, consts, steps):
    n = proj.shape[0]
    bs = n // steps
    grouped = lambda a: a.reshape(steps, bs, a.shape[1])
    blk3 = lambda w: pl.BlockSpec((1, bs, w), lambda i: (i, 0, 0))
    ret_blk = pl.BlockSpec((bs, RET_HEADS, RET_DK, RET_DV), lambda i: (i, 0, 0, 0))
    ssm_blk = pl.BlockSpec((bs, SSD_WIDTH, SSD_STATE), lambda i: (i, 0, 0))
    ins = [grouped(proj), s_ret, s_ssm, grouped(s_conv), *consts]
    in_specs = [blk3(proj.shape[1]), ret_blk, ssm_blk, blk3(s_conv.shape[1])] + [_resident(a.shape) for a in consts]
    out_shape = [jax.ShapeDtypeStruct((steps, bs, RET_WIDTH + SSD_WIDTH), F32), jax.ShapeDtypeStruct(s_ret.shape, F32),
                 jax.ShapeDtypeStruct(s_ssm.shape, F32), jax.ShapeDtypeStruct((steps, bs, s_conv.shape[1]), F32)]
    out_specs = [blk3(RET_WIDTH + SSD_WIDTH), ret_blk, ssm_blk, blk3(s_conv.shape[1])]
    return ins, in_specs, out_shape, out_specs


def _softmax_rows(s):
    m = jnp.max(s, axis=-1, keepdims=True)
    p = jnp.exp(s - m)
    return p / jnp.sum(p, axis=-1, keepdims=True)


def _xattn_prompt_kernel(h_ref, g_ref, wq_ref, mk_ref, mv_ref, wo_ref, qs_ref, ks_ref, vs_ref, o_ref, os_ref, att_s):
    h = h_ref[...]
    c = _rmsnorm(h, g_ref[...]).astype(BF16)
    qx = _dot(c, wq_ref[...]).astype(BF16)
    for hh in range(X_HEADS):
        lanes = slice(hh * X_HEAD_DIM, (hh + 1) * X_HEAD_DIM)
        s = _dot_nt(qx[:, lanes], mk_ref[0, :, lanes].astype(BF16)) * (X_HEAD_DIM ** -0.5)
        att = _softmax_rows(s).astype(BF16)
        att_s[:, lanes] = _dot(att, mv_ref[0, :, lanes].astype(BF16)).astype(BF16)
    o_ref[...] = h + _dot(att_s[...], wo_ref[...])
    _sample_attention(qs_ref, ks_ref, vs_ref, os_ref)


def _xattn_prompt(h, gain, w_q, mem_k, mem_v, w_o, q_s, cache_k, cache_v, *, batch, seq, tm):
    nt = seq // tm
    d = h.shape[1]
    bs = q_s.shape[0] // (batch * nt)
    row = pl.BlockSpec((tm, d), lambda b, t: (b * nt + t, 0))
    mem = pl.BlockSpec((1, MEM_TOKENS, d), lambda b, t: (b, 0, 0))
    row_s = pl.BlockSpec((bs, X_HEADS, X_HEAD_DIM), lambda b, t: (b * nt + t, 0, 0))
    mem_s = pl.BlockSpec((bs, MEM_TOKENS, X_HEADS, X_HEAD_DIM), lambda b, t: (b * nt + t, 0, 0, 0))
    return pl.pallas_call(
        _xattn_prompt_kernel,
        out_shape=[jax.ShapeDtypeStruct(h.shape, F32), jax.ShapeDtypeStruct(q_s.shape, F32)],
        grid=(batch, nt),
        in_specs=[row, _resident(gain.shape), _resident(w_q.shape), mem, mem, _resident(w_o.shape),
                  row_s, mem_s, mem_s],
        out_specs=[row, row_s],
        scratch_shapes=[pltpu.VMEM((tm, d), BF16)],
        compiler_params=_params(("arbitrary", "arbitrary")),
        name="xattn_prompt",
    )(h, gain, w_q, mem_k, mem_v, w_o, q_s, cache_k, cache_v)


def _rope_angles(pos):
    half = RET_DK // 2
    inv_freq = ROPE_BASE ** (-jnp.arange(half, dtype=F32) / half)
    ang = pos.astype(F32)[:, None] * inv_freq[None, :]
    return jnp.cos(ang), jnp.sin(ang)


def _rope_tables(pos):
    cos, sin = _rope_angles(pos)
    return jnp.concatenate([cos, cos], axis=-1), jnp.concatenate([-sin, sin], axis=-1)


def _retention_decay_tables(chunk):
    log_g = jnp.log1p(-jnp.exp2(-5.0 - jnp.arange(RET_HEADS, dtype=F32)))
    idx = jnp.arange(chunk, dtype=F32)
    diff = idx[:, None] - idx[None, :]
    causal = diff >= 0
    dmat = jnp.where(causal[None], jnp.exp(log_g[:, None, None] * jnp.where(causal, diff, 0.0)[None]), 0.0)
    q_dec = jnp.exp(log_g[:, None] * (idx[None, :] + 1.0))
    k_dec = jnp.exp(log_g[:, None] * (chunk - 1.0 - idx[None, :]))
    c_dec = jnp.exp(log_g * chunk)
    wide = lambda x: jnp.broadcast_to(x[..., None], x.shape + (V7X_LANES,))
    return dmat, wide(q_dec), k_dec, wide(c_dec)


def kernel(x_prompt, x_sample, mem_prompt, state_ret, state_ssm, state_conv, cache_mem_k, cache_mem_v, ffn1_norm,
           ffn1_w1, ffn1_w3, ffn1_w2, mix_norm, w_in, ret_gn_gain, conv_w, conv_b, dt_bias, A_log, D_skip, ssd_norm,
           w_out, x_norm, mem_norm, w_xq, w_xk, w_xv, w_xo, ffn2_norm, ffn2_w1, ffn2_w3, ffn2_w2, final_norm):
    bp, lp, d = x_prompt.shape
    bsz = x_sample.shape[0]
    depth = ffn1_w1.shape[0]
    row = lambda v: v.reshape(1, -1).astype(F32)
    lane_pad = lambda v: jnp.pad(row(v), ((0, 0), (0, DT_PAD - v.shape[-1])))

    cos_p, sin_p = _rope_tables(jnp.arange(lp))
    cos_pt, sin_pt = (a.T for a in _rope_angles(jnp.arange(lp)))
    cos_s, sin_s = _rope_tables(PAST_LEN + jnp.arange(x_sample.shape[1]))
    dmat, q_dec, k_dec, c_dec = _retention_decay_tables(CHUNK)
    gamma1 = _retention_decay_tables(1)[3]

    y_p = x_prompt.reshape(bp * lp, d)
    y_s = x_sample.reshape(bsz, d)
    outs = {k: [] for k in ("ret_p", "ssm_p", "conv_p", "memk", "memv", "ret_s", "ssm_s", "conv_s")}
    for l in range(depth):
        bf = lambda w: w[l].astype(BF16)
        w_in_f = w_in[l]
        w_in_l = jnp.pad(w_in_f, ((0, 0), (0, IN_PROJ_PAD - IN_PROJ_WIDTH))).astype(BF16)
        w_a = jnp.concatenate([w_in_f[:, Q_OFF:K_OFF], w_in_f[:, V_OFF:XBC_OFF]], axis=1).astype(BF16)
        w_kt = w_in_f[:, K_OFF:V_OFF].T.astype(BF16)
        w_xbc = w_in_f[:, XBC_OFF:DT_OFF].astype(BF16)
        w_dt = jnp.pad(jnp.tile(w_in_f[:, DT_OFF:], (1, 2)), ((0, 0), (0, DT_PAD - 2 * SSD_HEADS))).astype(BF16)
        shared = (row(ret_gn_gain[l]), conv_w[l], row(conv_b[l]))
        ssd_tail = (lane_pad(A_log[l]), row(jnp.repeat(D_skip[l], SSD_HEAD_DIM)), row(ssd_norm[l]))
        w_out_l, w_xq_l, w_xo_l = bf(w_out), bf(w_xq), bf(w_xo)

        mk, mv, mk4, mv4, f1_w1, f1_w3, f1_w2 = _memkv(
            mem_prompt, row(mem_norm[l]), bf(w_xk), bf(w_xv), casts=(ffn1_w1[l], ffn1_w3[l], ffn1_w2[l]))
        f1 = (row(ffn1_norm[l]), f1_w1, f1_w3, f1_w2)

        y_s = _ffn(y_s, *f1, tm=bsz)
        proj_s, = _linear(y_s, [w_in_l], gain=row(mix_norm[l]), tm=bsz)

        mixer_consts = (cos_s, sin_s, gamma1, *shared, lane_pad(dt_bias[l]), *ssd_tail)
        y_p, mix_s, ret_s, ssm_s, conv_s, f2_w1, f2_w3, f2_w2 = _ffn(
            y_p, *f1, mixers=(proj_s, state_ret[l], state_ssm[l].reshape(bsz, SSD_WIDTH, SSD_STATE),
                              state_conv[l].reshape(bsz, (CONV_WIDTH - 1) * CONV_CH), mixer_consts),
            casts=(ffn2_w1[l], ffn2_w3[l], ffn2_w2[l]), tm=512)
        f2 = (row(ffn2_norm[l]), f2_w1, f2_w3, f2_w2)
        y_s, = _linear(mix_s.reshape(bsz, d), [w_out_l], res=y_s, tm=bsz)
        q_s, = _linear(y_s, [w_xq_l], gain=row(x_norm[l]), tm=bsz)

        tm_p = 512
        y_p, ret_p, ssm_p, conv_p = _mix_prompt(
            y_p, row(mix_norm[l]), w_a, w_kt, w_xbc, w_dt, cos_p, sin_p, cos_pt, sin_pt, dmat,
            jnp.tile(q_dec, (1, tm_p // CHUNK, 1)), jnp.tile(k_dec, (1, tm_p // CHUNK)), c_dec,
            *shared, lane_pad(jnp.tile(dt_bias[l], 2)), *ssd_tail, w_out_l, batch=bp, seq=lp, tm=tm_p)

        y_p, att_s = _xattn_prompt(y_p, row(x_norm[l]), w_xq_l, mk, mv, w_xo_l, q_s.reshape(bsz, X_HEADS, X_HEAD_DIM),
                                   cache_mem_k[l], cache_mem_v[l], batch=bp, seq=lp, tm=512)
        y_p = _ffn(y_p, *f2, final_gain=row(final_norm) if l == depth - 1 else None, tm=512)
        y_s, = _linear(att_s.reshape(bsz, d), [w_xo_l], res=y_s, tm=bsz)
        y_s = _ffn(y_s, *f2, final_gain=row(final_norm) if l == depth - 1 else None, tm=bsz)

        outs["ret_p"].append(ret_p)
        outs["ssm_p"].append(ssm_p.reshape(bp, SSD_HEADS, SSD_HEAD_DIM, SSD_STATE))
        outs["conv_p"].append(conv_p)
        outs["memk"].append(mk4)
        outs["memv"].append(mv4)
        outs["ret_s"].append(ret_s)
        outs["ssm_s"].append(ssm_s.reshape(bsz, SSD_HEADS, SSD_HEAD_DIM, SSD_STATE))
        outs["conv_s"].append(conv_s.reshape(bsz, CONV_WIDTH - 1, CONV_CH))

    st = lambda k: jnp.stack(outs[k])
    return (y_p.reshape(bp, lp, d), y_s.reshape(bsz, x_sample.shape[1], d), st("ret_p"), st("ssm_p"), st("conv_p"),
            st("memk"), st("memv"), st("ret_s"), st("ssm_s"), st("conv_s"))
```

```python
import functools

import jax
import jax.numpy as jnp
from jax import lax
from jax.experimental import pallas as pl
from jax.experimental.pallas import tpu as pltpu

F32 = jnp.float32
BF16 = jnp.bfloat16

D_MODEL = 1024
PAST_LEN = 16384
RET_HEADS = 4
RET_DK = 128
RET_DV = 128
RET_WIDTH = RET_HEADS * RET_DV
SSD_HEADS = 8
SSD_HEAD_DIM = 64
SSD_WIDTH = SSD_HEADS * SSD_HEAD_DIM
SSD_GROUPS = 2
SSD_STATE = 128
SSD_PAIRS = SSD_HEADS // 2
CONV_WIDTH = 4
CONV_CH = SSD_WIDTH + 2 * SSD_GROUPS * SSD_STATE
CHUNK = 128
MEM_TOKENS = 256
X_HEADS = 4
X_HEAD_DIM = D_MODEL // X_HEADS
ROPE_BASE = 10000.0
EPS = 1e-6

Q_OFF = 0
K_OFF = Q_OFF + RET_HEADS * RET_DK
V_OFF = K_OFF + RET_HEADS * RET_DK
G_OFF = V_OFF + RET_WIDTH
Z_OFF = G_OFF + RET_WIDTH
XBC_OFF = Z_OFF + SSD_WIDTH
DT_OFF = XBC_OFF + CONV_CH
IN_PROJ_WIDTH = DT_OFF + SSD_HEADS

PQ_OFF = 0
PV_OFF = PQ_OFF + RET_HEADS * RET_DK
PG_OFF = PV_OFF + RET_WIDTH
PZ_OFF = PG_OFF + RET_WIDTH
P_WIDTH = PZ_OFF + SSD_WIDTH

V7X_LANES = 128
V7X_SUBLANES = 8
V7X_BF16_SUBLANES = 16
V7X_VMEM_LIMIT_BYTES = 56 * 1024 * 1024
DT_PAD = V7X_LANES
IN_PROJ_PAD = DT_OFF + DT_PAD
CONV_HIST = V7X_SUBLANES
CONV_SLABS = CONV_CH // V7X_LANES


def _params(sem):
    return pltpu.CompilerParams(dimension_semantics=sem, vmem_limit_bytes=V7X_VMEM_LIMIT_BYTES)


def _resident(shape):
    zeros = (0,) * len(shape)
    return pl.BlockSpec(shape, lambda *_: zeros, pipeline_mode=pl.Buffered(1))


def _rmsnorm(x, gain):
    ms = jnp.mean(x * x, axis=-1, keepdims=True)
    return x * lax.rsqrt(ms + EPS) * gain


def _dot(a, b):
    return jnp.dot(a, b, preferred_element_type=F32)


def _dot_nt(a, b):
    return lax.dot_general(a, b, (((1,), (1,)), ((), ())), preferred_element_type=F32)


def _dot_tn(a, b):
    return lax.dot_general(a, b, (((0,), (0,)), ((), ())), preferred_element_type=F32)


def _softplus(x):
    return jnp.maximum(x, 0.0) + jnp.log1p(jnp.exp(-jnp.abs(x)))


def _rotary(x, cos2, sin2):
    return x * cos2 + pltpu.roll(x, RET_DK // 2, axis=1) * sin2


def _group_norm(y):
    mu = jnp.mean(y, axis=-1, keepdims=True)
    d = y - mu
    var = jnp.mean(d * d, axis=-1, keepdims=True)
    return d * lax.rsqrt(var + EPS)


def _sample_attention(q_ref, k_ref, v_ref, o_ref):
    slabs = MEM_TOKENS * X_HEADS // V7X_SUBLANES
    for b in range(q_ref.shape[0]):
        k3 = k_ref[b].reshape(slabs, V7X_SUBLANES, X_HEAD_DIM)
        v3 = v_ref[b].reshape(slabs, V7X_SUBLANES, X_HEAD_DIM)
        q4 = q_ref[b]
        q8 = jnp.concatenate([q4] * (V7X_SUBLANES // X_HEADS), axis=0)
        s = jnp.sum(k3 * q8[None], axis=-1, keepdims=True) * (X_HEAD_DIM ** -0.5)
        m8 = jnp.max(s, axis=0)
        m4 = jnp.maximum(m8[0:X_HEADS], m8[X_HEADS:])
        p = jnp.exp(s - jnp.concatenate([m4, m4], axis=0)[None])
        acc = jnp.sum(p * v3, axis=0)
        den = jnp.sum(p, axis=0)
        o_ref[b] = (acc[0:X_HEADS] + acc[X_HEADS:]) / (den[0:X_HEADS] + den[X_HEADS:])


def _ffn_kernel(*refs, final_norm, with_attention, with_mixers, n_cast):
    refs = list(refs)
    x_ref, g_ref, w1_ref, w3_ref, w2_ref = refs[:5]
    del refs[:5]
    fg_ref = refs.pop(0) if final_norm else None
    attn_in = [refs.pop(0) for _ in range(3)] if with_attention else None
    mixer_in = [refs.pop(0) for _ in range(N_MIXER_IN)] if with_mixers else None
    cast_in = [refs.pop(0) for _ in range(n_cast)]
    o_ref = refs.pop(0)
    x = x_ref[...]
    xn = _rmsnorm(x, g_ref[...]).astype(BF16)
    a = _dot(xn, w1_ref[...])
    b = _dot(xn, w3_ref[...])
    hidden = (jax.nn.silu(a) * b).astype(BF16)
    out = x + 0.5 * _dot(hidden, w2_ref[...])
    if final_norm:
        out = _rmsnorm(out, fg_ref[...])
    o_ref[...] = out
    if with_attention:
        _sample_attention(*attn_in, refs.pop(0))
    if with_mixers:
        _sample_mixers(*mixer_in, *refs[:N_MIXER_OUT])
        del refs[:N_MIXER_OUT]
    _cast_blocks(cast_in, refs)


def _ffn(x, gain, w1, w3, w2, final_gain=None, attention=None, mixers=None, casts=(), *, tm):
    t, d = x.shape
    steps = t // tm
    row = pl.BlockSpec((tm, d), lambda i: (i, 0))
    ins = [x, gain, w1, w3, w2]
    specs = [row, _resident(gain.shape), _resident(w1.shape), _resident(w3.shape), _resident(w2.shape)]
    out_shape, out_specs = [jax.ShapeDtypeStruct((t, d), F32)], [row]
    if final_gain is not None:
        ins.append(final_gain)
        specs.append(_resident(final_gain.shape))
    if attention is not None:
        q, cache_k, cache_v = attention
        bs = q.shape[0] // steps
        row_s = pl.BlockSpec((bs, X_HEADS, X_HEAD_DIM), lambda i: (i, 0, 0))
        mem_s = pl.BlockSpec((bs, MEM_TOKENS, X_HEADS, X_HEAD_DIM), lambda i: (i, 0, 0, 0))
        ins += [q, cache_k, cache_v]
        specs += [row_s, mem_s, mem_s]
        out_shape.append(jax.ShapeDtypeStruct(q.shape, F32))
        out_specs.append(row_s)
    if mixers is not None:
        m_ins, m_specs, m_shape, m_out_specs = _sample_mixer_specs(*mixers, steps)
        ins += m_ins
        specs += m_specs
        out_shape += m_shape
        out_specs += m_out_specs
    c_in, c_shape, c_out = _cast_specs(casts, steps)
    ins += list(casts)
    specs += c_in
    out_shape += c_shape
    out_specs += c_out
    outs = pl.pallas_call(
        functools.partial(_ffn_kernel, final_norm=final_gain is not None, with_attention=attention is not None,
                          with_mixers=mixers is not None, n_cast=len(casts)),
        out_shape=out_shape,
        grid=(steps,),
        in_specs=specs,
        out_specs=out_specs,
        compiler_params=_params(("arbitrary",)),
        name="ffn",
    )(*ins)
    return outs if len(outs) > 1 else outs[0]


def _linear_kernel(*refs, has_norm, has_res, n_w):
    refs = list(refs)
    x_ref = refs.pop(0)
    g_ref = refs.pop(0) if has_norm else None
    r_ref = refs.pop(0) if has_res else None
    w_refs, o_refs = refs[:n_w], refs[n_w:]
    x = x_ref[...]
    if has_norm:
        x = _rmsnorm(x, g_ref[...])
    xb = x.astype(BF16)
    for w_ref, o_ref in zip(w_refs, o_refs, strict=True):
        y = _dot(xb, w_ref[...])
        if has_res:
            y = r_ref[...] + y
        o_ref[...] = y


def _linear(x, weights, gain=None, res=None, *, tm):
    t, k = x.shape
    ins, specs = [x], [pl.BlockSpec((tm, k), lambda i: (i, 0))]
    if gain is not None:
        ins.append(gain)
        specs.append(_resident(gain.shape))
    if res is not None:
        ins.append(res)
        specs.append(pl.BlockSpec((tm, res.shape[1]), lambda i: (i, 0)))
    for w in weights:
        ins.append(w)
        specs.append(_resident(w.shape))
    outs = pl.pallas_call(
        functools.partial(_linear_kernel, has_norm=gain is not None, has_res=res is not None, n_w=len(weights)),
        out_shape=[jax.ShapeDtypeStruct((t, w.shape[1]), F32) for w in weights],
        grid=(t // tm,),
        in_specs=specs,
        out_specs=[pl.BlockSpec((tm, w.shape[1]), lambda i: (i, 0)) for w in weights],
        compiler_params=_params(("arbitrary",)),
        name="linear",
    )(*ins)
    return outs


def _cast_specs(arrays, steps):
    in_specs, out_shape, out_specs = [], [], []
    for a in arrays:
        rows = pl.cdiv(pl.cdiv(a.shape[0], steps), V7X_BF16_SUBLANES) * V7X_BF16_SUBLANES
        last = pl.cdiv(a.shape[0], rows) - 1
        blk = pl.BlockSpec((rows, a.shape[1]), lambda i, last=last: (jnp.minimum(i, last), 0))
        in_specs.append(blk)
        out_specs.append(blk)
        out_shape.append(jax.ShapeDtypeStruct(a.shape, BF16))
    return in_specs, out_shape, out_specs


def _cast_blocks(in_refs, out_refs):
    for i_ref, o_ref in zip(in_refs, out_refs, strict=True):
        o_ref[...] = i_ref[...].astype(BF16)


def _memkv_kernel(*refs, n_cast):
    x_ref, g_ref, wk_ref, wv_ref = refs[:4]
    cast_in = refs[4:4 + n_cast]
    k_ref, v_ref, k4_ref, v4_ref = refs[4 + n_cast:8 + n_cast]
    xb = _rmsnorm(x_ref[...], g_ref[...]).astype(BF16)
    for w_ref, o_ref, o4_ref in ((wk_ref, k_ref, k4_ref), (wv_ref, v_ref, v4_ref)):
        y = _dot(xb, w_ref[...])
        o_ref[0] = y
        o4_ref[0] = y.reshape(MEM_TOKENS, X_HEADS, X_HEAD_DIM)
    _cast_blocks(cast_in, refs[8 + n_cast:])


def _memkv(mem, gain, w_k, w_v, casts=()):
    batch, m, d = mem.shape
    flat = jax.ShapeDtypeStruct((batch, m, d), F32)
    split = jax.ShapeDtypeStruct((batch, m, X_HEADS, X_HEAD_DIM), F32)
    flat_blk = pl.BlockSpec((1, m, d), lambda b: (b, 0, 0))
    split_blk = pl.BlockSpec((1, m, X_HEADS, X_HEAD_DIM), lambda b: (b, 0, 0, 0))
    c_in, c_shape, c_out = _cast_specs(casts, batch)
    return pl.pallas_call(
        functools.partial(_memkv_kernel, n_cast=len(casts)),
        out_shape=[flat, flat, split, split] + c_shape,
        grid=(batch,),
        in_specs=[pl.BlockSpec((m, d), lambda b: (b, 0)), _resident(gain.shape), _resident(w_k.shape),
                  _resident(w_v.shape)] + c_in,
        out_specs=[flat_blk, flat_blk, split_blk, split_blk] + c_out,
        compiler_params=_params(("arbitrary",)),
        name="memkv",
    )(mem.reshape(batch * m, d), gain, w_k, w_v, *casts)


def _cumsum_chunks(x):
    pos = lax.broadcasted_iota(jnp.int32, x.shape, 0) % CHUNK
    step = 1
    while step < CHUNK:
        x = x + jnp.where(pos >= step, pltpu.roll(x, step, axis=0), 0.0)
        step *= 2
    return x


def _mixp_project(tile, h_ref, g_ref, wa_ref, wkt_ref, wxbc_ref, wdt_ref, cos_ref, sin_ref, cost_ref, sint_ref,
                  qdec_ref, kdec_ref, cw_ref, cb_ref, dtb_ref, alog_ref, conv_ref, xbc_s, stage, *, tm, nt):
    q_s, qd_s, kr_s, kd_s, v_s, g_s, z_s, xc_s, pk_s = stage
    row_start = (tile % nt) == 0

    u = _rmsnorm(h_ref[...], g_ref[...]).astype(BF16)
    qvgz = _dot(u, wa_ref[...])
    kt = _dot_nt(wkt_ref[...], u)
    xbc = _dot(u, wxbc_ref[...])
    dt_raw = _dot(u, wdt_ref[...])

    for sl in range(CONV_SLABS):
        lanes = slice(sl * V7X_LANES, (sl + 1) * V7X_LANES)
        xbc_s[sl, 0:CONV_HIST, :] = jnp.where(row_start, 0.0, xbc_s[sl, 0:CONV_HIST, :])
        xbc_s[sl, CONV_HIST:CONV_HIST + tm, :] = xbc[:, lanes]
        conv = cb_ref[:, lanes]
        for i in range(CONV_WIDTH):
            off = CONV_HIST - (CONV_WIDTH - 1) + i
            conv = conv + cw_ref[i:i + 1, lanes] * xbc_s[sl, off:off + tm, :]
        xc_s[sl] = jax.nn.silu(conv)
        conv_ref[0, :, lanes] = xbc_s[sl, CONV_HIST + tm - (CONV_WIDTH - 1):CONV_HIST + tm, :]
        xbc_s[sl, 0:CONV_HIST, :] = xbc_s[sl, tm:tm + CONV_HIST, :]

    dt = _softplus(dt_raw + dtb_ref[...])
    a_cum = _cumsum_chunks(dt * (-jnp.exp(alog_ref[...])))
    head_lanes = lax.broadcasted_iota(jnp.int32, (tm, DT_PAD), 1) < SSD_HEADS
    pk_s[...] = jnp.where(head_lanes, a_cum, dt)

    cos2, sin2 = cos_ref[...], sin_ref[...]
    cos_t, sin_t = cost_ref[...], sint_ref[...]
    half = RET_DK // 2
    for hh in range(RET_HEADS):
        lanes = slice(hh * RET_DK, (hh + 1) * RET_DK)
        qr = _rotary(qvgz[:, PQ_OFF + hh * RET_DK:PQ_OFF + (hh + 1) * RET_DK], cos2, sin2)
        q_s[:, lanes] = qr.astype(BF16)
        qd_s[:, lanes] = (qr * qdec_ref[hh]).astype(BF16)
        k1 = kt[hh * RET_DK:hh * RET_DK + half, :]
        k2 = kt[hh * RET_DK + half:(hh + 1) * RET_DK, :]
        kr = jnp.concatenate([k1 * cos_t - k2 * sin_t, k1 * sin_t + k2 * cos_t], axis=0) * (RET_DK ** -0.5)
        kr_s[lanes, :] = kr.astype(BF16)
        kd_s[lanes, :] = (kr * kdec_ref[hh:hh + 1, :]).astype(BF16)
    v_s[...] = qvgz[:, PV_OFF:PV_OFF + RET_WIDTH].astype(BF16)
    g_s[...] = jax.nn.silu(qvgz[:, PG_OFF:PG_OFF + RET_WIDTH])
    z_s[...] = jax.nn.silu(qvgz[:, PZ_OFF:PZ_OFF + SSD_WIDTH])


def _mixp_heads(tile, h_ref, dmat_ref, cdec_ref, gn_ref, dsk_ref, sn_ref, wout_ref, o_ref, ret_ref, ssm_ref,
                mix_s, sret_s, sssm_s, stage, *, tm, nt):
    q_s, qd_s, kr_s, kd_s, v_s, g_s, z_s, xc_s, pk_s = stage
    nc = tm // CHUNK
    row_start = (tile % nt) == 0

    row_i = lax.broadcasted_iota(jnp.int32, (CHUNK, CHUNK), 0)
    col_i = lax.broadcasted_iota(jnp.int32, (CHUNK, CHUNK), 1)
    causal = row_i >= col_i
    low_lanes = col_i < SSD_HEAD_DIM
    bd_rows = lax.broadcasted_iota(jnp.int32, (2 * SSD_STATE, 2 * SSD_HEAD_DIM), 0)
    bd_cols = lax.broadcasted_iota(jnp.int32, (2 * SSD_STATE, 2 * SSD_HEAD_DIM), 1)
    bd_mask = (bd_rows < SSD_STATE) == (bd_cols < SSD_HEAD_DIM)

    inner, kv = {}, {}
    scores, c_exp, st_inc, st_dec = {}, {}, {}, {}
    for c in range(nc):
        rows = slice(c * CHUNK, (c + 1) * CHUNK)
        for hh in range(RET_HEADS):
            lanes = slice(hh * RET_DK, (hh + 1) * RET_DK)
            inner[c, hh] = (_dot(q_s[rows, lanes], kr_s[lanes, rows]) * dmat_ref[hh]).astype(BF16)
            kv[c, hh] = _dot(kd_s[lanes, rows], v_s[rows, lanes])

        pk = pk_s[rows, :]
        pk_t = pk.T
        a_t = pk_t[0:SSD_HEADS, :]
        dt_t = pk_t[SSD_HEADS:2 * SSD_HEADS, :]
        a_last = a_t[:, CHUNK - 1:CHUNK]
        w_t = jnp.exp(a_last - a_t) * dt_t
        chunk_dec = jnp.exp(a_last)
        for grp in range(SSD_GROUPS):
            cg = xc_s[SSD_PAIRS + SSD_GROUPS + grp, rows, :]
            b_t = xc_s[SSD_PAIRS + grp, rows, :].T
            gmat = _dot(cg.astype(BF16), b_t.astype(BF16))
            for pj in range(SSD_PAIRS // SSD_GROUPS):
                j = grp * (SSD_PAIRS // SSD_GROUPS) + pj
                heads = (2 * j, 2 * j + 1)
                x_pair = xc_s[j, rows, :].astype(BF16)
                b_w = jnp.concatenate([b_t * w_t[hd:hd + 1, :] for hd in heads], axis=0).astype(BF16)
                st_inc[c, j] = jnp.where(bd_mask, _dot(b_w, x_pair), 0.0)
                st_dec[c, j] = jnp.concatenate(
                    [jnp.broadcast_to(chunk_dec[hd:hd + 1, :], (SSD_STATE, 2 * SSD_HEAD_DIM)) for hd in heads], axis=0)
                for hd in heads:
                    a_col = jnp.broadcast_to(pk[:, hd:hd + 1], (CHUNK, CHUNK))
                    seg = a_col - a_t[hd:hd + 1, :]
                    lmat = jnp.where(causal, jnp.exp(jnp.where(causal, seg, 0.0)), 0.0)
                    scores[c, hd] = (gmat * lmat * dt_t[hd:hd + 1, :]).astype(BF16)
                    c_exp[c, hd] = (cg * jnp.exp(a_col)).astype(BF16)

    s_in, st_in = {}, {}
    for hh in range(RET_HEADS):
        s = jnp.where(row_start, 0.0, sret_s[hh])
        for c in range(nc):
            s_in[c, hh] = s.astype(BF16)
            s = cdec_ref[hh:hh + 1, :] * s + kv[c, hh]
        sret_s[hh] = s
        ret_ref[0, hh] = s
    for j in range(SSD_PAIRS):
        s = jnp.where(row_start, 0.0, sssm_s[j])
        for c in range(nc):
            st_in[c, j] = s.astype(BF16)
            s = st_dec[c, j] * s + st_inc[c, j]
        sssm_s[j] = s
        ssm_ref[0, j * 2 * SSD_HEAD_DIM:(j + 1) * 2 * SSD_HEAD_DIM, :] = (s[0:SSD_STATE] + s[SSD_STATE:]).T

    for c in range(nc):
        rows = slice(c * CHUNK, (c + 1) * CHUNK)
        for hh in range(RET_HEADS):
            lanes = slice(hh * RET_DV, (hh + 1) * RET_DV)
            lhs = jnp.concatenate([inner[c, hh], qd_s[rows, lanes]], axis=1)
            rhs = jnp.concatenate([v_s[rows, lanes], s_in[c, hh]], axis=0)
            r = _group_norm(_dot(lhs, rhs)) * gn_ref[:, lanes]
            mix_s[rows, lanes] = (g_s[rows, lanes] * r).astype(BF16)
        ys = []
        for j in range(SSD_PAIRS):
            heads = (2 * j, 2 * j + 1)
            x_pair = xc_s[j, rows, :]
            x_bd = jnp.concatenate([jnp.where(low_lanes, x_pair, 0.0), jnp.where(low_lanes, 0.0, x_pair)], axis=0)
            lhs = jnp.concatenate([scores[c, heads[0]], scores[c, heads[1]], c_exp[c, heads[0]], c_exp[c, heads[1]]],
                                  axis=1)
            rhs = jnp.concatenate([x_bd.astype(BF16), st_in[c, j]], axis=0)
            ys.append(_dot(lhs, rhs) + dsk_ref[:, j * 2 * SSD_HEAD_DIM:(j + 1) * 2 * SSD_HEAD_DIM] * x_pair)
        y_all = jnp.concatenate(ys, axis=1) * z_s[rows, :]
        mix_s[rows, RET_WIDTH:RET_WIDTH + SSD_WIDTH] = _rmsnorm(y_all, sn_ref[...]).astype(BF16)

    o_ref[...] = h_ref[...] + _dot(mix_s[...], wout_ref[...])


def _mix_prompt_kernel(h_ref, hp_ref, g_ref, wa_ref, wkt_ref, wxbc_ref, wdt_ref, cos_ref, sin_ref, cost_ref, sint_ref,
                       dmat_ref, qdec_ref, kdec_ref, cdec_ref, gn_ref, cw_ref, cb_ref, dtb_ref, alog_ref, dsk_ref,
                       sn_ref, wout_ref,
                       o_ref, ret_ref, ssm_ref, conv_ref,
                       xbc_s, mix_s, sret_s, sssm_s, *stage_s, tm, nt, n_tiles):
    s = pl.program_id(0)
    sets = (tuple(r.at[0] for r in stage_s), tuple(r.at[1] for r in stage_s))

    @pl.when(s == 0)
    def _():
        for r in stage_s:
            r[...] = jnp.zeros_like(r)
        sret_s[...] = jnp.zeros_like(sret_s)
        sssm_s[...] = jnp.zeros_like(sssm_s)
        xbc_s[...] = jnp.zeros_like(xbc_s)

    def step(write_set, read_set):
        _mixp_project(jnp.minimum(s, n_tiles - 1), h_ref, g_ref, wa_ref, wkt_ref, wxbc_ref, wdt_ref, cos_ref, sin_ref,
                      cost_ref, sint_ref, qdec_ref, kdec_ref, cw_ref, cb_ref, dtb_ref, alog_ref, conv_ref, xbc_s,
                      write_set, tm=tm, nt=nt)
        _mixp_heads(jnp.maximum(s - 1, 0), hp_ref, dmat_ref, cdec_ref, gn_ref, dsk_ref, sn_ref, wout_ref, o_ref, ret_ref,
                    ssm_ref, mix_s, sret_s, sssm_s, read_set, tm=tm, nt=nt)

    @pl.when(s % 2 == 0)
    def _():
        step(sets[0], sets[1])

    @pl.when(s % 2 == 1)
    def _():
        step(sets[1], sets[0])


def _mix_prompt(h, gain, w_a, w_kt, w_xbc, w_dt, cos2, sin2, cos_t, sin_t, dmat, qdec, kdec, cdec, gn, conv_w, conv_b,
                dt_bias, a_log, dskip, ssd_gain, w_out, *, batch, seq, tm):
    nt = seq // tm
    n_tiles = batch * nt
    d = h.shape[1]
    cur = lambda s: jnp.minimum(s, n_tiles - 1)
    prev = lambda s: jnp.maximum(s - 1, 0)
    row_cur = pl.BlockSpec((tm, d), lambda s: (cur(s), 0))
    row_prev = pl.BlockSpec((tm, d), lambda s: (prev(s), 0))
    pos = pl.BlockSpec((tm, RET_DK), lambda s: (cur(s) % nt, 0))
    pos_t = pl.BlockSpec((RET_DK // 2, tm), lambda s: (0, cur(s) % nt))
    consts = [gain, w_a, w_kt, w_xbc, w_dt]
    tail = [dmat, qdec, kdec, cdec, gn, conv_w, conv_b, dt_bias, a_log, dskip, ssd_gain, w_out]
    stage = [
        pltpu.VMEM((2, tm, RET_HEADS * RET_DK), BF16),
        pltpu.VMEM((2, tm, RET_HEADS * RET_DK), BF16),
        pltpu.VMEM((2, RET_HEADS * RET_DK, tm), BF16),
        pltpu.VMEM((2, RET_HEADS * RET_DK, tm), BF16),
        pltpu.VMEM((2, tm, RET_WIDTH), BF16),
        pltpu.VMEM((2, tm, RET_WIDTH), F32),
        pltpu.VMEM((2, tm, SSD_WIDTH), F32),
        pltpu.VMEM((2, CONV_SLABS, tm, V7X_LANES), F32),
        pltpu.VMEM((2, tm, DT_PAD), F32),
    ]
    return pl.pallas_call(
        functools.partial(_mix_prompt_kernel, tm=tm, nt=nt, n_tiles=n_tiles),
        out_shape=[
            jax.ShapeDtypeStruct(h.shape, F32),
            jax.ShapeDtypeStruct((batch, RET_HEADS, RET_DK, RET_DV), F32),
            jax.ShapeDtypeStruct((batch, SSD_WIDTH, SSD_STATE), F32),
            jax.ShapeDtypeStruct((batch, CONV_WIDTH - 1, CONV_CH), F32),
        ],
        grid=(n_tiles + 1,),
        in_specs=([row_cur, row_prev] + [_resident(a.shape) for a in consts] + [pos, pos, pos_t, pos_t]
                  + [_resident(a.shape) for a in tail]),
        out_specs=[
            row_prev,
            pl.BlockSpec((1, RET_HEADS, RET_DK, RET_DV), lambda s: (prev(s) // nt, 0, 0, 0)),
            pl.BlockSpec((1, SSD_WIDTH, SSD_STATE), lambda s: (prev(s) // nt, 0, 0)),
            pl.BlockSpec((1, CONV_WIDTH - 1, CONV_CH), lambda s: (cur(s) // nt, 0, 0)),
        ],
        scratch_shapes=[
            pltpu.VMEM((CONV_SLABS, CONV_HIST + tm, V7X_LANES), F32),
            pltpu.VMEM((tm, RET_WIDTH + SSD_WIDTH), BF16),
            pltpu.VMEM((RET_HEADS, RET_DK, RET_DV), F32),
            pltpu.VMEM((SSD_PAIRS, 2 * SSD_STATE, 2 * SSD_HEAD_DIM), F32),
        ] + stage,
        compiler_params=_params(("arbitrary",)),
        name="mix_prompt",
    )(h, h, *consts, cos2, sin2, cos_t, sin_t, *tail)


def _split_hi_lo(x):
    hi = x.astype(BF16).astype(F32)
    return hi, x - hi


def _sample_mixers(proj_ref, sret_ref, sssm_ref, sconv_ref, cos_ref, sin_ref, gam_ref,
                   gn_ref, cw_ref, cb_ref, dtb_ref, alog_ref, dsk_ref, sn_ref,
                   mix_ref, oret_ref, ossm_ref, oconv_ref):
    bs = proj_ref.shape[1]
    proj = proj_ref[0]
    xbc = proj[:, XBC_OFF:DT_OFF]
    hist = sconv_ref[0]
    taps = [hist[:, i * CONV_CH:(i + 1) * CONV_CH] for i in range(CONV_WIDTH - 1)] + [xbc]
    conv = cb_ref[...]
    for i in range(CONV_WIDTH):
        conv = conv + cw_ref[i:i + 1, :] * taps[i]
    oconv_ref[0] = jnp.concatenate(taps[1:], axis=1)
    xc = jax.nn.silu(conv)

    dt = _softplus(proj[:, DT_OFF:IN_PROJ_PAD] + dtb_ref[...])

    sub_i = lax.broadcasted_iota(jnp.int32, (bs, V7X_LANES), 0)
    row_of = lax.broadcasted_iota(jnp.int32, (4 * bs, V7X_LANES), 0) % bs
    cos2 = cos_ref[...]
    sin2 = sin_ref[...]

    def outer_lhs(x):
        hi, lo = _split_hi_lo(x)
        return jnp.concatenate([hi, hi, lo, lo], axis=0)

    def outer_rhs(x):
        hi, lo = _split_hi_lo(x)
        return jnp.concatenate([hi, lo, hi, lo], axis=0).astype(BF16)

    def only_sample(x4, b):
        return jnp.where(row_of == b, x4, 0.0).astype(BF16)

    for hh in range(RET_HEADS):
        lanes = slice(hh * RET_DK, (hh + 1) * RET_DK)
        qr = _rotary(proj[:, Q_OFF + hh * RET_DK:Q_OFF + (hh + 1) * RET_DK], cos2, sin2)
        kr = _rotary(proj[:, K_OFF + hh * RET_DK:K_OFF + (hh + 1) * RET_DK], cos2, sin2) * (RET_DK ** -0.5)
        vh = proj[:, V_OFF + hh * RET_DV:V_OFF + (hh + 1) * RET_DV]
        k4, v4, q_bf = outer_lhs(kr), outer_rhs(vh), qr.astype(BF16)
        gamma = gam_ref[hh:hh + 1, :]
        y = jnp.zeros((bs, RET_DV), F32)
        for b in range(bs):
            s_old = sret_ref[b, hh]
            oret_ref[b, hh] = gamma * s_old + _dot_tn(only_sample(k4, b), v4)
            y = jnp.where(sub_i == b, _dot(q_bf, s_old.astype(BF16)), y)
        y = gamma * y + jnp.sum(qr * kr, axis=-1, keepdims=True) * vh
        r = _group_norm(y) * gn_ref[:, lanes]
        mix_ref[0, :, lanes] = jax.nn.silu(proj[:, G_OFF + hh * RET_DV:G_OFF + (hh + 1) * RET_DV]) * r

    xs = xc[:, 0:SSD_WIDTH]
    head_of_lane = lax.broadcasted_iota(jnp.int32, (bs, SSD_WIDTH), 1) // SSD_HEAD_DIM
    dec = jnp.exp(dt * (-jnp.exp(alog_ref[...])))
    dt_wide = jnp.zeros((bs, SSD_WIDTH), F32)
    dec_wide = jnp.zeros((bs, SSD_WIDTH), F32)
    for hd in range(SSD_HEADS):
        dt_wide = jnp.where(head_of_lane == hd, dt[:, hd:hd + 1], dt_wide)
        dec_wide = jnp.where(head_of_lane == hd, dec[:, hd:hd + 1], dec_wide)
    xdt = xs * dt_wide
    x4 = outer_lhs(xdt)
    ys = []
    for j in range(SSD_PAIRS):
        grp = j // (SSD_PAIRS // SSD_GROUPS)
        lanes = slice(j * 2 * SSD_HEAD_DIM, (j + 1) * 2 * SSD_HEAD_DIM)
        bg = xc[:, SSD_WIDTH + grp * SSD_STATE:SSD_WIDTH + (grp + 1) * SSD_STATE]
        c_off = SSD_WIDTH + SSD_GROUPS * SSD_STATE + grp * SSD_STATE
        cg = xc[:, c_off:c_off + SSD_STATE]
        b4, c_bf = outer_rhs(bg), cg.astype(BF16)
        y = jnp.zeros((bs, 2 * SSD_HEAD_DIM), F32)
        for b in range(bs):
            s_old = sssm_ref[b, lanes, :]
            dec_rows = jnp.concatenate(
                [jnp.broadcast_to(dec[b:b + 1, hd:hd + 1], (SSD_HEAD_DIM, SSD_STATE)) for hd in (2 * j, 2 * j + 1)],
                axis=0)
            ossm_ref[b, lanes, :] = dec_rows * s_old + _dot_tn(only_sample(x4[:, lanes], b), b4)
            y = jnp.where(sub_i == b, _dot_nt(c_bf, s_old.astype(BF16)), y)
        ys.append(dec_wide[:, lanes] * y + jnp.sum(cg * bg, axis=-1, keepdims=True) * xdt[:, lanes])
    ys = (jnp.concatenate(ys, axis=1) + dsk_ref[...] * xs) * jax.nn.silu(proj[:, Z_OFF:Z_OFF + SSD_WIDTH])
    mix_ref[0, :, RET_WIDTH:RET_WIDTH + SSD_WIDTH] = _rmsnorm(ys, sn_ref[...])


N_MIXER_IN = 14
N_MIXER_OUT = 4


def _sample_mixer_specs(proj, s_ret, s_ssm, s_conv, consts, steps):
    n = proj.shape[0]
    bs = n // steps
    grouped = lambda a: a.reshape(steps, bs, a.shape[1])
    blk3 = lambda w: pl.BlockSpec((1, bs, w), lambda i: (i, 0, 0))
    ret_blk = pl.BlockSpec((bs, RET_HEADS, RET_DK, RET_DV), lambda i: (i, 0, 0, 0))
    ssm_blk = pl.BlockSpec((bs, SSD_WIDTH, SSD_STATE), lambda i: (i, 0, 0))
    ins = [grouped(proj), s_ret, s_ssm, grouped(s_conv), *consts]
    in_specs = [blk3(proj.shape[1]), ret_blk, ssm_blk, blk3(s_conv.shape[1])] + [_resident(a.shape) for a in consts]
    out_shape = [jax.ShapeDtypeStruct((steps, bs, RET_WIDTH + SSD_WIDTH), F32), jax.ShapeDtypeStruct(s_ret.shape, F32),
                 jax.ShapeDtypeStruct(s_ssm.shape, F32), jax.ShapeDtypeStruct((steps, bs, s_conv.shape[1]), F32)]
    out_specs = [blk3(RET_WIDTH + SSD_WIDTH), ret_blk, ssm_blk, blk3(s_conv.shape[1])]
    return ins, in_specs, out_shape, out_specs


def _softmax_rows(s):
    m = jnp.max(s, axis=-1, keepdims=True)
    p = jnp.exp(s - m)
    return p / jnp.sum(p, axis=-1, keepdims=True)


def _xattn_prompt_kernel(h_ref, g_ref, wq_ref, mk_ref, mv_ref, wo_ref, o_ref, att_s):
    h = h_ref[...]
    c = _rmsnorm(h, g_ref[...]).astype(BF16)
    qx = _dot(c, wq_ref[...]).astype(BF16)
    for hh in range(X_HEADS):
        lanes = slice(hh * X_HEAD_DIM, (hh + 1) * X_HEAD_DIM)
        s = _dot_nt(qx[:, lanes], mk_ref[0, :, lanes].astype(BF16)) * (X_HEAD_DIM ** -0.5)
        att = _softmax_rows(s).astype(BF16)
        att_s[:, lanes] = _dot(att, mv_ref[0, :, lanes].astype(BF16)).astype(BF16)
    o_ref[...] = h + _dot(att_s[...], wo_ref[...])


def _xattn_prompt(h, gain, w_q, mem_k, mem_v, w_o, *, batch, seq, tm):
    nt = seq // tm
    d = h.shape[1]
    row = pl.BlockSpec((tm, d), lambda b, t: (b * nt + t, 0))
    mem = pl.BlockSpec((1, MEM_TOKENS, d), lambda b, t: (b, 0, 0))
    return pl.pallas_call(
        _xattn_prompt_kernel,
        out_shape=jax.ShapeDtypeStruct(h.shape, F32),
        grid=(batch, nt),
        in_specs=[row, _resident(gain.shape), _resident(w_q.shape), mem, mem, _resident(w_o.shape)],
        out_specs=row,
        scratch_shapes=[pltpu.VMEM((tm, d), BF16)],
        compiler_params=_params(("arbitrary", "arbitrary")),
        name="xattn_prompt",
    )(h, gain, w_q, mem_k, mem_v, w_o)


def _rope_angles(pos):
    half = RET_DK // 2
    inv_freq = ROPE_BASE ** (-jnp.arange(half, dtype=F32) / half)
    ang = pos.astype(F32)[:, None] * inv_freq[None, :]
    return jnp.cos(ang), jnp.sin(ang)


def _rope_tables(pos):
    cos, sin = _rope_angles(pos)
    return jnp.concatenate([cos, cos], axis=-1), jnp.concatenate([-sin, sin], axis=-1)


def _retention_decay_tables(chunk):
    log_g = jnp.log1p(-jnp.exp2(-5.0 - jnp.arange(RET_HEADS, dtype=F32)))
    idx = jnp.arange(chunk, dtype=F32)
    diff = idx[:, None] - idx[None, :]
    causal = diff >= 0
    dmat = jnp.where(causal[None], jnp.exp(log_g[:, None, None] * jnp.where(causal, diff, 0.0)[None]), 0.0)
    q_dec = jnp.exp(log_g[:, None] * (idx[None, :] + 1.0))
    k_dec = jnp.exp(log_g[:, None] * (chunk - 1.0 - idx[None, :]))
    c_dec = jnp.exp(log_g * chunk)
    wide = lambda x: jnp.broadcast_to(x[..., None], x.shape + (V7X_LANES,))
    return dmat, wide(q_dec), k_dec, wide(c_dec)


def kernel(x_prompt, x_sample, mem_prompt, state_ret, state_ssm, state_conv, cache_mem_k, cache_mem_v, ffn1_norm,
           ffn1_w1, ffn1_w3, ffn1_w2, mix_norm, w_in, ret_gn_gain, conv_w, conv_b, dt_bias, A_log, D_skip, ssd_norm,
           w_out, x_norm, mem_norm, w_xq, w_xk, w_xv, w_xo, ffn2_norm, ffn2_w1, ffn2_w3, ffn2_w2, final_norm):
    bp, lp, d = x_prompt.shape
    bsz = x_sample.shape[0]
    depth = ffn1_w1.shape[0]
    row = lambda v: v.reshape(1, -1).astype(F32)
    lane_pad = lambda v: jnp.pad(row(v), ((0, 0), (0, DT_PAD - v.shape[-1])))

    cos_p, sin_p = _rope_tables(jnp.arange(lp))
    cos_pt, sin_pt = (a.T for a in _rope_angles(jnp.arange(lp)))
    cos_s, sin_s = _rope_tables(PAST_LEN + jnp.arange(x_sample.shape[1]))
    dmat, q_dec, k_dec, c_dec = _retention_decay_tables(CHUNK)
    gamma1 = _retention_decay_tables(1)[3]

    y_p = x_prompt.reshape(bp * lp, d)
    y_s = x_sample.reshape(bsz, d)
    outs = {k: [] for k in ("ret_p", "ssm_p", "conv_p", "memk", "memv", "ret_s", "ssm_s", "conv_s")}
    for l in range(depth):
        bf = lambda w: w[l].astype(BF16)
        w_in_f = w_in[l]
        w_in_l = jnp.pad(w_in_f, ((0, 0), (0, IN_PROJ_PAD - IN_PROJ_WIDTH))).astype(BF16)
        w_a = jnp.concatenate([w_in_f[:, Q_OFF:K_OFF], w_in_f[:, V_OFF:XBC_OFF]], axis=1).astype(BF16)
        w_kt = w_in_f[:, K_OFF:V_OFF].T.astype(BF16)
        w_xbc = w_in_f[:, XBC_OFF:DT_OFF].astype(BF16)
        w_dt = jnp.pad(jnp.tile(w_in_f[:, DT_OFF:], (1, 2)), ((0, 0), (0, DT_PAD - 2 * SSD_HEADS))).astype(BF16)
        shared = (row(ret_gn_gain[l]), conv_w[l], row(conv_b[l]))
        ssd_tail = (lane_pad(A_log[l]), row(jnp.repeat(D_skip[l], SSD_HEAD_DIM)), row(ssd_norm[l]))
        w_out_l, w_xq_l, w_xo_l = bf(w_out), bf(w_xq), bf(w_xo)

        mk, mv, mk4, mv4, f1_w1, f1_w3, f1_w2 = _memkv(
            mem_prompt, row(mem_norm[l]), bf(w_xk), bf(w_xv), casts=(ffn1_w1[l], ffn1_w3[l], ffn1_w2[l]))
        f1 = (row(ffn1_norm[l]), f1_w1, f1_w3, f1_w2)

        y_s = _ffn(y_s, *f1, tm=bsz)
        proj_s, = _linear(y_s, [w_in_l], gain=row(mix_norm[l]), tm=bsz)

        mixer_consts = (cos_s, sin_s, gamma1, *shared, lane_pad(dt_bias[l]), *ssd_tail)
        y_p, mix_s, ret_s, ssm_s, conv_s, f2_w1, f2_w3, f2_w2 = _ffn(
            y_p, *f1, mixers=(proj_s, state_ret[l], state_ssm[l].reshape(bsz, SSD_WIDTH, SSD_STATE),
                              state_conv[l].reshape(bsz, (CONV_WIDTH - 1) * CONV_CH), mixer_consts),
            casts=(ffn2_w1[l], ffn2_w3[l], ffn2_w2[l]), tm=512)
        f2 = (row(ffn2_norm[l]), f2_w1, f2_w3, f2_w2)
        y_s, = _linear(mix_s.reshape(bsz, d), [w_out_l], res=y_s, tm=bsz)
        q_s, = _linear(y_s, [w_xq_l], gain=row(x_norm[l]), tm=bsz)

        tm_p = 512
        y_p, ret_p, ssm_p, conv_p = _mix_prompt(
            y_p, row(mix_norm[l]), w_a, w_kt, w_xbc, w_dt, cos_p, sin_p, cos_pt, sin_pt, dmat,
            jnp.tile(q_dec, (1, tm_p // CHUNK, 1)), jnp.tile(k_dec, (1, tm_p // CHUNK)), c_dec,
            *shared, lane_pad(jnp.tile(dt_bias[l], 2)), *ssd_tail, w_out_l, batch=bp, seq=lp, tm=tm_p)

        y_p = _xattn_prompt(y_p, row(x_norm[l]), w_xq_l, mk, mv, w_xo_l, batch=bp, seq=lp, tm=1024)
        y_p, att_s = _ffn(y_p, *f2, final_gain=row(final_norm) if l == depth - 1 else None,
                          attention=(q_s.reshape(bsz, X_HEADS, X_HEAD_DIM), cache_mem_k[l], cache_mem_v[l]), tm=512)
        y_s, = _linear(att_s.reshape(bsz, d), [w_xo_l], res=y_s, tm=bsz)
        y_s = _ffn(y_s, *f2, final_gain=row(final_norm) if l == depth - 1 else None, tm=bsz)

        outs["ret_p"].append(ret_p)
        outs["ssm_p"].append(ssm_p.reshape(bp, SSD_HEADS, SSD_HEAD_DIM, SSD_STATE))
        outs["conv_p"].append(conv_p)
        outs["memk"].append(mk4)
        outs["memv"].append(mv4)
        outs["ret_s"].append(ret_s)
        outs["ssm_s"].append(ssm_s.reshape(bsz, SSD_HEADS, SSD_HEAD_DIM, SSD_STATE))
        outs["conv_s"].append(conv_s.reshape(bsz, CONV_WIDTH - 1, CONV_CH))

    st = lambda k: jnp.stack(outs[k])
    return (y_p.reshape(bp, lp, d), y_s.reshape(bsz, x_sample.shape[1], d), st("ret_p"), st("ssm_p"), st("conv_p"),
            st("memk"), st("memv"), st("ret_s"), st("ssm_s"), st("conv_s"))
```

```python
import functools

import jax
import jax.numpy as jnp
from jax import lax
from jax.experimental import pallas as pl
from jax.experimental.pallas import tpu as pltpu

F32 = jnp.float32
BF16 = jnp.bfloat16

D_MODEL = 1024
PAST_LEN = 16384
RET_HEADS = 4
RET_DK = 128
RET_DV = 128
RET_WIDTH = RET_HEADS * RET_DV
SSD_HEADS = 8
SSD_HEAD_DIM = 64
SSD_WIDTH = SSD_HEADS * SSD_HEAD_DIM
SSD_GROUPS = 2
SSD_STATE = 128
SSD_PAIRS = SSD_HEADS // 2
CONV_WIDTH = 4
CONV_CH = SSD_WIDTH + 2 * SSD_GROUPS * SSD_STATE
CHUNK = 128
MEM_TOKENS = 256
X_HEADS = 4
X_HEAD_DIM = D_MODEL // X_HEADS
ROPE_BASE = 10000.0
EPS = 1e-6

Q_OFF = 0
K_OFF = Q_OFF + RET_HEADS * RET_DK
V_OFF = K_OFF + RET_HEADS * RET_DK
G_OFF = V_OFF + RET_WIDTH
Z_OFF = G_OFF + RET_WIDTH
XBC_OFF = Z_OFF + SSD_WIDTH
DT_OFF = XBC_OFF + CONV_CH
IN_PROJ_WIDTH = DT_OFF + SSD_HEADS

V7X_LANES = 128
V7X_SUBLANES = 8
V7X_BF16_SUBLANES = 16
V7X_VMEM_LIMIT_BYTES = 56 * 1024 * 1024
DT_PAD = V7X_LANES
CONV_HIST = V7X_SUBLANES
CONV_SLABS = CONV_CH // V7X_LANES


def _params(sem):
    return pltpu.CompilerParams(dimension_semantics=sem, vmem_limit_bytes=V7X_VMEM_LIMIT_BYTES)


def _resident(shape):
    zeros = (0,) * len(shape)
    return pl.BlockSpec(shape, lambda *_: zeros, pipeline_mode=pl.Buffered(1))


def _rmsnorm(x, gain):
    ms = jnp.mean(x * x, axis=-1, keepdims=True)
    return x * lax.rsqrt(ms + EPS) * gain


def _dot(a, b):
    return jnp.dot(a, b, preferred_element_type=F32)


def _dot_nt(a, b):
    return lax.dot_general(a, b, (((1,), (1,)), ((), ())), preferred_element_type=F32)


def _dot_tn(a, b):
    return lax.dot_general(a, b, (((0,), (0,)), ((), ())), preferred_element_type=F32)


def _softplus(x):
    return jnp.maximum(x, 0.0) + jnp.log1p(jnp.exp(-jnp.abs(x)))


def _rotary(x, cos2, sin2):
    return x * cos2 + pltpu.roll(x, RET_DK // 2, axis=1) * sin2


def _group_norm(y):
    mu = jnp.mean(y, axis=-1, keepdims=True)
    d = y - mu
    var = jnp.mean(d * d, axis=-1, keepdims=True)
    return d * lax.rsqrt(var + EPS)


def _sample_attention(q_ref, k_ref, v_ref, o_ref):
    slabs = MEM_TOKENS * X_HEADS // V7X_SUBLANES
    for b in range(q_ref.shape[0]):
        k3 = k_ref[b].reshape(slabs, V7X_SUBLANES, X_HEAD_DIM)
        v3 = v_ref[b].reshape(slabs, V7X_SUBLANES, X_HEAD_DIM)
        q4 = q_ref[b]
        q8 = jnp.concatenate([q4] * (V7X_SUBLANES // X_HEADS), axis=0)
        s = jnp.sum(k3 * q8[None], axis=-1, keepdims=True) * (X_HEAD_DIM ** -0.5)
        m8 = jnp.max(s, axis=0)
        m4 = jnp.maximum(m8[0:X_HEADS], m8[X_HEADS:])
        p = jnp.exp(s - jnp.concatenate([m4, m4], axis=0)[None])
        acc = jnp.sum(p * v3, axis=0)
        den = jnp.sum(p, axis=0)
        o_ref[b] = (acc[0:X_HEADS] + acc[X_HEADS:]) / (den[0:X_HEADS] + den[X_HEADS:])


def _ffn_kernel(*refs, final_norm, with_attention, with_mixers, n_cast):
    refs = list(refs)
    x_ref, g_ref, w1_ref, w3_ref, w2_ref = refs[:5]
    del refs[:5]
    fg_ref = refs.pop(0) if final_norm else None
    attn_in = [refs.pop(0) for _ in range(3)] if with_attention else None
    mixer_in = [refs.pop(0) for _ in range(N_MIXER_IN)] if with_mixers else None
    cast_in = [refs.pop(0) for _ in range(n_cast)]
    o_ref = refs.pop(0)
    x = x_ref[...]
    xn = _rmsnorm(x, g_ref[...]).astype(BF16)
    a = _dot(xn, w1_ref[...])
    b = _dot(xn, w3_ref[...])
    hidden = (jax.nn.silu(a) * b).astype(BF16)
    out = x + 0.5 * _dot(hidden, w2_ref[...])
    if final_norm:
        out = _rmsnorm(out, fg_ref[...])
    o_ref[...] = out
    if with_attention:
        _sample_attention(*attn_in, refs.pop(0))
    if with_mixers:
        _sample_mixers(*mixer_in, *refs[:N_MIXER_OUT])
        del refs[:N_MIXER_OUT]
    _cast_blocks(cast_in, refs)


def _ffn(x, gain, w1, w3, w2, final_gain=None, attention=None, mixers=None, casts=(), *, tm):
    t, d = x.shape
    steps = t // tm
    row = pl.BlockSpec((tm, d), lambda i: (i, 0))
    ins = [x, gain, w1, w3, w2]
    specs = [row, _resident(gain.shape), _resident(w1.shape), _resident(w3.shape), _resident(w2.shape)]
    out_shape, out_specs = [jax.ShapeDtypeStruct((t, d), F32)], [row]
    if final_gain is not None:
        ins.append(final_gain)
        specs.append(_resident(final_gain.shape))
    if attention is not None:
        q, cache_k, cache_v = attention
        bs = q.shape[0] // steps
        row_s = pl.BlockSpec((bs, X_HEADS, X_HEAD_DIM), lambda i: (i, 0, 0))
        mem_s = pl.BlockSpec((bs, MEM_TOKENS, X_HEADS, X_HEAD_DIM), lambda i: (i, 0, 0, 0))
        ins += [q, cache_k, cache_v]
        specs += [row_s, mem_s, mem_s]
        out_shape.append(jax.ShapeDtypeStruct(q.shape, F32))
        out_specs.append(row_s)
    if mixers is not None:
        m_ins, m_specs, m_shape, m_out_specs = _sample_mixer_specs(*mixers, steps)
        ins += m_ins
        specs += m_specs
        out_shape += m_shape
        out_specs += m_out_specs
    c_in, c_shape, c_out = _cast_specs(casts, steps)
    ins += list(casts)
    specs += c_in
    out_shape += c_shape
    out_specs += c_out
    outs = pl.pallas_call(
        functools.partial(_ffn_kernel, final_norm=final_gain is not None, with_attention=attention is not None,
                          with_mixers=mixers is not None, n_cast=len(casts)),
        out_shape=out_shape,
        grid=(steps,),
        in_specs=specs,
        out_specs=out_specs,
        compiler_params=_params(("arbitrary",)),
        name="ffn",
    )(*ins)
    return outs if len(outs) > 1 else outs[0]


def _linear_kernel(*refs, has_norm, has_res, n_w):
    refs = list(refs)
    x_ref = refs.pop(0)
    g_ref = refs.pop(0) if has_norm else None
    r_ref = refs.pop(0) if has_res else None
    w_refs, o_refs = refs[:n_w], refs[n_w:]
    x = x_ref[...]
    if has_norm:
        x = _rmsnorm(x, g_ref[...])
    xb = x.astype(BF16)
    for w_ref, o_ref in zip(w_refs, o_refs, strict=True):
        y = _dot(xb, w_ref[...])
        if has_res:
            y = r_ref[...] + y
        o_ref[...] = y


def _linear(x, weights, gain=None, res=None, *, tm):
    t, k = x.shape
    ins, specs = [x], [pl.BlockSpec((tm, k), lambda i: (i, 0))]
    if gain is not None:
        ins.append(gain)
        specs.append(_resident(gain.shape))
    if res is not None:
        ins.append(res)
        specs.append(pl.BlockSpec((tm, res.shape[1]), lambda i: (i, 0)))
    for w in weights:
        ins.append(w)
        specs.append(_resident(w.shape))
    outs = pl.pallas_call(
        functools.partial(_linear_kernel, has_norm=gain is not None, has_res=res is not None, n_w=len(weights)),
        out_shape=[jax.ShapeDtypeStruct((t, w.shape[1]), F32) for w in weights],
        grid=(t // tm,),
        in_specs=specs,
        out_specs=[pl.BlockSpec((tm, w.shape[1]), lambda i: (i, 0)) for w in weights],
        compiler_params=_params(("arbitrary",)),
        name="linear",
    )(*ins)
    return outs


def _cast_specs(arrays, steps):
    in_specs, out_shape, out_specs = [], [], []
    for a in arrays:
        rows = pl.cdiv(pl.cdiv(a.shape[0], steps), V7X_BF16_SUBLANES) * V7X_BF16_SUBLANES
        last = pl.cdiv(a.shape[0], rows) - 1
        blk = pl.BlockSpec((rows, a.shape[1]), lambda i, last=last: (jnp.minimum(i, last), 0))
        in_specs.append(blk)
        out_specs.append(blk)
        out_shape.append(jax.ShapeDtypeStruct(a.shape, BF16))
    return in_specs, out_shape, out_specs


def _cast_blocks(in_refs, out_refs):
    for i_ref, o_ref in zip(in_refs, out_refs, strict=True):
        o_ref[...] = i_ref[...].astype(BF16)


def _memkv_kernel(*refs, n_cast):
    x_ref, g_ref, wk_ref, wv_ref = refs[:4]
    cast_in = refs[4:4 + n_cast]
    k_ref, v_ref, k4_ref, v4_ref = refs[4 + n_cast:8 + n_cast]
    xb = _rmsnorm(x_ref[...], g_ref[...]).astype(BF16)
    for w_ref, o_ref, o4_ref in ((wk_ref, k_ref, k4_ref), (wv_ref, v_ref, v4_ref)):
        y = _dot(xb, w_ref[...])
        o_ref[0] = y.astype(BF16)
        o4_ref[0] = y.reshape(MEM_TOKENS, X_HEADS, X_HEAD_DIM)
    _cast_blocks(cast_in, refs[8 + n_cast:])


def _memkv(mem, gain, w_k, w_v, casts=()):
    batch, m, d = mem.shape
    flat = jax.ShapeDtypeStruct((batch, m, d), BF16)
    split = jax.ShapeDtypeStruct((batch, m, X_HEADS, X_HEAD_DIM), F32)
    flat_blk = pl.BlockSpec((1, m, d), lambda b: (b, 0, 0))
    split_blk = pl.BlockSpec((1, m, X_HEADS, X_HEAD_DIM), lambda b: (b, 0, 0, 0))
    c_in, c_shape, c_out = _cast_specs(casts, batch)
    return pl.pallas_call(
        functools.partial(_memkv_kernel, n_cast=len(casts)),
        out_shape=[flat, flat, split, split] + c_shape,
        grid=(batch,),
        in_specs=[pl.BlockSpec((m, d), lambda b: (b, 0)), _resident(gain.shape), _resident(w_k.shape),
                  _resident(w_v.shape)] + c_in,
        out_specs=[flat_blk, flat_blk, split_blk, split_blk] + c_out,
        compiler_params=_params(("arbitrary",)),
        name="memkv",
    )(mem.reshape(batch * m, d), gain, w_k, w_v, *casts)


def _cumsum_chunks(x):
    pos = lax.broadcasted_iota(jnp.int32, x.shape, 0) % CHUNK
    step = 1
    while step < CHUNK:
        x = x + jnp.where(pos >= step, pltpu.roll(x, step, axis=0), 0.0)
        step *= 2
    return x


def _mixp_project(tile, h_ref, g_ref, win_ref, wkt_ref, wdt_ref, cos_ref, sin_ref, cost_ref, sint_ref,
                  qdec_ref, kdec_ref, cw_ref, cb_ref, dtb_ref, alog_ref, conv_ref, xbc_s, stage, *, tm, nt):
    q_s, qd_s, kr_s, kd_s, v_s, g_s, z_s, xc_s, pk_s = stage
    row_start = (tile % nt) == 0

    u = _rmsnorm(h_ref[...], g_ref[...]).astype(BF16)
    q = _dot(u, win_ref[:, Q_OFF:K_OFF])
    vgz = _dot(u, win_ref[:, V_OFF:XBC_OFF])
    kt = _dot_nt(wkt_ref[...], u)
    xbc = _dot(u, win_ref[:, XBC_OFF:DT_OFF])
    dt_raw = _dot(u, wdt_ref[...])

    for sl in range(CONV_SLABS):
        lanes = slice(sl * V7X_LANES, (sl + 1) * V7X_LANES)
        xbc_s[sl, 0:CONV_HIST, :] = jnp.where(row_start, 0.0, xbc_s[sl, 0:CONV_HIST, :])
        xbc_s[sl, CONV_HIST:CONV_HIST + tm, :] = xbc[:, lanes]
        conv = cb_ref[:, lanes]
        for i in range(CONV_WIDTH):
            off = CONV_HIST - (CONV_WIDTH - 1) + i
            conv = conv + cw_ref[i:i + 1, lanes] * xbc_s[sl, off:off + tm, :]
        xc_s[sl] = jax.nn.silu(conv)
        conv_ref[0, :, lanes] = xbc_s[sl, CONV_HIST + tm - (CONV_WIDTH - 1):CONV_HIST + tm, :]
        xbc_s[sl, 0:CONV_HIST, :] = xbc_s[sl, tm:tm + CONV_HIST, :]

    dt = _softplus(dt_raw + dtb_ref[...])
    a_cum = _cumsum_chunks(dt * (-jnp.exp(alog_ref[...])))
    head_lanes = lax.broadcasted_iota(jnp.int32, (tm, DT_PAD), 1) < SSD_HEADS
    pk_s[...] = jnp.where(head_lanes, a_cum, dt)

    cos2, sin2 = cos_ref[...], sin_ref[...]
    cos_t, sin_t = cost_ref[...], sint_ref[...]
    half = RET_DK // 2
    for hh in range(RET_HEADS):
        lanes = slice(hh * RET_DK, (hh + 1) * RET_DK)
        qr = _rotary(q[:, lanes], cos2, sin2)
        q_s[:, lanes] = qr.astype(BF16)
        qd_s[:, lanes] = (qr * qdec_ref[hh]).astype(BF16)
        k1 = kt[hh * RET_DK:hh * RET_DK + half, :]
        k2 = kt[hh * RET_DK + half:(hh + 1) * RET_DK, :]
        kr = jnp.concatenate([k1 * cos_t - k2 * sin_t, k1 * sin_t + k2 * cos_t], axis=0) * (RET_DK ** -0.5)
        kr_s[lanes, :] = kr.astype(BF16)
        kd_s[lanes, :] = (kr * kdec_ref[hh:hh + 1, :]).astype(BF16)
    v_s[...] = vgz[:, 0:RET_WIDTH].astype(BF16)
    g_s[...] = jax.nn.silu(vgz[:, G_OFF - V_OFF:Z_OFF - V_OFF])
    z_s[...] = jax.nn.silu(vgz[:, Z_OFF - V_OFF:XBC_OFF - V_OFF])


def _mixp_heads(tile, h_ref, dmat_ref, cdec_ref, gn_ref, dsk_ref, sn_ref, wout_ref, o_ref, ret_ref, ssm_ref,
                mix_s, sret_s, sssm_s, stage, *, tm, nt):
    q_s, qd_s, kr_s, kd_s, v_s, g_s, z_s, xc_s, pk_s = stage
    nc = tm // CHUNK
    row_start = (tile % nt) == 0

    row_i = lax.broadcasted_iota(jnp.int32, (CHUNK, CHUNK), 0)
    col_i = lax.broadcasted_iota(jnp.int32, (CHUNK, CHUNK), 1)
    causal = row_i >= col_i
    low_lanes = col_i < SSD_HEAD_DIM
    bd_rows = lax.broadcasted_iota(jnp.int32, (2 * SSD_STATE, 2 * SSD_HEAD_DIM), 0)
    bd_cols = lax.broadcasted_iota(jnp.int32, (2 * SSD_STATE, 2 * SSD_HEAD_DIM), 1)
    bd_mask = (bd_rows < SSD_STATE) == (bd_cols < SSD_HEAD_DIM)

    inner, kv = {}, {}
    scores, c_exp, st_inc, st_dec = {}, {}, {}, {}
    for c in range(nc):
        rows = slice(c * CHUNK, (c + 1) * CHUNK)
        for hh in range(RET_HEADS):
            lanes = slice(hh * RET_DK, (hh + 1) * RET_DK)
            inner[c, hh] = (_dot(q_s[rows, lanes], kr_s[lanes, rows]) * dmat_ref[hh]).astype(BF16)
            kv[c, hh] = _dot(kd_s[lanes, rows], v_s[rows, lanes])

        pk = pk_s[rows, :]
        pk_t = pk.T
        a_t = pk_t[0:SSD_HEADS, :]
        dt_t = pk_t[SSD_HEADS:2 * SSD_HEADS, :]
        a_last = a_t[:, CHUNK - 1:CHUNK]
        w_t = jnp.exp(a_last - a_t) * dt_t
        chunk_dec = jnp.exp(a_last)
        for grp in range(SSD_GROUPS):
            cg = xc_s[SSD_PAIRS + SSD_GROUPS + grp, rows, :]
            b_t = xc_s[SSD_PAIRS + grp, rows, :].T
            gmat = _dot(cg.astype(BF16), b_t.astype(BF16))
            for pj in range(SSD_PAIRS // SSD_GROUPS):
                j = grp * (SSD_PAIRS // SSD_GROUPS) + pj
                heads = (2 * j, 2 * j + 1)
                x_pair = xc_s[j, rows, :].astype(BF16)
                b_w = jnp.concatenate([b_t * w_t[hd:hd + 1, :] for hd in heads], axis=0).astype(BF16)
                st_inc[c, j] = jnp.where(bd_mask, _dot(b_w, x_pair), 0.0)
                st_dec[c, j] = jnp.concatenate(
                    [jnp.broadcast_to(chunk_dec[hd:hd + 1, :], (SSD_STATE, 2 * SSD_HEAD_DIM)) for hd in heads], axis=0)
                for hd in heads:
                    a_col = jnp.broadcast_to(pk[:, hd:hd + 1], (CHUNK, CHUNK))
                    seg = a_col - a_t[hd:hd + 1, :]
                    lmat = jnp.where(causal, jnp.exp(jnp.where(causal, seg, 0.0)), 0.0)
                    scores[c, hd] = (gmat * lmat * dt_t[hd:hd + 1, :]).astype(BF16)
                    c_exp[c, hd] = (cg * jnp.exp(a_col)).astype(BF16)

    s_in, st_in = {}, {}
    for hh in range(RET_HEADS):
        s = jnp.where(row_start, 0.0, sret_s[hh])
        for c in range(nc):
            s_in[c, hh] = s.astype(BF16)
            s = cdec_ref[hh:hh + 1, :] * s + kv[c, hh]
        sret_s[hh] = s
        ret_ref[0, hh] = s
    for j in range(SSD_PAIRS):
        s = jnp.where(row_start, 0.0, sssm_s[j])
        for c in range(nc):
            st_in[c, j] = s.astype(BF16)
            s = st_dec[c, j] * s + st_inc[c, j]
        sssm_s[j] = s
        ssm_ref[0, j * 2 * SSD_HEAD_DIM:(j + 1) * 2 * SSD_HEAD_DIM, :] = (s[0:SSD_STATE] + s[SSD_STATE:]).T

    for c in range(nc):
        rows = slice(c * CHUNK, (c + 1) * CHUNK)
        for hh in range(RET_HEADS):
            lanes = slice(hh * RET_DV, (hh + 1) * RET_DV)
            lhs = jnp.concatenate([inner[c, hh], qd_s[rows, lanes]], axis=1)
            rhs = jnp.concatenate([v_s[rows, lanes], s_in[c, hh]], axis=0)
            r = _group_norm(_dot(lhs, rhs)) * gn_ref[:, lanes]
            mix_s[rows, lanes] = (g_s[rows, lanes] * r).astype(BF16)
        ys = []
        for j in range(SSD_PAIRS):
            heads = (2 * j, 2 * j + 1)
            x_pair = xc_s[j, rows, :]
            x_bd = jnp.concatenate([jnp.where(low_lanes, x_pair, 0.0), jnp.where(low_lanes, 0.0, x_pair)], axis=0)
            lhs = jnp.concatenate([scores[c, heads[0]], scores[c, heads[1]], c_exp[c, heads[0]], c_exp[c, heads[1]]],
                                  axis=1)
            rhs = jnp.concatenate([x_bd.astype(BF16), st_in[c, j]], axis=0)
            ys.append(_dot(lhs, rhs) + dsk_ref[:, j * 2 * SSD_HEAD_DIM:(j + 1) * 2 * SSD_HEAD_DIM] * x_pair)
        y_all = jnp.concatenate(ys, axis=1) * z_s[rows, :]
        mix_s[rows, RET_WIDTH:RET_WIDTH + SSD_WIDTH] = _rmsnorm(y_all, sn_ref[...]).astype(BF16)

    o_ref[...] = h_ref[...] + _dot(mix_s[...], wout_ref[...])


def _mix_prompt_kernel(h_ref, hp_ref, g_ref, win_ref, wkt_ref, wdt_ref, cos_ref, sin_ref, cost_ref, sint_ref,
                       dmat_ref, qdec_ref, kdec_ref, cdec_ref, gn_ref, cw_ref, cb_ref, dtb_ref, alog_ref, dsk_ref,
                       sn_ref, wout_ref,
                       o_ref, ret_ref, ssm_ref, conv_ref,
                       xbc_s, mix_s, sret_s, sssm_s, *stage_s, tm, nt, n_tiles):
    s = pl.program_id(0)
    sets = (tuple(r.at[0] for r in stage_s), tuple(r.at[1] for r in stage_s))

    @pl.when(s == 0)
    def _():
        for r in stage_s:
            r[...] = jnp.zeros_like(r)
        sret_s[...] = jnp.zeros_like(sret_s)
        sssm_s[...] = jnp.zeros_like(sssm_s)
        xbc_s[...] = jnp.zeros_like(xbc_s)

    def step(write_set, read_set):
        _mixp_project(jnp.minimum(s, n_tiles - 1), h_ref, g_ref, win_ref, wkt_ref, wdt_ref, cos_ref, sin_ref,
                      cost_ref, sint_ref, qdec_ref, kdec_ref, cw_ref, cb_ref, dtb_ref, alog_ref, conv_ref, xbc_s,
                      write_set, tm=tm, nt=nt)
        _mixp_heads(jnp.maximum(s - 1, 0), hp_ref, dmat_ref, cdec_ref, gn_ref, dsk_ref, sn_ref, wout_ref, o_ref, ret_ref,
                    ssm_ref, mix_s, sret_s, sssm_s, read_set, tm=tm, nt=nt)

    @pl.when(s % 2 == 0)
    def _():
        step(sets[0], sets[1])

    @pl.when(s % 2 == 1)
    def _():
        step(sets[1], sets[0])


def _mix_prompt(h, gain, w_in, w_kt, w_dt, cos2, sin2, cos_t, sin_t, dmat, qdec, kdec, cdec, gn, conv_w, conv_b,
                dt_bias, a_log, dskip, ssd_gain, w_out, *, batch, seq, tm):
    nt = seq // tm
    n_tiles = batch * nt
    d = h.shape[1]
    cur = lambda s: jnp.minimum(s, n_tiles - 1)
    prev = lambda s: jnp.maximum(s - 1, 0)
    row_cur = pl.BlockSpec((tm, d), lambda s: (cur(s), 0))
    row_prev = pl.BlockSpec((tm, d), lambda s: (prev(s), 0))
    pos = pl.BlockSpec((tm, RET_DK), lambda s: (cur(s) % nt, 0))
    pos_t = pl.BlockSpec((RET_DK // 2, tm), lambda s: (0, cur(s) % nt))
    consts = [gain, w_in, w_kt, w_dt]
    tail = [dmat, qdec, kdec, cdec, gn, conv_w, conv_b, dt_bias, a_log, dskip, ssd_gain, w_out]
    stage = [
        pltpu.VMEM((2, tm, RET_HEADS * RET_DK), BF16),
        pltpu.VMEM((2, tm, RET_HEADS * RET_DK), BF16),
        pltpu.VMEM((2, RET_HEADS * RET_DK, tm), BF16),
        pltpu.VMEM((2, RET_HEADS * RET_DK, tm), BF16),
        pltpu.VMEM((2, tm, RET_WIDTH), BF16),
        pltpu.VMEM((2, tm, RET_WIDTH), F32),
        pltpu.VMEM((2, tm, SSD_WIDTH), F32),
        pltpu.VMEM((2, CONV_SLABS, tm, V7X_LANES), F32),
        pltpu.VMEM((2, tm, DT_PAD), F32),
    ]
    return pl.pallas_call(
        functools.partial(_mix_prompt_kernel, tm=tm, nt=nt, n_tiles=n_tiles),
        out_shape=[
            jax.ShapeDtypeStruct(h.shape, F32),
            jax.ShapeDtypeStruct((batch, RET_HEADS, RET_DK, RET_DV), F32),
            jax.ShapeDtypeStruct((batch, SSD_WIDTH, SSD_STATE), F32),
            jax.ShapeDtypeStruct((batch, CONV_WIDTH - 1, CONV_CH), F32),
        ],
        grid=(n_tiles + 1,),
        in_specs=([row_cur, row_prev] + [_resident(a.shape) for a in consts] + [pos, pos, pos_t, pos_t]
                  + [_resident(a.shape) for a in tail]),
        out_specs=[
            row_prev,
            pl.BlockSpec((1, RET_HEADS, RET_DK, RET_DV), lambda s: (prev(s) // nt, 0, 0, 0)),
            pl.BlockSpec((1, SSD_WIDTH, SSD_STATE), lambda s: (prev(s) // nt, 0, 0)),
            pl.BlockSpec((1, CONV_WIDTH - 1, CONV_CH), lambda s: (cur(s) // nt, 0, 0)),
        ],
        scratch_shapes=[
            pltpu.VMEM((CONV_SLABS, CONV_HIST + tm, V7X_LANES), F32),
            pltpu.VMEM((tm, RET_WIDTH + SSD_WIDTH), BF16),
            pltpu.VMEM((RET_HEADS, RET_DK, RET_DV), F32),
            pltpu.VMEM((SSD_PAIRS, 2 * SSD_STATE, 2 * SSD_HEAD_DIM), F32),
        ] + stage,
        compiler_params=_params(("arbitrary",)),
        name="mix_prompt",
    )(h, h, *consts, cos2, sin2, cos_t, sin_t, *tail)


def _split_hi_lo(x):
    hi = x.astype(BF16).astype(F32)
    return hi, x - hi


def _sample_mixers(proj_ref, sret_ref, sssm_ref, sconv_ref, cos_ref, sin_ref, gam_ref,
                   gn_ref, cw_ref, cb_ref, dtb_ref, alog_ref, dsk_ref, sn_ref,
                   mix_ref, oret_ref, ossm_ref, oconv_ref):
    bs = proj_ref.shape[1]
    proj = proj_ref[0]
    xbc = proj[:, XBC_OFF:DT_OFF]
    hist = sconv_ref[0]
    taps = [hist[:, i * CONV_CH:(i + 1) * CONV_CH] for i in range(CONV_WIDTH - 1)] + [xbc]
    conv = cb_ref[...]
    for i in range(CONV_WIDTH):
        conv = conv + cw_ref[i:i + 1, :] * taps[i]
    oconv_ref[0] = jnp.concatenate(taps[1:], axis=1)
    xc = jax.nn.silu(conv)

    dt = _softplus(proj[:, DT_OFF:IN_PROJ_WIDTH] + dtb_ref[:, 0:SSD_HEADS])

    sub_i = lax.broadcasted_iota(jnp.int32, (bs, V7X_LANES), 0)
    row_of = lax.broadcasted_iota(jnp.int32, (4 * bs, V7X_LANES), 0) % bs
    cos2 = cos_ref[...]
    sin2 = sin_ref[...]

    def outer_lhs(x):
        hi, lo = _split_hi_lo(x)
        return jnp.concatenate([hi, hi, lo, lo], axis=0)

    def outer_rhs(x):
        hi, lo = _split_hi_lo(x)
        return jnp.concatenate([hi, lo, hi, lo], axis=0).astype(BF16)

    def only_sample(x4, b):
        return jnp.where(row_of == b, x4, 0.0).astype(BF16)

    for hh in range(RET_HEADS):
        lanes = slice(hh * RET_DK, (hh + 1) * RET_DK)
        qr = _rotary(proj[:, Q_OFF + hh * RET_DK:Q_OFF + (hh + 1) * RET_DK], cos2, sin2)
        kr = _rotary(proj[:, K_OFF + hh * RET_DK:K_OFF + (hh + 1) * RET_DK], cos2, sin2) * (RET_DK ** -0.5)
        vh = proj[:, V_OFF + hh * RET_DV:V_OFF + (hh + 1) * RET_DV]
        k4, v4, q_bf = outer_lhs(kr), outer_rhs(vh), qr.astype(BF16)
        gamma = gam_ref[hh:hh + 1, :]
        y = jnp.zeros((bs, RET_DV), F32)
        for b in range(bs):
            s_old = sret_ref[b, hh]
            oret_ref[b, hh] = gamma * s_old + _dot_tn(only_sample(k4, b), v4)
            y = jnp.where(sub_i == b, _dot(q_bf, s_old.astype(BF16)), y)
        y = gamma * y + jnp.sum(qr * kr, axis=-1, keepdims=True) * vh
        r = _group_norm(y) * gn_ref[:, lanes]
        mix_ref[0, :, lanes] = jax.nn.silu(proj[:, G_OFF + hh * RET_DV:G_OFF + (hh + 1) * RET_DV]) * r

    xs = xc[:, 0:SSD_WIDTH]
    head_of_lane = lax.broadcasted_iota(jnp.int32, (bs, SSD_WIDTH), 1) // SSD_HEAD_DIM
    dec = jnp.exp(dt * (-jnp.exp(alog_ref[:, 0:SSD_HEADS])))
    dt_wide = jnp.zeros((bs, SSD_WIDTH), F32)
    dec_wide = jnp.zeros((bs, SSD_WIDTH), F32)
    for hd in range(SSD_HEADS):
        dt_wide = jnp.where(head_of_lane == hd, dt[:, hd:hd + 1], dt_wide)
        dec_wide = jnp.where(head_of_lane == hd, dec[:, hd:hd + 1], dec_wide)
    xdt = xs * dt_wide
    x4 = outer_lhs(xdt)
    ys = []
    for j in range(SSD_PAIRS):
        grp = j // (SSD_PAIRS // SSD_GROUPS)
        lanes = slice(j * 2 * SSD_HEAD_DIM, (j + 1) * 2 * SSD_HEAD_DIM)
        bg = xc[:, SSD_WIDTH + grp * SSD_STATE:SSD_WIDTH + (grp + 1) * SSD_STATE]
        c_off = SSD_WIDTH + SSD_GROUPS * SSD_STATE + grp * SSD_STATE
        cg = xc[:, c_off:c_off + SSD_STATE]
        b4, c_bf = outer_rhs(bg), cg.astype(BF16)
        y = jnp.zeros((bs, 2 * SSD_HEAD_DIM), F32)
        for b in range(bs):
            s_old = sssm_ref[b, lanes, :]
            dec_rows = jnp.concatenate(
                [jnp.broadcast_to(dec[b:b + 1, hd:hd + 1], (SSD_HEAD_DIM, SSD_STATE)) for hd in (2 * j, 2 * j + 1)],
                axis=0)
            ossm_ref[b, lanes, :] = dec_rows * s_old + _dot_tn(only_sample(x4[:, lanes], b), b4)
            y = jnp.where(sub_i == b, _dot_nt(c_bf, s_old.astype(BF16)), y)
        ys.append(dec_wide[:, lanes] * y + jnp.sum(cg * bg, axis=-1, keepdims=True) * xdt[:, lanes])
    ys = (jnp.concatenate(ys, axis=1) + dsk_ref[...] * xs) * jax.nn.silu(proj[:, Z_OFF:Z_OFF + SSD_WIDTH])
    mix_ref[0, :, RET_WIDTH:RET_WIDTH + SSD_WIDTH] = _rmsnorm(ys, sn_ref[...])


N_MIXER_IN = 14
N_MIXER_OUT = 4


def _sample_mixer_specs(proj, s_ret, s_ssm, s_conv, consts, steps):
    n = proj.shape[0]
    bs = n // steps
    grouped = lambda a: a.reshape(steps, bs, a.shape[1])
    blk3 = lambda w: pl.BlockSpec((1, bs, w), lambda i: (i, 0, 0))
    ret_blk = pl.BlockSpec((bs, RET_HEADS, RET_DK, RET_DV), lambda i: (i, 0, 0, 0))
    ssm_blk = pl.BlockSpec((bs, SSD_WIDTH, SSD_STATE), lambda i: (i, 0, 0))
    ins = [grouped(proj), s_ret, s_ssm, grouped(s_conv), *consts]
    in_specs = [blk3(proj.shape[1]), ret_blk, ssm_blk, blk3(s_conv.shape[1])] + [_resident(a.shape) for a in consts]
    out_shape = [jax.ShapeDtypeStruct((steps, bs, RET_WIDTH + SSD_WIDTH), F32), jax.ShapeDtypeStruct(s_ret.shape, F32),
                 jax.ShapeDtypeStruct(s_ssm.shape, F32), jax.ShapeDtypeStruct((steps, bs, s_conv.shape[1]), F32)]
    out_specs = [blk3(RET_WIDTH + SSD_WIDTH), ret_blk, ssm_blk, blk3(s_conv.shape[1])]
    return ins, in_specs, out_shape, out_specs


def _softmax_rows(s):
    m = jnp.max(s, axis=-1, keepdims=True)
    p = jnp.exp(s - m)
    return p / jnp.sum(p, axis=-1, keepdims=True)


def _xattn_prompt_kernel(h_ref, g_ref, wq_ref, mk_ref, mv_ref, wo_ref, o_ref, att_s):
    h = h_ref[...]
    c = _rmsnorm(h, g_ref[...]).astype(BF16)
    qx = _dot(c, wq_ref[...]).astype(BF16)
    for hh in range(X_HEADS):
        lanes = slice(hh * X_HEAD_DIM, (hh + 1) * X_HEAD_DIM)
        s = _dot_nt(qx[:, lanes], mk_ref[0, :, lanes]) * (X_HEAD_DIM ** -0.5)
        att = _softmax_rows(s).astype(BF16)
        att_s[:, lanes] = _dot(att, mv_ref[0, :, lanes]).astype(BF16)
    o_ref[...] = h + _dot(att_s[...], wo_ref[...])


def _xattn_prompt(h, gain, w_q, mem_k, mem_v, w_o, *, batch, seq, tm):
    nt = seq // tm
    d = h.shape[1]
    row = pl.BlockSpec((tm, d), lambda b, t: (b * nt + t, 0))
    mem = pl.BlockSpec((1, MEM_TOKENS, d), lambda b, t: (b, 0, 0))
    return pl.pallas_call(
        _xattn_prompt_kernel,
        out_shape=jax.ShapeDtypeStruct(h.shape, F32),
        grid=(batch, nt),
        in_specs=[row, _resident(gain.shape), _resident(w_q.shape), mem, mem, _resident(w_o.shape)],
        out_specs=row,
        scratch_shapes=[pltpu.VMEM((tm, d), BF16)],
        compiler_params=_params(("arbitrary", "arbitrary")),
        name="xattn_prompt",
    )(h, gain, w_q, mem_k, mem_v, w_o)


def _rope_angles(pos):
    half = RET_DK // 2
    inv_freq = ROPE_BASE ** (-jnp.arange(half, dtype=F32) / half)
    ang = pos.astype(F32)[:, None] * inv_freq[None, :]
    return jnp.cos(ang), jnp.sin(ang)


def _rope_tables(pos):
    cos, sin = _rope_angles(pos)
    return jnp.concatenate([cos, cos], axis=-1), jnp.concatenate([-sin, sin], axis=-1)


def _retention_decay_tables(chunk):
    log_g = jnp.log1p(-jnp.exp2(-5.0 - jnp.arange(RET_HEADS, dtype=F32)))
    idx = jnp.arange(chunk, dtype=F32)
    diff = idx[:, None] - idx[None, :]
    causal = diff >= 0
    dmat = jnp.where(causal[None], jnp.exp(log_g[:, None, None] * jnp.where(causal, diff, 0.0)[None]), 0.0)
    q_dec = jnp.exp(log_g[:, None] * (idx[None, :] + 1.0))
    k_dec = jnp.exp(log_g[:, None] * (chunk - 1.0 - idx[None, :]))
    c_dec = jnp.exp(log_g * chunk)
    wide = lambda x: jnp.broadcast_to(x[..., None], x.shape + (V7X_LANES,))
    return dmat, wide(q_dec), k_dec, wide(c_dec)


def kernel(x_prompt, x_sample, mem_prompt, state_ret, state_ssm, state_conv, cache_mem_k, cache_mem_v, ffn1_norm,
           ffn1_w1, ffn1_w3, ffn1_w2, mix_norm, w_in, ret_gn_gain, conv_w, conv_b, dt_bias, A_log, D_skip, ssd_norm,
           w_out, x_norm, mem_norm, w_xq, w_xk, w_xv, w_xo, ffn2_norm, ffn2_w1, ffn2_w3, ffn2_w2, final_norm):
    bp, lp, d = x_prompt.shape
    bsz = x_sample.shape[0]
    depth = ffn1_w1.shape[0]
    row = lambda v: v.reshape(1, -1).astype(F32)
    lane_pad = lambda v: jnp.pad(row(v), ((0, 0), (0, DT_PAD - v.shape[-1])))

    cos_p, sin_p = _rope_tables(jnp.arange(lp))
    cos_pt, sin_pt = (a.T for a in _rope_angles(jnp.arange(lp)))
    cos_s, sin_s = _rope_tables(PAST_LEN + jnp.arange(x_sample.shape[1]))
    dmat, q_dec, k_dec, c_dec = _retention_decay_tables(CHUNK)
    gamma1 = _retention_decay_tables(1)[3]

    y_p = x_prompt.reshape(bp * lp, d)
    y_s = x_sample.reshape(bsz, d)
    outs = {k: [] for k in ("ret_p", "ssm_p", "conv_p", "memk", "memv", "ret_s", "ssm_s", "conv_s")}
    for l in range(depth):
        bf = lambda w: w[l].astype(BF16)
        w_in_f = w_in[l]
        w_in_l = w_in_f.astype(BF16)
        w_kt = w_in_f[:, K_OFF:V_OFF].T.astype(BF16)
        w_dt = jnp.pad(jnp.tile(w_in_f[:, DT_OFF:], (1, 2)), ((0, 0), (0, DT_PAD - 2 * SSD_HEADS))).astype(BF16)
        shared = (row(ret_gn_gain[l]), conv_w[l], row(conv_b[l]))
        ssd_tail = (lane_pad(A_log[l]), row(jnp.repeat(D_skip[l], SSD_HEAD_DIM)), row(ssd_norm[l]))
        w_out_l, w_xq_l, w_xo_l = bf(w_out), bf(w_xq), bf(w_xo)

        mk, mv, mk4, mv4, f1_w1, f1_w3, f1_w2 = _memkv(
            mem_prompt, row(mem_norm[l]), bf(w_xk), bf(w_xv), casts=(ffn1_w1[l], ffn1_w3[l], ffn1_w2[l]))
        f1 = (row(ffn1_norm[l]), f1_w1, f1_w3, f1_w2)

        y_s = _ffn(y_s, *f1, tm=bsz)
        proj_s, = _linear(y_s, [w_in_l], gain=row(mix_norm[l]), tm=bsz)

        mixer_consts = (cos_s, sin_s, gamma1, *shared, lane_pad(dt_bias[l]), *ssd_tail)
        y_p, mix_s, ret_s, ssm_s, conv_s, f2_w1, f2_w3, f2_w2 = _ffn(
            y_p, *f1, mixers=(proj_s, state_ret[l], state_ssm[l].reshape(bsz, SSD_WIDTH, SSD_STATE),
                              state_conv[l].reshape(bsz, (CONV_WIDTH - 1) * CONV_CH), mixer_consts),
            casts=(ffn2_w1[l], ffn2_w3[l], ffn2_w2[l]), tm=512)
        f2 = (row(ffn2_norm[l]), f2_w1, f2_w3, f2_w2)
        y_s, = _linear(mix_s.reshape(bsz, d), [w_out_l], res=y_s, tm=bsz)
        q_s, = _linear(y_s, [w_xq_l], gain=row(x_norm[l]), tm=bsz)

        tm_p = 512
        y_p, ret_p, ssm_p, conv_p = _mix_prompt(
            y_p, row(mix_norm[l]), w_in_l, w_kt, w_dt, cos_p, sin_p, cos_pt, sin_pt, dmat,
            jnp.tile(q_dec, (1, tm_p // CHUNK, 1)), jnp.tile(k_dec, (1, tm_p // CHUNK)), c_dec,
            *shared, lane_pad(jnp.tile(dt_bias[l], 2)), *ssd_tail, w_out_l, batch=bp, seq=lp, tm=tm_p)

        y_p = _xattn_prompt(y_p, row(x_norm[l]), w_xq_l, mk, mv, w_xo_l, batch=bp, seq=lp, tm=1024)
        y_p, att_s = _ffn(y_p, *f2, final_gain=row(final_norm) if l == depth - 1 else None,
                          attention=(q_s.reshape(bsz, X_HEADS, X_HEAD_DIM), cache_mem_k[l], cache_mem_v[l]), tm=512)
        y_s, = _linear(att_s.reshape(bsz, d), [w_xo_l], res=y_s, tm=bsz)
        y_s = _ffn(y_s, *f2, final_gain=row(final_norm) if l == depth - 1 else None, tm=bsz)

        outs["ret_p"].append(ret_p)
        outs["ssm_p"].append(ssm_p.reshape(bp, SSD_HEADS, SSD_HEAD_DIM, SSD_STATE))
        outs["conv_p"].append(conv_p)
        outs["memk"].append(mk4)
        outs["memv"].append(mv4)
        outs["ret_s"].append(ret_s)
        outs["ssm_s"].append(ssm_s.reshape(bsz, SSD_HEADS, SSD_HEAD_DIM, SSD_STATE))
        outs["conv_s"].append(conv_s.reshape(bsz, CONV_WIDTH - 1, CONV_CH))

    st = lambda k: jnp.stack(outs[k])
    return (y_p.reshape(bp, lp, d), y_s.reshape(bsz, x_sample.shape[1], d), st("ret_p"), st("ssm_p"), st("conv_p"),
            st("memk"), st("memv"), st("ret_s"), st("ssm_s"), st("conv_s"))
```

```python
import functools

import jax
import jax.numpy as jnp
from jax import lax
from jax.experimental import pallas as pl
from jax.experimental.pallas import tpu as pltpu

F32 = jnp.float32
BF16 = jnp.bfloat16

D_MODEL = 1024
PAST_LEN = 16384
RET_HEADS = 4
RET_DK = 128
RET_DV = 128
RET_WIDTH = RET_HEADS * RET_DV
SSD_HEADS = 8
SSD_HEAD_DIM = 64
SSD_WIDTH = SSD_HEADS * SSD_HEAD_DIM
SSD_GROUPS = 2
SSD_STATE = 128
SSD_PAIRS = SSD_HEADS // 2
CONV_WIDTH = 4
CONV_CH = SSD_WIDTH + 2 * SSD_GROUPS * SSD_STATE
CHUNK = 128
MEM_TOKENS = 256
X_HEADS = 4
X_HEAD_DIM = D_MODEL // X_HEADS
ROPE_BASE = 10000.0
EPS = 1e-6

Q_OFF = 0
K_OFF = Q_OFF + RET_HEADS * RET_DK
V_OFF = K_OFF + RET_HEADS * RET_DK
G_OFF = V_OFF + RET_WIDTH
Z_OFF = G_OFF + RET_WIDTH
XBC_OFF = Z_OFF + SSD_WIDTH
DT_OFF = XBC_OFF + CONV_CH
IN_PROJ_WIDTH = DT_OFF + SSD_HEADS

V7X_LANES = 128
V7X_SUBLANES = 8
V7X_BF16_SUBLANES = 16
V7X_VMEM_LIMIT_BYTES = 56 * 1024 * 1024
DT_PAD = V7X_LANES
CONV_HIST = V7X_SUBLANES
CONV_SLABS = CONV_CH // V7X_LANES


def _params(sem):
    return pltpu.CompilerParams(dimension_semantics=sem, vmem_limit_bytes=V7X_VMEM_LIMIT_BYTES)


def _resident(shape):
    zeros = (0,) * len(shape)
    return pl.BlockSpec(shape, lambda *_: zeros, pipeline_mode=pl.Buffered(1))


def _rmsnorm(x, gain):
    ms = jnp.mean(x * x, axis=-1, keepdims=True)
    return x * lax.rsqrt(ms + EPS) * gain


def _dot(a, b):
    return jnp.dot(a, b, preferred_element_type=F32)


def _dot_nt(a, b):
    return lax.dot_general(a, b, (((1,), (1,)), ((), ())), preferred_element_type=F32)


def _dot_tn(a, b):
    return lax.dot_general(a, b, (((0,), (0,)), ((), ())), preferred_element_type=F32)


def _softplus(x):
    return jnp.maximum(x, 0.0) + jnp.log1p(jnp.exp(-jnp.abs(x)))


def _rotary(x, cos2, sin2):
    return x * cos2 + pltpu.roll(x, RET_DK // 2, axis=1) * sin2


def _group_norm(y):
    mu = jnp.mean(y, axis=-1, keepdims=True)
    d = y - mu
    var = jnp.mean(d * d, axis=-1, keepdims=True)
    return d * lax.rsqrt(var + EPS)


def _sample_attention(q_ref, k_ref, v_ref, o_ref):
    slabs = MEM_TOKENS * X_HEADS // V7X_SUBLANES
    for b in range(q_ref.shape[0]):
        k3 = k_ref[b].reshape(slabs, V7X_SUBLANES, X_HEAD_DIM)
        v3 = v_ref[b].reshape(slabs, V7X_SUBLANES, X_HEAD_DIM)
        q4 = q_ref[b]
        q8 = jnp.concatenate([q4] * (V7X_SUBLANES // X_HEADS), axis=0)
        s = jnp.sum(k3 * q8[None], axis=-1, keepdims=True) * (X_HEAD_DIM ** -0.5)
        m8 = jnp.max(s, axis=0)
        m4 = jnp.maximum(m8[0:X_HEADS], m8[X_HEADS:])
        p = jnp.exp(s - jnp.concatenate([m4, m4], axis=0)[None])
        acc = jnp.sum(p * v3, axis=0)
        den = jnp.sum(p, axis=0)
        o_ref[b] = (acc[0:X_HEADS] + acc[X_HEADS:]) / (den[0:X_HEADS] + den[X_HEADS:])


def _ffn_kernel(*refs, final_norm, with_attention, with_mixers, n_cast):
    refs = list(refs)
    x_ref, g_ref, w1_ref, w3_ref, w2_ref = refs[:5]
    del refs[:5]
    fg_ref = refs.pop(0) if final_norm else None
    attn_in = [refs.pop(0) for _ in range(3)] if with_attention else None
    mixer_in = [refs.pop(0) for _ in range(N_MIXER_IN)] if with_mixers else None
    cast_in = [refs.pop(0) for _ in range(n_cast)]
    o_ref = refs.pop(0)
    x = x_ref[...]
    xn = _rmsnorm(x, g_ref[...]).astype(BF16)
    a = _dot(xn, w1_ref[...])
    b = _dot(xn, w3_ref[...])
    hidden = (jax.nn.silu(a) * b).astype(BF16)
    out = x + 0.5 * _dot(hidden, w2_ref[...])
    if final_norm:
        out = _rmsnorm(out, fg_ref[...])
    o_ref[...] = out
    if with_attention:
        _sample_attention(*attn_in, refs.pop(0))
    if with_mixers:
        _sample_mixers(*mixer_in, *refs[:N_MIXER_OUT])
        del refs[:N_MIXER_OUT]
    _cast_blocks(cast_in, refs)


def _ffn(x, gain, w1, w3, w2, final_gain=None, attention=None, mixers=None, casts=(), *, tm):
    t, d = x.shape
    steps = t // tm
    row = pl.BlockSpec((tm, d), lambda i: (i, 0))
    ins = [x, gain, w1, w3, w2]
    specs = [row, _resident(gain.shape), _resident(w1.shape), _resident(w3.shape), _resident(w2.shape)]
    out_shape, out_specs = [jax.ShapeDtypeStruct((t, d), F32)], [row]
    if final_gain is not None:
        ins.append(final_gain)
        specs.append(_resident(final_gain.shape))
    if attention is not None:
        q, cache_k, cache_v = attention
        bs = q.shape[0] // steps
        row_s = pl.BlockSpec((bs, X_HEADS, X_HEAD_DIM), lambda i: (i, 0, 0))
        mem_s = pl.BlockSpec((bs, MEM_TOKENS, X_HEADS, X_HEAD_DIM), lambda i: (i, 0, 0, 0))
        ins += [q, cache_k, cache_v]
        specs += [row_s, mem_s, mem_s]
        out_shape.append(jax.ShapeDtypeStruct(q.shape, F32))
        out_specs.append(row_s)
    if mixers is not None:
        m_ins, m_specs, m_shape, m_out_specs = _sample_mixer_specs(*mixers, steps)
        ins += m_ins
        specs += m_specs
        out_shape += m_shape
        out_specs += m_out_specs
    c_in, c_shape, c_out = _cast_specs(casts, steps)
    ins += list(casts)
    specs += c_in
    out_shape += c_shape
    out_specs += c_out
    outs = pl.pallas_call(
        functools.partial(_ffn_kernel, final_norm=final_gain is not None, with_attention=attention is not None,
                          with_mixers=mixers is not None, n_cast=len(casts)),
        out_shape=out_shape,
        grid=(steps,),
        in_specs=specs,
        out_specs=out_specs,
        compiler_params=_params(("arbitrary",)),
        name="ffn",
    )(*ins)
    return outs if len(outs) > 1 else outs[0]


def _linear_kernel(*refs, has_norm, has_res, n_w):
    refs = list(refs)
    x_ref = refs.pop(0)
    g_ref = refs.pop(0) if has_norm else None
    r_ref = refs.pop(0) if has_res else None
    w_refs, o_refs = refs[:n_w], refs[n_w:]
    x = x_ref[...]
    if has_norm:
        x = _rmsnorm(x, g_ref[...])
    xb = x.astype(BF16)
    for w_ref, o_ref in zip(w_refs, o_refs, strict=True):
        y = _dot(xb, w_ref[...])
        if has_res:
            y = r_ref[...] + y
        o_ref[...] = y


def _linear(x, weights, gain=None, res=None, *, tm):
    t, k = x.shape
    ins, specs = [x], [pl.BlockSpec((tm, k), lambda i: (i, 0))]
    if gain is not None:
        ins.append(gain)
        specs.append(_resident(gain.shape))
    if res is not None:
        ins.append(res)
        specs.append(pl.BlockSpec((tm, res.shape[1]), lambda i: (i, 0)))
    for w in weights:
        ins.append(w)
        specs.append(_resident(w.shape))
    outs = pl.pallas_call(
        functools.partial(_linear_kernel, has_norm=gain is not None, has_res=res is not None, n_w=len(weights)),
        out_shape=[jax.ShapeDtypeStruct((t, w.shape[1]), F32) for w in weights],
        grid=(t // tm,),
        in_specs=specs,
        out_specs=[pl.BlockSpec((tm, w.shape[1]), lambda i: (i, 0)) for w in weights],
        compiler_params=_params(("arbitrary",)),
        name="linear",
    )(*ins)
    return outs


def _cast_specs(arrays, steps):
    in_specs, out_shape, out_specs = [], [], []
    for a in arrays:
        rows = pl.cdiv(pl.cdiv(a.shape[0], steps), V7X_BF16_SUBLANES) * V7X_BF16_SUBLANES
        last = pl.cdiv(a.shape[0], rows) - 1
        blk = pl.BlockSpec((rows, a.shape[1]), lambda i, last=last: (jnp.minimum(i, last), 0))
        in_specs.append(blk)
        out_specs.append(blk)
        out_shape.append(jax.ShapeDtypeStruct(a.shape, BF16))
    return in_specs, out_shape, out_specs


def _cast_blocks(in_refs, out_refs):
    for i_ref, o_ref in zip(in_refs, out_refs, strict=True):
        o_ref[...] = i_ref[...].astype(BF16)


def _memkv_kernel(*refs, n_cast):
    x_ref, g_ref, wk_ref, wv_ref = refs[:4]
    cast_in = refs[4:4 + n_cast]
    k_ref, v_ref, k4_ref, v4_ref = refs[4 + n_cast:8 + n_cast]
    xb = _rmsnorm(x_ref[...], g_ref[...]).astype(BF16)
    for w_ref, o_ref, o4_ref in ((wk_ref, k_ref, k4_ref), (wv_ref, v_ref, v4_ref)):
        y = _dot(xb, w_ref[...])
        o_ref[0] = y.astype(BF16)
        o4_ref[0] = y.reshape(MEM_TOKENS, X_HEADS, X_HEAD_DIM)
    _cast_blocks(cast_in, refs[8 + n_cast:])


def _memkv(mem, gain, w_k, w_v, casts=()):
    batch, m, d = mem.shape
    flat = jax.ShapeDtypeStruct((batch, m, d), BF16)
    split = jax.ShapeDtypeStruct((batch, m, X_HEADS, X_HEAD_DIM), F32)
    flat_blk = pl.BlockSpec((1, m, d), lambda b: (b, 0, 0))
    split_blk = pl.BlockSpec((1, m, X_HEADS, X_HEAD_DIM), lambda b: (b, 0, 0, 0))
    c_in, c_shape, c_out = _cast_specs(casts, batch)
    return pl.pallas_call(
        functools.partial(_memkv_kernel, n_cast=len(casts)),
        out_shape=[flat, flat, split, split] + c_shape,
        grid=(batch,),
        in_specs=[pl.BlockSpec((m, d), lambda b: (b, 0)), _resident(gain.shape), _resident(w_k.shape),
                  _resident(w_v.shape)] + c_in,
        out_specs=[flat_blk, flat_blk, split_blk, split_blk] + c_out,
        compiler_params=_params(("arbitrary",)),
        name="memkv",
    )(mem.reshape(batch * m, d), gain, w_k, w_v, *casts)


def _cumsum_chunks(x):
    pos = lax.broadcasted_iota(jnp.int32, x.shape, 0) % CHUNK
    step = 1
    while step < CHUNK:
        x = x + jnp.where(pos >= step, pltpu.roll(x, step, axis=0), 0.0)
        step *= 2
    return x


def _mixp_project(tile, h_ref, g_ref, win_ref, wkt_ref, wdt_ref, cos_ref, sin_ref, cost_ref, sint_ref,
                  qdec_ref, kdec_ref, cw_ref, cb_ref, dtb_ref, alog_ref, conv_ref, xbc_s, stage, *, tm, nt):
    q_s, qd_s, kr_s, kd_s, v_s, g_s, z_s, xc_s, pk_s = stage
    row_start = (tile % nt) == 0

    u = _rmsnorm(h_ref[...], g_ref[...]).astype(BF16)
    q = _dot(u, win_ref[:, Q_OFF:K_OFF])
    vgz = _dot(u, win_ref[:, V_OFF:XBC_OFF])
    kt = _dot_nt(wkt_ref[...], u)
    xbc = _dot(u, win_ref[:, XBC_OFF:DT_OFF])
    dt_raw = _dot(u, wdt_ref[...])

    for sl in range(CONV_SLABS):
        lanes = slice(sl * V7X_LANES, (sl + 1) * V7X_LANES)
        xbc_s[sl, 0:CONV_HIST, :] = jnp.where(row_start, 0.0, xbc_s[sl, 0:CONV_HIST, :])
        xbc_s[sl, CONV_HIST:CONV_HIST + tm, :] = xbc[:, lanes]
        conv = cb_ref[:, lanes]
        for i in range(CONV_WIDTH):
            off = CONV_HIST - (CONV_WIDTH - 1) + i
            conv = conv + cw_ref[i:i + 1, lanes] * xbc_s[sl, off:off + tm, :]
        xc_s[sl] = jax.nn.silu(conv)
        conv_ref[0, :, lanes] = xbc_s[sl, CONV_HIST + tm - (CONV_WIDTH - 1):CONV_HIST + tm, :]
        xbc_s[sl, 0:CONV_HIST, :] = xbc_s[sl, tm:tm + CONV_HIST, :]

    dt = _softplus(dt_raw + dtb_ref[...])
    a_cum = _cumsum_chunks(dt * (-jnp.exp(alog_ref[...])))
    head_lanes = lax.broadcasted_iota(jnp.int32, (tm, DT_PAD), 1) < SSD_HEADS
    pk_s[...] = jnp.where(head_lanes, a_cum, dt)

    cos2, sin2 = cos_ref[...], sin_ref[...]
    cos_t, sin_t = cost_ref[...], sint_ref[...]
    half = RET_DK // 2
    for hh in range(RET_HEADS):
        lanes = slice(hh * RET_DK, (hh + 1) * RET_DK)
        qr = _rotary(q[:, lanes], cos2, sin2)
        q_s[:, lanes] = qr.astype(BF16)
        qd_s[:, lanes] = (qr * qdec_ref[hh]).astype(BF16)
        k1 = kt[hh * RET_DK:hh * RET_DK + half, :]
        k2 = kt[hh * RET_DK + half:(hh + 1) * RET_DK, :]
        kr = jnp.concatenate([k1 * cos_t - k2 * sin_t, k1 * sin_t + k2 * cos_t], axis=0) * (RET_DK ** -0.5)
        kr_s[lanes, :] = kr.astype(BF16)
        kd_s[lanes, :] = (kr * kdec_ref[hh:hh + 1, :]).astype(BF16)
    v_s[...] = vgz[:, 0:RET_WIDTH].astype(BF16)
    g_s[...] = jax.nn.silu(vgz[:, G_OFF - V_OFF:Z_OFF - V_OFF])
    z_s[...] = jax.nn.silu(vgz[:, Z_OFF - V_OFF:XBC_OFF - V_OFF])


def _mixp_heads(tile, h_ref, dmat_ref, cdec_ref, gn_ref, dsk_ref, sn_ref, wout_ref, o_ref, ret_ref, ssm_ref,
                mix_s, sret_s, sssm_s, stage, *, tm, nt):
    q_s, qd_s, kr_s, kd_s, v_s, g_s, z_s, xc_s, pk_s = stage
    nc = tm // CHUNK
    row_start = (tile % nt) == 0

    row_i = lax.broadcasted_iota(jnp.int32, (CHUNK, CHUNK), 0)
    col_i = lax.broadcasted_iota(jnp.int32, (CHUNK, CHUNK), 1)
    causal = row_i >= col_i
    low_lanes = col_i < SSD_HEAD_DIM
    bd_rows = lax.broadcasted_iota(jnp.int32, (2 * SSD_STATE, 2 * SSD_HEAD_DIM), 0)
    bd_cols = lax.broadcasted_iota(jnp.int32, (2 * SSD_STATE, 2 * SSD_HEAD_DIM), 1)
    bd_mask = (bd_rows < SSD_STATE) == (bd_cols < SSD_HEAD_DIM)

    inner, kv = {}, {}
    scores, c_exp, st_inc, st_dec = {}, {}, {}, {}
    for c in range(nc):
        rows = slice(c * CHUNK, (c + 1) * CHUNK)
        for hh in range(RET_HEADS):
            lanes = slice(hh * RET_DK, (hh + 1) * RET_DK)
            inner[c, hh] = (_dot(q_s[rows, lanes], kr_s[lanes, rows]) * dmat_ref[hh]).astype(BF16)
            kv[c, hh] = _dot(kd_s[lanes, rows], v_s[rows, lanes])

        pk = pk_s[rows, :]
        pk_t = pk.T
        a_t = pk_t[0:SSD_HEADS, :]
        dt_t = pk_t[SSD_HEADS:2 * SSD_HEADS, :]
        a_last = a_t[:, CHUNK - 1:CHUNK]
        w_t = jnp.exp(a_last - a_t) * dt_t
        chunk_dec = jnp.exp(a_last)
        for grp in range(SSD_GROUPS):
            cg = xc_s[SSD_PAIRS + SSD_GROUPS + grp, rows, :]
            b_t = xc_s[SSD_PAIRS + grp, rows, :].T
            gmat = _dot(cg.astype(BF16), b_t.astype(BF16))
            for pj in range(SSD_PAIRS // SSD_GROUPS):
                j = grp * (SSD_PAIRS // SSD_GROUPS) + pj
                heads = (2 * j, 2 * j + 1)
                x_pair = xc_s[j, rows, :].astype(BF16)
                b_w = jnp.concatenate([b_t * w_t[hd:hd + 1, :] for hd in heads], axis=0).astype(BF16)
                st_inc[c, j] = jnp.where(bd_mask, _dot(b_w, x_pair), 0.0)
                st_dec[c, j] = jnp.concatenate(
                    [jnp.broadcast_to(chunk_dec[hd:hd + 1, :], (SSD_STATE, 2 * SSD_HEAD_DIM)) for hd in heads], axis=0)
                for hd in heads:
                    a_col = jnp.broadcast_to(pk[:, hd:hd + 1], (CHUNK, CHUNK))
                    seg = a_col - a_t[hd:hd + 1, :]
                    lmat = jnp.where(causal, jnp.exp(jnp.where(causal, seg, 0.0)), 0.0)
                    scores[c, hd] = (gmat * lmat * dt_t[hd:hd + 1, :]).astype(BF16)
                    c_exp[c, hd] = (cg * jnp.exp(a_col)).astype(BF16)

    s_in, st_in = {}, {}
    for hh in range(RET_HEADS):
        s = jnp.where(row_start, 0.0, sret_s[hh])
        for c in range(nc):
            s_in[c, hh] = s.astype(BF16)
            s = cdec_ref[hh:hh + 1, :] * s + kv[c, hh]
        sret_s[hh] = s
        ret_ref[0, hh] = s
    for j in range(SSD_PAIRS):
        s = jnp.where(row_start, 0.0, sssm_s[j])
        for c in range(nc):
            st_in[c, j] = s.astype(BF16)
            s = st_dec[c, j] * s + st_inc[c, j]
        sssm_s[j] = s
        ssm_ref[0, j * 2 * SSD_HEAD_DIM:(j + 1) * 2 * SSD_HEAD_DIM, :] = (s[0:SSD_STATE] + s[SSD_STATE:]).T

    for c in range(nc):
        rows = slice(c * CHUNK, (c + 1) * CHUNK)
        for hh in range(RET_HEADS):
            lanes = slice(hh * RET_DV, (hh + 1) * RET_DV)
            lhs = jnp.concatenate([inner[c, hh], qd_s[rows, lanes]], axis=1)
            rhs = jnp.concatenate([v_s[rows, lanes], s_in[c, hh]], axis=0)
            r = _group_norm(_dot(lhs, rhs)) * gn_ref[:, lanes]
            mix_s[rows, lanes] = (g_s[rows, lanes] * r).astype(BF16)
        ys = []
        for j in range(SSD_PAIRS):
            heads = (2 * j, 2 * j + 1)
            x_pair = xc_s[j, rows, :]
            x_bd = jnp.concatenate([jnp.where(low_lanes, x_pair, 0.0), jnp.where(low_lanes, 0.0, x_pair)], axis=0)
            lhs = jnp.concatenate([scores[c, heads[0]], scores[c, heads[1]], c_exp[c, heads[0]], c_exp[c, heads[1]]],
                                  axis=1)
            rhs = jnp.concatenate([x_bd.astype(BF16), st_in[c, j]], axis=0)
            ys.append(_dot(lhs, rhs) + dsk_ref[:, j * 2 * SSD_HEAD_DIM:(j + 1) * 2 * SSD_HEAD_DIM] * x_pair)
        y_all = jnp.concatenate(ys, axis=1) * z_s[rows, :]
        mix_s[rows, RET_WIDTH:RET_WIDTH + SSD_WIDTH] = _rmsnorm(y_all, sn_ref[...]).astype(BF16)

    o_ref[...] = h_ref[...] + _dot(mix_s[...], wout_ref[...])


def _mix_prompt_kernel(h_ref, hp_ref, g_ref, win_ref, wkt_ref, wdt_ref, cos_ref, sin_ref, cost_ref, sint_ref,
                       dmat_ref, qdec_ref, kdec_ref, cdec_ref, gn_ref, cw_ref, cb_ref, dtb_ref, alog_ref, dsk_ref,
                       sn_ref, wout_ref,
                       o_ref, ret_ref, ssm_ref, conv_ref,
                       xbc_s, mix_s, sret_s, sssm_s, *stage_s, tm, nt, n_tiles):
    s = pl.program_id(0)
    sets = (tuple(r.at[0] for r in stage_s), tuple(r.at[1] for r in stage_s))

    @pl.when(s == 0)
    def _():
        for r in stage_s:
            r[...] = jnp.zeros_like(r)
        sret_s[...] = jnp.zeros_like(sret_s)
        sssm_s[...] = jnp.zeros_like(sssm_s)
        xbc_s[...] = jnp.zeros_like(xbc_s)

    def step(write_set, read_set):
        _mixp_project(jnp.minimum(s, n_tiles - 1), h_ref, g_ref, win_ref, wkt_ref, wdt_ref, cos_ref, sin_ref,
                      cost_ref, sint_ref, qdec_ref, kdec_ref, cw_ref, cb_ref, dtb_ref, alog_ref, conv_ref, xbc_s,
                      write_set, tm=tm, nt=nt)
        _mixp_heads(jnp.maximum(s - 1, 0), hp_ref, dmat_ref, cdec_ref, gn_ref, dsk_ref, sn_ref, wout_ref, o_ref, ret_ref,
                    ssm_ref, mix_s, sret_s, sssm_s, read_set, tm=tm, nt=nt)

    @pl.when(s % 2 == 0)
    def _():
        step(sets[0], sets[1])

    @pl.when(s % 2 == 1)
    def _():
        step(sets[1], sets[0])


def _mix_prompt(h, gain, w_in, w_kt, w_dt, cos2, sin2, cos_t, sin_t, dmat, qdec, kdec, cdec, gn, conv_w, conv_b,
                dt_bias, a_log, dskip, ssd_gain, w_out, *, batch, seq, tm):
    nt = seq // tm
    n_tiles = batch * nt
    d = h.shape[1]
    cur = lambda s: jnp.minimum(s, n_tiles - 1)
    prev = lambda s: jnp.maximum(s - 1, 0)
    row_cur = pl.BlockSpec((tm, d), lambda s: (cur(s), 0))
    row_prev = pl.BlockSpec((tm, d), lambda s: (prev(s), 0))
    pos = pl.BlockSpec((tm, RET_DK), lambda s: (cur(s) % nt, 0))
    pos_t = pl.BlockSpec((RET_DK // 2, tm), lambda s: (0, cur(s) % nt))
    consts = [gain, w_in, w_kt, w_dt]
    tail = [dmat, qdec, kdec, cdec, gn, conv_w, conv_b, dt_bias, a_log, dskip, ssd_gain, w_out]
    stage = [
        pltpu.VMEM((2, tm, RET_HEADS * RET_DK), BF16),
        pltpu.VMEM((2, tm, RET_HEADS * RET_DK), BF16),
        pltpu.VMEM((2, RET_HEADS * RET_DK, tm), BF16),
        pltpu.VMEM((2, RET_HEADS * RET_DK, tm), BF16),
        pltpu.VMEM((2, tm, RET_WIDTH), BF16),
        pltpu.VMEM((2, tm, RET_WIDTH), F32),
        pltpu.VMEM((2, tm, SSD_WIDTH), F32),
        pltpu.VMEM((2, CONV_SLABS, tm, V7X_LANES), F32),
        pltpu.VMEM((2, tm, DT_PAD), F32),
    ]
    return pl.pallas_call(
        functools.partial(_mix_prompt_kernel, tm=tm, nt=nt, n_tiles=n_tiles),
        out_shape=[
            jax.ShapeDtypeStruct(h.shape, F32),
            jax.ShapeDtypeStruct((batch, RET_HEADS, RET_DK, RET_DV), F32),
            jax.ShapeDtypeStruct((batch, SSD_WIDTH, SSD_STATE), F32),
            jax.ShapeDtypeStruct((batch, CONV_WIDTH - 1, CONV_CH), F32),
        ],
        grid=(n_tiles + 1,),
        in_specs=([row_cur, row_prev] + [_resident(a.shape) for a in consts] + [pos, pos, pos_t, pos_t]
                  + [_resident(a.shape) for a in tail]),
        out_specs=[
            row_prev,
            pl.BlockSpec((1, RET_HEADS, RET_DK, RET_DV), lambda s: (prev(s) // nt, 0, 0, 0)),
            pl.BlockSpec((1, SSD_WIDTH, SSD_STATE), lambda s: (prev(s) // nt, 0, 0)),
            pl.BlockSpec((1, CONV_WIDTH - 1, CONV_CH), lambda s: (cur(s) // nt, 0, 0)),
        ],
        scratch_shapes=[
            pltpu.VMEM((CONV_SLABS, CONV_HIST + tm, V7X_LANES), F32),
            pltpu.VMEM((tm, RET_WIDTH + SSD_WIDTH), BF16),
            pltpu.VMEM((RET_HEADS, RET_DK, RET_DV), F32),
            pltpu.VMEM((SSD_PAIRS, 2 * SSD_STATE, 2 * SSD_HEAD_DIM), F32),
        ] + stage,
        compiler_params=_params(("arbitrary",)),
        name="mix_prompt",
    )(h, h, *consts, cos2, sin2, cos_t, sin_t, *tail)


def _split_hi_lo(x):
    hi = x.astype(BF16).astype(F32)
    return hi, x - hi


def _sample_mixers(proj_ref, sret_ref, sssm_ref, sconv_ref, cos_ref, sin_ref, gam_ref,
                   gn_ref, cw_ref, cb_ref, dtb_ref, alog_ref, dsk_ref, sn_ref,
                   mix_ref, oret_ref, ossm_ref, oconv_ref):
    bs = proj_ref.shape[1]
    proj = proj_ref[0]
    xbc = proj[:, XBC_OFF:DT_OFF]
    hist = sconv_ref[0]
    taps = [hist[:, i * CONV_CH:(i + 1) * CONV_CH] for i in range(CONV_WIDTH - 1)] + [xbc]
    conv = cb_ref[...]
    for i in range(CONV_WIDTH):
        conv = conv + cw_ref[i:i + 1, :] * taps[i]
    oconv_ref[0] = jnp.concatenate(taps[1:], axis=1)
    xc = jax.nn.silu(conv)

    dt = _softplus(proj[:, DT_OFF:IN_PROJ_WIDTH] + dtb_ref[:, 0:SSD_HEADS])

    sub_i = lax.broadcasted_iota(jnp.int32, (bs, V7X_LANES), 0)
    row_of = lax.broadcasted_iota(jnp.int32, (4 * bs, V7X_LANES), 0) % bs
    cos2 = cos_ref[...]
    sin2 = sin_ref[...]

    def outer_lhs(x):
        hi, lo = _split_hi_lo(x)
        return jnp.concatenate([hi, hi, lo, lo], axis=0)

    def outer_rhs(x):
        hi, lo = _split_hi_lo(x)
        return jnp.concatenate([hi, lo, hi, lo], axis=0).astype(BF16)

    def only_sample(x4, b):
        return jnp.where(row_of == b, x4, 0.0).astype(BF16)

    for hh in range(RET_HEADS):
        lanes = slice(hh * RET_DK, (hh + 1) * RET_DK)
        qr = _rotary(proj[:, Q_OFF + hh * RET_DK:Q_OFF + (hh + 1) * RET_DK], cos2, sin2)
        kr = _rotary(proj[:, K_OFF + hh * RET_DK:K_OFF + (hh + 1) * RET_DK], cos2, sin2) * (RET_DK ** -0.5)
        vh = proj[:, V_OFF + hh * RET_DV:V_OFF + (hh + 1) * RET_DV]
        k4, v4, q_bf = outer_lhs(kr), outer_rhs(vh), qr.astype(BF16)
        gamma = gam_ref[hh:hh + 1, :]
        y = jnp.zeros((bs, RET_DV), F32)
        for b in range(bs):
            s_old = sret_ref[b, hh]
            oret_ref[b, hh] = gamma * s_old + _dot_tn(only_sample(k4, b), v4)
            y = jnp.where(sub_i == b, _dot(q_bf, s_old.astype(BF16)), y)
        y = gamma * y + jnp.sum(qr * kr, axis=-1, keepdims=True) * vh
        r = _group_norm(y) * gn_ref[:, lanes]
        mix_ref[0, :, lanes] = jax.nn.silu(proj[:, G_OFF + hh * RET_DV:G_OFF + (hh + 1) * RET_DV]) * r

    xs = xc[:, 0:SSD_WIDTH]
    head_of_lane = lax.broadcasted_iota(jnp.int32, (bs, SSD_WIDTH), 1) // SSD_HEAD_DIM
    dec = jnp.exp(dt * (-jnp.exp(alog_ref[:, 0:SSD_HEADS])))
    dt_wide = jnp.zeros((bs, SSD_WIDTH), F32)
    dec_wide = jnp.zeros((bs, SSD_WIDTH), F32)
    for hd in range(SSD_HEADS):
        dt_wide = jnp.where(head_of_lane == hd, dt[:, hd:hd + 1], dt_wide)
        dec_wide = jnp.where(head_of_lane == hd, dec[:, hd:hd + 1], dec_wide)
    xdt = xs * dt_wide
    x4 = outer_lhs(xdt)
    ys = []
    for j in range(SSD_PAIRS):
        grp = j // (SSD_PAIRS // SSD_GROUPS)
        lanes = slice(j * 2 * SSD_HEAD_DIM, (j + 1) * 2 * SSD_HEAD_DIM)
        bg = xc[:, SSD_WIDTH + grp * SSD_STATE:SSD_WIDTH + (grp + 1) * SSD_STATE]
        c_off = SSD_WIDTH + SSD_GROUPS * SSD_STATE + grp * SSD_STATE
        cg = xc[:, c_off:c_off + SSD_STATE]
        b4, c_bf = outer_rhs(bg), cg.astype(BF16)
        y = jnp.zeros((bs, 2 * SSD_HEAD_DIM), F32)
        for b in range(bs):
            s_old = sssm_ref[b, lanes, :]
            dec_rows = jnp.concatenate(
                [jnp.broadcast_to(dec[b:b + 1, hd:hd + 1], (SSD_HEAD_DIM, SSD_STATE)) for hd in (2 * j, 2 * j + 1)],
                axis=0)
            ossm_ref[b, lanes, :] = dec_rows * s_old + _dot_tn(only_sample(x4[:, lanes], b), b4)
            y = jnp.where(sub_i == b, _dot_nt(c_bf, s_old.astype(BF16)), y)
        ys.append(dec_wide[:, lanes] * y + jnp.sum(cg * bg, axis=-1, keepdims=True) * xdt[:, lanes])
    ys = (jnp.concatenate(ys, axis=1) + dsk_ref[...] * xs) * jax.nn.silu(proj[:, Z_OFF:Z_OFF + SSD_WIDTH])
    mix_ref[0, :, RET_WIDTH:RET_WIDTH + SSD_WIDTH] = _rmsnorm(ys, sn_ref[...])


N_MIXER_IN = 14
N_MIXER_OUT = 4


def _sample_mixer_specs(proj, s_ret, s_ssm, s_conv, consts, steps):
    n = proj.shape[0]
    bs = n // steps
    grouped = lambda a: a.reshape(steps, bs, a.shape[1])
    blk3 = lambda w: pl.BlockSpec((1, bs, w), lambda i: (i, 0, 0))
    ret_blk = pl.BlockSpec((bs, RET_HEADS, RET_DK, RET_DV), lambda i: (i, 0, 0, 0))
    ssm_blk = pl.BlockSpec((bs, SSD_WIDTH, SSD_STATE), lambda i: (i, 0, 0))
    ins = [grouped(proj), s_ret, s_ssm, grouped(s_conv), *consts]
    in_specs = [blk3(proj.shape[1]), ret_blk, ssm_blk, blk3(s_conv.shape[1])] + [_resident(a.shape) for a in consts]
    out_shape = [jax.ShapeDtypeStruct((steps, bs, RET_WIDTH + SSD_WIDTH), F32), jax.ShapeDtypeStruct(s_ret.shape, F32),
                 jax.ShapeDtypeStruct(s_ssm.shape, F32), jax.ShapeDtypeStruct((steps, bs, s_conv.shape[1]), F32)]
    out_specs = [blk3(RET_WIDTH + SSD_WIDTH), ret_blk, ssm_blk, blk3(s_conv.shape[1])]
    return ins, in_specs, out_shape, out_specs


def _softmax_rows(s):
    m = jnp.max(s, axis=-1, keepdims=True)
    p = jnp.exp(s - m)
    return p / jnp.sum(p, axis=-1, keepdims=True)


def _xattn_prompt_kernel(h_ref, g_ref, wq_ref, mk_ref, mv_ref, wo_ref, o_ref, att_s):
    h = h_ref[...]
    c = _rmsnorm(h, g_ref[...]).astype(BF16)
    qx = _dot(c, wq_ref[...]).astype(BF16)
    for hh in range(X_HEADS):
        lanes = slice(hh * X_HEAD_DIM, (hh + 1) * X_HEAD_DIM)
        s = _dot_nt(qx[:, lanes], mk_ref[0, :, lanes]) * (X_HEAD_DIM ** -0.5)
        att = _softmax_rows(s).astype(BF16)
        att_s[:, lanes] = _dot(att, mv_ref[0, :, lanes]).astype(BF16)
    o_ref[...] = h + _dot(att_s[...], wo_ref[...])


def _xattn_prompt(h, gain, w_q, mem_k, mem_v, w_o, *, batch, seq, tm):
    nt = seq // tm
    d = h.shape[1]
    row = pl.BlockSpec((tm, d), lambda b, t: (b * nt + t, 0))
    mem = pl.BlockSpec((1, MEM_TOKENS, d), lambda b, t: (b, 0, 0))
    return pl.pallas_call(
        _xattn_prompt_kernel,
        out_shape=jax.ShapeDtypeStruct(h.shape, F32),
        grid=(batch, nt),
        in_specs=[row, _resident(gain.shape), _resident(w_q.shape), mem, mem, _resident(w_o.shape)],
        out_specs=row,
        scratch_shapes=[pltpu.VMEM((tm, d), BF16)],
        compiler_params=_params(("arbitrary", "arbitrary")),
        name="xattn_prompt",
    )(h, gain, w_q, mem_k, mem_v, w_o)


def _rope_angles(pos):
    half = RET_DK // 2
    inv_freq = ROPE_BASE ** (-jnp.arange(half, dtype=F32) / half)
    ang = pos.astype(F32)[:, None] * inv_freq[None, :]
    return jnp.cos(ang), jnp.sin(ang)


def _rope_tables(pos):
    cos, sin = _rope_angles(pos)
    return jnp.concatenate([cos, cos], axis=-1), jnp.concatenate([-sin, sin], axis=-1)


def _retention_decay_tables(chunk):
    log_g = jnp.log1p(-jnp.exp2(-5.0 - jnp.arange(RET_HEADS, dtype=F32)))
    idx = jnp.arange(chunk, dtype=F32)
    diff = idx[:, None] - idx[None, :]
    causal = diff >= 0
    dmat = jnp.where(causal[None], jnp.exp(log_g[:, None, None] * jnp.where(causal, diff, 0.0)[None]), 0.0)
    q_dec = jnp.exp(log_g[:, None] * (idx[None, :] + 1.0))
    k_dec = jnp.exp(log_g[:, None] * (chunk - 1.0 - idx[None, :]))
    c_dec = jnp.exp(log_g * chunk)
    wide = lambda x: jnp.broadcast_to(x[..., None], x.shape + (V7X_LANES,))
    return dmat, wide(q_dec), k_dec, wide(c_dec)


def kernel(x_prompt, x_sample, mem_prompt, state_ret, state_ssm, state_conv, cache_mem_k, cache_mem_v, ffn1_norm,
           ffn1_w1, ffn1_w3, ffn1_w2, mix_norm, w_in, ret_gn_gain, conv_w, conv_b, dt_bias, A_log, D_skip, ssd_norm,
           w_out, x_norm, mem_norm, w_xq, w_xk, w_xv, w_xo, ffn2_norm, ffn2_w1, ffn2_w3, ffn2_w2, final_norm):
    bp, lp, d = x_prompt.shape
    bsz = x_sample.shape[0]
    depth = ffn1_w1.shape[0]
    row = lambda v: v.reshape(1, -1).astype(F32)
    lane_pad = lambda v: jnp.pad(row(v), ((0, 0), (0, DT_PAD - v.shape[-1])))

    cos_p, sin_p = _rope_tables(jnp.arange(lp))
    cos_pt, sin_pt = (a.T for a in _rope_angles(jnp.arange(lp)))
    cos_s, sin_s = _rope_tables(PAST_LEN + jnp.arange(x_sample.shape[1]))
    dmat, q_dec, k_dec, c_dec = _retention_decay_tables(CHUNK)
    gamma1 = _retention_decay_tables(1)[3]

    y_p = x_prompt.reshape(bp * lp, d)
    y_s = x_sample.reshape(bsz, d)
    outs = {k: [] for k in ("ret_p", "ssm_p", "conv_p", "memk", "memv", "ret_s", "ssm_s", "conv_s")}
    for l in range(depth):
        bf = lambda w: w[l].astype(BF16)
        w_in_f = w_in[l]
        w_in_l = w_in_f.astype(BF16)
        w_kt = w_in_f[:, K_OFF:V_OFF].T.astype(BF16)
        w_dt = jnp.pad(jnp.tile(w_in_f[:, DT_OFF:], (1, 2)), ((0, 0), (0, DT_PAD - 2 * SSD_HEADS))).astype(BF16)
        shared = (row(ret_gn_gain[l]), conv_w[l], row(conv_b[l]))
        ssd_tail = (lane_pad(A_log[l]), row(jnp.repeat(D_skip[l], SSD_HEAD_DIM)), row(ssd_norm[l]))
        w_out_l, w_xq_l, w_xo_l = bf(w_out), bf(w_xq), bf(w_xo)

        mk, mv, mk4, mv4, f1_w1, f1_w3, f1_w2 = _memkv(
            mem_prompt, row(mem_norm[l]), bf(w_xk), bf(w_xv), casts=(ffn1_w1[l], ffn1_w3[l], ffn1_w2[l]))
        f1 = (row(ffn1_norm[l]), f1_w1, f1_w3, f1_w2)

        y_s = _ffn(y_s, *f1, tm=bsz)
        proj_s, = _linear(y_s, [w_in_l], gain=row(mix_norm[l]), tm=bsz)

        mixer_consts = (cos_s, sin_s, gamma1, *shared, lane_pad(dt_bias[l]), *ssd_tail)
        y_p, mix_s, ret_s, ssm_s, conv_s, f2_w1, f2_w3, f2_w2 = _ffn(
            y_p, *f1, mixers=(proj_s, state_ret[l], state_ssm[l].reshape(bsz, SSD_WIDTH, SSD_STATE),
                              state_conv[l].reshape(bsz, (CONV_WIDTH - 1) * CONV_CH), mixer_consts),
            casts=(ffn2_w1[l], ffn2_w3[l], ffn2_w2[l]), tm=512)
        f2 = (row(ffn2_norm[l]), f2_w1, f2_w3, f2_w2)
        y_s, = _linear(mix_s.reshape(bsz, d), [w_out_l], res=y_s, tm=bsz)
        q_s, = _linear(y_s, [w_xq_l], gain=row(x_norm[l]), tm=bsz)

        tm_p = 512
        y_p, ret_p, ssm_p, conv_p = _mix_prompt(
            y_p, row(mix_norm[l]), w_in_l, w_kt, w_dt, cos_p, sin_p, cos_pt, sin_pt, dmat,
            jnp.tile(q_dec, (1, tm_p // CHUNK, 1)), jnp.tile(k_dec, (1, tm_p // CHUNK)), c_dec,
            *shared, lane_pad(jnp.tile(dt_bias[l], 2)), *ssd_tail, w_out_l, batch=bp, seq=lp, tm=tm_p)

        y_p = _xattn_prompt(y_p, row(x_norm[l]), w_xq_l, mk, mv, w_xo_l, batch=bp, seq=lp, tm=2048)
        y_p, att_s = _ffn(y_p, *f2, final_gain=row(final_norm) if l == depth - 1 else None,
                          attention=(q_s.reshape(bsz, X_HEADS, X_HEAD_DIM), cache_mem_k[l], cache_mem_v[l]), tm=512)
        y_s, = _linear(att_s.reshape(bsz, d), [w_xo_l], res=y_s, tm=bsz)
        y_s = _ffn(y_s, *f2, final_gain=row(final_norm) if l == depth - 1 else None, tm=bsz)

        outs["ret_p"].append(ret_p)
        outs["ssm_p"].append(ssm_p.reshape(bp, SSD_HEADS, SSD_HEAD_DIM, SSD_STATE))
        outs["conv_p"].append(conv_p)
        outs["memk"].append(mk4)
        outs["memv"].append(mv4)
        outs["ret_s"].append(ret_s)
        outs["ssm_s"].append(ssm_s.reshape(bsz, SSD_HEADS, SSD_HEAD_DIM, SSD_STATE))
        outs["conv_s"].append(conv_s.reshape(bsz, CONV_WIDTH - 1, CONV_CH))

    st = lambda k: jnp.stack(outs[k])
    return (y_p.reshape(bp, lp, d), y_s.reshape(bsz, x_sample.shape[1], d), st("ret_p"), st("ssm_p"), st("conv_p"),
            st("memk"), st("memv"), st("ret_s"), st("ssm_s"), st("conv_s"))
```

```python
import functools

import jax
import jax.numpy as jnp
from jax import lax
from jax.experimental import pallas as pl
from jax.experimental.pallas import tpu as pltpu

F32 = jnp.float32
BF16 = jnp.bfloat16

D_MODEL = 1024
PAST_LEN = 16384
RET_HEADS = 4
RET_DK = 128
RET_DV = 128
RET_WIDTH = RET_HEADS * RET_DV
SSD_HEADS = 8
SSD_HEAD_DIM = 64
SSD_WIDTH = SSD_HEADS * SSD_HEAD_DIM
SSD_GROUPS = 2
SSD_STATE = 128
SSD_PAIRS = SSD_HEADS // 2
CONV_WIDTH = 4
CONV_CH = SSD_WIDTH + 2 * SSD_GROUPS * SSD_STATE
CHUNK = 128
MEM_TOKENS = 256
X_HEADS = 4
X_HEAD_DIM = D_MODEL // X_HEADS
ROPE_BASE = 10000.0
EPS = 1e-6

Q_OFF = 0
K_OFF = Q_OFF + RET_HEADS * RET_DK
V_OFF = K_OFF + RET_HEADS * RET_DK
G_OFF = V_OFF + RET_WIDTH
Z_OFF = G_OFF + RET_WIDTH
XBC_OFF = Z_OFF + SSD_WIDTH
DT_OFF = XBC_OFF + CONV_CH
IN_PROJ_WIDTH = DT_OFF + SSD_HEADS

V7X_LANES = 128
V7X_SUBLANES = 8
V7X_BF16_SUBLANES = 16
V7X_MXU_COLUMNS = 256
V7X_VMEM_LIMIT_BYTES = 56 * 1024 * 1024
DT_PAD = V7X_LANES
CONV_HIST = V7X_SUBLANES
CONV_SLABS = CONV_CH // V7X_LANES


def _params(sem):
    return pltpu.CompilerParams(dimension_semantics=sem, vmem_limit_bytes=V7X_VMEM_LIMIT_BYTES)


def _resident(shape):
    zeros = (0,) * len(shape)
    return pl.BlockSpec(shape, lambda *_: zeros, pipeline_mode=pl.Buffered(1))


def _rmsnorm(x, gain):
    ms = jnp.mean(x * x, axis=-1, keepdims=True)
    return x * lax.rsqrt(ms + EPS) * gain


def _dot(a, b):
    return jnp.dot(a, b, preferred_element_type=F32)


def _dot_nt(a, b):
    return lax.dot_general(a, b, (((1,), (1,)), ((), ())), preferred_element_type=F32)


def _dot_tn(a, b):
    return lax.dot_general(a, b, (((0,), (0,)), ((), ())), preferred_element_type=F32)


def _softplus(x):
    return jnp.maximum(x, 0.0) + jnp.log1p(jnp.exp(-jnp.abs(x)))


def _rotary(x, cos2, sin2):
    return x * cos2 + pltpu.roll(x, RET_DK // 2, axis=1) * sin2


def _group_norm(y):
    mu = jnp.mean(y, axis=-1, keepdims=True)
    d = y - mu
    var = jnp.mean(d * d, axis=-1, keepdims=True)
    return d * lax.rsqrt(var + EPS)


def _sample_attention(q_ref, k_ref, v_ref, o_ref):
    slabs = MEM_TOKENS * X_HEADS // V7X_SUBLANES
    for b in range(q_ref.shape[0]):
        k3 = k_ref[b].reshape(slabs, V7X_SUBLANES, X_HEAD_DIM)
        v3 = v_ref[b].reshape(slabs, V7X_SUBLANES, X_HEAD_DIM)
        q4 = q_ref[b]
        q8 = jnp.concatenate([q4] * (V7X_SUBLANES // X_HEADS), axis=0)
        s = jnp.sum(k3 * q8[None], axis=-1, keepdims=True) * (X_HEAD_DIM ** -0.5)
        m8 = jnp.max(s, axis=0)
        m4 = jnp.maximum(m8[0:X_HEADS], m8[X_HEADS:])
        p = jnp.exp(s - jnp.concatenate([m4, m4], axis=0)[None])
        acc = jnp.sum(p * v3, axis=0)
        den = jnp.sum(p, axis=0)
        o_ref[b] = (acc[0:X_HEADS] + acc[X_HEADS:]) / (den[0:X_HEADS] + den[X_HEADS:])


def _ffn_kernel(*refs, final_norm, with_attention, with_mixers, n_cast):
    refs = list(refs)
    x_ref, g_ref, w1_ref, w3_ref, w2_ref = refs[:5]
    del refs[:5]
    fg_ref = refs.pop(0) if final_norm else None
    attn_in = [refs.pop(0) for _ in range(3)] if with_attention else None
    mixer_in = [refs.pop(0) for _ in range(N_MIXER_IN)] if with_mixers else None
    cast_in = [refs.pop(0) for _ in range(n_cast)]
    o_ref = refs.pop(0)
    x = x_ref[...]
    xn = _rmsnorm(x, g_ref[...]).astype(BF16)
    hidden = []
    for j in range(0, w1_ref.shape[1], V7X_MXU_COLUMNS):
        cols = slice(j, j + V7X_MXU_COLUMNS)
        hidden.append((jax.nn.silu(_dot(xn, w1_ref[:, cols])) * _dot(xn, w3_ref[:, cols])).astype(BF16))
    out = x + 0.5 * _dot(jnp.concatenate(hidden, axis=1), w2_ref[...])
    if final_norm:
        out = _rmsnorm(out, fg_ref[...])
    o_ref[...] = out
    if with_attention:
        _sample_attention(*attn_in, refs.pop(0))
    if with_mixers:
        _sample_mixers(*mixer_in, *refs[:N_MIXER_OUT])
        del refs[:N_MIXER_OUT]
    _cast_blocks(cast_in, refs)


def _ffn(x, gain, w1, w3, w2, final_gain=None, attention=None, mixers=None, casts=(), *, tm):
    t, d = x.shape
    steps = t // tm
    row = pl.BlockSpec((tm, d), lambda i: (i, 0))
    ins = [x, gain, w1, w3, w2]
    specs = [row, _resident(gain.shape), _resident(w1.shape), _resident(w3.shape), _resident(w2.shape)]
    out_shape, out_specs = [jax.ShapeDtypeStruct((t, d), F32)], [row]
    if final_gain is not None:
        ins.append(final_gain)
        specs.append(_resident(final_gain.shape))
    if attention is not None:
        q, cache_k, cache_v = attention
        bs = q.shape[0] // steps
        row_s = pl.BlockSpec((bs, X_HEADS, X_HEAD_DIM), lambda i: (i, 0, 0))
        mem_s = pl.BlockSpec((bs, MEM_TOKENS, X_HEADS, X_HEAD_DIM), lambda i: (i, 0, 0, 0))
        ins += [q, cache_k, cache_v]
        specs += [row_s, mem_s, mem_s]
        out_shape.append(jax.ShapeDtypeStruct(q.shape, F32))
        out_specs.append(row_s)
    if mixers is not None:
        m_ins, m_specs, m_shape, m_out_specs = _sample_mixer_specs(*mixers, steps)
        ins += m_ins
        specs += m_specs
        out_shape += m_shape
        out_specs += m_out_specs
    c_in, c_shape, c_out = _cast_specs(casts, steps)
    ins += list(casts)
    specs += c_in
    out_shape += c_shape
    out_specs += c_out
    outs = pl.pallas_call(
        functools.partial(_ffn_kernel, final_norm=final_gain is not None, with_attention=attention is not None,
                          with_mixers=mixers is not None, n_cast=len(casts)),
        out_shape=out_shape,
        grid=(steps,),
        in_specs=specs,
        out_specs=out_specs,
        compiler_params=_params(("arbitrary",)),
        name="ffn",
    )(*ins)
    return outs if len(outs) > 1 else outs[0]


def _ffn_stream_kernel(*refs, final_norm):
    if final_norm:
        x_ref, g_ref, w1_ref, w3_ref, w2_ref, fg_ref, o_ref, xn_s, acc_s = refs
    else:
        x_ref, g_ref, w1_ref, w3_ref, w2_ref, o_ref, xn_s, acc_s = refs
    j = pl.program_id(0)

    @pl.when(j == 0)
    def _():
        xn_s[...] = _rmsnorm(x_ref[...], g_ref[...]).astype(BF16)
        acc_s[...] = jnp.zeros_like(acc_s)

    xn = xn_s[...]
    hidden = (jax.nn.silu(_dot(xn, w1_ref[...])) * _dot(xn, w3_ref[...])).astype(BF16)
    acc_s[...] += _dot(hidden, w2_ref[...])

    @pl.when(j == pl.num_programs(0) - 1)
    def _():
        out = x_ref[...] + 0.5 * acc_s[...]
        if final_norm:
            out = _rmsnorm(out, fg_ref[...])
        o_ref[...] = out


def _ffn_stream(x, gain, w1, w3, w2, final_gain=None):
    t, d = x.shape
    ff = w1.shape[1]
    whole = pl.BlockSpec((t, d), lambda j: (0, 0))
    cols = pl.BlockSpec((d, V7X_MXU_COLUMNS), lambda j: (0, j))
    ins = [x, gain, w1, w3, w2]
    specs = [whole, _resident(gain.shape), cols, cols, pl.BlockSpec((V7X_MXU_COLUMNS, d), lambda j: (j, 0))]
    if final_gain is not None:
        ins.append(final_gain)
        specs.append(_resident(final_gain.shape))
    return pl.pallas_call(
        functools.partial(_ffn_stream_kernel, final_norm=final_gain is not None),
        out_shape=jax.ShapeDtypeStruct((t, d), F32),
        grid=(ff // V7X_MXU_COLUMNS,),
        in_specs=specs,
        out_specs=whole,
        scratch_shapes=[pltpu.VMEM((t, d), BF16), pltpu.VMEM((t, d), F32)],
        compiler_params=_params(("arbitrary",)),
        name="ffn_stream",
    )(*ins)


def _linear_kernel(*refs, has_norm, has_res, n_w):
    refs = list(refs)
    x_ref = refs.pop(0)
    g_ref = refs.pop(0) if has_norm else None
    r_ref = refs.pop(0) if has_res else None
    w_refs, o_refs = refs[:n_w], refs[n_w:]
    x = x_ref[...]
    if has_norm:
        x = _rmsnorm(x, g_ref[...])
    xb = x.astype(BF16)
    for w_ref, o_ref in zip(w_refs, o_refs, strict=True):
        y = _dot(xb, w_ref[...])
        if has_res:
            y = r_ref[...] + y
        o_ref[...] = y


def _linear(x, weights, gain=None, res=None, *, tm):
    t, k = x.shape
    ins, specs = [x], [pl.BlockSpec((tm, k), lambda i: (i, 0))]
    if gain is not None:
        ins.append(gain)
        specs.append(_resident(gain.shape))
    if res is not None:
        ins.append(res)
        specs.append(pl.BlockSpec((tm, res.shape[1]), lambda i: (i, 0)))
    for w in weights:
        ins.append(w)
        specs.append(_resident(w.shape))
    outs = pl.pallas_call(
        functools.partial(_linear_kernel, has_norm=gain is not None, has_res=res is not None, n_w=len(weights)),
        out_shape=[jax.ShapeDtypeStruct((t, w.shape[1]), F32) for w in weights],
        grid=(t // tm,),
        in_specs=specs,
        out_specs=[pl.BlockSpec((tm, w.shape[1]), lambda i: (i, 0)) for w in weights],
        compiler_params=_params(("arbitrary",)),
        name="linear",
    )(*ins)
    return outs


def _cast_specs(arrays, steps):
    in_specs, out_shape, out_specs = [], [], []
    for a in arrays:
        rows = pl.cdiv(pl.cdiv(a.shape[0], steps), V7X_BF16_SUBLANES) * V7X_BF16_SUBLANES
        last = pl.cdiv(a.shape[0], rows) - 1
        blk = pl.BlockSpec((rows, a.shape[1]), lambda i, last=last: (jnp.minimum(i, last), 0))
        in_specs.append(blk)
        out_specs.append(blk)
        out_shape.append(jax.ShapeDtypeStruct(a.shape, BF16))
    return in_specs, out_shape, out_specs


def _cast_blocks(in_refs, out_refs):
    for i_ref, o_ref in zip(in_refs, out_refs, strict=True):
        o_ref[...] = i_ref[...].astype(BF16)


def _memkv_kernel(*refs, n_cast):
    x_ref, g_ref, wk_ref, wv_ref = refs[:4]
    cast_in = refs[4:4 + n_cast]
    k_ref, v_ref, k4_ref, v4_ref = refs[4 + n_cast:8 + n_cast]
    xb = _rmsnorm(x_ref[...], g_ref[...]).astype(BF16)
    for w_ref, o_ref, o4_ref in ((wk_ref, k_ref, k4_ref), (wv_ref, v_ref, v4_ref)):
        y = _dot(xb, w_ref[...])
        o_ref[0] = y.astype(BF16)
        o4_ref[0] = y.reshape(MEM_TOKENS, X_HEADS, X_HEAD_DIM)
    _cast_blocks(cast_in, refs[8 + n_cast:])


def _memkv(mem, gain, w_k, w_v, casts=()):
    batch, m, d = mem.shape
    flat = jax.ShapeDtypeStruct((batch, m, d), BF16)
    split = jax.ShapeDtypeStruct((batch, m, X_HEADS, X_HEAD_DIM), F32)
    flat_blk = pl.BlockSpec((1, m, d), lambda b: (b, 0, 0))
    split_blk = pl.BlockSpec((1, m, X_HEADS, X_HEAD_DIM), lambda b: (b, 0, 0, 0))
    c_in, c_shape, c_out = _cast_specs(casts, batch)
    return pl.pallas_call(
        functools.partial(_memkv_kernel, n_cast=len(casts)),
        out_shape=[flat, flat, split, split] + c_shape,
        grid=(batch,),
        in_specs=[pl.BlockSpec((m, d), lambda b: (b, 0)), _resident(gain.shape), _resident(w_k.shape),
                  _resident(w_v.shape)] + c_in,
        out_specs=[flat_blk, flat_blk, split_blk, split_blk] + c_out,
        compiler_params=_params(("arbitrary",)),
        name="memkv",
    )(mem.reshape(batch * m, d), gain, w_k, w_v, *casts)


def _cumsum_chunks(x):
    pos = lax.broadcasted_iota(jnp.int32, x.shape, 0) % CHUNK
    step = 1
    while step < CHUNK:
        x = x + jnp.where(pos >= step, pltpu.roll(x, step, axis=0), 0.0)
        step *= 2
    return x


def _mixp_project(tile, h_ref, g_ref, win_ref, wkt_ref, wdt_ref, cos_ref, sin_ref, cost_ref, sint_ref,
                  qdec_ref, kdec_ref, cw_ref, cb_ref, dtb_ref, alog_ref, conv_ref, xbc_s, stage, *, tm, nt):
    q_s, qd_s, kr_s, kd_s, v_s, g_s, z_s, xc_s, pk_s = stage
    row_start = (tile % nt) == 0

    u = _rmsnorm(h_ref[...], g_ref[...]).astype(BF16)
    q = _dot(u, win_ref[:, Q_OFF:K_OFF])
    vgz = _dot(u, win_ref[:, V_OFF:XBC_OFF])
    kt = _dot_nt(wkt_ref[...], u)
    xbc = _dot(u, win_ref[:, XBC_OFF:DT_OFF])
    dt_raw = _dot(u, wdt_ref[...])

    for sl in range(CONV_SLABS):
        lanes = slice(sl * V7X_LANES, (sl + 1) * V7X_LANES)
        xbc_s[sl, 0:CONV_HIST, :] = jnp.where(row_start, 0.0, xbc_s[sl, 0:CONV_HIST, :])
        xbc_s[sl, CONV_HIST:CONV_HIST + tm, :] = xbc[:, lanes]
        conv = cb_ref[:, lanes]
        for i in range(CONV_WIDTH):
            off = CONV_HIST - (CONV_WIDTH - 1) + i
            conv = conv + cw_ref[i:i + 1, lanes] * xbc_s[sl, off:off + tm, :]
        xc_s[sl] = jax.nn.silu(conv)
        conv_ref[0, :, lanes] = xbc_s[sl, CONV_HIST + tm - (CONV_WIDTH - 1):CONV_HIST + tm, :]
        xbc_s[sl, 0:CONV_HIST, :] = xbc_s[sl, tm:tm + CONV_HIST, :]

    dt = _softplus(dt_raw + dtb_ref[...])
    a_cum = _cumsum_chunks(dt * (-jnp.exp(alog_ref[...])))
    head_lanes = lax.broadcasted_iota(jnp.int32, (tm, DT_PAD), 1) < SSD_HEADS
    pk_s[...] = jnp.where(head_lanes, a_cum, dt)

    cos2, sin2 = cos_ref[...], sin_ref[...]
    cos_t, sin_t = cost_ref[...], sint_ref[...]
    half = RET_DK // 2
    for hh in range(RET_HEADS):
        lanes = slice(hh * RET_DK, (hh + 1) * RET_DK)
        qr = _rotary(q[:, lanes], cos2, sin2)
        q_s[:, lanes] = qr.astype(BF16)
        qd_s[:, lanes] = (qr * qdec_ref[hh]).astype(BF16)
        k1 = kt[hh * RET_DK:hh * RET_DK + half, :]
        k2 = kt[hh * RET_DK + half:(hh + 1) * RET_DK, :]
        kr = jnp.concatenate([k1 * cos_t - k2 * sin_t, k1 * sin_t + k2 * cos_t], axis=0) * (RET_DK ** -0.5)
        kr_s[lanes, :] = kr.astype(BF16)
        kd_s[lanes, :] = (kr * kdec_ref[hh:hh + 1, :]).astype(BF16)
    v_s[...] = vgz[:, 0:RET_WIDTH].astype(BF16)
    g_s[...] = jax.nn.silu(vgz[:, G_OFF - V_OFF:Z_OFF - V_OFF])
    z_s[...] = jax.nn.silu(vgz[:, Z_OFF - V_OFF:XBC_OFF - V_OFF])


def _mixp_heads(tile, h_ref, dmat_ref, cdec_ref, gn_ref, dsk_ref, sn_ref, wout_ref, o_ref, ret_ref, ssm_ref,
                mix_s, sret_s, sssm_s, stage, *, tm, nt):
    q_s, qd_s, kr_s, kd_s, v_s, g_s, z_s, xc_s, pk_s = stage
    nc = tm // CHUNK
    row_start = (tile % nt) == 0

    row_i = lax.broadcasted_iota(jnp.int32, (CHUNK, CHUNK), 0)
    col_i = lax.broadcasted_iota(jnp.int32, (CHUNK, CHUNK), 1)
    causal = row_i >= col_i
    low_lanes = col_i < SSD_HEAD_DIM
    bd_rows = lax.broadcasted_iota(jnp.int32, (2 * SSD_STATE, 2 * SSD_HEAD_DIM), 0)
    bd_cols = lax.broadcasted_iota(jnp.int32, (2 * SSD_STATE, 2 * SSD_HEAD_DIM), 1)
    bd_mask = (bd_rows < SSD_STATE) == (bd_cols < SSD_HEAD_DIM)

    inner, kv = {}, {}
    scores, c_exp, st_inc, st_dec = {}, {}, {}, {}
    for c in range(nc):
        rows = slice(c * CHUNK, (c + 1) * CHUNK)
        for hh in range(RET_HEADS):
            lanes = slice(hh * RET_DK, (hh + 1) * RET_DK)
            inner[c, hh] = (_dot(q_s[rows, lanes], kr_s[lanes, rows]) * dmat_ref[hh]).astype(BF16)
            kv[c, hh] = _dot(kd_s[lanes, rows], v_s[rows, lanes])

        pk = pk_s[rows, :]
        pk_t = pk.T
        a_t = pk_t[0:SSD_HEADS, :]
        dt_t = pk_t[SSD_HEADS:2 * SSD_HEADS, :]
        a_last = a_t[:, CHUNK - 1:CHUNK]
        w_t = jnp.exp(a_last - a_t) * dt_t
        chunk_dec = jnp.exp(a_last)
        for grp in range(SSD_GROUPS):
            cg = xc_s[SSD_PAIRS + SSD_GROUPS + grp, rows, :]
            b_t = xc_s[SSD_PAIRS + grp, rows, :].T
            gmat = _dot(cg.astype(BF16), b_t.astype(BF16))
            for pj in range(SSD_PAIRS // SSD_GROUPS):
                j = grp * (SSD_PAIRS // SSD_GROUPS) + pj
                heads = (2 * j, 2 * j + 1)
                x_pair = xc_s[j, rows, :].astype(BF16)
                b_w = jnp.concatenate([b_t * w_t[hd:hd + 1, :] for hd in heads], axis=0).astype(BF16)
                st_inc[c, j] = jnp.where(bd_mask, _dot(b_w, x_pair), 0.0)
                st_dec[c, j] = jnp.concatenate(
                    [jnp.broadcast_to(chunk_dec[hd:hd + 1, :], (SSD_STATE, 2 * SSD_HEAD_DIM)) for hd in heads], axis=0)
                for hd in heads:
                    a_col = jnp.broadcast_to(pk[:, hd:hd + 1], (CHUNK, CHUNK))
                    seg = a_col - a_t[hd:hd + 1, :]
                    lmat = jnp.where(causal, jnp.exp(jnp.where(causal, seg, 0.0)), 0.0)
                    scores[c, hd] = (gmat * lmat * dt_t[hd:hd + 1, :]).astype(BF16)
                    c_exp[c, hd] = (cg * jnp.exp(a_col)).astype(BF16)

    s_in, st_in = {}, {}
    for hh in range(RET_HEADS):
        s = jnp.where(row_start, 0.0, sret_s[hh])
        for c in range(nc):
            s_in[c, hh] = s.astype(BF16)
            s = cdec_ref[hh:hh + 1, :] * s + kv[c, hh]
        sret_s[hh] = s
        ret_ref[0, hh] = s
    for j in range(SSD_PAIRS):
        s = jnp.where(row_start, 0.0, sssm_s[j])
        for c in range(nc):
            st_in[c, j] = s.astype(BF16)
            s = st_dec[c, j] * s + st_inc[c, j]
        sssm_s[j] = s
        ssm_ref[0, j * 2 * SSD_HEAD_DIM:(j + 1) * 2 * SSD_HEAD_DIM, :] = (s[0:SSD_STATE] + s[SSD_STATE:]).T

    for c in range(nc):
        rows = slice(c * CHUNK, (c + 1) * CHUNK)
        for hh in range(RET_HEADS):
            lanes = slice(hh * RET_DV, (hh + 1) * RET_DV)
            lhs = jnp.concatenate([inner[c, hh], qd_s[rows, lanes]], axis=1)
            rhs = jnp.concatenate([v_s[rows, lanes], s_in[c, hh]], axis=0)
            r = _group_norm(_dot(lhs, rhs)) * gn_ref[:, lanes]
            mix_s[rows, lanes] = (g_s[rows, lanes] * r).astype(BF16)
        ys = []
        for j in range(SSD_PAIRS):
            heads = (2 * j, 2 * j + 1)
            x_pair = xc_s[j, rows, :]
            x_bd = jnp.concatenate([jnp.where(low_lanes, x_pair, 0.0), jnp.where(low_lanes, 0.0, x_pair)], axis=0)
            lhs = jnp.concatenate([scores[c, heads[0]], scores[c, heads[1]], c_exp[c, heads[0]], c_exp[c, heads[1]]],
                                  axis=1)
            rhs = jnp.concatenate([x_bd.astype(BF16), st_in[c, j]], axis=0)
            ys.append(_dot(lhs, rhs) + dsk_ref[:, j * 2 * SSD_HEAD_DIM:(j + 1) * 2 * SSD_HEAD_DIM] * x_pair)
        y_all = jnp.concatenate(ys, axis=1) * z_s[rows, :]
        mix_s[rows, RET_WIDTH:RET_WIDTH + SSD_WIDTH] = _rmsnorm(y_all, sn_ref[...]).astype(BF16)

    o_ref[...] = h_ref[...] + _dot(mix_s[...], wout_ref[...])


def _mix_prompt_kernel(h_ref, hp_ref, g_ref, win_ref, wkt_ref, wdt_ref, cos_ref, sin_ref, cost_ref, sint_ref,
                       dmat_ref, qdec_ref, kdec_ref, cdec_ref, gn_ref, cw_ref, cb_ref, dtb_ref, alog_ref, dsk_ref,
                       sn_ref, wout_ref,
                       o_ref, ret_ref, ssm_ref, conv_ref,
                       xbc_s, mix_s, sret_s, sssm_s, *stage_s, tm, nt, n_tiles):
    s = pl.program_id(0)
    sets = (tuple(r.at[0] for r in stage_s), tuple(r.at[1] for r in stage_s))

    @pl.when(s == 0)
    def _():
        for r in stage_s:
            r[...] = jnp.zeros_like(r)
        sret_s[...] = jnp.zeros_like(sret_s)
        sssm_s[...] = jnp.zeros_like(sssm_s)
        xbc_s[...] = jnp.zeros_like(xbc_s)

    def step(write_set, read_set):
        _mixp_project(jnp.minimum(s, n_tiles - 1), h_ref, g_ref, win_ref, wkt_ref, wdt_ref, cos_ref, sin_ref,
                      cost_ref, sint_ref, qdec_ref, kdec_ref, cw_ref, cb_ref, dtb_ref, alog_ref, conv_ref, xbc_s,
                      write_set, tm=tm, nt=nt)
        _mixp_heads(jnp.maximum(s - 1, 0), hp_ref, dmat_ref, cdec_ref, gn_ref, dsk_ref, sn_ref, wout_ref, o_ref, ret_ref,
                    ssm_ref, mix_s, sret_s, sssm_s, read_set, tm=tm, nt=nt)

    @pl.when(s % 2 == 0)
    def _():
        step(sets[0], sets[1])

    @pl.when(s % 2 == 1)
    def _():
        step(sets[1], sets[0])


def _mix_prompt(h, gain, w_in, w_kt, w_dt, cos2, sin2, cos_t, sin_t, dmat, qdec, kdec, cdec, gn, conv_w, conv_b,
                dt_bias, a_log, dskip, ssd_gain, w_out, *, batch, seq, tm):
    nt = seq // tm
    n_tiles = batch * nt
    d = h.shape[1]
    cur = lambda s: jnp.minimum(s, n_tiles - 1)
    prev = lambda s: jnp.maximum(s - 1, 0)
    row_cur = pl.BlockSpec((tm, d), lambda s: (cur(s), 0))
    row_prev = pl.BlockSpec((tm, d), lambda s: (prev(s), 0))
    pos = pl.BlockSpec((tm, RET_DK), lambda s: (cur(s) % nt, 0))
    pos_t = pl.BlockSpec((RET_DK // 2, tm), lambda s: (0, cur(s) % nt))
    consts = [gain, w_in, w_kt, w_dt]
    tail = [dmat, qdec, kdec, cdec, gn, conv_w, conv_b, dt_bias, a_log, dskip, ssd_gain, w_out]
    stage = [
        pltpu.VMEM((2, tm, RET_HEADS * RET_DK), BF16),
        pltpu.VMEM((2, tm, RET_HEADS * RET_DK), BF16),
        pltpu.VMEM((2, RET_HEADS * RET_DK, tm), BF16),
        pltpu.VMEM((2, RET_HEADS * RET_DK, tm), BF16),
        pltpu.VMEM((2, tm, RET_WIDTH), BF16),
        pltpu.VMEM((2, tm, RET_WIDTH), F32),
        pltpu.VMEM((2, tm, SSD_WIDTH), F32),
        pltpu.VMEM((2, CONV_SLABS, tm, V7X_LANES), F32),
        pltpu.VMEM((2, tm, DT_PAD), F32),
    ]
    return pl.pallas_call(
        functools.partial(_mix_prompt_kernel, tm=tm, nt=nt, n_tiles=n_tiles),
        out_shape=[
            jax.ShapeDtypeStruct(h.shape, F32),
            jax.ShapeDtypeStruct((batch, RET_HEADS, RET_DK, RET_DV), F32),
            jax.ShapeDtypeStruct((batch, SSD_WIDTH, SSD_STATE), F32),
            jax.ShapeDtypeStruct((batch, CONV_WIDTH - 1, CONV_CH), F32),
        ],
        grid=(n_tiles + 1,),
        in_specs=([row_cur, row_prev] + [_resident(a.shape) for a in consts] + [pos, pos, pos_t, pos_t]
                  + [_resident(a.shape) for a in tail]),
        out_specs=[
            row_prev,
            pl.BlockSpec((1, RET_HEADS, RET_DK, RET_DV), lambda s: (prev(s) // nt, 0, 0, 0)),
            pl.BlockSpec((1, SSD_WIDTH, SSD_STATE), lambda s: (prev(s) // nt, 0, 0)),
            pl.BlockSpec((1, CONV_WIDTH - 1, CONV_CH), lambda s: (cur(s) // nt, 0, 0)),
        ],
        scratch_shapes=[
            pltpu.VMEM((CONV_SLABS, CONV_HIST + tm, V7X_LANES), F32),
            pltpu.VMEM((tm, RET_WIDTH + SSD_WIDTH), BF16),
            pltpu.VMEM((RET_HEADS, RET_DK, RET_DV), F32),
            pltpu.VMEM((SSD_PAIRS, 2 * SSD_STATE, 2 * SSD_HEAD_DIM), F32),
        ] + stage,
        compiler_params=_params(("arbitrary",)),
        name="mix_prompt",
    )(h, h, *consts, cos2, sin2, cos_t, sin_t, *tail)


def _split_hi_lo(x):
    hi = x.astype(BF16).astype(F32)
    return hi, x - hi


def _sample_mixers(proj_ref, sret_ref, sssm_ref, sconv_ref, cos_ref, sin_ref, gam_ref,
                   gn_ref, cw_ref, cb_ref, dtb_ref, alog_ref, dsk_ref, sn_ref,
                   mix_ref, oret_ref, ossm_ref, oconv_ref):
    bs = proj_ref.shape[1]
    proj = proj_ref[0]
    xbc = proj[:, XBC_OFF:DT_OFF]
    hist = sconv_ref[0]
    taps = [hist[:, i * CONV_CH:(i + 1) * CONV_CH] for i in range(CONV_WIDTH - 1)] + [xbc]
    conv = cb_ref[...]
    for i in range(CONV_WIDTH):
        conv = conv + cw_ref[i:i + 1, :] * taps[i]
    oconv_ref[0] = jnp.concatenate(taps[1:], axis=1)
    xc = jax.nn.silu(conv)

    dt = _softplus(proj[:, DT_OFF:IN_PROJ_WIDTH] + dtb_ref[:, 0:SSD_HEADS])

    sub_i = lax.broadcasted_iota(jnp.int32, (bs, V7X_LANES), 0)
    row_of = lax.broadcasted_iota(jnp.int32, (4 * bs, V7X_LANES), 0) % bs
    cos2 = cos_ref[...]
    sin2 = sin_ref[...]

    def outer_lhs(x):
        hi, lo = _split_hi_lo(x)
        return jnp.concatenate([hi, hi, lo, lo], axis=0)

    def outer_rhs(x):
        hi, lo = _split_hi_lo(x)
        return jnp.concatenate([hi, lo, hi, lo], axis=0).astype(BF16)

    def only_sample(x4, b):
        return jnp.where(row_of == b, x4, 0.0).astype(BF16)

    for hh in range(RET_HEADS):
        lanes = slice(hh * RET_DK, (hh + 1) * RET_DK)
        qr = _rotary(proj[:, Q_OFF + hh * RET_DK:Q_OFF + (hh + 1) * RET_DK], cos2, sin2)
        kr = _rotary(proj[:, K_OFF + hh * RET_DK:K_OFF + (hh + 1) * RET_DK], cos2, sin2) * (RET_DK ** -0.5)
        vh = proj[:, V_OFF + hh * RET_DV:V_OFF + (hh + 1) * RET_DV]
        k4, v4, q_bf = outer_lhs(kr), outer_rhs(vh), qr.astype(BF16)
        gamma = gam_ref[hh:hh + 1, :]
        y = jnp.zeros((bs, RET_DV), F32)
        for b in range(bs):
            s_old = sret_ref[b, hh]
            oret_ref[b, hh] = gamma * s_old + _dot_tn(only_sample(k4, b), v4)
            y = jnp.where(sub_i == b, _dot(q_bf, s_old.astype(BF16)), y)
        y = gamma * y + jnp.sum(qr * kr, axis=-1, keepdims=True) * vh
        r = _group_norm(y) * gn_ref[:, lanes]
        mix_ref[0, :, lanes] = jax.nn.silu(proj[:, G_OFF + hh * RET_DV:G_OFF + (hh + 1) * RET_DV]) * r

    xs = xc[:, 0:SSD_WIDTH]
    head_of_lane = lax.broadcasted_iota(jnp.int32, (bs, SSD_WIDTH), 1) // SSD_HEAD_DIM
    dec = jnp.exp(dt * (-jnp.exp(alog_ref[:, 0:SSD_HEADS])))
    dt_wide = jnp.zeros((bs, SSD_WIDTH), F32)
    dec_wide = jnp.zeros((bs, SSD_WIDTH), F32)
    for hd in range(SSD_HEADS):
        dt_wide = jnp.where(head_of_lane == hd, dt[:, hd:hd + 1], dt_wide)
        dec_wide = jnp.where(head_of_lane == hd, dec[:, hd:hd + 1], dec_wide)
    xdt = xs * dt_wide
    x4 = outer_lhs(xdt)
    ys = []
    for j in range(SSD_PAIRS):
        grp = j // (SSD_PAIRS // SSD_GROUPS)
        lanes = slice(j * 2 * SSD_HEAD_DIM, (j + 1) * 2 * SSD_HEAD_DIM)
        bg = xc[:, SSD_WIDTH + grp * SSD_STATE:SSD_WIDTH + (grp + 1) * SSD_STATE]
        c_off = SSD_WIDTH + SSD_GROUPS * SSD_STATE + grp * SSD_STATE
        cg = xc[:, c_off:c_off + SSD_STATE]
        b4, c_bf = outer_rhs(bg), cg.astype(BF16)
        y = jnp.zeros((bs, 2 * SSD_HEAD_DIM), F32)
        for b in range(bs):
            s_old = sssm_ref[b, lanes, :]
            dec_rows = jnp.concatenate(
                [jnp.broadcast_to(dec[b:b + 1, hd:hd + 1], (SSD_HEAD_DIM, SSD_STATE)) for hd in (2 * j, 2 * j + 1)],
                axis=0)
            ossm_ref[b, lanes, :] = dec_rows * s_old + _dot_tn(only_sample(x4[:, lanes], b), b4)
            y = jnp.where(sub_i == b, _dot_nt(c_bf, s_old.astype(BF16)), y)
        ys.append(dec_wide[:, lanes] * y + jnp.sum(cg * bg, axis=-1, keepdims=True) * xdt[:, lanes])
    ys = (jnp.concatenate(ys, axis=1) + dsk_ref[...] * xs) * jax.nn.silu(proj[:, Z_OFF:Z_OFF + SSD_WIDTH])
    mix_ref[0, :, RET_WIDTH:RET_WIDTH + SSD_WIDTH] = _rmsnorm(ys, sn_ref[...])


N_MIXER_IN = 14
N_MIXER_OUT = 4


def _sample_mixer_specs(proj, s_ret, s_ssm, s_conv, consts, steps):
    n = proj.shape[0]
    bs = n // steps
    grouped = lambda a: a.reshape(steps, bs, a.shape[1])
    blk3 = lambda w: pl.BlockSpec((1, bs, w), lambda i: (i, 0, 0))
    ret_blk = pl.BlockSpec((bs, RET_HEADS, RET_DK, RET_DV), lambda i: (i, 0, 0, 0))
    ssm_blk = pl.BlockSpec((bs, SSD_WIDTH, SSD_STATE), lambda i: (i, 0, 0))
    ins = [grouped(proj), s_ret, s_ssm, grouped(s_conv), *consts]
    in_specs = [blk3(proj.shape[1]), ret_blk, ssm_blk, blk3(s_conv.shape[1])] + [_resident(a.shape) for a in consts]
    out_shape = [jax.ShapeDtypeStruct((steps, bs, RET_WIDTH + SSD_WIDTH), F32), jax.ShapeDtypeStruct(s_ret.shape, F32),
                 jax.ShapeDtypeStruct(s_ssm.shape, F32), jax.ShapeDtypeStruct((steps, bs, s_conv.shape[1]), F32)]
    out_specs = [blk3(RET_WIDTH + SSD_WIDTH), ret_blk, ssm_blk, blk3(s_conv.shape[1])]
    return ins, in_specs, out_shape, out_specs


def _softmax_rows(s):
    m = jnp.max(s, axis=-1, keepdims=True)
    p = jnp.exp(s - m)
    return p / jnp.sum(p, axis=-1, keepdims=True)


def _xattn_prompt_kernel(h_ref, g_ref, wq_ref, mk_ref, mv_ref, wo_ref, o_ref, att_s):
    h = h_ref[...]
    c = _rmsnorm(h, g_ref[...]).astype(BF16)
    qx = _dot(c, wq_ref[...]).astype(BF16)
    for hh in range(X_HEADS):
        lanes = slice(hh * X_HEAD_DIM, (hh + 1) * X_HEAD_DIM)
        s = _dot_nt(qx[:, lanes], mk_ref[0, :, lanes]) * (X_HEAD_DIM ** -0.5)
        att = _softmax_rows(s).astype(BF16)
        att_s[:, lanes] = _dot(att, mv_ref[0, :, lanes]).astype(BF16)
    o_ref[...] = h + _dot(att_s[...], wo_ref[...])


def _xattn_prompt(h, gain, w_q, mem_k, mem_v, w_o, *, batch, seq, tm):
    nt = seq // tm
    d = h.shape[1]
    row = pl.BlockSpec((tm, d), lambda b, t: (b * nt + t, 0))
    mem = pl.BlockSpec((1, MEM_TOKENS, d), lambda b, t: (b, 0, 0))
    return pl.pallas_call(
        _xattn_prompt_kernel,
        out_shape=jax.ShapeDtypeStruct(h.shape, F32),
        grid=(batch, nt),
        in_specs=[row, _resident(gain.shape), _resident(w_q.shape), mem, mem, _resident(w_o.shape)],
        out_specs=row,
        scratch_shapes=[pltpu.VMEM((tm, d), BF16)],
        compiler_params=_params(("arbitrary", "arbitrary")),
        name="xattn_prompt",
    )(h, gain, w_q, mem_k, mem_v, w_o)


def _rope_angles(pos):
    half = RET_DK // 2
    inv_freq = ROPE_BASE ** (-jnp.arange(half, dtype=F32) / half)
    ang = pos.astype(F32)[:, None] * inv_freq[None, :]
    return jnp.cos(ang), jnp.sin(ang)


def _rope_tables(pos):
    cos, sin = _rope_angles(pos)
    return jnp.concatenate([cos, cos], axis=-1), jnp.concatenate([-sin, sin], axis=-1)


def _retention_decay_tables(chunk):
    log_g = jnp.log1p(-jnp.exp2(-5.0 - jnp.arange(RET_HEADS, dtype=F32)))
    idx = jnp.arange(chunk, dtype=F32)
    diff = idx[:, None] - idx[None, :]
    causal = diff >= 0
    dmat = jnp.where(causal[None], jnp.exp(log_g[:, None, None] * jnp.where(causal, diff, 0.0)[None]), 0.0)
    q_dec = jnp.exp(log_g[:, None] * (idx[None, :] + 1.0))
    k_dec = jnp.exp(log_g[:, None] * (chunk - 1.0 - idx[None, :]))
    c_dec = jnp.exp(log_g * chunk)
    wide = lambda x: jnp.broadcast_to(x[..., None], x.shape + (V7X_LANES,))
    return dmat, wide(q_dec), k_dec, wide(c_dec)


def kernel(x_prompt, x_sample, mem_prompt, state_ret, state_ssm, state_conv, cache_mem_k, cache_mem_v, ffn1_norm,
           ffn1_w1, ffn1_w3, ffn1_w2, mix_norm, w_in, ret_gn_gain, conv_w, conv_b, dt_bias, A_log, D_skip, ssd_norm,
           w_out, x_norm, mem_norm, w_xq, w_xk, w_xv, w_xo, ffn2_norm, ffn2_w1, ffn2_w3, ffn2_w2, final_norm):
    bp, lp, d = x_prompt.shape
    bsz = x_sample.shape[0]
    depth = ffn1_w1.shape[0]
    row = lambda v: v.reshape(1, -1).astype(F32)
    lane_pad = lambda v: jnp.pad(row(v), ((0, 0), (0, DT_PAD - v.shape[-1])))

    cos_p, sin_p = _rope_tables(jnp.arange(lp))
    cos_pt, sin_pt = (a.T for a in _rope_angles(jnp.arange(lp)))
    cos_s, sin_s = _rope_tables(PAST_LEN + jnp.arange(x_sample.shape[1]))
    dmat, q_dec, k_dec, c_dec = _retention_decay_tables(CHUNK)
    gamma1 = _retention_decay_tables(1)[3]

    y_p = x_prompt.reshape(bp * lp, d)
    y_s = x_sample.reshape(bsz, d)
    outs = {k: [] for k in ("ret_p", "ssm_p", "conv_p", "memk", "memv", "ret_s", "ssm_s", "conv_s")}
    for l in range(depth):
        bf = lambda w: w[l].astype(BF16)
        w_in_f = w_in[l]
        w_in_l = w_in_f.astype(BF16)
        w_kt = w_in_f[:, K_OFF:V_OFF].T.astype(BF16)
        w_dt = jnp.pad(jnp.tile(w_in_f[:, DT_OFF:], (1, 2)), ((0, 0), (0, DT_PAD - 2 * SSD_HEADS))).astype(BF16)
        shared = (row(ret_gn_gain[l]), conv_w[l], row(conv_b[l]))
        ssd_tail = (lane_pad(A_log[l]), row(jnp.repeat(D_skip[l], SSD_HEAD_DIM)), row(ssd_norm[l]))
        w_out_l, w_xq_l, w_xo_l = bf(w_out), bf(w_xq), bf(w_xo)

        mk, mv, mk4, mv4, f1_w1, f1_w3, f1_w2 = _memkv(
            mem_prompt, row(mem_norm[l]), bf(w_xk), bf(w_xv), casts=(ffn1_w1[l], ffn1_w3[l], ffn1_w2[l]))
        f1 = (row(ffn1_norm[l]), f1_w1, f1_w3, f1_w2)

        y_s = _ffn_stream(y_s, *f1)
        proj_s, = _linear(y_s, [w_in_l], gain=row(mix_norm[l]), tm=bsz)

        mixer_consts = (cos_s, sin_s, gamma1, *shared, lane_pad(dt_bias[l]), *ssd_tail)
        y_p, mix_s, ret_s, ssm_s, conv_s, f2_w1, f2_w3, f2_w2 = _ffn(
            y_p, *f1, mixers=(proj_s, state_ret[l], state_ssm[l].reshape(bsz, SSD_WIDTH, SSD_STATE),
                              state_conv[l].reshape(bsz, (CONV_WIDTH - 1) * CONV_CH), mixer_consts),
            casts=(ffn2_w1[l], ffn2_w3[l], ffn2_w2[l]), tm=512)
        f2 = (row(ffn2_norm[l]), f2_w1, f2_w3, f2_w2)
        y_s, = _linear(mix_s.reshape(bsz, d), [w_out_l], res=y_s, tm=bsz)
        q_s, = _linear(y_s, [w_xq_l], gain=row(x_norm[l]), tm=bsz)

        tm_p = 512
        y_p, ret_p, ssm_p, conv_p = _mix_prompt(
            y_p, row(mix_norm[l]), w_in_l, w_kt, w_dt, cos_p, sin_p, cos_pt, sin_pt, dmat,
            jnp.tile(q_dec, (1, tm_p // CHUNK, 1)), jnp.tile(k_dec, (1, tm_p // CHUNK)), c_dec,
            *shared, lane_pad(jnp.tile(dt_bias[l], 2)), *ssd_tail, w_out_l, batch=bp, seq=lp, tm=tm_p)

        y_p = _xattn_prompt(y_p, row(x_norm[l]), w_xq_l, mk, mv, w_xo_l, batch=bp, seq=lp, tm=1024)
        y_p, att_s = _ffn(y_p, *f2, final_gain=row(final_norm) if l == depth - 1 else None,
                          attention=(q_s.reshape(bsz, X_HEADS, X_HEAD_DIM), cache_mem_k[l], cache_mem_v[l]), tm=512)
        y_s, = _linear(att_s.reshape(bsz, d), [w_xo_l], res=y_s, tm=bsz)
        y_s = _ffn_stream(y_s, *f2, final_gain=row(final_norm) if l == depth - 1 else None)

        outs["ret_p"].append(ret_p)
        outs["ssm_p"].append(ssm_p.reshape(bp, SSD_HEADS, SSD_HEAD_DIM, SSD_STATE))
        outs["conv_p"].append(conv_p)
        outs["memk"].append(mk4)
        outs["memv"].append(mv4)
        outs["ret_s"].append(ret_s)
        outs["ssm_s"].append(ssm_s.reshape(bsz, SSD_HEADS, SSD_HEAD_DIM, SSD_STATE))
        outs["conv_s"].append(conv_s.reshape(bsz, CONV_WIDTH - 1, CONV_CH))

    st = lambda k: jnp.stack(outs[k])
    return (y_p.reshape(bp, lp, d), y_s.reshape(bsz, x_sample.shape[1], d), st("ret_p"), st("ssm_p"), st("conv_p"),
            st("memk"), st("memv"), st("ret_s"), st("ssm_s"), st("conv_s"))
```

```python
import functools

import jax
import jax.numpy as jnp
from jax import lax
from jax.experimental import pallas as pl
from jax.experimental.pallas import tpu as pltpu

F32 = jnp.float32
BF16 = jnp.bfloat16

D_MODEL = 1024
PAST_LEN = 16384
RET_HEADS = 4
RET_DK = 128
RET_DV = 128
RET_WIDTH = RET_HEADS * RET_DV
SSD_HEADS = 8
SSD_HEAD_DIM = 64
SSD_WIDTH = SSD_HEADS * SSD_HEAD_DIM
SSD_GROUPS = 2
SSD_STATE = 128
SSD_PAIRS = SSD_HEADS // 2
CONV_WIDTH = 4
CONV_CH = SSD_WIDTH + 2 * SSD_GROUPS * SSD_STATE
CHUNK = 128
MEM_TOKENS = 256
X_HEADS = 4
X_HEAD_DIM = D_MODEL // X_HEADS
ROPE_BASE = 10000.0
EPS = 1e-6

Q_OFF = 0
K_OFF = Q_OFF + RET_HEADS * RET_DK
V_OFF = K_OFF + RET_HEADS * RET_DK
G_OFF = V_OFF + RET_WIDTH
Z_OFF = G_OFF + RET_WIDTH
XBC_OFF = Z_OFF + SSD_WIDTH
DT_OFF = XBC_OFF + CONV_CH
IN_PROJ_WIDTH = DT_OFF + SSD_HEADS

V7X_LANES = 128
V7X_SUBLANES = 8
V7X_BF16_SUBLANES = 16
V7X_MXU_COLUMNS = 256
V7X_VMEM_LIMIT_BYTES = 56 * 1024 * 1024
DT_PAD = V7X_LANES
CONV_HIST = V7X_SUBLANES
CONV_SLABS = CONV_CH // V7X_LANES


def _params(sem):
    return pltpu.CompilerParams(dimension_semantics=sem, vmem_limit_bytes=V7X_VMEM_LIMIT_BYTES)


def _resident(shape):
    zeros = (0,) * len(shape)
    return pl.BlockSpec(shape, lambda *_: zeros, pipeline_mode=pl.Buffered(1))


def _rmsnorm(x, gain):
    ms = jnp.mean(x * x, axis=-1, keepdims=True)
    return x * lax.rsqrt(ms + EPS) * gain


def _dot(a, b):
    return jnp.dot(a, b, preferred_element_type=F32)


def _dot_nt(a, b):
    return lax.dot_general(a, b, (((1,), (1,)), ((), ())), preferred_element_type=F32)


def _dot_tn(a, b):
    return lax.dot_general(a, b, (((0,), (0,)), ((), ())), preferred_element_type=F32)


def _softplus(x):
    return jnp.maximum(x, 0.0) + jnp.log1p(jnp.exp(-jnp.abs(x)))


def _rotary(x, cos2, sin2):
    return x * cos2 + pltpu.roll(x, RET_DK // 2, axis=1) * sin2


def _group_norm(y):
    mu = jnp.mean(y, axis=-1, keepdims=True)
    d = y - mu
    var = jnp.mean(d * d, axis=-1, keepdims=True)
    return d * lax.rsqrt(var + EPS)


def _sample_attention(q_ref, k_ref, v_ref, o_ref):
    slabs = MEM_TOKENS * X_HEADS // V7X_SUBLANES
    for b in range(q_ref.shape[0]):
        k3 = k_ref[b].reshape(slabs, V7X_SUBLANES, X_HEAD_DIM)
        v3 = v_ref[b].reshape(slabs, V7X_SUBLANES, X_HEAD_DIM)
        q4 = q_ref[b]
        q8 = jnp.concatenate([q4] * (V7X_SUBLANES // X_HEADS), axis=0)
        s = jnp.sum(k3 * q8[None], axis=-1, keepdims=True) * (X_HEAD_DIM ** -0.5)
        m8 = jnp.max(s, axis=0)
        m4 = jnp.maximum(m8[0:X_HEADS], m8[X_HEADS:])
        p = jnp.exp(s - jnp.concatenate([m4, m4], axis=0)[None])
        acc = jnp.sum(p * v3, axis=0)
        den = jnp.sum(p, axis=0)
        o_ref[b] = (acc[0:X_HEADS] + acc[X_HEADS:]) / (den[0:X_HEADS] + den[X_HEADS:])


def _ffn_kernel(*refs, final_norm, with_attention, with_mixers, n_cast):
    refs = list(refs)
    x_ref, g_ref, w1_ref, w3_ref, w2_ref = refs[:5]
    del refs[:5]
    fg_ref = refs.pop(0) if final_norm else None
    attn_in = [refs.pop(0) for _ in range(3)] if with_attention else None
    mixer_in = [refs.pop(0) for _ in range(N_MIXER_IN)] if with_mixers else None
    cast_in = [refs.pop(0) for _ in range(n_cast)]
    o_ref = refs.pop(0)
    x = x_ref[...]
    xn = _rmsnorm(x, g_ref[...]).astype(BF16)
    hidden = []
    for j in range(0, w1_ref.shape[1], V7X_MXU_COLUMNS):
        cols = slice(j, j + V7X_MXU_COLUMNS)
        hidden.append((jax.nn.silu(_dot(xn, w1_ref[:, cols])) * _dot(xn, w3_ref[:, cols])).astype(BF16))
    out = x + 0.5 * _dot(jnp.concatenate(hidden, axis=1), w2_ref[...])
    if final_norm:
        out = _rmsnorm(out, fg_ref[...])
    o_ref[...] = out
    if with_attention:
        _sample_attention(*attn_in, refs.pop(0))
    if with_mixers:
        _sample_mixers(*mixer_in, *refs[:N_MIXER_OUT])
        del refs[:N_MIXER_OUT]
    _cast_blocks(cast_in, refs)


def _ffn(x, gain, w1, w3, w2, final_gain=None, attention=None, mixers=None, casts=(), *, tm):
    t, d = x.shape
    steps = t // tm
    row = pl.BlockSpec((tm, d), lambda i: (i, 0))
    ins = [x, gain, w1, w3, w2]
    specs = [row, _resident(gain.shape), _resident(w1.shape), _resident(w3.shape), _resident(w2.shape)]
    out_shape, out_specs = [jax.ShapeDtypeStruct((t, d), F32)], [row]
    if final_gain is not None:
        ins.append(final_gain)
        specs.append(_resident(final_gain.shape))
    if attention is not None:
        q, cache_k, cache_v = attention
        bs = q.shape[0] // steps
        row_s = pl.BlockSpec((bs, X_HEADS, X_HEAD_DIM), lambda i: (i, 0, 0))
        mem_s = pl.BlockSpec((bs, MEM_TOKENS, X_HEADS, X_HEAD_DIM), lambda i: (i, 0, 0, 0))
        ins += [q, cache_k, cache_v]
        specs += [row_s, mem_s, mem_s]
        out_shape.append(jax.ShapeDtypeStruct(q.shape, F32))
        out_specs.append(row_s)
    if mixers is not None:
        m_ins, m_specs, m_shape, m_out_specs = _sample_mixer_specs(*mixers, steps)
        ins += m_ins
        specs += m_specs
        out_shape += m_shape
        out_specs += m_out_specs
    c_in, c_shape, c_out = _cast_specs(casts, steps)
    ins += list(casts)
    specs += c_in
    out_shape += c_shape
    out_specs += c_out
    outs = pl.pallas_call(
        functools.partial(_ffn_kernel, final_norm=final_gain is not None, with_attention=attention is not None,
                          with_mixers=mixers is not None, n_cast=len(casts)),
        out_shape=out_shape,
        grid=(steps,),
        in_specs=specs,
        out_specs=out_specs,
        compiler_params=_params(("arbitrary",)),
        name="ffn",
    )(*ins)
    return outs if len(outs) > 1 else outs[0]


def _linear_kernel(*refs, has_norm, has_res, n_w):
    refs = list(refs)
    x_ref = refs.pop(0)
    g_ref = refs.pop(0) if has_norm else None
    r_ref = refs.pop(0) if has_res else None
    w_refs, o_refs = refs[:n_w], refs[n_w:]
    x = x_ref[...]
    if has_norm:
        x = _rmsnorm(x, g_ref[...])
    xb = x.astype(BF16)
    for w_ref, o_ref in zip(w_refs, o_refs, strict=True):
        y = _dot(xb, w_ref[...])
        if has_res:
            y = r_ref[...] + y
        o_ref[...] = y


def _linear(x, weights, gain=None, res=None, *, tm):
    t, k = x.shape
    ins, specs = [x], [pl.BlockSpec((tm, k), lambda i: (i, 0))]
    if gain is not None:
        ins.append(gain)
        specs.append(_resident(gain.shape))
    if res is not None:
        ins.append(res)
        specs.append(pl.BlockSpec((tm, res.shape[1]), lambda i: (i, 0)))
    for w in weights:
        ins.append(w)
        specs.append(_resident(w.shape))
    outs = pl.pallas_call(
        functools.partial(_linear_kernel, has_norm=gain is not None, has_res=res is not None, n_w=len(weights)),
        out_shape=[jax.ShapeDtypeStruct((t, w.shape[1]), F32) for w in weights],
        grid=(t // tm,),
        in_specs=specs,
        out_specs=[pl.BlockSpec((tm, w.shape[1]), lambda i: (i, 0)) for w in weights],
        compiler_params=_params(("arbitrary",)),
        name="linear",
    )(*ins)
    return outs


def _cast_specs(arrays, steps):
    in_specs, out_shape, out_specs = [], [], []
    for a in arrays:
        rows = pl.cdiv(pl.cdiv(a.shape[0], steps), V7X_BF16_SUBLANES) * V7X_BF16_SUBLANES
        last = pl.cdiv(a.shape[0], rows) - 1
        blk = pl.BlockSpec((rows, a.shape[1]), lambda i, last=last: (jnp.minimum(i, last), 0))
        in_specs.append(blk)
        out_specs.append(blk)
        out_shape.append(jax.ShapeDtypeStruct(a.shape, BF16))
    return in_specs, out_shape, out_specs


def _cast_blocks(in_refs, out_refs):
    for i_ref, o_ref in zip(in_refs, out_refs, strict=True):
        o_ref[...] = i_ref[...].astype(BF16)


def _memkv_kernel(*refs, n_cast):
    x_ref, g_ref, wk_ref, wv_ref = refs[:4]
    cast_in = refs[4:4 + n_cast]
    k_ref, v_ref, k4_ref, v4_ref = refs[4 + n_cast:8 + n_cast]
    xb = _rmsnorm(x_ref[...], g_ref[...]).astype(BF16)
    for w_ref, o_ref, o4_ref in ((wk_ref, k_ref, k4_ref), (wv_ref, v_ref, v4_ref)):
        y = _dot(xb, w_ref[...])
        o_ref[0] = y.astype(BF16)
        o4_ref[0] = y.reshape(MEM_TOKENS, X_HEADS, X_HEAD_DIM)
    _cast_blocks(cast_in, refs[8 + n_cast:])


def _memkv(mem, gain, w_k, w_v, casts=()):
    batch, m, d = mem.shape
    flat = jax.ShapeDtypeStruct((batch, m, d), BF16)
    split = jax.ShapeDtypeStruct((batch, m, X_HEADS, X_HEAD_DIM), F32)
    flat_blk = pl.BlockSpec((1, m, d), lambda b: (b, 0, 0))
    split_blk = pl.BlockSpec((1, m, X_HEADS, X_HEAD_DIM), lambda b: (b, 0, 0, 0))
    c_in, c_shape, c_out = _cast_specs(casts, batch)
    return pl.pallas_call(
        functools.partial(_memkv_kernel, n_cast=len(casts)),
        out_shape=[flat, flat, split, split] + c_shape,
        grid=(batch,),
        in_specs=[pl.BlockSpec((m, d), lambda b: (b, 0)), _resident(gain.shape), _resident(w_k.shape),
                  _resident(w_v.shape)] + c_in,
        out_specs=[flat_blk, flat_blk, split_blk, split_blk] + c_out,
        compiler_params=_params(("arbitrary",)),
        name="memkv",
    )(mem.reshape(batch * m, d), gain, w_k, w_v, *casts)


def _cumsum_chunks(x):
    pos = lax.broadcasted_iota(jnp.int32, x.shape, 0) % CHUNK
    step = 1
    while step < CHUNK:
        x = x + jnp.where(pos >= step, pltpu.roll(x, step, axis=0), 0.0)
        step *= 2
    return x


def _mixp_project(tile, h_ref, g_ref, win_ref, wkt_ref, wdt_ref, cos_ref, sin_ref, cost_ref, sint_ref,
                  qdec_ref, kdec_ref, cw_ref, cb_ref, dtb_ref, alog_ref, conv_ref, xbc_s, stage, *, tm, nt):
    q_s, qd_s, kr_s, kd_s, v_s, g_s, z_s, xc_s, pk_s = stage
    row_start = (tile % nt) == 0

    u = _rmsnorm(h_ref[...], g_ref[...]).astype(BF16)
    q = _dot(u, win_ref[:, Q_OFF:K_OFF])
    vgz = _dot(u, win_ref[:, V_OFF:XBC_OFF])
    kt = _dot_nt(wkt_ref[...], u)
    xbc = _dot(u, win_ref[:, XBC_OFF:DT_OFF])
    dt_raw = _dot(u, wdt_ref[...])

    for sl in range(CONV_SLABS):
        lanes = slice(sl * V7X_LANES, (sl + 1) * V7X_LANES)
        xbc_s[sl, 0:CONV_HIST, :] = jnp.where(row_start, 0.0, xbc_s[sl, 0:CONV_HIST, :])
        xbc_s[sl, CONV_HIST:CONV_HIST + tm, :] = xbc[:, lanes]
        conv = cb_ref[:, lanes]
        for i in range(CONV_WIDTH):
            off = CONV_HIST - (CONV_WIDTH - 1) + i
            conv = conv + cw_ref[i:i + 1, lanes] * xbc_s[sl, off:off + tm, :]
        xc_s[sl] = jax.nn.silu(conv)
        conv_ref[0, :, lanes] = xbc_s[sl, CONV_HIST + tm - (CONV_WIDTH - 1):CONV_HIST + tm, :]
        xbc_s[sl, 0:CONV_HIST, :] = xbc_s[sl, tm:tm + CONV_HIST, :]

    dt = _softplus(dt_raw + dtb_ref[...])
    a_cum = _cumsum_chunks(dt * (-jnp.exp(alog_ref[...])))
    head_lanes = lax.broadcasted_iota(jnp.int32, (tm, DT_PAD), 1) < SSD_HEADS
    pk_s[...] = jnp.where(head_lanes, a_cum, dt)

    cos2, sin2 = cos_ref[...], sin_ref[...]
    cos_t, sin_t = cost_ref[...], sint_ref[...]
    half = RET_DK // 2
    for hh in range(RET_HEADS):
        lanes = slice(hh * RET_DK, (hh + 1) * RET_DK)
        qr = _rotary(q[:, lanes], cos2, sin2)
        q_s[:, lanes] = qr.astype(BF16)
        qd_s[:, lanes] = (qr * qdec_ref[hh]).astype(BF16)
        k1 = kt[hh * RET_DK:hh * RET_DK + half, :]
        k2 = kt[hh * RET_DK + half:(hh + 1) * RET_DK, :]
        kr = jnp.concatenate([k1 * cos_t - k2 * sin_t, k1 * sin_t + k2 * cos_t], axis=0) * (RET_DK ** -0.5)
        kr_s[lanes, :] = kr.astype(BF16)
        kd_s[lanes, :] = (kr * kdec_ref[hh:hh + 1, :]).astype(BF16)
    v_s[...] = vgz[:, 0:RET_WIDTH].astype(BF16)
    g_s[...] = jax.nn.silu(vgz[:, G_OFF - V_OFF:Z_OFF - V_OFF])
    z_s[...] = jax.nn.silu(vgz[:, Z_OFF - V_OFF:XBC_OFF - V_OFF])


def _mixp_heads(tile, h_ref, dmat_ref, cdec_ref, gn_ref, dsk_ref, sn_ref, wout_ref, o_ref, ret_ref, ssm_ref,
                mix_s, sret_s, sssm_s, stage, *, tm, nt):
    q_s, qd_s, kr_s, kd_s, v_s, g_s, z_s, xc_s, pk_s = stage
    nc = tm // CHUNK
    row_start = (tile % nt) == 0

    row_i = lax.broadcasted_iota(jnp.int32, (CHUNK, CHUNK), 0)
    col_i = lax.broadcasted_iota(jnp.int32, (CHUNK, CHUNK), 1)
    causal = row_i >= col_i
    low_lanes = col_i < SSD_HEAD_DIM
    bd_rows = lax.broadcasted_iota(jnp.int32, (2 * SSD_STATE, 2 * SSD_HEAD_DIM), 0)
    bd_cols = lax.broadcasted_iota(jnp.int32, (2 * SSD_STATE, 2 * SSD_HEAD_DIM), 1)
    bd_mask = (bd_rows < SSD_STATE) == (bd_cols < SSD_HEAD_DIM)

    inner, kv = {}, {}
    scores, c_exp, st_inc, st_dec = {}, {}, {}, {}
    for c in range(nc):
        rows = slice(c * CHUNK, (c + 1) * CHUNK)
        for hh in range(RET_HEADS):
            lanes = slice(hh * RET_DK, (hh + 1) * RET_DK)
            inner[c, hh] = (_dot(q_s[rows, lanes], kr_s[lanes, rows]) * dmat_ref[hh]).astype(BF16)
            kv[c, hh] = _dot(kd_s[lanes, rows], v_s[rows, lanes])

        pk = pk_s[rows, :]
        pk_t = pk.T
        a_t = pk_t[0:SSD_HEADS, :]
        dt_t = pk_t[SSD_HEADS:2 * SSD_HEADS, :]
        a_last = a_t[:, CHUNK - 1:CHUNK]
        w_t = jnp.exp(a_last - a_t) * dt_t
        chunk_dec = jnp.exp(a_last)
        for grp in range(SSD_GROUPS):
            cg = xc_s[SSD_PAIRS + SSD_GROUPS + grp, rows, :]
            b_t = xc_s[SSD_PAIRS + grp, rows, :].T
            gmat = _dot(cg.astype(BF16), b_t.astype(BF16))
            for pj in range(SSD_PAIRS // SSD_GROUPS):
                j = grp * (SSD_PAIRS // SSD_GROUPS) + pj
                heads = (2 * j, 2 * j + 1)
                x_pair = xc_s[j, rows, :].astype(BF16)
                b_w = jnp.concatenate([b_t * w_t[hd:hd + 1, :] for hd in heads], axis=0).astype(BF16)
                st_inc[c, j] = jnp.where(bd_mask, _dot(b_w, x_pair), 0.0)
                st_dec[c, j] = jnp.concatenate(
                    [jnp.broadcast_to(chunk_dec[hd:hd + 1, :], (SSD_STATE, 2 * SSD_HEAD_DIM)) for hd in heads], axis=0)
                for hd in heads:
                    a_col = jnp.broadcast_to(pk[:, hd:hd + 1], (CHUNK, CHUNK))
                    seg = a_col - a_t[hd:hd + 1, :]
                    lmat = jnp.where(causal, jnp.exp(jnp.where(causal, seg, 0.0)), 0.0)
                    scores[c, hd] = (gmat * lmat * dt_t[hd:hd + 1, :]).astype(BF16)
                    c_exp[c, hd] = (cg * jnp.exp(a_col)).astype(BF16)

    s_in, st_in = {}, {}
    for hh in range(RET_HEADS):
        s = jnp.where(row_start, 0.0, sret_s[hh])
        for c in range(nc):
            s_in[c, hh] = s.astype(BF16)
            s = cdec_ref[hh:hh + 1, :] * s + kv[c, hh]
        sret_s[hh] = s
        ret_ref[0, hh] = s
    for j in range(SSD_PAIRS):
        s = jnp.where(row_start, 0.0, sssm_s[j])
        for c in range(nc):
            st_in[c, j] = s.astype(BF16)
            s = st_dec[c, j] * s + st_inc[c, j]
        sssm_s[j] = s
        ssm_ref[0, j * 2 * SSD_HEAD_DIM:(j + 1) * 2 * SSD_HEAD_DIM, :] = (s[0:SSD_STATE] + s[SSD_STATE:]).T

    for c in range(nc):
        rows = slice(c * CHUNK, (c + 1) * CHUNK)
        for hh in range(RET_HEADS):
            lanes = slice(hh * RET_DV, (hh + 1) * RET_DV)
            lhs = jnp.concatenate([inner[c, hh], qd_s[rows, lanes]], axis=1)
            rhs = jnp.concatenate([v_s[rows, lanes], s_in[c, hh]], axis=0)
            r = _group_norm(_dot(lhs, rhs)) * gn_ref[:, lanes]
            mix_s[rows, lanes] = (g_s[rows, lanes] * r).astype(BF16)
        ys = []
        for j in range(SSD_PAIRS):
            heads = (2 * j, 2 * j + 1)
            x_pair = xc_s[j, rows, :]
            x_bd = jnp.concatenate([jnp.where(low_lanes, x_pair, 0.0), jnp.where(low_lanes, 0.0, x_pair)], axis=0)
            lhs = jnp.concatenate([scores[c, heads[0]], scores[c, heads[1]], c_exp[c, heads[0]], c_exp[c, heads[1]]],
                                  axis=1)
            rhs = jnp.concatenate([x_bd.astype(BF16), st_in[c, j]], axis=0)
            ys.append(_dot(lhs, rhs) + dsk_ref[:, j * 2 * SSD_HEAD_DIM:(j + 1) * 2 * SSD_HEAD_DIM] * x_pair)
        y_all = jnp.concatenate(ys, axis=1) * z_s[rows, :]
        mix_s[rows, RET_WIDTH:RET_WIDTH + SSD_WIDTH] = _rmsnorm(y_all, sn_ref[...]).astype(BF16)

    o_ref[...] = h_ref[...] + _dot(mix_s[...], wout_ref[...])


def _mix_prompt_kernel(h_ref, hp_ref, g_ref, win_ref, wkt_ref, wdt_ref, cos_ref, sin_ref, cost_ref, sint_ref,
                       dmat_ref, qdec_ref, kdec_ref, cdec_ref, gn_ref, cw_ref, cb_ref, dtb_ref, alog_ref, dsk_ref,
                       sn_ref, wout_ref,
                       o_ref, ret_ref, ssm_ref, conv_ref,
                       xbc_s, mix_s, sret_s, sssm_s, *stage_s, tm, nt, n_tiles):
    s = pl.program_id(0)
    sets = (tuple(r.at[0] for r in stage_s), tuple(r.at[1] for r in stage_s))

    @pl.when(s == 0)
    def _():
        for r in stage_s:
            r[...] = jnp.zeros_like(r)
        sret_s[...] = jnp.zeros_like(sret_s)
        sssm_s[...] = jnp.zeros_like(sssm_s)
        xbc_s[...] = jnp.zeros_like(xbc_s)

    def step(write_set, read_set):
        _mixp_project(jnp.minimum(s, n_tiles - 1), h_ref, g_ref, win_ref, wkt_ref, wdt_ref, cos_ref, sin_ref,
                      cost_ref, sint_ref, qdec_ref, kdec_ref, cw_ref, cb_ref, dtb_ref, alog_ref, conv_ref, xbc_s,
                      write_set, tm=tm, nt=nt)
        _mixp_heads(jnp.maximum(s - 1, 0), hp_ref, dmat_ref, cdec_ref, gn_ref, dsk_ref, sn_ref, wout_ref, o_ref, ret_ref,
                    ssm_ref, mix_s, sret_s, sssm_s, read_set, tm=tm, nt=nt)

    @pl.when(s % 2 == 0)
    def _():
        step(sets[0], sets[1])

    @pl.when(s % 2 == 1)
    def _():
        step(sets[1], sets[0])


def _mix_prompt(h, gain, w_in, w_kt, w_dt, cos2, sin2, cos_t, sin_t, dmat, qdec, kdec, cdec, gn, conv_w, conv_b,
                dt_bias, a_log, dskip, ssd_gain, w_out, *, batch, seq, tm):
    nt = seq // tm
    n_tiles = batch * nt
    d = h.shape[1]
    cur = lambda s: jnp.minimum(s, n_tiles - 1)
    prev = lambda s: jnp.maximum(s - 1, 0)
    row_cur = pl.BlockSpec((tm, d), lambda s: (cur(s), 0))
    row_prev = pl.BlockSpec((tm, d), lambda s: (prev(s), 0))
    pos = pl.BlockSpec((tm, RET_DK), lambda s: (cur(s) % nt, 0))
    pos_t = pl.BlockSpec((RET_DK // 2, tm), lambda s: (0, cur(s) % nt))
    consts = [gain, w_in, w_kt, w_dt]
    tail = [dmat, qdec, kdec, cdec, gn, conv_w, conv_b, dt_bias, a_log, dskip, ssd_gain, w_out]
    stage = [
        pltpu.VMEM((2, tm, RET_HEADS * RET_DK), BF16),
        pltpu.VMEM((2, tm, RET_HEADS * RET_DK), BF16),
        pltpu.VMEM((2, RET_HEADS * RET_DK, tm), BF16),
        pltpu.VMEM((2, RET_HEADS * RET_DK, tm), BF16),
        pltpu.VMEM((2, tm, RET_WIDTH), BF16),
        pltpu.VMEM((2, tm, RET_WIDTH), F32),
        pltpu.VMEM((2, tm, SSD_WIDTH), F32),
        pltpu.VMEM((2, CONV_SLABS, tm, V7X_LANES), F32),
        pltpu.VMEM((2, tm, DT_PAD), F32),
    ]
    return pl.pallas_call(
        functools.partial(_mix_prompt_kernel, tm=tm, nt=nt, n_tiles=n_tiles),
        out_shape=[
            jax.ShapeDtypeStruct(h.shape, F32),
            jax.ShapeDtypeStruct((batch, RET_HEADS, RET_DK, RET_DV), F32),
            jax.ShapeDtypeStruct((batch, SSD_WIDTH, SSD_STATE), F32),
            jax.ShapeDtypeStruct((batch, CONV_WIDTH - 1, CONV_CH), F32),
        ],
        grid=(n_tiles + 1,),
        in_specs=([row_cur, row_prev] + [_resident(a.shape) for a in consts] + [pos, pos, pos_t, pos_t]
                  + [_resident(a.shape) for a in tail]),
        out_specs=[
            row_prev,
            pl.BlockSpec((1, RET_HEADS, RET_DK, RET_DV), lambda s: (prev(s) // nt, 0, 0, 0)),
            pl.BlockSpec((1, SSD_WIDTH, SSD_STATE), lambda s: (prev(s) // nt, 0, 0)),
            pl.BlockSpec((1, CONV_WIDTH - 1, CONV_CH), lambda s: (cur(s) // nt, 0, 0)),
        ],
        scratch_shapes=[
            pltpu.VMEM((CONV_SLABS, CONV_HIST + tm, V7X_LANES), F32),
            pltpu.VMEM((tm, RET_WIDTH + SSD_WIDTH), BF16),
            pltpu.VMEM((RET_HEADS, RET_DK, RET_DV), F32),
            pltpu.VMEM((SSD_PAIRS, 2 * SSD_STATE, 2 * SSD_HEAD_DIM), F32),
        ] + stage,
        compiler_params=_params(("arbitrary",)),
        name="mix_prompt",
    )(h, h, *consts, cos2, sin2, cos_t, sin_t, *tail)


def _split_hi_lo(x):
    hi = x.astype(BF16).astype(F32)
    return hi, x - hi


def _sample_mixers(proj_ref, sret_ref, sssm_ref, sconv_ref, cos_ref, sin_ref, gam_ref,
                   gn_ref, cw_ref, cb_ref, dtb_ref, alog_ref, dsk_ref, sn_ref,
                   mix_ref, oret_ref, ossm_ref, oconv_ref):
    bs = proj_ref.shape[1]
    proj = proj_ref[0]
    xbc = proj[:, XBC_OFF:DT_OFF]
    hist = sconv_ref[0]
    taps = [hist[:, i * CONV_CH:(i + 1) * CONV_CH] for i in range(CONV_WIDTH - 1)] + [xbc]
    conv = cb_ref[...]
    for i in range(CONV_WIDTH):
        conv = conv + cw_ref[i:i + 1, :] * taps[i]
    oconv_ref[0] = jnp.concatenate(taps[1:], axis=1)
    xc = jax.nn.silu(conv)

    dt = _softplus(proj[:, DT_OFF:IN_PROJ_WIDTH] + dtb_ref[:, 0:SSD_HEADS])

    sub_i = lax.broadcasted_iota(jnp.int32, (bs, V7X_LANES), 0)
    row_of = lax.broadcasted_iota(jnp.int32, (4 * bs, V7X_LANES), 0) % bs
    cos2 = cos_ref[...]
    sin2 = sin_ref[...]

    def outer_lhs(x):
        hi, lo = _split_hi_lo(x)
        return jnp.concatenate([hi, hi, lo, lo], axis=0)

    def outer_rhs(x):
        hi, lo = _split_hi_lo(x)
        return jnp.concatenate([hi, lo, hi, lo], axis=0).astype(BF16)

    def only_sample(x4, b):
        return jnp.where(row_of == b, x4, 0.0).astype(BF16)

    for hh in range(RET_HEADS):
        lanes = slice(hh * RET_DK, (hh + 1) * RET_DK)
        qr = _rotary(proj[:, Q_OFF + hh * RET_DK:Q_OFF + (hh + 1) * RET_DK], cos2, sin2)
        kr = _rotary(proj[:, K_OFF + hh * RET_DK:K_OFF + (hh + 1) * RET_DK], cos2, sin2) * (RET_DK ** -0.5)
        vh = proj[:, V_OFF + hh * RET_DV:V_OFF + (hh + 1) * RET_DV]
        k4, v4, q_bf = outer_lhs(kr), outer_rhs(vh), qr.astype(BF16)
        gamma = gam_ref[hh:hh + 1, :]
        y = jnp.zeros((bs, RET_DV), F32)
        for b in range(bs):
            s_old = sret_ref[b, hh]
            oret_ref[b, hh] = gamma * s_old + _dot_tn(only_sample(k4, b), v4)
            y = jnp.where(sub_i == b, _dot(q_bf, s_old.astype(BF16)), y)
        y = gamma * y + jnp.sum(qr * kr, axis=-1, keepdims=True) * vh
        r = _group_norm(y) * gn_ref[:, lanes]
        mix_ref[0, :, lanes] = jax.nn.silu(proj[:, G_OFF + hh * RET_DV:G_OFF + (hh + 1) * RET_DV]) * r

    xs = xc[:, 0:SSD_WIDTH]
    head_of_lane = lax.broadcasted_iota(jnp.int32, (bs, SSD_WIDTH), 1) // SSD_HEAD_DIM
    dec = jnp.exp(dt * (-jnp.exp(alog_ref[:, 0:SSD_HEADS])))
    dt_wide = jnp.zeros((bs, SSD_WIDTH), F32)
    dec_wide = jnp.zeros((bs, SSD_WIDTH), F32)
    for hd in range(SSD_HEADS):
        dt_wide = jnp.where(head_of_lane == hd, dt[:, hd:hd + 1], dt_wide)
        dec_wide = jnp.where(head_of_lane == hd, dec[:, hd:hd + 1], dec_wide)
    xdt = xs * dt_wide
    x4 = outer_lhs(xdt)
    ys = []
    for j in range(SSD_PAIRS):
        grp = j // (SSD_PAIRS // SSD_GROUPS)
        lanes = slice(j * 2 * SSD_HEAD_DIM, (j + 1) * 2 * SSD_HEAD_DIM)
        bg = xc[:, SSD_WIDTH + grp * SSD_STATE:SSD_WIDTH + (grp + 1) * SSD_STATE]
        c_off = SSD_WIDTH + SSD_GROUPS * SSD_STATE + grp * SSD_STATE
        cg = xc[:, c_off:c_off + SSD_STATE]
        b4, c_bf = outer_rhs(bg), cg.astype(BF16)
        y = jnp.zeros((bs, 2 * SSD_HEAD_DIM), F32)
        for b in range(bs):
            s_old = sssm_ref[b, lanes, :]
            dec_rows = jnp.concatenate(
                [jnp.broadcast_to(dec[b:b + 1, hd:hd + 1], (SSD_HEAD_DIM, SSD_STATE)) for hd in (2 * j, 2 * j + 1)],
                axis=0)
            ossm_ref[b, lanes, :] = dec_rows * s_old + _dot_tn(only_sample(x4[:, lanes], b), b4)
            y = jnp.where(sub_i == b, _dot_nt(c_bf, s_old.astype(BF16)), y)
        ys.append(dec_wide[:, lanes] * y + jnp.sum(cg * bg, axis=-1, keepdims=True) * xdt[:, lanes])
    ys = (jnp.concatenate(ys, axis=1) + dsk_ref[...] * xs) * jax.nn.silu(proj[:, Z_OFF:Z_OFF + SSD_WIDTH])
    mix_ref[0, :, RET_WIDTH:RET_WIDTH + SSD_WIDTH] = _rmsnorm(ys, sn_ref[...])


N_MIXER_IN = 14
N_MIXER_OUT = 4


def _sample_mixer_specs(proj, s_ret, s_ssm, s_conv, consts, steps):
    n = proj.shape[0]
    bs = n // steps
    grouped = lambda a: a.reshape(steps, bs, a.shape[1])
    blk3 = lambda w: pl.BlockSpec((1, bs, w), lambda i: (i, 0, 0))
    ret_blk = pl.BlockSpec((bs, RET_HEADS, RET_DK, RET_DV), lambda i: (i, 0, 0, 0))
    ssm_blk = pl.BlockSpec((bs, SSD_WIDTH, SSD_STATE), lambda i: (i, 0, 0))
    ins = [grouped(proj), s_ret, s_ssm, grouped(s_conv), *consts]
    in_specs = [blk3(proj.shape[1]), ret_blk, ssm_blk, blk3(s_conv.shape[1])] + [_resident(a.shape) for a in consts]
    out_shape = [jax.ShapeDtypeStruct((steps, bs, RET_WIDTH + SSD_WIDTH), F32), jax.ShapeDtypeStruct(s_ret.shape, F32),
                 jax.ShapeDtypeStruct(s_ssm.shape, F32), jax.ShapeDtypeStruct((steps, bs, s_conv.shape[1]), F32)]
    out_specs = [blk3(RET_WIDTH + SSD_WIDTH), ret_blk, ssm_blk, blk3(s_conv.shape[1])]
    return ins, in_specs, out_shape, out_specs


def _softmax_rows(s):
    m = jnp.max(s, axis=-1, keepdims=True)
    p = jnp.exp(s - m)
    return p / jnp.sum(p, axis=-1, keepdims=True)


def _xattn_prompt_kernel(h_ref, g_ref, wq_ref, mk_ref, mv_ref, wo_ref, o_ref, att_s):
    h = h_ref[...]
    c = _rmsnorm(h, g_ref[...]).astype(BF16)
    for hh in range(X_HEADS):
        lanes = slice(hh * X_HEAD_DIM, (hh + 1) * X_HEAD_DIM)
        if hh % 2 == 0:
            qx2 = _dot(c, wq_ref[:, hh * X_HEAD_DIM:(hh + 2) * X_HEAD_DIM]).astype(BF16)
        qx_h = qx2[:, (hh % 2) * X_HEAD_DIM:(hh % 2 + 1) * X_HEAD_DIM]
        s = _dot_nt(qx_h, mk_ref[0, :, lanes]) * (X_HEAD_DIM ** -0.5)
        att = _softmax_rows(s).astype(BF16)
        att_s[:, lanes] = _dot(att, mv_ref[0, :, lanes]).astype(BF16)
    o_ref[...] = h + _dot(att_s[...], wo_ref[...])


def _xattn_prompt(h, gain, w_q, mem_k, mem_v, w_o, *, batch, seq, tm):
    nt = seq // tm
    d = h.shape[1]
    row = pl.BlockSpec((tm, d), lambda b, t: (b * nt + t, 0))
    mem = pl.BlockSpec((1, MEM_TOKENS, d), lambda b, t: (b, 0, 0))
    return pl.pallas_call(
        _xattn_prompt_kernel,
        out_shape=jax.ShapeDtypeStruct(h.shape, F32),
        grid=(batch, nt),
        in_specs=[row, _resident(gain.shape), _resident(w_q.shape), mem, mem, _resident(w_o.shape)],
        out_specs=row,
        scratch_shapes=[pltpu.VMEM((tm, d), BF16)],
        compiler_params=_params(("arbitrary", "arbitrary")),
        name="xattn_prompt",
    )(h, gain, w_q, mem_k, mem_v, w_o)


def _rope_angles(pos):
    half = RET_DK // 2
    inv_freq = ROPE_BASE ** (-jnp.arange(half, dtype=F32) / half)
    ang = pos.astype(F32)[:, None] * inv_freq[None, :]
    return jnp.cos(ang), jnp.sin(ang)


def _rope_tables(pos):
    cos, sin = _rope_angles(pos)
    return jnp.concatenate([cos, cos], axis=-1), jnp.concatenate([-sin, sin], axis=-1)


def _retention_decay_tables(chunk):
    log_g = jnp.log1p(-jnp.exp2(-5.0 - jnp.arange(RET_HEADS, dtype=F32)))
    idx = jnp.arange(chunk, dtype=F32)
    diff = idx[:, None] - idx[None, :]
    causal = diff >= 0
    dmat = jnp.where(causal[None], jnp.exp(log_g[:, None, None] * jnp.where(causal, diff, 0.0)[None]), 0.0)
    q_dec = jnp.exp(log_g[:, None] * (idx[None, :] + 1.0))
    k_dec = jnp.exp(log_g[:, None] * (chunk - 1.0 - idx[None, :]))
    c_dec = jnp.exp(log_g * chunk)
    wide = lambda x: jnp.broadcast_to(x[..., None], x.shape + (V7X_LANES,))
    return dmat, wide(q_dec), k_dec, wide(c_dec)


def kernel(x_prompt, x_sample, mem_prompt, state_ret, state_ssm, state_conv, cache_mem_k, cache_mem_v, ffn1_norm,
           ffn1_w1, ffn1_w3, ffn1_w2, mix_norm, w_in, ret_gn_gain, conv_w, conv_b, dt_bias, A_log, D_skip, ssd_norm,
           w_out, x_norm, mem_norm, w_xq, w_xk, w_xv, w_xo, ffn2_norm, ffn2_w1, ffn2_w3, ffn2_w2, final_norm):
    bp, lp, d = x_prompt.shape
    bsz = x_sample.shape[0]
    depth = ffn1_w1.shape[0]
    row = lambda v: v.reshape(1, -1).astype(F32)
    lane_pad = lambda v: jnp.pad(row(v), ((0, 0), (0, DT_PAD - v.shape[-1])))

    cos_p, sin_p = _rope_tables(jnp.arange(lp))
    cos_pt, sin_pt = (a.T for a in _rope_angles(jnp.arange(lp)))
    cos_s, sin_s = _rope_tables(PAST_LEN + jnp.arange(x_sample.shape[1]))
    dmat, q_dec, k_dec, c_dec = _retention_decay_tables(CHUNK)
    gamma1 = _retention_decay_tables(1)[3]

    y_p = x_prompt.reshape(bp * lp, d)
    y_s = x_sample.reshape(bsz, d)
    outs = {k: [] for k in ("ret_p", "ssm_p", "conv_p", "memk", "memv", "ret_s", "ssm_s", "conv_s")}
    for l in range(depth):
        bf = lambda w: w[l].astype(BF16)
        w_in_f = w_in[l]
        w_in_l = w_in_f.astype(BF16)
        w_kt = w_in_f[:, K_OFF:V_OFF].T.astype(BF16)
        w_dt = jnp.pad(jnp.tile(w_in_f[:, DT_OFF:], (1, 2)), ((0, 0), (0, DT_PAD - 2 * SSD_HEADS))).astype(BF16)
        shared = (row(ret_gn_gain[l]), conv_w[l], row(conv_b[l]))
        ssd_tail = (lane_pad(A_log[l]), row(jnp.repeat(D_skip[l], SSD_HEAD_DIM)), row(ssd_norm[l]))
        w_out_l, w_xq_l, w_xo_l = bf(w_out), bf(w_xq), bf(w_xo)

        mk, mv, mk4, mv4, f1_w1, f1_w3, f1_w2 = _memkv(
            mem_prompt, row(mem_norm[l]), bf(w_xk), bf(w_xv), casts=(ffn1_w1[l], ffn1_w3[l], ffn1_w2[l]))
        f1 = (row(ffn1_norm[l]), f1_w1, f1_w3, f1_w2)

        y_s = _ffn(y_s, *f1, tm=bsz)
        proj_s, = _linear(y_s, [w_in_l], gain=row(mix_norm[l]), tm=bsz)

        mixer_consts = (cos_s, sin_s, gamma1, *shared, lane_pad(dt_bias[l]), *ssd_tail)
        y_p, mix_s, ret_s, ssm_s, conv_s, f2_w1, f2_w3, f2_w2 = _ffn(
            y_p, *f1, mixers=(proj_s, state_ret[l], state_ssm[l].reshape(bsz, SSD_WIDTH, SSD_STATE),
                              state_conv[l].reshape(bsz, (CONV_WIDTH - 1) * CONV_CH), mixer_consts),
            casts=(ffn2_w1[l], ffn2_w3[l], ffn2_w2[l]), tm=512)
        f2 = (row(ffn2_norm[l]), f2_w1, f2_w3, f2_w2)
        y_s, = _linear(mix_s.reshape(bsz, d), [w_out_l], res=y_s, tm=bsz)
        q_s, = _linear(y_s, [w_xq_l], gain=row(x_norm[l]), tm=bsz)

        tm_p = 512
        y_p, ret_p, ssm_p, conv_p = _mix_prompt(
            y_p, row(mix_norm[l]), w_in_l, w_kt, w_dt, cos_p, sin_p, cos_pt, sin_pt, dmat,
            jnp.tile(q_dec, (1, tm_p // CHUNK, 1)), jnp.tile(k_dec, (1, tm_p // CHUNK)), c_dec,
            *shared, lane_pad(jnp.tile(dt_bias[l], 2)), *ssd_tail, w_out_l, batch=bp, seq=lp, tm=tm_p)

        y_p = _xattn_prompt(y_p, row(x_norm[l]), w_xq_l, mk, mv, w_xo_l, batch=bp, seq=lp, tm=1024)
        y_p, att_s = _ffn(y_p, *f2, final_gain=row(final_norm) if l == depth - 1 else None,
                          attention=(q_s.reshape(bsz, X_HEADS, X_HEAD_DIM), cache_mem_k[l], cache_mem_v[l]), tm=512)
        y_s, = _linear(att_s.reshape(bsz, d), [w_xo_l], res=y_s, tm=bsz)
        y_s = _ffn(y_s, *f2, final_gain=row(final_norm) if l == depth - 1 else None, tm=bsz)

        outs["ret_p"].append(ret_p)
        outs["ssm_p"].append(ssm_p.reshape(bp, SSD_HEADS, SSD_HEAD_DIM, SSD_STATE))
        outs["conv_p"].append(conv_p)
        outs["memk"].append(mk4)
        outs["memv"].append(mv4)
        outs["ret_s"].append(ret_s)
        outs["ssm_s"].append(ssm_s.reshape(bsz, SSD_HEADS, SSD_HEAD_DIM, SSD_STATE))
        outs["conv_s"].append(conv_s.reshape(bsz, CONV_WIDTH - 1, CONV_CH))

    st = lambda k: jnp.stack(outs[k])
    return (y_p.reshape(bp, lp, d), y_s.reshape(bsz, x_sample.shape[1], d), st("ret_p"), st("ssm_p"), st("conv_p"),
            st("memk"), st("memv"), st("ret_s"), st("ssm_s"), st("conv_s"))
```
